```python
import numpy as np
import jax
import jax.numpy as jnp
from jax import lax

D_MODEL = 2048
BATCH = 8
SEQ = 2048
DEPTH = 1

MEM_LEN = 256
SSD_EXPAND = 2
SSD_INNER = SSD_EXPAND * D_MODEL
SSD_HEAD_DIM = 64
SSD_HEADS = SSD_INNER // SSD_HEAD_DIM
SSD_GROUPS = 8
SSD_STATE = 128
SSD_CONV = 4
SSD_CHUNK = 256
SSD_XBC = SSD_INNER + 2 * SSD_GROUPS * SSD_STATE
HEAD_DIM = 128
NSA_HEADS = D_MODEL // HEAD_DIM
NSA_KV_HEADS = 4
NSA_WIDTH = NSA_HEADS * HEAD_DIM
NSA_KV_WIDTH = NSA_KV_HEADS * HEAD_DIM
CMP_BLK = 32
CMP_STRIDE = 16
CMP_HID = 256
SEL_BLK = 64
N_SEL = 16
WINDOW = 512
Q_BLK = 128
FORCE_SCORE = 1.0e4
X_HEADS = 4
X_WIDTH = X_HEADS * HEAD_DIM
N_GROUPS = 8
EXPERTS_PER_GROUP = 8
N_EXPERTS = N_GROUPS * EXPERTS_PER_GROUP
TOP_K = 2
EXPERT_HIDDEN = 1408
MOE_BLK = 128
ROPE_THETA = 10000.0
EPS = 1e-6
NEG_INF = -1e30

IN_SPLITS = (SSD_INNER, SSD_XBC, SSD_HEADS,
             NSA_WIDTH,
             NSA_KV_WIDTH, NSA_KV_WIDTH,
             NSA_KV_WIDTH, NSA_KV_WIDTH,
             NSA_KV_WIDTH, NSA_KV_WIDTH,
             3 * NSA_HEADS,
             D_MODEL, D_MODEL)
IN_COLS = sum(IN_SPLITS)

kernel_name = "hybrid_ssd_nsa_memxattn_hmoe_layer"


def rms_norm(x, w):
    xf = x.astype(jnp.float32)
    y = xf * lax.rsqrt(jnp.mean(xf * xf, axis=-1, keepdims=True) + EPS)
    return (y * w.astype(jnp.float32)).astype(x.dtype)


def rope(x, pos):
    half = x.shape[-1] // 2
    inv = ROPE_THETA ** (-jnp.arange(half, dtype=jnp.float32) / half)
    ang = pos.astype(jnp.float32)[:, None] * inv[None, :]
    shape = (1, pos.shape[0]) + (1,) * (x.ndim - 3) + (half,)
    cos = jnp.cos(ang).reshape(shape)
    sin = jnp.sin(ang).reshape(shape)
    x1 = x[..., :half].astype(jnp.float32)
    x2 = x[..., half:].astype(jnp.float32)
    return jnp.concatenate([x1 * cos - x2 * sin, x2 * cos + x1 * sin], axis=-1).astype(x.dtype)


def masked_softmax(s, mask):
    s = jnp.where(mask, s.astype(jnp.float32), NEG_INF)
    p = jax.nn.softmax(s, axis=-1)
    return jnp.where(mask, p, 0.0)


def causal_dwconv(u, w, b):
    c = u.shape[-1]
    y = lax.conv_general_dilated(u, w.reshape(SSD_CONV, 1, c).astype(u.dtype), window_strides=(1,),
                                 padding=[(SSD_CONV - 1, 0)], dimension_numbers=('NWC', 'WIO', 'NWC'),
                                 feature_group_count=c)
    return y + b


def ssd_chunked(x, da, bm, cm):
    b, t, h, p = x.shape
    g, n = bm.shape[2], bm.shape[3]
    r = h // g
    pad = (-t) % SSD_CHUNK
    if pad:
        x = jnp.pad(x, ((0, 0), (0, pad), (0, 0), (0, 0)))
        da = jnp.pad(da, ((0, 0), (0, pad), (0, 0)))
        bm = jnp.pad(bm, ((0, 0), (0, pad), (0, 0), (0, 0)))
        cm = jnp.pad(cm, ((0, 0), (0, pad), (0, 0), (0, 0)))
    tp = t + pad
    nc, L = tp // SSD_CHUNK, SSD_CHUNK
    x = x.reshape(b, nc, L, g, r, p)
    bm = bm.reshape(b, nc, L, g, n)
    cm = cm.reshape(b, nc, L, g, n)
    da = da.astype(jnp.float32).reshape(b, nc, L, g, r).transpose(0, 1, 3, 4, 2)
    acum = jnp.cumsum(da, axis=-1)
    tri = jnp.tril(jnp.ones((L, L), dtype=bool))
    decay = jnp.exp(jnp.where(tri, acum[..., :, None] - acum[..., None, :], -jnp.inf))
    cb = jnp.einsum('bclgn,bcsgn->bcgls', cm, bm)
    y_diag = jnp.einsum('bcgrls,bcsgrp->bclgrp', cb[:, :, :, None] * decay, x)
    decay_to_end = jnp.exp(acum[..., -1:] - acum)
    states = jnp.einsum('bclgn,bcgrl,bclgrp->bcgrpn', bm, decay_to_end, x)
    chunk_decay = jnp.exp(acum[..., -1])

    def step(hstate, inp):
        s_c, d_c = inp
        return hstate * d_c[..., None, None] + s_c, hstate

    h0 = jnp.zeros((b, g, r, p, n), states.dtype)
    _, h_prev = lax.scan(step, h0, (jnp.swapaxes(states, 0, 1), jnp.swapaxes(chunk_decay, 0, 1)))
    h_prev = jnp.swapaxes(h_prev, 0, 1)
    y_off = jnp.einsum('bclgn,bcgrpn,bcgrl->bclgrp', cm, h_prev, jnp.exp(acum))
    return (y_diag + y_off).reshape(b, tp, h, p)[:, :t]


def ssd_branch(z, xbc, dt_raw, conv_w, conv_b, dt_bias, a_log, d_skip, norm_w):
    b, t, _ = z.shape
    xbc = jax.nn.silu(causal_dwconv(xbc, conv_w, conv_b))
    gn = SSD_GROUPS * SSD_STATE
    xs = xbc[..., :SSD_INNER].reshape(b, t, SSD_HEADS, SSD_HEAD_DIM)
    bm = xbc[..., SSD_INNER:SSD_INNER + gn].reshape(b, t, SSD_GROUPS, SSD_STATE)
    cm = xbc[..., SSD_INNER + gn:].reshape(b, t, SSD_GROUPS, SSD_STATE)
    dt = jax.nn.softplus(dt_raw.astype(jnp.float32) + dt_bias.astype(jnp.float32))
    a = -jnp.exp(a_log.astype(jnp.float32))
    y = ssd_chunked(xs * dt[..., None], dt * a, bm, cm)
    y = y + d_skip[:, None] * xs
    yg = (y.reshape(b, t, SSD_INNER) * jax.nn.silu(z)).reshape(b, t, SSD_GROUPS, SSD_INNER // SSD_GROUPS)
    yg = rms_norm(yg, norm_w.reshape(SSD_GROUPS, SSD_INNER // SSD_GROUPS))
    return yg.reshape(b, t, SSD_INNER).astype(z.dtype)


def compress(raw, pe, w1, w2):
    t = raw.shape[1]
    ncmp = (t - CMP_BLK) // CMP_STRIDE + 1
    idx = CMP_STRIDE * jnp.arange(ncmp)[:, None] + jnp.arange(CMP_BLK)[None, :]
    blocks = raw[:, idx] + pe[:, None, :]
    hid = jax.nn.silu(jnp.einsum('bcjgd,jdf->bgcf', blocks, w1))
    return jnp.einsum('bgcf,fd->bgcd', hid, w2)


def nsa_sequence(q, kc, vc, ks, vs, kw_pad, vw_pad, gates):
    g, r, t, hd = q.shape
    ncmp = kc.shape[1]
    nsel_blocks = t // SEL_BLK
    n_sel = min(N_SEL, nsel_blocks)
    scale = hd ** -0.5
    comp_end = jnp.arange(ncmp) * CMP_STRIDE + CMP_BLK - 1
    ci = jnp.arange(ncmp)[:, None]
    sj = jnp.arange(nsel_blocks)[None, :]
    overlap = ((ci * CMP_STRIDE < (sj + 1) * SEL_BLK) &
               (ci * CMP_STRIDE + CMP_BLK > sj * SEL_BLK)).astype(jnp.float32)
    ks_blk = ks.reshape(g, nsel_blocks, SEL_BLK, hd)
    vs_blk = vs.reshape(g, nsel_blocks, SEL_BLK, hd)
    g_idx = jnp.arange(g)[:, None, None]
    sblk = jnp.arange(nsel_blocks)[None, :]

    def block(qb):
        q0 = qb * Q_BLK
        qq = lax.dynamic_slice_in_dim(q, q0, Q_BLK, axis=2)
        gb = lax.dynamic_slice_in_dim(gates, q0, Q_BLK, axis=3)
        tq = q0 + jnp.arange(Q_BLK)
        s_c = jnp.einsum('grqd,gcd->grqc', qq, kc) * scale
        p_c = masked_softmax(s_c, comp_end[None, :] <= tq[:, None])
        o_c = jnp.einsum('grqc,gcd->grqd', p_c.astype(vc.dtype), vc)
        imp = jnp.einsum('grqc,cs->gqs', p_c, overlap)
        blk_t = (tq // SEL_BLK)[:, None]
        valid = sblk <= blk_t
        forced = (sblk == 0) | (sblk == blk_t) | (sblk == blk_t - 1)
        imp = jnp.where(valid, jnp.where(forced, FORCE_SCORE, imp), -jnp.inf)
        _, sel = lax.top_k(imp, n_sel)
        k_sel = ks_blk[g_idx, sel]
        v_sel = vs_blk[g_idx, sel]
        s_s = jnp.einsum('grqd,gqnkd->grqnk', qq, k_sel) * scale
        kpos = sel[..., None] * SEL_BLK + jnp.arange(SEL_BLK)
        m_s = (kpos <= tq[None, :, None, None]).reshape(g, 1, Q_BLK, n_sel * SEL_BLK)
        p_s = masked_softmax(s_s.reshape(g, r, Q_BLK, n_sel * SEL_BLK), m_s)
        o_s = jnp.einsum('grqnk,gqnkd->grqd', p_s.reshape(s_s.shape).astype(v_sel.dtype), v_sel)
        kwin = lax.dynamic_slice_in_dim(kw_pad, q0, WINDOW + Q_BLK, axis=1)
        vwin = lax.dynamic_slice_in_dim(vw_pad, q0, WINDOW + Q_BLK, axis=1)
        s_w = jnp.einsum('grqd,gkd->grqk', qq, kwin) * scale
        kp = q0 - WINDOW + jnp.arange(WINDOW + Q_BLK)
        diff = tq[:, None] - kp[None, :]
        m_w = (kp[None, :] >= 0) & (diff >= 0) & (diff < WINDOW)
        p_w = masked_softmax(s_w, m_w)
        o_w = jnp.einsum('grqk,gkd->grqd', p_w.astype(vwin.dtype), vwin)
        out = gb[0][..., None] * o_c + gb[1][..., None] * o_s + gb[2][..., None] * o_w
        return out.astype(q.dtype)

    out = lax.map(block, jnp.arange(t // Q_BLK))
    return out.transpose(0, 3, 1, 2, 4).reshape(t, g * r * hd)


def nsa_branch(q, kc_raw, vc_raw, ks, vs, kw, vw, gate_logits, pos, q_norm_w, k_norm_w,
               pe_k, w1_k, w2_k, pe_v, w1_v, w2_v):
    b, t, _ = q.shape
    g, r = NSA_KV_HEADS, NSA_HEADS // NSA_KV_HEADS
    qh = rope(rms_norm(q.reshape(b, t, g, r, HEAD_DIM), q_norm_w), pos).transpose(0, 2, 3, 1, 4)

    def keys(k, wn):
        return rope(rms_norm(k.reshape(b, t, g, HEAD_DIM), wn), pos)

    def heads(v):
        return v.reshape(b, t, g, HEAD_DIM)

    kc = compress(keys(kc_raw, k_norm_w[0]), pe_k, w1_k, w2_k)
    vc = compress(heads(vc_raw), pe_v, w1_v, w2_v)
    ks_ = keys(ks, k_norm_w[1]).transpose(0, 2, 1, 3)
    vs_ = heads(vs).transpose(0, 2, 1, 3)
    wpad = ((0, 0), (0, 0), (WINDOW, 0), (0, 0))
    kw_ = jnp.pad(keys(kw, k_norm_w[2]).transpose(0, 2, 1, 3), wpad)
    vw_ = jnp.pad(heads(vw).transpose(0, 2, 1, 3), wpad)
    gates = jax.nn.sigmoid(gate_logits.astype(jnp.float32)).reshape(b, t, 3, g, r).transpose(0, 2, 3, 4, 1)
    return lax.map(lambda a: nsa_sequence(*a), (qh, kc, vc, ks_, vs_, kw_, vw_, gates))


def hybrid_mixer(h, norm_w, w_in, conv_w, conv_b, dt_bias, a_log, d_skip, ssd_norm_w,
                 q_norm_w, k_norm_w, pe_k, w1_k, w2_k, pe_v, w1_v, w2_v, w_up_ssd, w_up_nsa, w_out):
    t = h.shape[1]
    pos = jnp.arange(t)
    hn = rms_norm(h, norm_w)
    proj = hn @ w_in
    offs = np.cumsum(IN_SPLITS)[:-1].tolist()
    (z, xbc, dt_raw, q, kc_raw, vc_raw, ks, vs, kw, vw, nsa_g, g_ssd, g_nsa) = jnp.split(proj, offs, axis=-1)
    y_ssd = ssd_branch(z, xbc, dt_raw, conv_w, conv_b, dt_bias, a_log, d_skip, ssd_norm_w)
    y_nsa = nsa_branch(q, kc_raw, vc_raw, ks, vs, kw, vw, nsa_g, pos, q_norm_w, k_norm_w,
                       pe_k, w1_k, w2_k, pe_v, w1_v, w2_v)
    merged = jax.nn.sigmoid(g_ssd) * (y_ssd @ w_up_ssd) + jax.nn.sigmoid(g_nsa) * (y_nsa @ w_up_nsa)
    return h + merged @ w_out


def memory_xattn(h, mem, norm_w, mem_norm_w, wq, wkv, q_norm_w, k_norm_w, wo):
    b, t, _ = h.shape
    m = mem.shape[1]
    hn = rms_norm(h, norm_w)
    mn = rms_norm(mem, mem_norm_w)
    q = rms_norm((hn @ wq).reshape(b, t, X_HEADS, HEAD_DIM), q_norm_w)
    kv = (mn @ wkv).reshape(b, m, 2, X_HEADS, HEAD_DIM)
    k = rms_norm(kv[:, :, 0], k_norm_w)
    v = kv[:, :, 1]
    s = jnp.einsum('bthd,bmhd->bhtm', q, k).astype(jnp.float32) * HEAD_DIM ** -0.5
    p = jax.nn.softmax(s, axis=-1).astype(v.dtype)
    o = jnp.einsum('bhtm,bmhd->bthd', p, v).reshape(b, t, X_WIDTH)
    return h + o @ wo


def hier_moe(h, norm_w, rg_w, rg_b, re_w, re_b, w_gate, w_up, w_down):
    b, t, d = h.shape
    n = b * t
    hf = rms_norm(h, norm_w).reshape(n, d)
    pg = jax.nn.softmax((hf @ rg_w + rg_b).astype(jnp.float32), axis=-1)
    grp = jnp.argmax(pg, axis=-1)
    pg_top = jnp.take_along_axis(pg, grp[:, None], axis=-1)
    le = (hf @ re_w + re_b).astype(jnp.float32).reshape(n, N_GROUPS, EXPERTS_PER_GROUP)
    le = jnp.take_along_axis(le, grp[:, None, None], axis=1)[:, 0]
    top_p, top_i = lax.top_k(jax.nn.softmax(le, axis=-1), TOP_K)
    wts = (pg_top * top_p / jnp.sum(top_p, axis=-1, keepdims=True)).reshape(-1)
    eid = (grp[:, None] * EXPERTS_PER_GROUP + top_i).reshape(-1).astype(jnp.int32)
    tok = jnp.repeat(jnp.arange(n, dtype=jnp.int32), TOP_K)
    n_assign = n * TOP_K
    order = jnp.argsort(eid)
    e_sorted = eid[order]
    counts = jnp.bincount(eid, length=N_EXPERTS)
    start = jnp.cumsum(counts) - counts
    padded = (counts + MOE_BLK - 1) // MOE_BLK * MOE_BLK
    pend = jnp.cumsum(padded)
    pstart = pend - padded
    dest = pstart[e_sorted] + jnp.arange(n_assign) - start[e_sorted]
    n_blk = (n_assign + N_EXPERTS * (MOE_BLK - 1) + MOE_BLK - 1) // MOE_BLK
    rows = n_blk * MOE_BLK
    tok_buf = jnp.full((rows,), n, jnp.int32).at[dest].set(tok[order])
    w_buf = jnp.zeros((rows,), jnp.float32).at[dest].set(wts[order])
    blk_e = jnp.minimum(jnp.searchsorted(pend, jnp.arange(n_blk) * MOE_BLK, side='right'), N_EXPERTS - 1)
    x_buf = jnp.concatenate([hf, jnp.zeros((1, d), hf.dtype)], axis=0)[tok_buf].reshape(n_blk, MOE_BLK, d)

    def expert_block(args):
        xb, e = args
        return (jax.nn.silu(xb @ w_gate[e]) * (xb @ w_up[e])) @ w_down[e]

    y_buf = lax.map(expert_block, (x_buf, blk_e)).reshape(rows, d)
    out = jnp.zeros((n + 1, d), y_buf.dtype).at[tok_buf].add(y_buf * w_buf[:, None].astype(y_buf.dtype))
    return h + out[:n].reshape(b, t, d)


def setup_inputs(seed: int = 0) -> dict:
    key = jax.random.key(seed)
    ks = jax.random.split(key, 40)
    f32 = jnp.float32

    def nrm(k, shape, scale):
        return jax.random.normal(k, (DEPTH,) + shape, f32) * scale

    def gain(k, shape):
        return 1.0 + 0.02 * jax.random.normal(k, (DEPTH,) + shape, f32)

    dt0 = jnp.exp(jax.random.uniform(ks[6], (DEPTH, SSD_HEADS), f32, np.log(1e-3), np.log(1e-1)))
    return {
        "x": jax.random.normal(ks[0], (BATCH, SEQ, D_MODEL), f32),
        "mem": jax.random.normal(ks[1], (BATCH, MEM_LEN, D_MODEL), f32),
        "norm1_w": gain(ks[2], (D_MODEL,)),
        "w_in": nrm(ks[3], (D_MODEL, IN_COLS), D_MODEL ** -0.5),
        "ssd_conv_w": nrm(ks[4], (SSD_CONV, SSD_XBC), SSD_CONV ** -0.5),
        "ssd_conv_b": nrm(ks[5], (SSD_XBC,), 0.02),
        "ssd_dt_bias": dt0 + jnp.log(-jnp.expm1(-dt0)),
        "ssd_a_log": jnp.log(jax.random.uniform(ks[7], (DEPTH, SSD_HEADS), f32, 1.0, 16.0)),
        "ssd_d": gain(ks[8], (SSD_HEADS,)),
        "ssd_norm_w": gain(ks[9], (SSD_INNER,)),
        "nsa_q_norm_w": gain(ks[10], (HEAD_DIM,)),
        "nsa_k_norm_w": gain(ks[11], (3, HEAD_DIM)),
        "cmp_pe_k": nrm(ks[12], (CMP_BLK, HEAD_DIM), 0.02),
        "cmp_w1_k": nrm(ks[13], (CMP_BLK, HEAD_DIM, CMP_HID), (CMP_BLK * HEAD_DIM) ** -0.5),
        "cmp_w2_k": nrm(ks[14], (CMP_HID, HEAD_DIM), CMP_HID ** -0.5),
        "cmp_pe_v": nrm(ks[15], (CMP_BLK, HEAD_DIM), 0.02),
        "cmp_w1_v": nrm(ks[16], (CMP_BLK, HEAD_DIM, CMP_HID), (CMP_BLK * HEAD_DIM) ** -0.5),
        "cmp_w2_v": nrm(ks[17], (CMP_HID, HEAD_DIM), CMP_HID ** -0.5),
        "w_up_ssd": nrm(ks[18], (SSD_INNER, D_MODEL), SSD_INNER ** -0.5),
        "w_up_nsa": nrm(ks[19], (NSA_WIDTH, D_MODEL), NSA_WIDTH ** -0.5),
        "w_out": nrm(ks[20], (D_MODEL, D_MODEL), D_MODEL ** -0.5),
        "norm2_w": gain(ks[21], (D_MODEL,)),
        "mem_norm_w": gain(ks[22], (D_MODEL,)),
        "xq_w": nrm(ks[23], (D_MODEL, X_WIDTH), D_MODEL ** -0.5),
        "xkv_w": nrm(ks[24], (D_MODEL, 2 * X_WIDTH), D_MODEL ** -0.5),
        "x_q_norm_w": gain(ks[25], (HEAD_DIM,)),
        "x_k_norm_w": gain(ks[26], (HEAD_DIM,)),
        "xo_w": nrm(ks[27], (X_WIDTH, D_MODEL), X_WIDTH ** -0.5),
        "norm3_w": gain(ks[28], (D_MODEL,)),
        "router_g_w": nrm(ks[29], (D_MODEL, N_GROUPS), D_MODEL ** -0.5),
        "router_g_b": nrm(ks[30], (N_GROUPS,), 0.01),
        "router_e_w": nrm(ks[31], (D_MODEL, N_EXPERTS), D_MODEL ** -0.5),
        "router_e_b": nrm(ks[32], (N_EXPERTS,), 0.01),
        "moe_w_gate": nrm(ks[33], (N_EXPERTS, D_MODEL, EXPERT_HIDDEN), D_MODEL ** -0.5),
        "moe_w_up": nrm(ks[34], (N_EXPERTS, D_MODEL, EXPERT_HIDDEN), D_MODEL ** -0.5),
        "moe_w_down": nrm(ks[35], (N_EXPERTS, EXPERT_HIDDEN, D_MODEL), EXPERT_HIDDEN ** -0.5),
    }


def reference(x, mem, norm1_w, w_in, ssd_conv_w, ssd_conv_b, ssd_dt_bias, ssd_a_log, ssd_d, ssd_norm_w,
              nsa_q_norm_w, nsa_k_norm_w, cmp_pe_k, cmp_w1_k, cmp_w2_k, cmp_pe_v, cmp_w1_v, cmp_w2_v,
              w_up_ssd, w_up_nsa, w_out, norm2_w, mem_norm_w, xq_w, xkv_w, x_q_norm_w, x_k_norm_w, xo_w,
              norm3_w, router_g_w, router_g_b, router_e_w, router_e_b, moe_w_gate, moe_w_up, moe_w_down):
    h = x
    for l in range(DEPTH):
        h = hybrid_mixer(h, norm1_w[l], w_in[l], ssd_conv_w[l], ssd_conv_b[l], ssd_dt_bias[l], ssd_a_log[l],
                         ssd_d[l], ssd_norm_w[l], nsa_q_norm_w[l], nsa_k_norm_w[l], cmp_pe_k[l], cmp_w1_k[l],
                         cmp_w2_k[l], cmp_pe_v[l], cmp_w1_v[l], cmp_w2_v[l], w_up_ssd[l], w_up_nsa[l], w_out[l])
        h = memory_xattn(h, mem, norm2_w[l], mem_norm_w[l], xq_w[l], xkv_w[l], x_q_norm_w[l], x_k_norm_w[l], xo_w[l])
        h = hier_moe(h, norm3_w[l], router_g_w[l], router_g_b[l], router_e_w[l], router_e_b[l],
                     moe_w_gate[l], moe_w_up[l], moe_w_down[l])
    return h.astype(x.dtype)
```

```python
import functools

import numpy as np
import jax
import jax.numpy as jnp
from jax import lax
from jax.experimental import pallas as pl
from jax.experimental.pallas import tpu as pltpu

F32 = jnp.float32
BF16 = jnp.bfloat16

D_MODEL = 2048
SSD_INNER = 4096
SSD_HEAD_DIM = 64
SSD_HEADS = 64
SSD_GROUPS = 8
SSD_STATE = 128
SSD_CONV = 4
SSD_CHUNK = 256
HEADS_PER_GROUP = SSD_HEADS // SSD_GROUPS
GROUP_WIDTH = SSD_INNER // SSD_GROUPS
HEAD_DIM = 128
NSA_HEADS = 16
NSA_KV_HEADS = 4
NSA_REP = NSA_HEADS // NSA_KV_HEADS
NSA_WIDTH = NSA_HEADS * HEAD_DIM
NSA_KV_WIDTH = NSA_KV_HEADS * HEAD_DIM
CMP_BLK = 32
CMP_STRIDE = 16
CMP_HID = 256
SEL_BLK = 64
SEL_SHIFT = 6
N_SEL = 16
WINDOW = 512
Q_BLK = 128
SEL_KEY_TILE = 256
FORCE_SCORE = 1.0e4
X_HEADS = 4
X_WIDTH = X_HEADS * HEAD_DIM
N_GROUPS = 8
EXPERTS_PER_GROUP = 8
N_EXPERTS = 64
TOP_K = 2
EXPERT_HIDDEN = 1408
MOE_ROWS = 256
ROPE_THETA = 10000.0
EPS = 1e-6
NEG_INF = -1e30
LANES = 128
VMEM_LIMIT = 56 * 1024 * 1024

ZX_COLS = SSD_INNER + SSD_INNER + 2 * SSD_GROUPS * SSD_STATE
QKV_COLS = NSA_WIDTH + 6 * NSA_KV_WIDTH
GATE_COLS = 2 * D_MODEL
SMALL_COLS = LANES


def _params(*sem):
    return pltpu.CompilerParams(dimension_semantics=sem, vmem_limit_bytes=VMEM_LIMIT)


def _rms(x, w):
    ms = jnp.mean(x * x, axis=-1, keepdims=True)
    return x * lax.rsqrt(ms + EPS) * w


def _sigmoid(x):
    return 1.0 / (1.0 + jnp.exp(-x))


def _silu(x):
    return x * _sigmoid(x)


def _softplus(x):
    return jnp.maximum(x, 0.0) + jnp.log1p(jnp.exp(-jnp.abs(x)))


def _dot(a, b):
    return jnp.dot(a, b, preferred_element_type=F32)


def _dot_nt(a, b):
    return lax.dot_general(a, b, (((1,), (1,)), ((), ())), preferred_element_type=F32)


def _masked_softmax(s, mask):
    s = jnp.where(mask, s, NEG_INF)
    e = jnp.exp(s - jnp.max(s, axis=-1, keepdims=True))
    p = e / jnp.sum(e, axis=-1, keepdims=True)
    return jnp.where(mask, p, 0.0)


def _rmsnorm_kernel(x_ref, w_ref, o_ref):
    o_ref[...] = _rms(x_ref[...], w_ref[...]).astype(o_ref.dtype)


def rmsnorm_rows(x, w, tm=512):
    m, d = x.shape
    return pl.pallas_call(
        _rmsnorm_kernel,
        grid=(m // tm,),
        in_specs=[pl.BlockSpec((tm, d), lambda i: (i, 0)), pl.BlockSpec((1, d), lambda i: (0, 0))],
        out_specs=pl.BlockSpec((tm, d), lambda i: (i, 0)),
        out_shape=jax.ShapeDtypeStruct((m, d), BF16),
        compiler_params=_params("parallel"),
        name="rmsnorm_rows",
    )(x, w.reshape(1, d))


def _matmul_kernel(x_ref, w_ref, o_ref):
    o_ref[...] = _dot(x_ref[...], w_ref[...]).astype(o_ref.dtype)


def matmul(x, w, out_dtype=F32, tm=512, tn=512):
    m, k = x.shape
    n = w.shape[1]
    tm, tn = min(tm, m), min(tn, n)
    return pl.pallas_call(
        _matmul_kernel,
        grid=(m // tm, n // tn),
        in_specs=[pl.BlockSpec((tm, k), lambda i, j: (i, 0)), pl.BlockSpec((k, tn), lambda i, j: (0, j))],
        out_specs=pl.BlockSpec((tm, tn), lambda i, j: (i, j)),
        out_shape=jax.ShapeDtypeStruct((m, n), out_dtype),
        compiler_params=_params("parallel", "arbitrary"),
        name="matmul",
    )(x, w)


def _ssd_kernel(z_ref, x_ref, b_ref, c_ref, dt_ref, dtt_ref, cwx_ref, cwb_ref, cwc_ref, cbx_ref, cbb_ref, cbc_ref,
                hp_ref, hpt_ref, nw_ref, y_ref, state_ref, halo_x_ref, halo_b_ref, halo_c_ref):
    L = x_ref.shape[0]
    chunk = pl.program_id(2)

    @pl.when(chunk == 0)
    def _():
        state_ref[...] = jnp.zeros_like(state_ref)
        halo_x_ref[...] = jnp.zeros_like(halo_x_ref)
        halo_b_ref[...] = jnp.zeros_like(halo_b_ref)
        halo_c_ref[...] = jnp.zeros_like(halo_c_ref)

    def conv_silu(u, halo_ref, w, b):
        ext = jnp.concatenate([halo_ref[...], u], axis=0)
        acc = None
        for k in range(SSD_CONV):
            shifted = pltpu.roll(ext, SSD_CONV - 1 - k, 0)[8:8 + L] if k < SSD_CONV - 1 else u
            term = w[k:k + 1, :] * shifted
            acc = term if acc is None else acc + term
        halo_ref[...] = u[L - 8:, :]
        return _silu(acc + b)

    xs = conv_silu(x_ref[...], halo_x_ref, cwx_ref[...], cbx_ref[...])
    bm = conv_silu(b_ref[...], halo_b_ref, cwb_ref[...], cbb_ref[...])
    cm = conv_silu(c_ref[...], halo_c_ref, cwc_ref[...], cbc_ref[...])
    bm16, cm16 = bm.astype(BF16), cm.astype(BF16)

    hp = hp_ref[0]
    hpt = hpt_ref[0]
    dt = _softplus(dt_ref[0] + hp[0:1, :])
    dtt = _softplus(dtt_ref[0] + hpt[:, 0:1])
    da = dt * (-jnp.exp(hp[1:2, :]))
    dat = dtt * (-jnp.exp(hpt[:, 1:2]))
    row = lax.broadcasted_iota(jnp.int32, (L, L), 0)
    col = lax.broadcasted_iota(jnp.int32, (L, L), 1)
    tri = row >= col
    hi = lax.Precision.HIGHEST
    acum = jnp.dot(jnp.where(tri, 1.0, 0.0), da, precision=hi, preferred_element_type=F32)
    acumt = jnp.dot(dat, jnp.where(tri, 0.0, 1.0) + jnp.where(row == col, 1.0, 0.0),
                    precision=hi, preferred_element_type=F32)
    last = acum[L - 1:L, :]
    dte = jnp.exp(last - acum)
    eac = jnp.exp(acum)
    cdec = jnp.exp(last)

    cb = _dot_nt(cm16, bm16)
    lane = lax.broadcasted_iota(jnp.int32, (L, LANES), 1)
    first = lane < SSD_HEAD_DIM
    srow = lax.broadcasted_iota(jnp.int32, (LANES, SSD_STATE), 0) < SSD_HEAD_DIM
    z = z_ref[...]
    d_skip = hp[2:3, :]
    outs = []
    for p in range(HEADS_PER_GROUP // 2):
        h0, h1 = 2 * p, 2 * p + 1

        def pick(a, rows=first):
            return jnp.where(rows, a[:, h0:h0 + 1], a[:, h1:h1 + 1])

        xp = xs[:, p * LANES:(p + 1) * LANES]
        xdt = xp * pick(dt)
        xdt16 = xdt.astype(BF16)
        y = jnp.zeros((L, LANES), F32)
        for h, msk in ((h0, first), (h1, jnp.logical_not(first))):
            diff = acum[:, h:h + 1] - acumt[h:h + 1, :]
            dec = jnp.exp(jnp.where(tri, diff, -jnp.inf))
            y = y + _dot((cb * dec).astype(BF16), jnp.where(msk, xdt16, jnp.zeros_like(xdt16)))
        hprev = state_ref[p]
        y = y + _dot_nt(cm16, hprev.astype(BF16)) * pick(eac)
        xe_t = (xdt * pick(dte)).T.astype(BF16)
        state_ref[p] = hprev * jnp.where(srow, cdec[:, h0:h0 + 1], cdec[:, h1:h1 + 1]) + _dot(xe_t, bm16)
        outs.append(y + pick(d_skip, first[0:1, :]) * xp)
    y = jnp.concatenate(outs, axis=1) * _silu(z)
    y_ref[...] = _rms(y, nw_ref[...]).astype(y_ref.dtype)


def ssd_branch(zx, small, conv_w, conv_b, dt_bias, a_log, d_skip, norm_w, batch, seq):
    n = batch * seq
    L = min(SSD_CHUNK, seq)
    nc = seq // L
    g, hpg = SSD_GROUPS, HEADS_PER_GROUP
    dt_raw = small[:, :SSD_HEADS].reshape(n, g, hpg)
    dt_g = dt_raw.transpose(1, 0, 2)
    dtt_g = dt_raw.transpose(1, 2, 0)
    hp = jnp.stack([dt_bias, a_log, d_skip], axis=0).reshape(3, g, hpg).transpose(1, 0, 2)
    hp = jnp.pad(hp, ((0, 0), (0, 8 - 3), (0, 0)))
    hpt = hp.transpose(0, 2, 1)
    cw = conv_w
    cbias = conv_b.reshape(1, -1)
    nxb = SSD_INNER // GROUP_WIDTH
    rowblk = lambda b, gi, c: b * nc + c
    bc0 = SSD_INNER // SSD_STATE
    grid = (batch, g, nc)
    in_specs = [
        pl.BlockSpec((L, GROUP_WIDTH), lambda b, gi, c: (rowblk(b, gi, c), gi)),
        pl.BlockSpec((L, GROUP_WIDTH), lambda b, gi, c: (rowblk(b, gi, c), nxb + gi)),
        pl.BlockSpec((L, SSD_STATE), lambda b, gi, c: (rowblk(b, gi, c), 2 * bc0 + gi)),
        pl.BlockSpec((L, SSD_STATE), lambda b, gi, c: (rowblk(b, gi, c), 2 * bc0 + g + gi)),
        pl.BlockSpec((1, L, hpg), lambda b, gi, c: (gi, rowblk(b, gi, c), 0)),
        pl.BlockSpec((1, hpg, L), lambda b, gi, c: (gi, 0, rowblk(b, gi, c))),
        pl.BlockSpec((SSD_CONV, GROUP_WIDTH), lambda b, gi, c: (0, gi)),
        pl.BlockSpec((SSD_CONV, SSD_STATE), lambda b, gi, c: (0, bc0 + gi)),
        pl.BlockSpec((SSD_CONV, SSD_STATE), lambda b, gi, c: (0, bc0 + g + gi)),
        pl.BlockSpec((1, GROUP_WIDTH), lambda b, gi, c: (0, gi)),
        pl.BlockSpec((1, SSD_STATE), lambda b, gi, c: (0, bc0 + gi)),
        pl.BlockSpec((1, SSD_STATE), lambda b, gi, c: (0, bc0 + g + gi)),
        pl.BlockSpec((1, 8, hpg), lambda b, gi, c: (gi, 0, 0)),
        pl.BlockSpec((1, hpg, 8), lambda b, gi, c: (gi, 0, 0)),
        pl.BlockSpec((1, GROUP_WIDTH), lambda b, gi, c: (0, gi)),
    ]
    return pl.pallas_call(
        _ssd_kernel,
        grid=grid,
        in_specs=in_specs,
        out_specs=pl.BlockSpec((L, GROUP_WIDTH), lambda b, gi, c: (rowblk(b, gi, c), gi)),
        out_shape=jax.ShapeDtypeStruct((n, SSD_INNER), BF16),
        scratch_shapes=[pltpu.VMEM((HEADS_PER_GROUP // 2, LANES, SSD_STATE), F32),
                        pltpu.VMEM((8, GROUP_WIDTH), F32), pltpu.VMEM((8, SSD_STATE), F32),
                        pltpu.VMEM((8, SSD_STATE), F32)],
        compiler_params=_params("parallel", "parallel", "arbitrary"),
        name="ssd_branch",
    )(zx, zx, zx, zx, dt_g, dtt_g, cw, cw, cw, cbias, cbias, cbias, hp, hpt, norm_w.reshape(1, -1))


def _nsa_prep_kernel(q_ref, kc_ref, vc_ref, ks_ref, vs_ref, kw_ref, vw_ref, cos_ref, sin_ref, qw_ref, kw3_ref,
                     qo_ref, kco_ref, vco_ref, kso_ref, vso_ref, kwo_ref, vwo_ref):
    cos, sin = cos_ref[...], sin_ref[...]

    def norm_rope(x, w):
        y = _rms(x, w)
        return y * cos + pltpu.roll(y, HEAD_DIM // 2, 1) * sin

    qw = qw_ref[...]
    for g in range(NSA_KV_HEADS):
        sl = slice(g * HEAD_DIM, (g + 1) * HEAD_DIM)
        for r in range(NSA_REP):
            h = g * NSA_REP + r
            qo_ref[0, g, r] = norm_rope(q_ref[:, h * HEAD_DIM:(h + 1) * HEAD_DIM], qw).astype(qo_ref.dtype)
        kco_ref[0, g] = norm_rope(kc_ref[:, sl], kw3_ref[0:1, :])
        vco_ref[0, g] = vc_ref[:, sl]
        kso_ref[0, g] = norm_rope(ks_ref[:, sl], kw3_ref[1:2, :]).astype(kso_ref.dtype)
        vso_ref[0, g] = vs_ref[:, sl].astype(vso_ref.dtype)
        kwo_ref[0, g] = norm_rope(kw_ref[:, sl], kw3_ref[2:3, :]).astype(kwo_ref.dtype)
        vwo_ref[0, g] = vw_ref[:, sl].astype(vwo_ref.dtype)


def nsa_prep(qkv, q_norm_w, k_norm_w, batch, seq, tt=256):
    half = HEAD_DIM // 2
    inv = ROPE_THETA ** (-jnp.arange(half, dtype=F32) / half)
    ang = jnp.arange(seq).astype(F32)[:, None] * inv[None, :]
    cos = jnp.concatenate([jnp.cos(ang), jnp.cos(ang)], axis=-1)
    sin = jnp.concatenate([-jnp.sin(ang), jnp.sin(ang)], axis=-1)
    nt = seq // tt
    kvb = NSA_WIDTH // NSA_KV_WIDTH
    kv_spec = lambda j: pl.BlockSpec((tt, NSA_KV_WIDTH), lambda b, t: (b * nt + t, kvb + j))
    head_spec = pl.BlockSpec((1, NSA_KV_HEADS, tt, HEAD_DIM), lambda b, t: (b, 0, t, 0))
    head_shape = lambda dt: jax.ShapeDtypeStruct((batch, NSA_KV_HEADS, seq, HEAD_DIM), dt)
    return pl.pallas_call(
        _nsa_prep_kernel,
        grid=(batch, nt),
        in_specs=[pl.BlockSpec((tt, NSA_WIDTH), lambda b, t: (b * nt + t, 0))] + [kv_spec(j) for j in range(6)] + [
            pl.BlockSpec((tt, HEAD_DIM), lambda b, t: (t, 0)), pl.BlockSpec((tt, HEAD_DIM), lambda b, t: (t, 0)),
            pl.BlockSpec((1, HEAD_DIM), lambda b, t: (0, 0)), pl.BlockSpec((3, HEAD_DIM), lambda b, t: (0, 0))],
        out_specs=[pl.BlockSpec((1, NSA_KV_HEADS, NSA_REP, tt, HEAD_DIM), lambda b, t: (b, 0, 0, t, 0)),
                   head_spec, head_spec, head_spec, head_spec, head_spec, head_spec],
        out_shape=[jax.ShapeDtypeStruct((batch, NSA_KV_HEADS, NSA_REP, seq, HEAD_DIM), BF16),
                   head_shape(F32), head_shape(F32), head_shape(BF16), head_shape(BF16), head_shape(BF16),
                   head_shape(BF16)],
        compiler_params=_params("parallel", "parallel"),
        name="nsa_prep",
    )(qkv, qkv, qkv, qkv, qkv, qkv, qkv, cos, sin, q_norm_w.reshape(1, -1), k_norm_w)


def _compress_kernel(s_ref, pea_ref, peb_ref, w1a_ref, w1b_ref, w2_ref, o_ref, *, ncmp):
    s = s_ref[0, 0]
    u = _dot((s + pea_ref[...]).astype(BF16), w1a_ref[...])
    v = _dot((s + peb_ref[...]).astype(BF16), w1b_ref[...])
    nseg = s.shape[0]
    hid = _silu(u + pltpu.roll(v, nseg - 1, 0))
    comp = _dot(hid.astype(BF16), w2_ref[...])
    rowi = lax.broadcasted_iota(jnp.int32, comp.shape, 0)
    o_ref[0, 0] = jnp.where(rowi < ncmp, comp, 0.0).astype(o_ref.dtype)


def compress(raw, pe, w1, w2):
    b, g, t, hd = raw.shape
    nseg = t // CMP_STRIDE
    segw = CMP_STRIDE * hd
    segs = raw.reshape(b, g, nseg, segw)
    pe_a, pe_b = pe[:CMP_STRIDE].reshape(1, segw), pe[CMP_STRIDE:].reshape(1, segw)
    w1a = w1[:CMP_STRIDE].reshape(segw, CMP_HID).astype(BF16)
    w1b = w1[CMP_STRIDE:].reshape(segw, CMP_HID).astype(BF16)
    full = lambda shape: pl.BlockSpec(shape, lambda i, j: (0,) * len(shape))
    return pl.pallas_call(
        functools.partial(_compress_kernel, ncmp=nseg - 1),
        grid=(b, g),
        in_specs=[pl.BlockSpec((1, 1, nseg, segw), lambda i, j: (i, j, 0, 0)), full((1, segw)), full((1, segw)),
                  full((segw, CMP_HID)), full((segw, CMP_HID)), full((CMP_HID, hd))],
        out_specs=pl.BlockSpec((1, 1, nseg, hd), lambda i, j: (i, j, 0, 0)),
        out_shape=jax.ShapeDtypeStruct((b, g, nseg, hd), BF16),
        compiler_params=_params("parallel", "parallel"),
        name="nsa_compress",
    )(segs, pe_a, pe_b, w1a, w1b, w2.astype(BF16))


def _nsa_attn_kernel(q_ref, kc_ref, vc_ref, ks_ref, vs_ref, kw_ref, vw_ref, gl_ref, ovt_ref, ex_ref, o_ref, sel_ref,
                     *, seq, n_sel):
    qb = pl.program_id(2)
    q0 = qb * Q_BLK
    rows = NSA_REP * Q_BLK
    scale = HEAD_DIM ** -0.5
    q = q_ref[0, 0].reshape(rows, HEAD_DIM)
    tq = q0 + (lax.broadcasted_iota(jnp.int32, (rows, 1), 0) & (Q_BLK - 1))

    ncp = kc_ref.shape[2]
    s_c = _dot_nt(q, kc_ref[0, 0]) * scale
    cend = lax.broadcasted_iota(jnp.int32, (1, ncp), 1) * CMP_STRIDE + (CMP_BLK - 1)
    p_c = _masked_softmax(s_c, cend <= tq).astype(BF16)
    o_c = _dot(p_c, vc_ref[0, 0])

    nsb = ovt_ref.shape[0]
    imp4 = _dot_nt(ovt_ref[...], p_c)
    imp = imp4[:, 0:Q_BLK]
    for r in range(1, NSA_REP):
        imp = imp + imp4[:, r * Q_BLK:(r + 1) * Q_BLK]
    blk_t = (q0 + lax.broadcasted_iota(jnp.int32, (nsb, Q_BLK), 1)) >> SEL_SHIFT
    sb = lax.broadcasted_iota(jnp.int32, (nsb, Q_BLK), 0)
    forced = (sb == 0) | (sb == blk_t) | (sb == blk_t - 1)
    imp = jnp.where(sb <= blk_t, jnp.where(forced, FORCE_SCORE, imp), -jnp.inf)
    rank = jnp.zeros((nsb, Q_BLK), F32)
    for i in range(nsb):
        ri = imp[i:i + 1, :]
        beats = (ri > imp) | ((ri == imp) & (sb > i))
        rank = rank + jnp.where(beats, 1.0, 0.0)
    sel = jnp.where(rank < n_sel, 1.0, 0.0).T.astype(BF16)
    kt_sz = sel_ref.shape[2]
    for kt in range(seq // kt_sz):
        sel_ref[kt] = _dot(sel, ex_ref[:, kt * kt_sz:(kt + 1) * kt_sz])

    def sel_tile(kt, carry):
        m, l, acc = carry
        k0 = pl.multiple_of(kt * kt_sz, kt_sz)
        s = _dot_nt(q, ks_ref[0, 0, pl.ds(k0, kt_sz), :]) * scale
        kpos = k0 + lax.broadcasted_iota(jnp.int32, (1, kt_sz), 1)
        chosen = jnp.concatenate([sel_ref[kt]] * NSA_REP, axis=0) > 0.5
        mask = chosen & (kpos <= tq)
        s = jnp.where(mask, s, NEG_INF)
        m_new = jnp.maximum(m, jnp.max(s, axis=-1, keepdims=True))
        alpha = jnp.exp(m - m_new)
        p = jnp.where(mask, jnp.exp(s - m_new), 0.0)
        l = alpha * l + jnp.sum(p, axis=-1, keepdims=True)
        acc = alpha * acc + _dot(p.astype(BF16), vs_ref[0, 0, pl.ds(k0, kt_sz), :])
        return m_new, l, acc

    n_tiles = (q0 + Q_BLK + kt_sz - 1) // kt_sz
    init = (jnp.full((rows, 1), NEG_INF, F32), jnp.zeros((rows, 1), F32), jnp.zeros((rows, HEAD_DIM), F32))
    _, l_s, acc_s = lax.fori_loop(0, n_tiles, sel_tile, init)
    o_s = acc_s / l_s

    wk = min(WINDOW + Q_BLK, seq)
    start = pl.multiple_of(jnp.clip(q0 - WINDOW, 0, seq - wk), Q_BLK)
    s_w = _dot_nt(q, kw_ref[0, 0, pl.ds(start, wk), :]) * scale
    dist = tq - (start + lax.broadcasted_iota(jnp.int32, (1, wk), 1))
    p_w = _masked_softmax(s_w, (dist >= 0) & (dist < WINDOW)).astype(BF16)
    o_w = _dot(p_w, vw_ref[0, 0, pl.ds(start, wk), :])

    gate = _sigmoid(gl_ref[0, 0])
    for r in range(NSA_REP):
        rs = slice(r * Q_BLK, (r + 1) * Q_BLK)
        out = (gate[:, r:r + 1] * o_c[rs] + gate[:, NSA_REP + r:NSA_REP + r + 1] * o_s[rs]
               + gate[:, 2 * NSA_REP + r:2 * NSA_REP + r + 1] * o_w[rs])
        o_ref[:, r * HEAD_DIM:(r + 1) * HEAD_DIM] = out.astype(o_ref.dtype)


def nsa_attention(qh, kcc, vcc, ksn, vsn, kwn, vwn, gate_logits, batch, seq):
    nq = seq // Q_BLK
    nsb = seq // SEL_BLK
    n_sel = min(N_SEL, nsb)
    ncp = kcc.shape[2]
    ncmp = (seq - CMP_BLK) // CMP_STRIDE + 1
    kt_sz = min(SEL_KEY_TILE, seq)
    ci = np.arange(ncp)[None, :]
    sj = np.arange(nsb)[:, None]
    ovt = ((ci * CMP_STRIDE < (sj + 1) * SEL_BLK) & (ci * CMP_STRIDE + CMP_BLK > sj * SEL_BLK) & (ci < ncmp))
    ovt = jnp.asarray(ovt, BF16)
    ex = jnp.asarray(np.arange(seq)[None, :] // SEL_BLK == sj, BF16)
    seq_spec = pl.BlockSpec((1, 1, seq, HEAD_DIM), lambda b, g, i: (b, g, 0, 0))
    cmp_spec = pl.BlockSpec((1, 1, ncp, HEAD_DIM), lambda b, g, i: (b, g, 0, 0))
    return pl.pallas_call(
        functools.partial(_nsa_attn_kernel, seq=seq, n_sel=n_sel),
        grid=(batch, NSA_KV_HEADS, nq),
        in_specs=[pl.BlockSpec((1, 1, NSA_REP, Q_BLK, HEAD_DIM), lambda b, g, i: (b, g, 0, i, 0)),
                  cmp_spec, cmp_spec, seq_spec, seq_spec, seq_spec, seq_spec,
                  pl.BlockSpec((1, 1, Q_BLK, 3 * NSA_REP), lambda b, g, i: (b, g, i, 0)),
                  pl.BlockSpec((nsb, ncp), lambda b, g, i: (0, 0)),
                  pl.BlockSpec((nsb, seq), lambda b, g, i: (0, 0))],
        out_specs=pl.BlockSpec((Q_BLK, NSA_REP * HEAD_DIM), lambda b, g, i: (b * nq + i, g)),
        out_shape=jax.ShapeDtypeStruct((batch * seq, NSA_WIDTH), BF16),
        scratch_shapes=[pltpu.VMEM((seq // kt_sz, Q_BLK, kt_sz), F32)],
        compiler_params=_params("parallel", "parallel", "arbitrary"),
        name="nsa_attention",
    )(qh, kcc, vcc, ksn, vsn, kwn, vwn, gate_logits, ovt, ex)


def _merge_kernel(ys_ref, yn_ref, ws_ref, wn_ref, gs_ref, gn_ref, o_ref):
    up_s = _dot(ys_ref[...], ws_ref[...])
    up_n = _dot(yn_ref[...], wn_ref[...])
    o_ref[...] = (_sigmoid(gs_ref[...]) * up_s + _sigmoid(gn_ref[...]) * up_n).astype(o_ref.dtype)


def merge_mixers(y_ssd, y_nsa, w_up_ssd, w_up_nsa, gates, tm=512, tn=512):
    m = y_ssd.shape[0]
    nb = D_MODEL // tn
    return pl.pallas_call(
        _merge_kernel,
        grid=(m // tm, nb),
        in_specs=[pl.BlockSpec((tm, SSD_INNER), lambda i, j: (i, 0)), pl.BlockSpec((tm, NSA_WIDTH), lambda i, j: (i, 0)),
                  pl.BlockSpec((SSD_INNER, tn), lambda i, j: (0, j)), pl.BlockSpec((NSA_WIDTH, tn), lambda i, j: (0, j)),
                  pl.BlockSpec((tm, tn), lambda i, j: (i, j)), pl.BlockSpec((tm, tn), lambda i, j: (i, nb + j))],
        out_specs=pl.BlockSpec((tm, tn), lambda i, j: (i, j)),
        out_shape=jax.ShapeDtypeStruct((m, D_MODEL), BF16),
        compiler_params=_params("parallel", "arbitrary"),
        name="merge_mixers",
    )(y_ssd, y_nsa, w_up_ssd, w_up_nsa, gates, gates)


def _residual_matmul_kernel(h_ref, x_ref, w_ref, o_ref):
    o_ref[...] = h_ref[...] + _dot(x_ref[...], w_ref[...])


def residual_matmul(h, x, w, tm=512, tn=512):
    m, k = x.shape
    n = w.shape[1]
    return pl.pallas_call(
        _residual_matmul_kernel,
        grid=(m // tm, n // tn),
        in_specs=[pl.BlockSpec((tm, tn), lambda i, j: (i, j)), pl.BlockSpec((tm, k), lambda i, j: (i, 0)),
                  pl.BlockSpec((k, tn), lambda i, j: (0, j))],
        out_specs=pl.BlockSpec((tm, tn), lambda i, j: (i, j)),
        out_shape=jax.ShapeDtypeStruct((m, n), F32),
        compiler_params=_params("parallel", "arbitrary"),
        name="residual_matmul",
    )(h, x, w)


def _xattn_router_kernel(h_ref, kv_ref, n2_ref, wq_ref, qn_ref, kn_ref, wo_ref, n3_ref, rw_ref, rb_ref,
                         h2_ref, hf_ref, lg_ref):
    h = h_ref[...]
    qp = _dot(_rms(h, n2_ref[...]).astype(BF16), wq_ref[...])
    kv = kv_ref[...]
    scale = HEAD_DIM ** -0.5
    heads = []
    for hd in range(X_HEADS):
        sl = slice(hd * HEAD_DIM, (hd + 1) * HEAD_DIM)
        qh = _rms(qp[:, sl], qn_ref[...]).astype(BF16)
        kh = _rms(kv[:, sl], kn_ref[...]).astype(BF16)
        vh = kv[:, X_WIDTH + hd * HEAD_DIM:X_WIDTH + (hd + 1) * HEAD_DIM].astype(BF16)
        s = _dot_nt(qh, kh) * scale
        e = jnp.exp(s - jnp.max(s, axis=-1, keepdims=True))
        p = e / jnp.sum(e, axis=-1, keepdims=True)
        heads.append(_dot(p.astype(BF16), vh))
    o = jnp.concatenate(heads, axis=1).astype(BF16)
    h2 = h + _dot(o, wo_ref[...])
    h2_ref[...] = h2
    hf = _rms(h2, n3_ref[...]).astype(BF16)
    hf_ref[...] = hf
    lg_ref[...] = _dot(hf, rw_ref[...]) + rb_ref[...]


def xattn_router(h1, kv, norm2_w, wq, q_norm_w, k_norm_w, wo, norm3_w, router_w, router_b, batch, seq, tm=512):
    n = batch * seq
    tm = min(tm, seq)
    per_seq = seq // tm
    mlen = kv.shape[0] // batch
    full = lambda shape: pl.BlockSpec(shape, lambda i: (0,) * len(shape))
    return pl.pallas_call(
        _xattn_router_kernel,
        grid=(n // tm,),
        in_specs=[pl.BlockSpec((tm, D_MODEL), lambda i: (i, 0)),
                  pl.BlockSpec((mlen, 2 * X_WIDTH), lambda i: (i // per_seq, 0)),
                  full((1, D_MODEL)), full((D_MODEL, X_WIDTH)), full((1, HEAD_DIM)), full((1, HEAD_DIM)),
                  full((X_WIDTH, D_MODEL)), full((1, D_MODEL)), full((D_MODEL, LANES)), full((1, LANES))],
        out_specs=[pl.BlockSpec((tm, D_MODEL), lambda i: (i, 0)), pl.BlockSpec((tm, D_MODEL), lambda i: (i, 0)),
                   pl.BlockSpec((tm, LANES), lambda i: (i, 0))],
        out_shape=[jax.ShapeDtypeStruct((n, D_MODEL), F32), jax.ShapeDtypeStruct((n, D_MODEL), BF16),
                   jax.ShapeDtypeStruct((n, LANES), F32)],
        compiler_params=_params("parallel"),
        name="xattn_router",
    )(h1, kv, norm2_w.reshape(1, -1), wq, q_norm_w.reshape(1, -1), k_norm_w.reshape(1, -1), wo,
      norm3_w.reshape(1, -1), router_w, router_b)


def _moe_ffn_kernel(blk_e_ref, n_used_ref, x_ref, wg_ref, wu_ref, wd_ref, wt_ref, y_ref):
    i = pl.program_id(0)

    @pl.when(i < n_used_ref[0])
    def _():
        x = x_ref[...]
        act = _silu(_dot(x, wg_ref[0])) * _dot(x, wu_ref[0])
        y_ref[...] = _dot(act.astype(BF16), wd_ref[0]) * wt_ref[...]

    @pl.when(i >= n_used_ref[0])
    def _():
        y_ref[...] = jnp.zeros_like(y_ref)


def moe_ffn(x_buf, w_buf, blk_e, n_used, w_gate, w_up, w_down, rows_per_blk):
    rows, d = x_buf.shape
    n_blk = rows // rows_per_blk
    hid = w_gate.shape[2]
    grid_spec = pltpu.PrefetchScalarGridSpec(
        num_scalar_prefetch=2,
        grid=(n_blk,),
        in_specs=[pl.BlockSpec((rows_per_blk, d), lambda i, be, nu: (i, 0)),
                  pl.BlockSpec((1, d, hid), lambda i, be, nu: (be[i], 0, 0)),
                  pl.BlockSpec((1, d, hid), lambda i, be, nu: (be[i], 0, 0)),
                  pl.BlockSpec((1, hid, d), lambda i, be, nu: (be[i], 0, 0)),
                  pl.BlockSpec((rows_per_blk, 1), lambda i, be, nu: (i, 0))],
        out_specs=pl.BlockSpec((rows_per_blk, d), lambda i, be, nu: (i, 0)),
    )
    return pl.pallas_call(
        _moe_ffn_kernel,
        grid_spec=grid_spec,
        out_shape=jax.ShapeDtypeStruct((rows, d), F32),
        compiler_params=_params("arbitrary"),
        name="moe_ffn",
    )(blk_e, n_used, x_buf, w_gate, w_up, w_down, w_buf)


def hier_moe(h2, hf, logits, w_gate, w_up, w_down, rows_per_blk=MOE_ROWS):
    n, d = h2.shape
    n_exp = w_gate.shape[0]
    pg = jax.nn.softmax(logits[:, :N_GROUPS], axis=-1)
    grp = jnp.argmax(pg, axis=-1)
    pg_top = jnp.take_along_axis(pg, grp[:, None], axis=-1)
    le = logits[:, N_GROUPS:N_GROUPS + n_exp].reshape(n, N_GROUPS, EXPERTS_PER_GROUP)
    le = jnp.take_along_axis(le, grp[:, None, None], axis=1)[:, 0]
    top_p, top_i = lax.top_k(jax.nn.softmax(le, axis=-1), TOP_K)
    wts = (pg_top * top_p / jnp.sum(top_p, axis=-1, keepdims=True)).reshape(-1)
    eid = (grp[:, None] * EXPERTS_PER_GROUP + top_i).reshape(-1).astype(jnp.int32)
    tok = jnp.repeat(jnp.arange(n, dtype=jnp.int32), TOP_K)
    n_assign = n * TOP_K
    order = jnp.argsort(eid)
    e_sorted = eid[order]
    counts = jnp.bincount(eid, length=n_exp)
    start = jnp.cumsum(counts) - counts
    padded = (counts + rows_per_blk - 1) // rows_per_blk * rows_per_blk
    pend = jnp.cumsum(padded)
    pstart = pend - padded
    dest = (pstart[e_sorted] + jnp.arange(n_assign) - start[e_sorted]).astype(jnp.int32)
    n_blk = (n_assign + n_exp * (rows_per_blk - 1) + rows_per_blk - 1) // rows_per_blk
    rows = n_blk * rows_per_blk
    tok_buf = jnp.full((rows,), n, jnp.int32).at[dest].set(tok[order])
    w_buf = jnp.zeros((rows,), F32).at[dest].set(wts[order])
    blk_e = jnp.minimum(jnp.searchsorted(pend, jnp.arange(n_blk) * rows_per_blk, side='right'), n_exp - 1)
    n_used = (pend[-1] // rows_per_blk).astype(jnp.int32).reshape(1)
    x_buf = jnp.concatenate([hf, jnp.zeros((1, d), hf.dtype)], axis=0)[tok_buf]
    y_buf = moe_ffn(x_buf, w_buf.reshape(rows, 1), blk_e.astype(jnp.int32), n_used, w_gate, w_up, w_down,
                    rows_per_blk)
    slot = jnp.zeros((n_assign,), jnp.int32).at[order].set(dest).reshape(n, TOP_K)
    return h2 + y_buf[slot[:, 0]] + y_buf[slot[:, 1]]


def _pack_w_in(w_in):
    offs = np.cumsum([0, SSD_INNER, SSD_INNER + 2 * SSD_GROUPS * SSD_STATE, SSD_HEADS, NSA_WIDTH] + [NSA_KV_WIDTH] * 6
                     + [3 * NSA_HEADS, D_MODEL, D_MODEL])
    piece = lambda i: w_in[:, offs[i]:offs[i + 1]]
    w_zx = jnp.concatenate([piece(0), piece(1)], axis=1)
    w_qkv = jnp.concatenate([piece(i) for i in range(3, 10)], axis=1)
    w_gate = jnp.concatenate([piece(11), piece(12)], axis=1)
    pad = jnp.zeros((w_in.shape[0], SMALL_COLS - SSD_HEADS - 3 * NSA_HEADS), w_in.dtype)
    w_small = jnp.concatenate([piece(2), piece(10), pad], axis=1)
    return [w.astype(BF16) for w in (w_zx, w_qkv, w_gate, w_small)]


def _layer(x, mem, norm1_w, w_in, ssd_conv_w, ssd_conv_b, ssd_dt_bias, ssd_a_log, ssd_d, ssd_norm_w,
           nsa_q_norm_w, nsa_k_norm_w, cmp_pe_k, cmp_w1_k, cmp_w2_k, cmp_pe_v, cmp_w1_v, cmp_w2_v,
           w_up_ssd, w_up_nsa, w_out, norm2_w, mem_norm_w, xq_w, xkv_w, x_q_norm_w, x_k_norm_w, xo_w,
           norm3_w, router_g_w, router_g_b, router_e_w, router_e_b, moe_w_gate, moe_w_up, moe_w_down):
    batch, seq, d = x.shape
    n = batch * seq
    xf = x.reshape(n, d)
    hn = rmsnorm_rows(xf, norm1_w)
    w_zx, w_qkv, w_mg, w_small = _pack_w_in(w_in)
    zx = matmul(hn, w_zx)
    qkv = matmul(hn, w_qkv)
    mgates = matmul(hn, w_mg)
    small = matmul(hn, w_small)
    y_ssd = ssd_branch(zx, small, ssd_conv_w, ssd_conv_b, ssd_dt_bias, ssd_a_log, ssd_d, ssd_norm_w, batch, seq)
    qh, kcn, vcr, ksn, vsn, kwn, vwn = nsa_prep(qkv, nsa_q_norm_w, nsa_k_norm_w, batch, seq)
    kcc = compress(kcn, cmp_pe_k, cmp_w1_k, cmp_w2_k)
    vcc = compress(vcr, cmp_pe_v, cmp_w1_v, cmp_w2_v)
    gl = small[:, SSD_HEADS:SSD_HEADS + 3 * NSA_HEADS].reshape(batch, seq, 3, NSA_KV_HEADS, NSA_REP)
    gl = gl.transpose(0, 3, 1, 2, 4).reshape(batch, NSA_KV_HEADS, seq, 3 * NSA_REP)
    y_nsa = nsa_attention(qh, kcc, vcc, ksn, vsn, kwn, vwn, gl, batch, seq)
    merged = merge_mixers(y_ssd, y_nsa, w_up_ssd.astype(BF16), w_up_nsa.astype(BF16), mgates)
    h1 = residual_matmul(xf, merged, w_out.astype(BF16))
    mlen = mem.shape[1]
    mn = rmsnorm_rows(mem.reshape(batch * mlen, d), mem_norm_w, tm=min(512, batch * mlen))
    kv = matmul(mn, xkv_w.astype(BF16))
    n_exp = router_e_w.shape[1]
    rpad = LANES - N_GROUPS - n_exp
    router_w = jnp.concatenate([router_g_w, router_e_w, jnp.zeros((d, rpad), F32)], axis=1).astype(BF16)
    router_b = jnp.concatenate([router_g_b, router_e_b, jnp.zeros((rpad,), F32)]).reshape(1, LANES)
    h2, hf, logits = xattn_router(h1, kv, norm2_w, xq_w.astype(BF16), x_q_norm_w, x_k_norm_w, xo_w.astype(BF16),
                                  norm3_w, router_w, router_b, batch, seq)
    out = hier_moe(h2, hf, logits, moe_w_gate.astype(BF16), moe_w_up.astype(BF16), moe_w_down.astype(BF16))
    return out.reshape(batch, seq, d)


def kernel(x, mem, norm1_w, w_in, ssd_conv_w, ssd_conv_b, ssd_dt_bias, ssd_a_log, ssd_d, ssd_norm_w, nsa_q_norm_w, nsa_k_norm_w, cmp_pe_k, cmp_w1_k, cmp_w2_k, cmp_pe_v, cmp_w1_v, cmp_w2_v, w_up_ssd, w_up_nsa, w_out, norm2_w, mem_norm_w, xq_w, xkv_w, x_q_norm_w, x_k_norm_w, xo_w, norm3_w, router_g_w, router_g_b, router_e_w, router_e_b, moe_w_gate, moe_w_up, moe_w_down):
    h = x
    for l in range(norm1_w.shape[0]):
        h = _layer(h, mem, norm1_w[l], w_in[l], ssd_conv_w[l], ssd_conv_b[l], ssd_dt_bias[l], ssd_a_log[l], ssd_d[l],
                   ssd_norm_w[l], nsa_q_norm_w[l], nsa_k_norm_w[l], cmp_pe_k[l], cmp_w1_k[l], cmp_w2_k[l], cmp_pe_v[l],
                   cmp_w1_v[l], cmp_w2_v[l], w_up_ssd[l], w_up_nsa[l], w_out[l], norm2_w[l], mem_norm_w[l], xq_w[l],
                   xkv_w[l], x_q_norm_w[l], x_k_norm_w[l], xo_w[l], norm3_w[l], router_g_w[l], router_g_b[l],
                   router_e_w[l], router_e_b[l], moe_w_gate[l], moe_w_up[l], moe_w_down[l])
    return h.astype(x.dtype)
```

```python
import functools

import numpy as np
import jax
import jax.numpy as jnp
from jax import lax
from jax.experimental import pallas as pl
from jax.experimental.pallas import tpu as pltpu

F32 = jnp.float32
BF16 = jnp.bfloat16

D_MODEL = 2048
SSD_INNER = 4096
SSD_HEAD_DIM = 64
SSD_HEADS = 64
SSD_GROUPS = 8
SSD_STATE = 128
SSD_CONV = 4
SSD_CHUNK = 256
HEADS_PER_GROUP = SSD_HEADS // SSD_GROUPS
GROUP_WIDTH = SSD_INNER // SSD_GROUPS
HEAD_DIM = 128
NSA_HEADS = 16
NSA_KV_HEADS = 4
NSA_REP = NSA_HEADS // NSA_KV_HEADS
NSA_WIDTH = NSA_HEADS * HEAD_DIM
NSA_KV_WIDTH = NSA_KV_HEADS * HEAD_DIM
CMP_BLK = 32
CMP_STRIDE = 16
CMP_HID = 256
SEL_BLK = 64
SEL_SHIFT = 6
N_SEL = 16
WINDOW = 512
Q_BLK = 128
SEL_KEY_TILE = 512
FORCE_SCORE = 1.0e4
X_HEADS = 4
X_WIDTH = X_HEADS * HEAD_DIM
N_GROUPS = 8
EXPERTS_PER_GROUP = 8
N_EXPERTS = 64
TOP_K = 2
EXPERT_HIDDEN = 1408
MOE_ROWS = 512
MOE_SUB = 256
MOE_HID_TILE = 256
ROPE_THETA = 10000.0
EPS = 1e-6
NEG_INF = -1e30
Q_PRESCALE = HEAD_DIM ** -0.5 * float(np.log2(np.e))
LANES = 128
VMEM_LIMIT = 56 * 1024 * 1024

ZX_COLS = SSD_INNER + SSD_INNER + 2 * SSD_GROUPS * SSD_STATE
QKV_COLS = NSA_WIDTH + 6 * NSA_KV_WIDTH
GATE_COLS = 2 * D_MODEL
SMALL_COLS = LANES


def _params(*sem):
    return pltpu.CompilerParams(dimension_semantics=sem, vmem_limit_bytes=VMEM_LIMIT)


def _rms(x, w):
    ms = jnp.mean(x * x, axis=-1, keepdims=True)
    return x * lax.rsqrt(ms + EPS) * w


def _sigmoid(x):
    return 1.0 / (1.0 + jnp.exp(-x))


def _silu(x):
    return x * _sigmoid(x)


def _softplus(x):
    return jnp.maximum(x, 0.0) + jnp.log1p(jnp.exp(-jnp.abs(x)))


def _dot(a, b):
    return jnp.dot(a, b, preferred_element_type=F32)


def _dot_nt(a, b):
    return lax.dot_general(a, b, (((1,), (1,)), ((), ())), preferred_element_type=F32)


def _masked_softmax(s, mask):
    s = jnp.where(mask, s, NEG_INF)
    e = jnp.exp(s - jnp.max(s, axis=-1, keepdims=True))
    p = e / jnp.sum(e, axis=-1, keepdims=True)
    return jnp.where(mask, p, 0.0)


def _rmsnorm_kernel(x_ref, w_ref, o_ref):
    o_ref[...] = _rms(x_ref[...], w_ref[...]).astype(o_ref.dtype)


def rmsnorm_rows(x, w, tm=512):
    m, d = x.shape
    return pl.pallas_call(
        _rmsnorm_kernel,
        grid=(m // tm,),
        in_specs=[pl.BlockSpec((tm, d), lambda i: (i, 0)), pl.BlockSpec((1, d), lambda i: (0, 0))],
        out_specs=pl.BlockSpec((tm, d), lambda i: (i, 0)),
        out_shape=jax.ShapeDtypeStruct((m, d), BF16),
        compiler_params=_params("parallel"),
        name="rmsnorm_rows",
    )(x, w.reshape(1, d))


def _matmul_kernel(x_ref, w_ref, o_ref):
    o_ref[...] = _dot(x_ref[...], w_ref[...]).astype(o_ref.dtype)


def matmul(x, w, out_dtype=F32, tm=512, tn=512):
    m, k = x.shape
    n = w.shape[1]
    tm, tn = min(tm, m), min(tn, n)
    return pl.pallas_call(
        _matmul_kernel,
        grid=(m // tm, n // tn),
        in_specs=[pl.BlockSpec((tm, k), lambda i, j: (i, 0)), pl.BlockSpec((k, tn), lambda i, j: (0, j))],
        out_specs=pl.BlockSpec((tm, tn), lambda i, j: (i, j)),
        out_shape=jax.ShapeDtypeStruct((m, n), out_dtype),
        compiler_params=_params("parallel", "arbitrary"),
        name="matmul",
    )(x, w)


def _matmul_wcast_kernel(x_ref, w_ref, *rest, shift):
    if shift:
        wx_ref, o_ref, wb_ref = rest
    else:
        o_ref, wb_ref = rest

    @pl.when(pl.program_id(1) == 0)
    def _():
        if shift:
            tn = w_ref.shape[1]
            w = jnp.concatenate([w_ref[...], wx_ref[...]], axis=1)[:, shift:shift + tn]
        else:
            w = w_ref[...]
        wb_ref[...] = w.astype(BF16)

    o_ref[...] = _dot(x_ref[...], wb_ref[...]).astype(o_ref.dtype)


def matmul_wcast(x, w, col0, ncols, out_dtype=F32, tm=512, tn=512):
    m, k = x.shape
    tn = min(tn, ncols)
    shift = col0 % tn
    base = (col0 - shift) // tn
    assert ncols % tn == 0 and shift <= LANES
    in_specs = [pl.BlockSpec((tm, k), lambda j, i: (i, 0)), pl.BlockSpec((k, tn), lambda j, i: (0, base + j))]
    args = [x, w]
    if shift:
        per = tn // LANES
        in_specs.append(pl.BlockSpec((k, LANES), lambda j, i: (0, (base + j + 1) * per)))
        args.append(w)
    return pl.pallas_call(
        functools.partial(_matmul_wcast_kernel, shift=shift),
        grid=(ncols // tn, m // tm),
        in_specs=in_specs,
        out_specs=pl.BlockSpec((tm, tn), lambda j, i: (i, j)),
        out_shape=jax.ShapeDtypeStruct((m, ncols), out_dtype),
        scratch_shapes=[pltpu.VMEM((k, tn), BF16)],
        compiler_params=_params("parallel", "arbitrary"),
        name="matmul_wcast",
    )(*args)


def _ssd_kernel(z_ref, x_ref, b_ref, c_ref, dt_ref, dtt_ref, cwx_ref, cwb_ref, cwc_ref, cbx_ref, cbb_ref, cbc_ref,
                hp_ref, hpt_ref, nw_ref, y_ref, state_ref, halo_x_ref, halo_b_ref, halo_c_ref):
    L = x_ref.shape[0]
    chunk = pl.program_id(2)

    @pl.when(chunk == 0)
    def _():
        state_ref[...] = jnp.zeros_like(state_ref)
        halo_x_ref[...] = jnp.zeros_like(halo_x_ref)
        halo_b_ref[...] = jnp.zeros_like(halo_b_ref)
        halo_c_ref[...] = jnp.zeros_like(halo_c_ref)

    def conv_silu(u, halo_ref, w, b):
        ext = jnp.concatenate([halo_ref[...], u], axis=0)
        acc = None
        for k in range(SSD_CONV):
            shifted = pltpu.roll(ext, SSD_CONV - 1 - k, 0)[8:8 + L] if k < SSD_CONV - 1 else u
            term = w[k:k + 1, :] * shifted
            acc = term if acc is None else acc + term
        halo_ref[...] = u[L - 8:, :]
        return _silu(acc + b)

    xs = conv_silu(x_ref[...], halo_x_ref, cwx_ref[...], cbx_ref[...])
    bm = conv_silu(b_ref[...], halo_b_ref, cwb_ref[...], cbb_ref[...])
    cm = conv_silu(c_ref[...], halo_c_ref, cwc_ref[...], cbc_ref[...])
    bm16, cm16 = bm.astype(BF16), cm.astype(BF16)

    hp = hp_ref[0]
    hpt = hpt_ref[0]
    dt = _softplus(dt_ref[0] + hp[0:1, :])
    dtt = _softplus(dtt_ref[0] + hpt[:, 0:1])
    da = dt * (-jnp.exp(hp[1:2, :]))
    dat = dtt * (-jnp.exp(hpt[:, 1:2]))
    row = lax.broadcasted_iota(jnp.int32, (L, L), 0)
    col = lax.broadcasted_iota(jnp.int32, (L, L), 1)
    tri = row >= col
    hi = lax.Precision.HIGHEST
    acum = jnp.dot(jnp.where(tri, 1.0, 0.0), da, precision=hi, preferred_element_type=F32)
    acumt = jnp.dot(dat, jnp.where(tri, 0.0, 1.0) + jnp.where(row == col, 1.0, 0.0),
                    precision=hi, preferred_element_type=F32)
    last = acum[L - 1:L, :]
    dte = jnp.exp(last - acum)
    eac = jnp.exp(acum)
    cdec = jnp.exp(last)

    cb = _dot_nt(cm16, bm16)
    lane = lax.broadcasted_iota(jnp.int32, (L, LANES), 1)
    first = lane < SSD_HEAD_DIM
    srow = lax.broadcasted_iota(jnp.int32, (LANES, SSD_STATE), 0) < SSD_HEAD_DIM
    z = z_ref[...]
    d_skip = hp[2:3, :]
    outs = []
    for p in range(HEADS_PER_GROUP // 2):
        h0, h1 = 2 * p, 2 * p + 1

        def pick(a, rows=first):
            return jnp.where(rows, a[:, h0:h0 + 1], a[:, h1:h1 + 1])

        xp = xs[:, p * LANES:(p + 1) * LANES]
        xdt = xp * pick(dt)
        xdt16 = xdt.astype(BF16)
        y = jnp.zeros((L, LANES), F32)
        for h, msk in ((h0, first), (h1, jnp.logical_not(first))):
            diff = acum[:, h:h + 1] - acumt[h:h + 1, :]
            dec = jnp.exp(jnp.where(tri, diff, -jnp.inf))
            y = y + _dot((cb * dec).astype(BF16), jnp.where(msk, xdt16, jnp.zeros_like(xdt16)))
        hprev = state_ref[p]
        y = y + _dot_nt(cm16, hprev.astype(BF16)) * pick(eac)
        xe_t = (xdt * pick(dte)).T.astype(BF16)
        state_ref[p] = hprev * jnp.where(srow, cdec[:, h0:h0 + 1], cdec[:, h1:h1 + 1]) + _dot(xe_t, bm16)
        outs.append(y + pick(d_skip, first[0:1, :]) * xp)
    y = jnp.concatenate(outs, axis=1) * _silu(z)
    y_ref[...] = _rms(y, nw_ref[...]).astype(y_ref.dtype)


def ssd_branch(zx, small, conv_w, conv_b, dt_bias, a_log, d_skip, norm_w, batch, seq):
    n = batch * seq
    L = min(SSD_CHUNK, seq)
    nc = seq // L
    g, hpg = SSD_GROUPS, HEADS_PER_GROUP
    dt_raw = small[:, :SSD_HEADS].reshape(n, g, hpg)
    dt_g = dt_raw.transpose(1, 0, 2)
    dtt_g = dt_raw.transpose(1, 2, 0)
    hp = jnp.stack([dt_bias, a_log, d_skip], axis=0).reshape(3, g, hpg).transpose(1, 0, 2)
    hp = jnp.pad(hp, ((0, 0), (0, 8 - 3), (0, 0)))
    hpt = hp.transpose(0, 2, 1)
    cw = conv_w
    cbias = conv_b.reshape(1, -1)
    nxb = SSD_INNER // GROUP_WIDTH
    rowblk = lambda b, gi, c: b * nc + c
    bc0 = SSD_INNER // SSD_STATE
    grid = (batch, g, nc)
    in_specs = [
        pl.BlockSpec((L, GROUP_WIDTH), lambda b, gi, c: (rowblk(b, gi, c), gi)),
        pl.BlockSpec((L, GROUP_WIDTH), lambda b, gi, c: (rowblk(b, gi, c), nxb + gi)),
        pl.BlockSpec((L, SSD_STATE), lambda b, gi, c: (rowblk(b, gi, c), 2 * bc0 + gi)),
        pl.BlockSpec((L, SSD_STATE), lambda b, gi, c: (rowblk(b, gi, c), 2 * bc0 + g + gi)),
        pl.BlockSpec((1, L, hpg), lambda b, gi, c: (gi, rowblk(b, gi, c), 0)),
        pl.BlockSpec((1, hpg, L), lambda b, gi, c: (gi, 0, rowblk(b, gi, c))),
        pl.BlockSpec((SSD_CONV, GROUP_WIDTH), lambda b, gi, c: (0, gi)),
        pl.BlockSpec((SSD_CONV, SSD_STATE), lambda b, gi, c: (0, bc0 + gi)),
        pl.BlockSpec((SSD_CONV, SSD_STATE), lambda b, gi, c: (0, bc0 + g + gi)),
        pl.BlockSpec((1, GROUP_WIDTH), lambda b, gi, c: (0, gi)),
        pl.BlockSpec((1, SSD_STATE), lambda b, gi, c: (0, bc0 + gi)),
        pl.BlockSpec((1, SSD_STATE), lambda b, gi, c: (0, bc0 + g + gi)),
        pl.BlockSpec((1, 8, hpg), lambda b, gi, c: (gi, 0, 0)),
        pl.BlockSpec((1, hpg, 8), lambda b, gi, c: (gi, 0, 0)),
        pl.BlockSpec((1, GROUP_WIDTH), lambda b, gi, c: (0, gi)),
    ]
    return pl.pallas_call(
        _ssd_kernel,
        grid=grid,
        in_specs=in_specs,
        out_specs=pl.BlockSpec((L, GROUP_WIDTH), lambda b, gi, c: (rowblk(b, gi, c), gi)),
        out_shape=jax.ShapeDtypeStruct((n, SSD_INNER), BF16),
        scratch_shapes=[pltpu.VMEM((HEADS_PER_GROUP // 2, LANES, SSD_STATE), F32),
                        pltpu.VMEM((8, GROUP_WIDTH), F32), pltpu.VMEM((8, SSD_STATE), F32),
                        pltpu.VMEM((8, SSD_STATE), F32)],
        compiler_params=_params("parallel", "parallel", "arbitrary"),
        name="ssd_branch",
    )(zx, zx, zx, zx, dt_g, dtt_g, cw, cw, cw, cbias, cbias, cbias, hp, hpt, norm_w.reshape(1, -1))


def _nsa_prep_kernel(q_ref, kc_ref, vc_ref, ks_ref, vs_ref, kw_ref, vw_ref, cos_ref, sin_ref, qw_ref, kw3_ref,
                     qo_ref, kco_ref, vco_ref, kso_ref, vso_ref, kwo_ref, vwo_ref):
    cos, sin = cos_ref[...], sin_ref[...]

    def norm_rope(x, w):
        y = _rms(x, w)
        return y * cos + pltpu.roll(y, HEAD_DIM // 2, 1) * sin

    qw = qw_ref[...]
    tt = q_ref.shape[0]
    for g in range(NSA_KV_HEADS):
        sl = slice(g * HEAD_DIM, (g + 1) * HEAD_DIM)
        for r in range(NSA_REP):
            h = g * NSA_REP + r
            qh = norm_rope(q_ref[:, h * HEAD_DIM:(h + 1) * HEAD_DIM], qw) * Q_PRESCALE
            qo_ref[0, g, r] = qh.astype(qo_ref.dtype)
        kco_ref[0, g] = norm_rope(kc_ref[:, sl], kw3_ref[0:1, :])
        vco_ref[0, g] = vc_ref[:, sl]
        kso_ref[0, g] = norm_rope(ks_ref[:, sl], kw3_ref[1:2, :]).astype(kso_ref.dtype)
        kwo_ref[0, g] = norm_rope(kw_ref[:, sl], kw3_ref[2:3, :]).astype(kwo_ref.dtype)
        vst = vs_ref[:, sl].T.astype(vso_ref.dtype)
        for j in range(tt // SEL_KEY_TILE):
            vso_ref[0, g, j] = vst[:, j * SEL_KEY_TILE:(j + 1) * SEL_KEY_TILE]
        vwt = vw_ref[:, sl].T.astype(vwo_ref.dtype)
        for j in range(tt // Q_BLK):
            vwo_ref[0, g, j] = vwt[:, j * Q_BLK:(j + 1) * Q_BLK]


def nsa_prep(qkv, q_norm_w, k_norm_w, batch, seq, tt=SEL_KEY_TILE):
    half = HEAD_DIM // 2
    inv = ROPE_THETA ** (-jnp.arange(half, dtype=F32) / half)
    ang = jnp.arange(seq).astype(F32)[:, None] * inv[None, :]
    cos = jnp.concatenate([jnp.cos(ang), jnp.cos(ang)], axis=-1)
    sin = jnp.concatenate([-jnp.sin(ang), jnp.sin(ang)], axis=-1)
    nt = seq // tt
    kvb = NSA_WIDTH // NSA_KV_WIDTH
    kv_spec = lambda j: pl.BlockSpec((tt, NSA_KV_WIDTH), lambda b, t: (b * nt + t, kvb + j))
    head_spec = pl.BlockSpec((1, NSA_KV_HEADS, tt, HEAD_DIM), lambda b, t: (b, 0, t, 0))
    head_shape = lambda dt: jax.ShapeDtypeStruct((batch, NSA_KV_HEADS, seq, HEAD_DIM), dt)
    return pl.pallas_call(
        _nsa_prep_kernel,
        grid=(batch, nt),
        in_specs=[pl.BlockSpec((tt, NSA_WIDTH), lambda b, t: (b * nt + t, 0))] + [kv_spec(j) for j in range(6)] + [
            pl.BlockSpec((tt, HEAD_DIM), lambda b, t: (t, 0)), pl.BlockSpec((tt, HEAD_DIM), lambda b, t: (t, 0)),
            pl.BlockSpec((1, HEAD_DIM), lambda b, t: (0, 0)), pl.BlockSpec((3, HEAD_DIM), lambda b, t: (0, 0))],
        out_specs=[pl.BlockSpec((1, NSA_KV_HEADS, NSA_REP, tt, HEAD_DIM), lambda b, t: (b, 0, 0, t, 0)),
                   head_spec, head_spec, head_spec,
                   pl.BlockSpec((1, NSA_KV_HEADS, tt // SEL_KEY_TILE, HEAD_DIM, SEL_KEY_TILE),
                                lambda b, t: (b, 0, t, 0, 0)),
                   head_spec,
                   pl.BlockSpec((1, NSA_KV_HEADS, tt // Q_BLK, HEAD_DIM, Q_BLK), lambda b, t: (b, 0, t, 0, 0))],
        out_shape=[jax.ShapeDtypeStruct((batch, NSA_KV_HEADS, NSA_REP, seq, HEAD_DIM), BF16),
                   head_shape(F32), head_shape(F32), head_shape(BF16),
                   jax.ShapeDtypeStruct((batch, NSA_KV_HEADS, seq // SEL_KEY_TILE, HEAD_DIM, SEL_KEY_TILE), BF16),
                   head_shape(BF16),
                   jax.ShapeDtypeStruct((batch, NSA_KV_HEADS, seq // Q_BLK, HEAD_DIM, Q_BLK), BF16)],
        compiler_params=_params("parallel", "parallel"),
        name="nsa_prep",
    )(qkv, qkv, qkv, qkv, qkv, qkv, qkv, cos, sin, q_norm_w.reshape(1, -1), k_norm_w)


def _compress_kernel(x_ref, pe_ref, w1_ref, w2_ref, o_ref, *, ncmp, transpose_out):
    nseg = x_ref.shape[2] // CMP_STRIDE
    u = jnp.zeros((nseg, CMP_HID), F32)
    v = jnp.zeros((nseg, CMP_HID), F32)
    for j in range(CMP_STRIDE):
        xj = x_ref[0, 0, pl.ds(j, nseg, stride=CMP_STRIDE), :]
        u = u + _dot((xj + pe_ref[j:j + 1, :]).astype(BF16), w1_ref[j])
        v = v + _dot((xj + pe_ref[CMP_STRIDE + j:CMP_STRIDE + j + 1, :]).astype(BF16), w1_ref[CMP_STRIDE + j])
    hid = _silu(u + pltpu.roll(v, nseg - 1, 0))
    comp = _dot(hid.astype(BF16), w2_ref[...])
    rowi = lax.broadcasted_iota(jnp.int32, comp.shape, 0)
    comp = jnp.where(rowi < ncmp, comp, 0.0)
    o_ref[0, 0] = (comp.T if transpose_out else comp).astype(o_ref.dtype)


def compress(raw, pe, w1, w2, transpose_out):
    b, g, t, hd = raw.shape
    nseg = t // CMP_STRIDE
    full = lambda shape: pl.BlockSpec(shape, lambda i, j: (0,) * len(shape))
    out_dims = (hd, nseg) if transpose_out else (nseg, hd)
    return pl.pallas_call(
        functools.partial(_compress_kernel, ncmp=nseg - 1, transpose_out=transpose_out),
        grid=(b, g),
        in_specs=[pl.BlockSpec((1, 1, t, hd), lambda i, j: (i, j, 0, 0)), full((CMP_BLK, hd)),
                  full((CMP_BLK, hd, CMP_HID)), full((CMP_HID, hd))],
        out_specs=pl.BlockSpec((1, 1) + out_dims, lambda i, j: (i, j, 0, 0)),
        out_shape=jax.ShapeDtypeStruct((b, g) + out_dims, BF16),
        compiler_params=_params("parallel", "parallel"),
        name="nsa_compress",
    )(raw, pe, w1.astype(BF16), w2.astype(BF16))


def _nsa_attn_kernel(q_ref, kc_ref, vct_ref, ks_ref, vst_ref, kw_ref, vwt_ref, glt_ref, ovt_ref, o_ref,
                     selb_ref, acc_ref, ocw_ref, *, seq, n_sel):
    qb = pl.program_id(2)
    q0 = qb * Q_BLK
    rows = NSA_REP * Q_BLK
    q = q_ref[0, 0].reshape(rows, HEAD_DIM)
    tq = q0 + lax.broadcasted_iota(jnp.int32, (1, Q_BLK), 1)
    head = lambda r: slice(r * Q_BLK, (r + 1) * Q_BLK)

    wk = WINDOW + Q_BLK
    start = pl.multiple_of(jnp.clip(q0 - WINDOW, 0, seq - wk), Q_BLK)
    s_w = _dot_nt(kw_ref[0, 0, pl.ds(start, wk), :], q)

    ncp = kc_ref.shape[2]
    s_c = _dot_nt(kc_ref[0, 0], q)
    cend = lax.broadcasted_iota(jnp.int32, (ncp, 1), 0) * CMP_STRIDE + (CMP_BLK - 1)
    m_c = cend <= tq
    ps = []
    for r in range(NSA_REP):
        s = jnp.where(m_c, s_c[:, head(r)], NEG_INF)
        e = jnp.exp2(s - jnp.max(s, axis=0, keepdims=True))
        p = jnp.where(m_c, e * (1.0 / jnp.sum(e, axis=0, keepdims=True)), 0.0)
        ps.append(p.astype(BF16))
    p_c = jnp.concatenate(ps, axis=1)
    o_c = _dot(vct_ref[0, 0], p_c)

    nsb = ovt_ref.shape[0]
    imp4 = _dot(ovt_ref[...], p_c)
    imp = imp4[:, head(0)]
    for r in range(1, NSA_REP):
        imp = imp + imp4[:, head(r)]
    blk_t = (q0 + lax.broadcasted_iota(jnp.int32, (nsb, Q_BLK), 1)) >> SEL_SHIFT
    sb = lax.broadcasted_iota(jnp.int32, (nsb, Q_BLK), 0)
    forced = (sb == 0) | (sb == blk_t) | (sb == blk_t - 1)
    imp = jnp.where(sb <= blk_t, jnp.where(forced, FORCE_SCORE, imp), -jnp.inf)
    rank = jnp.zeros((nsb, Q_BLK), F32)
    for i in range(nsb):
        ri = imp[i:i + 1, :]
        beats = (ri > imp) | ((ri == imp) & (sb > i))
        rank = rank + jnp.where(beats, 1.0, 0.0)
    selb_ref[...] = jnp.where(rank < n_sel, 0.0, NEG_INF)

    dist = tq - (start + lax.broadcasted_iota(jnp.int32, (wk, 1), 0))
    bias_w = jnp.where((dist >= 0) & (dist < WINDOW), 0.0, NEG_INF)
    ps, ls = [], []
    for r in range(NSA_REP):
        s = s_w[:, head(r)] + bias_w
        p = jnp.exp2(s - jnp.max(s, axis=0, keepdims=True))
        ls.append(jnp.sum(p, axis=0, keepdims=True))
        ps.append(p.astype(BF16))
    t0 = start // Q_BLK
    vw_t = jnp.concatenate([vwt_ref[0, 0, t0 + j] for j in range(wk // Q_BLK)], axis=1)
    o_w = _dot(vw_t, jnp.concatenate(ps, axis=1)) * (1.0 / jnp.concatenate(ls, axis=1))

    gate = _sigmoid(glt_ref[0, 0])
    for r in range(NSA_REP):
        ocw_ref[:, head(r)] = (gate[r:r + 1, :] * o_c[:, head(r)]
                               + gate[2 * NSA_REP + r:2 * NSA_REP + r + 1, :] * o_w[:, head(r)])

    kt_sz = vst_ref.shape[4]
    blocks_per_tile = kt_sz // SEL_BLK
    acc_ref[...] = jnp.zeros_like(acc_ref)

    def sel_tile(kt, carry):
        m, l = carry
        k0 = pl.multiple_of(kt * kt_sz, kt_sz)
        s_all = _dot_nt(ks_ref[0, 0, pl.ds(k0, kt_sz), :], q)
        kpos = k0 + lax.broadcasted_iota(jnp.int32, (kt_sz, 1), 0)
        bias = jnp.concatenate(
            [jnp.broadcast_to(selb_ref[pl.ds(kt * blocks_per_tile + i, 1), :], (SEL_BLK, Q_BLK))
             for i in range(blocks_per_tile)], axis=0)
        bias = jnp.where(kpos <= tq, bias, NEG_INF)
        ps, ms, ls, alphas = [], [], [], []
        for r in range(NSA_REP):
            s = s_all[:, head(r)] + bias
            m_old = m[:, head(r)]
            m_new = jnp.maximum(m_old, jnp.max(s, axis=0, keepdims=True))
            p = jnp.exp2(s - m_new)
            alpha = jnp.exp2(m_old - m_new)
            ls.append(alpha * l[:, head(r)] + jnp.sum(p, axis=0, keepdims=True))
            ms.append(m_new)
            alphas.append(alpha)
            ps.append(p.astype(BF16))
        pv = _dot(vst_ref[0, 0, kt], jnp.concatenate(ps, axis=1))
        acc_ref[...] = acc_ref[...] * jnp.concatenate(alphas, axis=1) + pv
        return jnp.concatenate(ms, axis=1), jnp.concatenate(ls, axis=1)

    n_tiles = (q0 + Q_BLK + kt_sz - 1) // kt_sz
    init = (jnp.full((1, rows), NEG_INF, F32), jnp.zeros((1, rows), F32))
    _, l_s = lax.fori_loop(0, n_tiles, sel_tile, init)
    o_s = acc_ref[...] * (1.0 / l_s)

    for r in range(NSA_REP):
        out_t = ocw_ref[:, head(r)] + gate[NSA_REP + r:NSA_REP + r + 1, :] * o_s[:, head(r)]
        o_ref[:, r * HEAD_DIM:(r + 1) * HEAD_DIM] = out_t.T.astype(o_ref.dtype)


def nsa_attention(qh, kcc, vcct, ksn, vst, kwn, vwt, gate_logits_t, batch, seq):
    assert seq % SEL_KEY_TILE == 0 and seq >= WINDOW + Q_BLK
    nq = seq // Q_BLK
    nsb = seq // SEL_BLK
    n_sel = min(N_SEL, nsb)
    ncp = kcc.shape[2]
    ncmp = (seq - CMP_BLK) // CMP_STRIDE + 1
    ci = np.arange(ncp)[None, :]
    sj = np.arange(nsb)[:, None]
    ovt = ((ci * CMP_STRIDE < (sj + 1) * SEL_BLK) & (ci * CMP_STRIDE + CMP_BLK > sj * SEL_BLK) & (ci < ncmp))
    ovt = jnp.asarray(ovt, BF16)
    seq_spec = pl.BlockSpec((1, 1, seq, HEAD_DIM), lambda b, g, i: (b, g, 0, 0))
    cmp_spec = pl.BlockSpec((1, 1, ncp, HEAD_DIM), lambda b, g, i: (b, g, 0, 0))
    tiled = lambda a: pl.BlockSpec((1, 1) + a.shape[2:], lambda b, g, i: (b, g, 0, 0, 0))
    return pl.pallas_call(
        functools.partial(_nsa_attn_kernel, seq=seq, n_sel=n_sel),
        grid=(batch, NSA_KV_HEADS, nq),
        in_specs=[pl.BlockSpec((1, 1, NSA_REP, Q_BLK, HEAD_DIM), lambda b, g, i: (b, g, 0, i, 0)),
                  cmp_spec, pl.BlockSpec((1, 1, HEAD_DIM, ncp), lambda b, g, i: (b, g, 0, 0)),
                  seq_spec, tiled(vst), seq_spec, tiled(vwt),
                  pl.BlockSpec((1, 1, 3 * NSA_REP, Q_BLK), lambda b, g, i: (b, g, 0, i)),
                  pl.BlockSpec((nsb, ncp), lambda b, g, i: (0, 0))],
        out_specs=pl.BlockSpec((Q_BLK, NSA_REP * HEAD_DIM), lambda b, g, i: (b * nq + i, g)),
        out_shape=jax.ShapeDtypeStruct((batch * seq, NSA_WIDTH), BF16),
        scratch_shapes=[pltpu.VMEM((nsb, Q_BLK), F32), pltpu.VMEM((HEAD_DIM, NSA_REP * Q_BLK), F32),
                        pltpu.VMEM((HEAD_DIM, NSA_REP * Q_BLK), F32)],
        compiler_params=_params("parallel", "parallel", "arbitrary"),
        name="nsa_attention",
    )(qh, kcc, vcct, ksn, vst, kwn, vwt, gate_logits_t, ovt)


def _merge_kernel(ys_ref, yn_ref, ws_ref, wn_ref, gs_ref, gn_ref, o_ref):
    up_s = _dot(ys_ref[...], ws_ref[...])
    up_n = _dot(yn_ref[...], wn_ref[...])
    o_ref[...] = (_sigmoid(gs_ref[...]) * up_s + _sigmoid(gn_ref[...]) * up_n).astype(o_ref.dtype)


def merge_mixers(y_ssd, y_nsa, w_up_ssd, w_up_nsa, gates, tm=512, tn=512):
    m = y_ssd.shape[0]
    nb = D_MODEL // tn
    return pl.pallas_call(
        _merge_kernel,
        grid=(m // tm, nb),
        in_specs=[pl.BlockSpec((tm, SSD_INNER), lambda i, j: (i, 0)), pl.BlockSpec((tm, NSA_WIDTH), lambda i, j: (i, 0)),
                  pl.BlockSpec((SSD_INNER, tn), lambda i, j: (0, j)), pl.BlockSpec((NSA_WIDTH, tn), lambda i, j: (0, j)),
                  pl.BlockSpec((tm, tn), lambda i, j: (i, j)), pl.BlockSpec((tm, tn), lambda i, j: (i, nb + j))],
        out_specs=pl.BlockSpec((tm, tn), lambda i, j: (i, j)),
        out_shape=jax.ShapeDtypeStruct((m, D_MODEL), BF16),
        compiler_params=_params("parallel", "arbitrary"),
        name="merge_mixers",
    )(y_ssd, y_nsa, w_up_ssd, w_up_nsa, gates, gates)


def _residual_matmul_kernel(h_ref, x_ref, w_ref, o_ref):
    o_ref[...] = h_ref[...] + _dot(x_ref[...], w_ref[...])


def residual_matmul(h, x, w, tm=512, tn=512):
    m, k = x.shape
    n = w.shape[1]
    return pl.pallas_call(
        _residual_matmul_kernel,
        grid=(m // tm, n // tn),
        in_specs=[pl.BlockSpec((tm, tn), lambda i, j: (i, j)), pl.BlockSpec((tm, k), lambda i, j: (i, 0)),
                  pl.BlockSpec((k, tn), lambda i, j: (0, j))],
        out_specs=pl.BlockSpec((tm, tn), lambda i, j: (i, j)),
        out_shape=jax.ShapeDtypeStruct((m, n), F32),
        compiler_params=_params("parallel", "arbitrary"),
        name="residual_matmul",
    )(h, x, w)


def _xattn_router_kernel(h_ref, kv_ref, n2_ref, wq_ref, qn_ref, kn_ref, wo_ref, n3_ref, rw_ref, rb_ref,
                         h2_ref, hf_ref, lg_ref):
    h = h_ref[...]
    qp = _dot(_rms(h, n2_ref[...]).astype(BF16), wq_ref[...])
    kv = kv_ref[...]
    scale = HEAD_DIM ** -0.5
    heads = []
    for hd in range(X_HEADS):
        sl = slice(hd * HEAD_DIM, (hd + 1) * HEAD_DIM)
        qh = _rms(qp[:, sl], qn_ref[...]).astype(BF16)
        kh = _rms(kv[:, sl], kn_ref[...]).astype(BF16)
        vh = kv[:, X_WIDTH + hd * HEAD_DIM:X_WIDTH + (hd + 1) * HEAD_DIM].astype(BF16)
        s = _dot_nt(qh, kh) * scale
        e = jnp.exp(s - jnp.max(s, axis=-1, keepdims=True))
        p = e / jnp.sum(e, axis=-1, keepdims=True)
        heads.append(_dot(p.astype(BF16), vh))
    o = jnp.concatenate(heads, axis=1).astype(BF16)
    h2 = h + _dot(o, wo_ref[...])
    h2_ref[...] = h2
    hf = _rms(h2, n3_ref[...]).astype(BF16)
    hf_ref[...] = hf
    lg_ref[...] = _dot(hf, rw_ref[...]) + rb_ref[...]


def xattn_router(h1, kv, norm2_w, wq, q_norm_w, k_norm_w, wo, norm3_w, router_w, router_b, batch, seq, tm=512):
    n = batch * seq
    tm = min(tm, seq)
    per_seq = seq // tm
    mlen = kv.shape[0] // batch
    full = lambda shape: pl.BlockSpec(shape, lambda i: (0,) * len(shape))
    return pl.pallas_call(
        _xattn_router_kernel,
        grid=(n // tm,),
        in_specs=[pl.BlockSpec((tm, D_MODEL), lambda i: (i, 0)),
                  pl.BlockSpec((mlen, 2 * X_WIDTH), lambda i: (i // per_seq, 0)),
                  full((1, D_MODEL)), full((D_MODEL, X_WIDTH)), full((1, HEAD_DIM)), full((1, HEAD_DIM)),
                  full((X_WIDTH, D_MODEL)), full((1, D_MODEL)), full((D_MODEL, LANES)), full((1, LANES))],
        out_specs=[pl.BlockSpec((tm, D_MODEL), lambda i: (i, 0)), pl.BlockSpec((tm, D_MODEL), lambda i: (i, 0)),
                   pl.BlockSpec((tm, LANES), lambda i: (i, 0))],
        out_shape=[jax.ShapeDtypeStruct((n, D_MODEL), F32), jax.ShapeDtypeStruct((n, D_MODEL), BF16),
                   jax.ShapeDtypeStruct((n, LANES), F32)],
        compiler_params=_params("parallel"),
        name="xattn_router",
    )(h1, kv, norm2_w.reshape(1, -1), wq, q_norm_w.reshape(1, -1), k_norm_w.reshape(1, -1), wo,
      norm3_w.reshape(1, -1), router_w, router_b)


def _moe_ffn_kernel(blk_e_ref, nsub_ref, x_ref, wg_ref, wu_ref, wd_ref, wt_ref, y_ref, *, hid):
    i, c = pl.program_id(0), pl.program_id(1)
    th = wg_ref.shape[2]
    nsub = nsub_ref[i]

    @pl.when(c == 0)
    def _():
        y_ref[...] = jnp.zeros_like(y_ref)

    @pl.when(nsub > 0)
    def _():
        col_ok = c * th + lax.broadcasted_iota(jnp.int32, (1, th), 1) < hid
        row_ok = c * th + lax.broadcasted_iota(jnp.int32, (th, 1), 0) < hid
        wgu = jnp.concatenate([wg_ref[0], wu_ref[0]], axis=1).astype(BF16)
        wd = jnp.where(row_ok, wd_ref[0], 0.0).astype(BF16)
        for s in range(y_ref.shape[0] // MOE_SUB):
            @pl.when(s < nsub)
            def _():
                rs = slice(s * MOE_SUB, (s + 1) * MOE_SUB)
                gu = _dot(x_ref[rs, :], wgu)
                act = jnp.where(col_ok, _silu(gu[:, :th]) * gu[:, th:], 0.0)
                y_ref[rs, :] += _dot(act.astype(BF16), wd)

    @pl.when(c == pl.num_programs(1) - 1)
    def _():
        y_ref[...] = y_ref[...] * wt_ref[...]


def moe_ffn(x_buf, w_buf, blk_e, nsub, w_gate, w_up, w_down, rows_per_blk):
    rows, d = x_buf.shape
    n_blk = rows // rows_per_blk
    hid = w_gate.shape[2]
    th = MOE_HID_TILE
    nc = pl.cdiv(hid, th)
    tile = lambda i, c, ns: jnp.where(ns[i] > 0, c, nc - 1)
    grid_spec = pltpu.PrefetchScalarGridSpec(
        num_scalar_prefetch=2,
        grid=(n_blk, nc),
        in_specs=[pl.BlockSpec((rows_per_blk, d), lambda i, c, be, ns: (i, 0)),
                  pl.BlockSpec((1, d, th), lambda i, c, be, ns: (be[i], 0, tile(i, c, ns))),
                  pl.BlockSpec((1, d, th), lambda i, c, be, ns: (be[i], 0, tile(i, c, ns))),
                  pl.BlockSpec((1, th, d), lambda i, c, be, ns: (be[i], tile(i, c, ns), 0)),
                  pl.BlockSpec((rows_per_blk, 1), lambda i, c, be, ns: (i, 0))],
        out_specs=pl.BlockSpec((rows_per_blk, d), lambda i, c, be, ns: (i, 0)),
    )
    return pl.pallas_call(
        functools.partial(_moe_ffn_kernel, hid=hid),
        grid_spec=grid_spec,
        out_shape=jax.ShapeDtypeStruct((rows, d), F32),
        compiler_params=_params("arbitrary", "arbitrary"),
        name="moe_ffn",
    )(blk_e, nsub, x_buf, w_gate, w_up, w_down, w_buf)


def hier_moe(h2, hf, logits, w_gate, w_up, w_down, rows_per_blk=MOE_ROWS):
    n, d = h2.shape
    n_exp = w_gate.shape[0]
    pg = jax.nn.softmax(logits[:, :N_GROUPS], axis=-1)
    grp = jnp.argmax(pg, axis=-1)
    pg_top = jnp.take_along_axis(pg, grp[:, None], axis=-1)
    le = logits[:, N_GROUPS:N_GROUPS + n_exp].reshape(n, N_GROUPS, EXPERTS_PER_GROUP)
    le = jnp.take_along_axis(le, grp[:, None, None], axis=1)[:, 0]
    top_p, top_i = lax.top_k(jax.nn.softmax(le, axis=-1), TOP_K)
    wts = (pg_top * top_p / jnp.sum(top_p, axis=-1, keepdims=True)).reshape(-1)
    eid = (grp[:, None] * EXPERTS_PER_GROUP + top_i).reshape(-1).astype(jnp.int32)
    tok = jnp.repeat(jnp.arange(n, dtype=jnp.int32), TOP_K)
    n_assign = n * TOP_K
    order = jnp.argsort(eid)
    e_sorted = eid[order]
    counts = jnp.bincount(eid, length=n_exp)
    start = jnp.cumsum(counts) - counts
    padded = (counts + rows_per_blk - 1) // rows_per_blk * rows_per_blk
    pend = jnp.cumsum(padded)
    pstart = pend - padded
    dest = (pstart[e_sorted] + jnp.arange(n_assign) - start[e_sorted]).astype(jnp.int32)
    n_blk = (n_assign + n_exp * (rows_per_blk - 1) + rows_per_blk - 1) // rows_per_blk
    rows = n_blk * rows_per_blk
    tok_buf = jnp.full((rows,), n, jnp.int32).at[dest].set(tok[order])
    w_buf = jnp.zeros((rows,), F32).at[dest].set(wts[order])
    blk_e = jnp.minimum(jnp.searchsorted(pend, jnp.arange(n_blk) * rows_per_blk, side='right'), n_exp - 1)
    filled = jnp.clip(pstart[blk_e] + counts[blk_e] - jnp.arange(n_blk) * rows_per_blk, 0, rows_per_blk)
    nsub = ((filled + MOE_SUB - 1) // MOE_SUB).astype(jnp.int32)
    x_buf = jnp.concatenate([hf, jnp.zeros((1, d), hf.dtype)], axis=0)[tok_buf]
    y_buf = moe_ffn(x_buf, w_buf.reshape(rows, 1), blk_e.astype(jnp.int32), nsub, w_gate, w_up, w_down,
                    rows_per_blk)
    slot = jnp.zeros((n_assign,), jnp.int32).at[order].set(dest).reshape(n, TOP_K)
    return h2 + y_buf[slot[:, 0]] + y_buf[slot[:, 1]]


COL_DT = ZX_COLS
COL_QKV = COL_DT + SSD_HEADS
COL_NSA_GATE = COL_QKV + QKV_COLS
COL_MERGE_GATE = COL_NSA_GATE + 3 * NSA_HEADS


def _small_w_in(w_in):
    pad = jnp.zeros((w_in.shape[0], SMALL_COLS - SSD_HEADS - 3 * NSA_HEADS), w_in.dtype)
    return jnp.concatenate([w_in[:, COL_DT:COL_QKV], w_in[:, COL_NSA_GATE:COL_MERGE_GATE], pad], axis=1)


def _layer(x, mem, norm1_w, w_in, ssd_conv_w, ssd_conv_b, ssd_dt_bias, ssd_a_log, ssd_d, ssd_norm_w,
           nsa_q_norm_w, nsa_k_norm_w, cmp_pe_k, cmp_w1_k, cmp_w2_k, cmp_pe_v, cmp_w1_v, cmp_w2_v,
           w_up_ssd, w_up_nsa, w_out, norm2_w, mem_norm_w, xq_w, xkv_w, x_q_norm_w, x_k_norm_w, xo_w,
           norm3_w, router_g_w, router_g_b, router_e_w, router_e_b, moe_w_gate, moe_w_up, moe_w_down):
    batch, seq, d = x.shape
    n = batch * seq
    xf = x.reshape(n, d)
    hn = rmsnorm_rows(xf, norm1_w)
    zx = matmul_wcast(hn, w_in, 0, ZX_COLS)
    qkv = matmul_wcast(hn, w_in, COL_QKV, QKV_COLS)
    mgates = matmul_wcast(hn, w_in, COL_MERGE_GATE, GATE_COLS)
    small = matmul_wcast(hn, _small_w_in(w_in), 0, SMALL_COLS)
    y_ssd = ssd_branch(zx, small, ssd_conv_w, ssd_conv_b, ssd_dt_bias, ssd_a_log, ssd_d, ssd_norm_w, batch, seq)
    qh, kcn, vcr, ksn, vsn, kwn, vwn = nsa_prep(qkv, nsa_q_norm_w, nsa_k_norm_w, batch, seq)
    kcc = compress(kcn, cmp_pe_k, cmp_w1_k, cmp_w2_k, transpose_out=False)
    vcct = compress(vcr, cmp_pe_v, cmp_w1_v, cmp_w2_v, transpose_out=True)
    gl = small[:, SSD_HEADS:SSD_HEADS + 3 * NSA_HEADS].reshape(batch, seq, 3, NSA_KV_HEADS, NSA_REP)
    gl = gl.transpose(0, 3, 2, 4, 1).reshape(batch, NSA_KV_HEADS, 3 * NSA_REP, seq)
    y_nsa = nsa_attention(qh, kcc, vcct, ksn, vsn, kwn, vwn, gl, batch, seq)
    merged = merge_mixers(y_ssd, y_nsa, w_up_ssd.astype(BF16), w_up_nsa.astype(BF16), mgates)
    h1 = residual_matmul(xf, merged, w_out.astype(BF16))
    mlen = mem.shape[1]
    mn = rmsnorm_rows(mem.reshape(batch * mlen, d), mem_norm_w, tm=min(512, batch * mlen))
    kv = matmul(mn, xkv_w.astype(BF16))
    n_exp = router_e_w.shape[1]
    rpad = LANES - N_GROUPS - n_exp
    router_w = jnp.concatenate([router_g_w, router_e_w, jnp.zeros((d, rpad), F32)], axis=1).astype(BF16)
    router_b = jnp.concatenate([router_g_b, router_e_b, jnp.zeros((rpad,), F32)]).reshape(1, LANES)
    h2, hf, logits = xattn_router(h1, kv, norm2_w, xq_w.astype(BF16), x_q_norm_w, x_k_norm_w, xo_w.astype(BF16),
                                  norm3_w, router_w, router_b, batch, seq)
    out = hier_moe(h2, hf, logits, moe_w_gate, moe_w_up, moe_w_down)
    return out.reshape(batch, seq, d)


def kernel(x, mem, norm1_w, w_in, ssd_conv_w, ssd_conv_b, ssd_dt_bias, ssd_a_log, ssd_d, ssd_norm_w, nsa_q_norm_w, nsa_k_norm_w, cmp_pe_k, cmp_w1_k, cmp_w2_k, cmp_pe_v, cmp_w1_v, cmp_w2_v, w_up_ssd, w_up_nsa, w_out, norm2_w, mem_norm_w, xq_w, xkv_w, x_q_norm_w, x_k_norm_w, xo_w, norm3_w, router_g_w, router_g_b, router_e_w, router_e_b, moe_w_gate, moe_w_up, moe_w_down):
    h = x
    for l in range(norm1_w.shape[0]):
        h = _layer(h, mem, norm1_w[l], w_in[l], ssd_conv_w[l], ssd_conv_b[l], ssd_dt_bias[l], ssd_a_log[l], ssd_d[l],
                   ssd_norm_w[l], nsa_q_norm_w[l], nsa_k_norm_w[l], cmp_pe_k[l], cmp_w1_k[l], cmp_w2_k[l], cmp_pe_v[l],
                   cmp_w1_v[l], cmp_w2_v[l], w_up_ssd[l], w_up_nsa[l], w_out[l], norm2_w[l], mem_norm_w[l], xq_w[l],
                   xkv_w[l], x_q_norm_w[l], x_k_norm_w[l], xo_w[l], norm3_w[l], router_g_w[l], router_g_b[l],
                   router_e_w[l], router_e_b[l], moe_w_gate[l], moe_w_up[l], moe_w_down[l])
    return h.astype(x.dtype)
```

```python
import functools

import numpy as np
import jax
import jax.numpy as jnp
from jax import lax
from jax.experimental import pallas as pl
from jax.experimental.pallas import tpu as pltpu

F32 = jnp.float32
BF16 = jnp.bfloat16

D_MODEL = 2048
SSD_INNER = 4096
SSD_HEAD_DIM = 64
SSD_HEADS = 64
SSD_GROUPS = 8
SSD_STATE = 128
SSD_CONV = 4
SSD_CHUNK = 256
HEADS_PER_GROUP = SSD_HEADS // SSD_GROUPS
GROUP_WIDTH = SSD_INNER // SSD_GROUPS
HEAD_DIM = 128
NSA_HEADS = 16
NSA_KV_HEADS = 4
NSA_REP = NSA_HEADS // NSA_KV_HEADS
NSA_WIDTH = NSA_HEADS * HEAD_DIM
NSA_KV_WIDTH = NSA_KV_HEADS * HEAD_DIM
CMP_BLK = 32
CMP_STRIDE = 16
CMP_HID = 256
SEL_BLK = 64
SEL_SHIFT = 6
N_SEL = 16
WINDOW = 512
Q_BLK = 128
SEL_KEY_TILE = 512
FORCE_SCORE = 1.0e4
X_HEADS = 4
X_WIDTH = X_HEADS * HEAD_DIM
N_GROUPS = 8
EXPERTS_PER_GROUP = 8
N_EXPERTS = 64
TOP_K = 2
EXPERT_HIDDEN = 1408
MOE_ROWS = 512
MOE_SUB = 256
MOE_HID_TILE = 256
ROPE_THETA = 10000.0
EPS = 1e-6
NEG_INF = -1e30
Q_PRESCALE = HEAD_DIM ** -0.5 * float(np.log2(np.e))
LANES = 128
VMEM_LIMIT = 56 * 1024 * 1024

ZX_COLS = SSD_INNER + SSD_INNER + 2 * SSD_GROUPS * SSD_STATE
QKV_COLS = NSA_WIDTH + 6 * NSA_KV_WIDTH
GATE_COLS = 2 * D_MODEL
SMALL_COLS = LANES


def _params(*sem):
    return pltpu.CompilerParams(dimension_semantics=sem, vmem_limit_bytes=VMEM_LIMIT)


def _rms(x, w):
    ms = jnp.mean(x * x, axis=-1, keepdims=True)
    return x * lax.rsqrt(ms + EPS) * w


def _sigmoid(x):
    return 1.0 / (1.0 + jnp.exp(-x))


def _silu(x):
    return x * _sigmoid(x)


def _softplus(x):
    return jnp.maximum(x, 0.0) + jnp.log1p(jnp.exp(-jnp.abs(x)))


def _dot(a, b):
    return jnp.dot(a, b, preferred_element_type=F32)


def _dot_nt(a, b):
    return lax.dot_general(a, b, (((1,), (1,)), ((), ())), preferred_element_type=F32)


def _masked_softmax(s, mask):
    s = jnp.where(mask, s, NEG_INF)
    e = jnp.exp(s - jnp.max(s, axis=-1, keepdims=True))
    p = e / jnp.sum(e, axis=-1, keepdims=True)
    return jnp.where(mask, p, 0.0)


def _rmsnorm_kernel(x_ref, w_ref, o_ref):
    o_ref[...] = _rms(x_ref[...], w_ref[...]).astype(o_ref.dtype)


def rmsnorm_rows(x, w, tm=512):
    m, d = x.shape
    return pl.pallas_call(
        _rmsnorm_kernel,
        grid=(m // tm,),
        in_specs=[pl.BlockSpec((tm, d), lambda i: (i, 0)), pl.BlockSpec((1, d), lambda i: (0, 0))],
        out_specs=pl.BlockSpec((tm, d), lambda i: (i, 0)),
        out_shape=jax.ShapeDtypeStruct((m, d), BF16),
        compiler_params=_params("parallel"),
        name="rmsnorm_rows",
    )(x, w.reshape(1, d))


def _matmul_kernel(x_ref, w_ref, o_ref):
    o_ref[...] = _dot(x_ref[...], w_ref[...]).astype(o_ref.dtype)


def matmul(x, w, out_dtype=F32, tm=512, tn=512):
    m, k = x.shape
    n = w.shape[1]
    tm, tn = min(tm, m), min(tn, n)
    return pl.pallas_call(
        _matmul_kernel,
        grid=(m // tm, n // tn),
        in_specs=[pl.BlockSpec((tm, k), lambda i, j: (i, 0)), pl.BlockSpec((k, tn), lambda i, j: (0, j))],
        out_specs=pl.BlockSpec((tm, tn), lambda i, j: (i, j)),
        out_shape=jax.ShapeDtypeStruct((m, n), out_dtype),
        compiler_params=_params("parallel", "arbitrary"),
        name="matmul",
    )(x, w)


def _matmul_wcast_kernel(x_ref, w_ref, *rest, shift):
    if shift:
        wx_ref, o_ref, wb_ref = rest
    else:
        o_ref, wb_ref = rest

    @pl.when(pl.program_id(1) == 0)
    def _():
        if shift:
            tn = w_ref.shape[1]
            w = jnp.concatenate([w_ref[...], wx_ref[...]], axis=1)[:, shift:shift + tn]
        else:
            w = w_ref[...]
        wb_ref[...] = w.astype(BF16)

    o_ref[...] = _dot(x_ref[...], wb_ref[...]).astype(o_ref.dtype)


def matmul_wcast(x, w, col0, ncols, out_dtype=F32, tm=1024, tn=512):
    m, k = x.shape
    tn = min(tn, ncols)
    shift = col0 % tn
    base = (col0 - shift) // tn
    assert ncols % tn == 0 and shift <= LANES
    in_specs = [pl.BlockSpec((tm, k), lambda j, i: (i, 0)), pl.BlockSpec((k, tn), lambda j, i: (0, base + j))]
    args = [x, w]
    if shift:
        per = tn // LANES
        in_specs.append(pl.BlockSpec((k, LANES), lambda j, i: (0, (base + j + 1) * per)))
        args.append(w)
    return pl.pallas_call(
        functools.partial(_matmul_wcast_kernel, shift=shift),
        grid=(ncols // tn, m // tm),
        in_specs=in_specs,
        out_specs=pl.BlockSpec((tm, tn), lambda j, i: (i, j)),
        out_shape=jax.ShapeDtypeStruct((m, ncols), out_dtype),
        scratch_shapes=[pltpu.VMEM((k, tn), BF16)],
        compiler_params=_params("parallel", "arbitrary"),
        name="matmul_wcast",
    )(*args)


def _split3(a):
    h = a.astype(BF16)
    r = a - h.astype(F32)
    m = r.astype(BF16)
    return h, m, (r - m.astype(F32)).astype(BF16)


def _ssd_kernel(z_ref, x_ref, b_ref, c_ref, dt_ref, dtt_ref, cwx_ref, cwb_ref, cwc_ref, cbx_ref, cbb_ref, cbc_ref,
                hp_ref, hpt_ref, nw_ref, y_ref, state_ref, ext_x_ref, ext_b_ref, ext_c_ref):
    L = x_ref.shape[0]
    H = L // 2
    chunk = pl.program_id(2)

    @pl.when(chunk == 0)
    def _():
        state_ref[...] = jnp.zeros_like(state_ref)
        ext_x_ref[...] = jnp.zeros_like(ext_x_ref)
        ext_b_ref[...] = jnp.zeros_like(ext_b_ref)
        ext_c_ref[...] = jnp.zeros_like(ext_c_ref)

    def conv_silu(u_ref, ext_ref, w, b):
        ext_ref[0:8, :] = ext_ref[L:L + 8, :]
        u = u_ref[...]
        ext_ref[8:8 + L, :] = u
        acc = b + w[SSD_CONV - 1:SSD_CONV, :] * u
        for k in range(SSD_CONV - 1):
            acc = acc + w[k:k + 1, :] * ext_ref[pl.ds(8 - (SSD_CONV - 1) + k, L), :]
        return _silu(acc)

    xs = conv_silu(x_ref, ext_x_ref, cwx_ref[...], cbx_ref[...])
    bm = conv_silu(b_ref, ext_b_ref, cwb_ref[...], cbb_ref[...])
    cm = conv_silu(c_ref, ext_c_ref, cwc_ref[...], cbc_ref[...])
    bm16, cm16 = bm.astype(BF16), cm.astype(BF16)

    hp = hp_ref[0]
    hpt = hpt_ref[0]
    dt = _softplus(dt_ref[0] + hp[0:1, :])
    dtt = _softplus(dtt_ref[0] + hpt[:, 0:1])
    da = dt * (-jnp.exp(hp[1:2, :]))
    dat = dtt * (-jnp.exp(hpt[:, 1:2]))
    row = lax.broadcasted_iota(jnp.int32, (L, L), 0)
    col = lax.broadcasted_iota(jnp.int32, (L, L), 1)
    tril = jnp.where(row >= col, 1.0, 0.0).astype(BF16)
    triu = jnp.where(row <= col, 1.0, 0.0).astype(BF16)
    acum = sum(_dot(tril, part) for part in _split3(da))
    acumt = sum(_dot(part, triu) for part in _split3(dat))
    last = acum[L - 1:L, :]
    dte = jnp.exp(last - acum)
    eac = jnp.exp(acum)
    cdec = jnp.exp(last)

    cb = _dot_nt(cm16, bm16)
    cb00, cb10, cb11 = cb[:H, :H], cb[H:, :H], cb[H:, H:]
    tri = lax.broadcasted_iota(jnp.int32, (H, H), 0) >= lax.broadcasted_iota(jnp.int32, (H, H), 1)
    lane = lax.broadcasted_iota(jnp.int32, (L, LANES), 1)
    first = lane < SSD_HEAD_DIM
    srow = lax.broadcasted_iota(jnp.int32, (LANES, SSD_STATE), 0) < SSD_HEAD_DIM
    z = z_ref[...]
    d_skip = hp[2:3, :]
    outs = []
    for p in range(HEADS_PER_GROUP // 2):
        h0, h1 = 2 * p, 2 * p + 1

        def pick(a, rows=first):
            return jnp.where(rows, a[:, h0:h0 + 1], a[:, h1:h1 + 1])

        xp = xs[:, p * LANES:(p + 1) * LANES]
        xdt = xp * pick(dt)
        xdt16 = xdt.astype(BF16)
        y_top = jnp.zeros((H, LANES), F32)
        y_bot = jnp.zeros((H, LANES), F32)
        for h, msk in ((h0, first), (h1, jnp.logical_not(first))):
            a_col, a_row = acum[:, h:h + 1], acumt[h:h + 1, :]
            d00 = jnp.exp(jnp.where(tri, a_col[:H] - a_row[:, :H], -jnp.inf))
            d10 = jnp.exp(a_col[H:] - a_row[:, :H])
            d11 = jnp.exp(jnp.where(tri, a_col[H:] - a_row[:, H:], -jnp.inf))
            xh = jnp.where(msk, xdt16, jnp.zeros_like(xdt16))
            y_top = y_top + _dot((cb00 * d00).astype(BF16), xh[:H])
            y_bot = y_bot + _dot((cb10 * d10).astype(BF16), xh[:H]) + _dot((cb11 * d11).astype(BF16), xh[H:])
        y = jnp.concatenate([y_top, y_bot], axis=0)
        hprev = state_ref[p]
        y = y + _dot_nt(cm16, hprev.astype(BF16)) * pick(eac)
        xe_t = (xdt * pick(dte)).T.astype(BF16)
        state_ref[p] = hprev * jnp.where(srow, cdec[:, h0:h0 + 1], cdec[:, h1:h1 + 1]) + _dot(xe_t, bm16)
        outs.append(y + pick(d_skip, first[0:1, :]) * xp)
    y = jnp.concatenate(outs, axis=1) * _silu(z)
    y_ref[...] = _rms(y, nw_ref[...]).astype(y_ref.dtype)


def ssd_branch(zx, small, conv_w, conv_b, dt_bias, a_log, d_skip, norm_w, batch, seq):
    n = batch * seq
    L = min(SSD_CHUNK, seq)
    nc = seq // L
    g, hpg = SSD_GROUPS, HEADS_PER_GROUP
    dt_raw = small[:, :SSD_HEADS].reshape(n, g, hpg)
    dt_g = dt_raw.transpose(1, 0, 2)
    dtt_g = dt_raw.transpose(1, 2, 0)
    hp = jnp.stack([dt_bias, a_log, d_skip], axis=0).reshape(3, g, hpg).transpose(1, 0, 2)
    hp = jnp.pad(hp, ((0, 0), (0, 8 - 3), (0, 0)))
    hpt = hp.transpose(0, 2, 1)
    cw = conv_w
    cbias = conv_b.reshape(1, -1)
    nxb = SSD_INNER // GROUP_WIDTH
    rowblk = lambda b, gi, c: b * nc + c
    bc0 = SSD_INNER // SSD_STATE
    grid = (batch, g, nc)
    in_specs = [
        pl.BlockSpec((L, GROUP_WIDTH), lambda b, gi, c: (rowblk(b, gi, c), gi)),
        pl.BlockSpec((L, GROUP_WIDTH), lambda b, gi, c: (rowblk(b, gi, c), nxb + gi)),
        pl.BlockSpec((L, SSD_STATE), lambda b, gi, c: (rowblk(b, gi, c), 2 * bc0 + gi)),
        pl.BlockSpec((L, SSD_STATE), lambda b, gi, c: (rowblk(b, gi, c), 2 * bc0 + g + gi)),
        pl.BlockSpec((1, L, hpg), lambda b, gi, c: (gi, rowblk(b, gi, c), 0)),
        pl.BlockSpec((1, hpg, L), lambda b, gi, c: (gi, 0, rowblk(b, gi, c))),
        pl.BlockSpec((SSD_CONV, GROUP_WIDTH), lambda b, gi, c: (0, gi)),
        pl.BlockSpec((SSD_CONV, SSD_STATE), lambda b, gi, c: (0, bc0 + gi)),
        pl.BlockSpec((SSD_CONV, SSD_STATE), lambda b, gi, c: (0, bc0 + g + gi)),
        pl.BlockSpec((1, GROUP_WIDTH), lambda b, gi, c: (0, gi)),
        pl.BlockSpec((1, SSD_STATE), lambda b, gi, c: (0, bc0 + gi)),
        pl.BlockSpec((1, SSD_STATE), lambda b, gi, c: (0, bc0 + g + gi)),
        pl.BlockSpec((1, 8, hpg), lambda b, gi, c: (gi, 0, 0)),
        pl.BlockSpec((1, hpg, 8), lambda b, gi, c: (gi, 0, 0)),
        pl.BlockSpec((1, GROUP_WIDTH), lambda b, gi, c: (0, gi)),
    ]
    return pl.pallas_call(
        _ssd_kernel,
        grid=grid,
        in_specs=in_specs,
        out_specs=pl.BlockSpec((L, GROUP_WIDTH), lambda b, gi, c: (rowblk(b, gi, c), gi)),
        out_shape=jax.ShapeDtypeStruct((n, SSD_INNER), BF16),
        scratch_shapes=[pltpu.VMEM((HEADS_PER_GROUP // 2, LANES, SSD_STATE), F32),
                        pltpu.VMEM((L + 8, GROUP_WIDTH), F32), pltpu.VMEM((L + 8, SSD_STATE), F32),
                        pltpu.VMEM((L + 8, SSD_STATE), F32)],
        compiler_params=_params("parallel", "parallel", "arbitrary"),
        name="ssd_branch",
    )(zx, zx, zx, zx, dt_g, dtt_g, cw, cw, cw, cbias, cbias, cbias, hp, hpt, norm_w.reshape(1, -1))


def _nsa_prep_kernel(q_ref, kc_ref, vc_ref, ks_ref, vs_ref, kw_ref, vw_ref, cos_ref, sin_ref, qw_ref, kw3_ref,
                     qo_ref, kco_ref, vco_ref, kso_ref, vso_ref, kwo_ref, vwo_ref):
    cos, sin = cos_ref[...], sin_ref[...]

    def norm_rope(x, w):
        y = _rms(x, w)
        return y * cos + pltpu.roll(y, HEAD_DIM // 2, 1) * sin

    qw = qw_ref[...]
    tt = q_ref.shape[0]
    for g in range(NSA_KV_HEADS):
        sl = slice(g * HEAD_DIM, (g + 1) * HEAD_DIM)
        for r in range(NSA_REP):
            h = g * NSA_REP + r
            qh = norm_rope(q_ref[:, h * HEAD_DIM:(h + 1) * HEAD_DIM], qw) * Q_PRESCALE
            qo_ref[0, g, r] = qh.astype(qo_ref.dtype)
        kco_ref[0, g] = norm_rope(kc_ref[:, sl], kw3_ref[0:1, :])
        vco_ref[0, g] = vc_ref[:, sl]
        kso_ref[0, g] = norm_rope(ks_ref[:, sl], kw3_ref[1:2, :]).astype(kso_ref.dtype)
        kwo_ref[0, g] = norm_rope(kw_ref[:, sl], kw3_ref[2:3, :]).astype(kwo_ref.dtype)
        vst = vs_ref[:, sl].T.astype(vso_ref.dtype)
        for j in range(tt // SEL_KEY_TILE):
            vso_ref[0, g, j] = vst[:, j * SEL_KEY_TILE:(j + 1) * SEL_KEY_TILE]
        vwt = vw_ref[:, sl].T.astype(vwo_ref.dtype)
        for j in range(tt // Q_BLK):
            vwo_ref[0, g, j] = vwt[:, j * Q_BLK:(j + 1) * Q_BLK]


def nsa_prep(qkv, q_norm_w, k_norm_w, batch, seq, tt=SEL_KEY_TILE):
    half = HEAD_DIM // 2
    inv = ROPE_THETA ** (-jnp.arange(half, dtype=F32) / half)
    ang = jnp.arange(seq).astype(F32)[:, None] * inv[None, :]
    cos = jnp.concatenate([jnp.cos(ang), jnp.cos(ang)], axis=-1)
    sin = jnp.concatenate([-jnp.sin(ang), jnp.sin(ang)], axis=-1)
    nt = seq // tt
    kvb = NSA_WIDTH // NSA_KV_WIDTH
    kv_spec = lambda j: pl.BlockSpec((tt, NSA_KV_WIDTH), lambda b, t: (b * nt + t, kvb + j))
    head_spec = pl.BlockSpec((1, NSA_KV_HEADS, tt, HEAD_DIM), lambda b, t: (b, 0, t, 0))
    head_shape = lambda dt: jax.ShapeDtypeStruct((batch, NSA_KV_HEADS, seq, HEAD_DIM), dt)
    return pl.pallas_call(
        _nsa_prep_kernel,
        grid=(batch, nt),
        in_specs=[pl.BlockSpec((tt, NSA_WIDTH), lambda b, t: (b * nt + t, 0))] + [kv_spec(j) for j in range(6)] + [
            pl.BlockSpec((tt, HEAD_DIM), lambda b, t: (t, 0)), pl.BlockSpec((tt, HEAD_DIM), lambda b, t: (t, 0)),
            pl.BlockSpec((1, HEAD_DIM), lambda b, t: (0, 0)), pl.BlockSpec((3, HEAD_DIM), lambda b, t: (0, 0))],
        out_specs=[pl.BlockSpec((1, NSA_KV_HEADS, NSA_REP, tt, HEAD_DIM), lambda b, t: (b, 0, 0, t, 0)),
                   head_spec, head_spec, head_spec,
                   pl.BlockSpec((1, NSA_KV_HEADS, tt // SEL_KEY_TILE, HEAD_DIM, SEL_KEY_TILE),
                                lambda b, t: (b, 0, t, 0, 0)),
                   head_spec,
                   pl.BlockSpec((1, NSA_KV_HEADS, tt // Q_BLK, HEAD_DIM, Q_BLK), lambda b, t: (b, 0, t, 0, 0))],
        out_shape=[jax.ShapeDtypeStruct((batch, NSA_KV_HEADS, NSA_REP, seq, HEAD_DIM), BF16),
                   head_shape(F32), head_shape(F32), head_shape(BF16),
                   jax.ShapeDtypeStruct((batch, NSA_KV_HEADS, seq // SEL_KEY_TILE, HEAD_DIM, SEL_KEY_TILE), BF16),
                   head_shape(BF16),
                   jax.ShapeDtypeStruct((batch, NSA_KV_HEADS, seq // Q_BLK, HEAD_DIM, Q_BLK), BF16)],
        compiler_params=_params("parallel", "parallel"),
        name="nsa_prep",
    )(qkv, qkv, qkv, qkv, qkv, qkv, qkv, cos, sin, q_norm_w.reshape(1, -1), k_norm_w)


def _compress_kernel(x_ref, pe_ref, w1_ref, w2_ref, o_ref, *, ncmp, transpose_out):
    nseg = x_ref.shape[2] // CMP_STRIDE
    u = jnp.zeros((nseg, CMP_HID), F32)
    v = jnp.zeros((nseg, CMP_HID), F32)
    for j in range(CMP_STRIDE):
        xj = x_ref[0, 0, pl.ds(j, nseg, stride=CMP_STRIDE), :]
        u = u + _dot((xj + pe_ref[j:j + 1, :]).astype(BF16), w1_ref[j])
        v = v + _dot((xj + pe_ref[CMP_STRIDE + j:CMP_STRIDE + j + 1, :]).astype(BF16), w1_ref[CMP_STRIDE + j])
    hid = _silu(u + pltpu.roll(v, nseg - 1, 0))
    comp = _dot(hid.astype(BF16), w2_ref[...])
    rowi = lax.broadcasted_iota(jnp.int32, comp.shape, 0)
    comp = jnp.where(rowi < ncmp, comp, 0.0)
    o_ref[0, 0] = (comp.T if transpose_out else comp).astype(o_ref.dtype)


def compress(raw, pe, w1, w2, transpose_out):
    b, g, t, hd = raw.shape
    nseg = t // CMP_STRIDE
    full = lambda shape: pl.BlockSpec(shape, lambda i, j: (0,) * len(shape))
    out_dims = (hd, nseg) if transpose_out else (nseg, hd)
    return pl.pallas_call(
        functools.partial(_compress_kernel, ncmp=nseg - 1, transpose_out=transpose_out),
        grid=(b, g),
        in_specs=[pl.BlockSpec((1, 1, t, hd), lambda i, j: (i, j, 0, 0)), full((CMP_BLK, hd)),
                  full((CMP_BLK, hd, CMP_HID)), full((CMP_HID, hd))],
        out_specs=pl.BlockSpec((1, 1) + out_dims, lambda i, j: (i, j, 0, 0)),
        out_shape=jax.ShapeDtypeStruct((b, g) + out_dims, BF16),
        compiler_params=_params("parallel", "parallel"),
        name="nsa_compress",
    )(raw, pe, w1.astype(BF16), w2.astype(BF16))


def _nsa_attn_kernel(q_ref, kc_ref, vct_ref, ks_ref, vst_ref, kw_ref, vwt_ref, glt_ref, ovt_ref, o_ref,
                     selb_ref, acc_ref, ocw_ref, *, seq, n_sel):
    qb = pl.program_id(2)
    q0 = qb * Q_BLK
    rows = NSA_REP * Q_BLK
    q = q_ref[0, 0].reshape(rows, HEAD_DIM)
    tq = q0 + lax.broadcasted_iota(jnp.int32, (1, Q_BLK), 1)
    head = lambda r: slice(r * Q_BLK, (r + 1) * Q_BLK)

    wk = WINDOW + Q_BLK
    start = pl.multiple_of(jnp.clip(q0 - WINDOW, 0, seq - wk), Q_BLK)
    s_w = _dot_nt(kw_ref[0, 0, pl.ds(start, wk), :], q)

    ncp = kc_ref.shape[2]
    s_c = _dot_nt(kc_ref[0, 0], q)
    cend = lax.broadcasted_iota(jnp.int32, (ncp, 1), 0) * CMP_STRIDE + (CMP_BLK - 1)
    m_c = cend <= tq
    ps = []
    for r in range(NSA_REP):
        s = jnp.where(m_c, s_c[:, head(r)], NEG_INF)
        e = jnp.exp2(s - jnp.max(s, axis=0, keepdims=True))
        p = jnp.where(m_c, e * (1.0 / jnp.sum(e, axis=0, keepdims=True)), 0.0)
        ps.append(p.astype(BF16))
    p_c = jnp.concatenate(ps, axis=1)
    o_c = _dot(vct_ref[0, 0], p_c)

    nsb = ovt_ref.shape[0]
    imp4 = _dot(ovt_ref[...], p_c)
    imp = imp4[:, head(0)]
    for r in range(1, NSA_REP):
        imp = imp + imp4[:, head(r)]
    blk_t = (q0 + lax.broadcasted_iota(jnp.int32, (nsb, Q_BLK), 1)) >> SEL_SHIFT
    sb = lax.broadcasted_iota(jnp.int32, (nsb, Q_BLK), 0)
    forced = (sb == 0) | (sb == blk_t) | (sb == blk_t - 1)
    imp = jnp.where(sb <= blk_t, jnp.where(forced, FORCE_SCORE, imp), -jnp.inf)
    rank = jnp.zeros((nsb, Q_BLK), F32)
    for i in range(nsb):
        ri = imp[i:i + 1, :]
        beats = (ri > imp) | ((ri == imp) & (sb > i))
        rank = rank + jnp.where(beats, 1.0, 0.0)
    selb_ref[...] = jnp.where(rank < n_sel, 0.0, NEG_INF)

    dist = tq - (start + lax.broadcasted_iota(jnp.int32, (wk, 1), 0))
    bias_w = jnp.where((dist >= 0) & (dist < WINDOW), 0.0, NEG_INF)
    ps, ls = [], []
    for r in range(NSA_REP):
        s = s_w[:, head(r)] + bias_w
        p = jnp.exp2(s - jnp.max(s, axis=0, keepdims=True))
        ls.append(jnp.sum(p, axis=0, keepdims=True))
        ps.append(p.astype(BF16))
    t0 = start // Q_BLK
    vw_t = jnp.concatenate([vwt_ref[0, 0, t0 + j] for j in range(wk // Q_BLK)], axis=1)
    o_w = _dot(vw_t, jnp.concatenate(ps, axis=1)) * (1.0 / jnp.concatenate(ls, axis=1))

    gate = _sigmoid(glt_ref[0, 0])
    for r in range(NSA_REP):
        ocw_ref[:, head(r)] = (gate[r:r + 1, :] * o_c[:, head(r)]
                               + gate[2 * NSA_REP + r:2 * NSA_REP + r + 1, :] * o_w[:, head(r)])

    kt_sz = vst_ref.shape[4]
    blocks_per_tile = kt_sz // SEL_BLK
    acc_ref[...] = jnp.zeros_like(acc_ref)

    def sel_tile(kt, carry):
        m, l = carry
        k0 = pl.multiple_of(kt * kt_sz, kt_sz)
        s_all = _dot_nt(ks_ref[0, 0, pl.ds(k0, kt_sz), :], q)
        kpos = k0 + lax.broadcasted_iota(jnp.int32, (kt_sz, 1), 0)
        bias = jnp.concatenate(
            [jnp.broadcast_to(selb_ref[pl.ds(kt * blocks_per_tile + i, 1), :], (SEL_BLK, Q_BLK))
             for i in range(blocks_per_tile)], axis=0)
        bias = jnp.where(kpos <= tq, bias, NEG_INF)
        ps, ms, ls, alphas = [], [], [], []
        for r in range(NSA_REP):
            s = s_all[:, head(r)] + bias
            m_old = m[:, head(r)]
            m_new = jnp.maximum(m_old, jnp.max(s, axis=0, keepdims=True))
            p = jnp.exp2(s - m_new)
            alpha = jnp.exp2(m_old - m_new)
            ls.append(alpha * l[:, head(r)] + jnp.sum(p, axis=0, keepdims=True))
            ms.append(m_new)
            alphas.append(alpha)
            ps.append(p.astype(BF16))
        pv = _dot(vst_ref[0, 0, kt], jnp.concatenate(ps, axis=1))
        acc_ref[...] = acc_ref[...] * jnp.concatenate(alphas, axis=1) + pv
        return jnp.concatenate(ms, axis=1), jnp.concatenate(ls, axis=1)

    n_tiles = (q0 + Q_BLK + kt_sz - 1) // kt_sz
    init = (jnp.full((1, rows), NEG_INF, F32), jnp.zeros((1, rows), F32))
    _, l_s = lax.fori_loop(0, n_tiles, sel_tile, init)
    o_s = acc_ref[...] * (1.0 / l_s)

    for r in range(NSA_REP):
        out_t = ocw_ref[:, head(r)] + gate[NSA_REP + r:NSA_REP + r + 1, :] * o_s[:, head(r)]
        o_ref[:, r * HEAD_DIM:(r + 1) * HEAD_DIM] = out_t.T.astype(o_ref.dtype)


def nsa_attention(qh, kcc, vcct, ksn, vst, kwn, vwt, gate_logits_t, batch, seq):
    assert seq % SEL_KEY_TILE == 0 and seq >= WINDOW + Q_BLK
    nq = seq // Q_BLK
    nsb = seq // SEL_BLK
    n_sel = min(N_SEL, nsb)
    ncp = kcc.shape[2]
    ncmp = (seq - CMP_BLK) // CMP_STRIDE + 1
    ci = np.arange(ncp)[None, :]
    sj = np.arange(nsb)[:, None]
    ovt = ((ci * CMP_STRIDE < (sj + 1) * SEL_BLK) & (ci * CMP_STRIDE + CMP_BLK > sj * SEL_BLK) & (ci < ncmp))
    ovt = jnp.asarray(ovt, BF16)
    seq_spec = pl.BlockSpec((1, 1, seq, HEAD_DIM), lambda b, g, i: (b, g, 0, 0))
    cmp_spec = pl.BlockSpec((1, 1, ncp, HEAD_DIM), lambda b, g, i: (b, g, 0, 0))
    tiled = lambda a: pl.BlockSpec((1, 1) + a.shape[2:], lambda b, g, i: (b, g, 0, 0, 0))
    return pl.pallas_call(
        functools.partial(_nsa_attn_kernel, seq=seq, n_sel=n_sel),
        grid=(batch, NSA_KV_HEADS, nq),
        in_specs=[pl.BlockSpec((1, 1, NSA_REP, Q_BLK, HEAD_DIM), lambda b, g, i: (b, g, 0, i, 0)),
                  cmp_spec, pl.BlockSpec((1, 1, HEAD_DIM, ncp), lambda b, g, i: (b, g, 0, 0)),
                  seq_spec, tiled(vst), seq_spec, tiled(vwt),
                  pl.BlockSpec((1, 1, 3 * NSA_REP, Q_BLK), lambda b, g, i: (b, g, 0, i)),
                  pl.BlockSpec((nsb, ncp), lambda b, g, i: (0, 0))],
        out_specs=pl.BlockSpec((Q_BLK, NSA_REP * HEAD_DIM), lambda b, g, i: (b * nq + i, g)),
        out_shape=jax.ShapeDtypeStruct((batch * seq, NSA_WIDTH), BF16),
        scratch_shapes=[pltpu.VMEM((nsb, Q_BLK), F32), pltpu.VMEM((HEAD_DIM, NSA_REP * Q_BLK), F32),
                        pltpu.VMEM((HEAD_DIM, NSA_REP * Q_BLK), F32)],
        compiler_params=_params("parallel", "parallel", "arbitrary"),
        name="nsa_attention",
    )(qh, kcc, vcct, ksn, vst, kwn, vwt, gate_logits_t, ovt)


def _merge_kernel(ys_ref, yn_ref, ws_ref, wn_ref, gs_ref, gn_ref, o_ref):
    up_s = _dot(ys_ref[...], ws_ref[...])
    up_n = _dot(yn_ref[...], wn_ref[...])
    o_ref[...] = (_sigmoid(gs_ref[...]) * up_s + _sigmoid(gn_ref[...]) * up_n).astype(o_ref.dtype)


def merge_mixers(y_ssd, y_nsa, w_up_ssd, w_up_nsa, gates, tm=512, tn=512):
    m = y_ssd.shape[0]
    nb = D_MODEL // tn
    return pl.pallas_call(
        _merge_kernel,
        grid=(m // tm, nb),
        in_specs=[pl.BlockSpec((tm, SSD_INNER), lambda i, j: (i, 0)), pl.BlockSpec((tm, NSA_WIDTH), lambda i, j: (i, 0)),
                  pl.BlockSpec((SSD_INNER, tn), lambda i, j: (0, j)), pl.BlockSpec((NSA_WIDTH, tn), lambda i, j: (0, j)),
                  pl.BlockSpec((tm, tn), lambda i, j: (i, j)), pl.BlockSpec((tm, tn), lambda i, j: (i, nb + j))],
        out_specs=pl.BlockSpec((tm, tn), lambda i, j: (i, j)),
        out_shape=jax.ShapeDtypeStruct((m, D_MODEL), BF16),
        compiler_params=_params("parallel", "arbitrary"),
        name="merge_mixers",
    )(y_ssd, y_nsa, w_up_ssd, w_up_nsa, gates, gates)


def _residual_matmul_kernel(h_ref, x_ref, w_ref, o_ref):
    o_ref[...] = h_ref[...] + _dot(x_ref[...], w_ref[...])


def residual_matmul(h, x, w, tm=1024, tn=1024):
    m, k = x.shape
    n = w.shape[1]
    return pl.pallas_call(
        _residual_matmul_kernel,
        grid=(m // tm, n // tn),
        in_specs=[pl.BlockSpec((tm, tn), lambda i, j: (i, j)), pl.BlockSpec((tm, k), lambda i, j: (i, 0)),
                  pl.BlockSpec((k, tn), lambda i, j: (0, j))],
        out_specs=pl.BlockSpec((tm, tn), lambda i, j: (i, j)),
        out_shape=jax.ShapeDtypeStruct((m, n), F32),
        compiler_params=_params("parallel", "arbitrary"),
        name="residual_matmul",
    )(h, x, w)


def _xattn_router_kernel(h_ref, kv_ref, n2_ref, wq_ref, qn_ref, kn_ref, wo_ref, n3_ref, rw_ref, rb_ref,
                         h2_ref, hf_ref, lg_ref):
    h = h_ref[...]
    qp = _dot(_rms(h, n2_ref[...]).astype(BF16), wq_ref[...])
    kv = kv_ref[...]
    scale = HEAD_DIM ** -0.5
    heads = []
    for hd in range(X_HEADS):
        sl = slice(hd * HEAD_DIM, (hd + 1) * HEAD_DIM)
        qh = _rms(qp[:, sl], qn_ref[...]).astype(BF16)
        kh = _rms(kv[:, sl], kn_ref[...]).astype(BF16)
        vh = kv[:, X_WIDTH + hd * HEAD_DIM:X_WIDTH + (hd + 1) * HEAD_DIM].astype(BF16)
        s = _dot_nt(qh, kh) * scale
        e = jnp.exp(s - jnp.max(s, axis=-1, keepdims=True))
        p = e / jnp.sum(e, axis=-1, keepdims=True)
        heads.append(_dot(p.astype(BF16), vh))
    o = jnp.concatenate(heads, axis=1).astype(BF16)
    h2 = h + _dot(o, wo_ref[...])
    h2_ref[...] = h2
    hf = _rms(h2, n3_ref[...]).astype(BF16)
    hf_ref[...] = hf
    lg_ref[...] = _dot(hf, rw_ref[...]) + rb_ref[...]


def xattn_router(h1, kv, norm2_w, wq, q_norm_w, k_norm_w, wo, norm3_w, router_w, router_b, batch, seq, tm=512):
    n = batch * seq
    tm = min(tm, seq)
    per_seq = seq // tm
    mlen = kv.shape[0] // batch
    full = lambda shape: pl.BlockSpec(shape, lambda i: (0,) * len(shape))
    return pl.pallas_call(
        _xattn_router_kernel,
        grid=(n // tm,),
        in_specs=[pl.BlockSpec((tm, D_MODEL), lambda i: (i, 0)),
                  pl.BlockSpec((mlen, 2 * X_WIDTH), lambda i: (i // per_seq, 0)),
                  full((1, D_MODEL)), full((D_MODEL, X_WIDTH)), full((1, HEAD_DIM)), full((1, HEAD_DIM)),
                  full((X_WIDTH, D_MODEL)), full((1, D_MODEL)), full((D_MODEL, LANES)), full((1, LANES))],
        out_specs=[pl.BlockSpec((tm, D_MODEL), lambda i: (i, 0)), pl.BlockSpec((tm, D_MODEL), lambda i: (i, 0)),
                   pl.BlockSpec((tm, LANES), lambda i: (i, 0))],
        out_shape=[jax.ShapeDtypeStruct((n, D_MODEL), F32), jax.ShapeDtypeStruct((n, D_MODEL), BF16),
                   jax.ShapeDtypeStruct((n, LANES), F32)],
        compiler_params=_params("parallel"),
        name="xattn_router",
    )(h1, kv, norm2_w.reshape(1, -1), wq, q_norm_w.reshape(1, -1), k_norm_w.reshape(1, -1), wo,
      norm3_w.reshape(1, -1), router_w, router_b)


def _moe_ffn_kernel(blk_e_ref, nsub_ref, x_ref, wg_ref, wu_ref, wd_ref, y_ref, *, hid):
    i, c = pl.program_id(0), pl.program_id(1)
    th = wg_ref.shape[2]
    nsub = nsub_ref[i]

    @pl.when(c == 0)
    def _():
        y_ref[...] = jnp.zeros_like(y_ref)

    @pl.when(nsub > 0)
    def _():
        col_ok = c * th + lax.broadcasted_iota(jnp.int32, (1, th), 1) < hid
        row_ok = c * th + lax.broadcasted_iota(jnp.int32, (th, 1), 0) < hid
        wgu = jnp.concatenate([wg_ref[0], wu_ref[0]], axis=1).astype(BF16)
        wd = jnp.where(row_ok, wd_ref[0], 0.0).astype(BF16)
        for s in range(y_ref.shape[0] // MOE_SUB):
            @pl.when(s < nsub)
            def _():
                rs = slice(s * MOE_SUB, (s + 1) * MOE_SUB)
                gu = _dot(x_ref[rs, :], wgu)
                act = jnp.where(col_ok, _silu(gu[:, :th]) * gu[:, th:], 0.0)
                y_ref[rs, :] += _dot(act.astype(BF16), wd)


def moe_ffn(x_buf, blk_e, nsub, w_gate, w_up, w_down, rows_per_blk):
    rows, d = x_buf.shape
    n_blk = rows // rows_per_blk
    hid = w_gate.shape[2]
    th = MOE_HID_TILE
    nc = pl.cdiv(hid, th)
    tile = lambda i, c, ns: jnp.where(ns[i] > 0, c, nc - 1)
    grid_spec = pltpu.PrefetchScalarGridSpec(
        num_scalar_prefetch=2,
        grid=(n_blk, nc),
        in_specs=[pl.BlockSpec((rows_per_blk, d), lambda i, c, be, ns: (i, 0)),
                  pl.BlockSpec((1, d, th), lambda i, c, be, ns: (be[i], 0, tile(i, c, ns))),
                  pl.BlockSpec((1, d, th), lambda i, c, be, ns: (be[i], 0, tile(i, c, ns))),
                  pl.BlockSpec((1, th, d), lambda i, c, be, ns: (be[i], tile(i, c, ns), 0))],
        out_specs=pl.BlockSpec((rows_per_blk, d), lambda i, c, be, ns: (i, 0)),
    )
    return pl.pallas_call(
        functools.partial(_moe_ffn_kernel, hid=hid),
        grid_spec=grid_spec,
        out_shape=jax.ShapeDtypeStruct((rows, d), F32),
        compiler_params=_params("arbitrary", "arbitrary"),
        name="moe_ffn",
    )(blk_e, nsub, x_buf, w_gate, w_up, w_down)


def hier_moe(h2, hf, logits, w_gate, w_up, w_down, rows_per_blk=MOE_ROWS):
    n, d = h2.shape
    n_exp = w_gate.shape[0]
    pg = jax.nn.softmax(logits[:, :N_GROUPS], axis=-1)
    grp = jnp.argmax(pg, axis=-1)
    pg_top = jnp.take_along_axis(pg, grp[:, None], axis=-1)
    le = logits[:, N_GROUPS:N_GROUPS + n_exp].reshape(n, N_GROUPS, EXPERTS_PER_GROUP)
    le = jnp.take_along_axis(le, grp[:, None, None], axis=1)[:, 0]
    top_p, top_i = lax.top_k(jax.nn.softmax(le, axis=-1), TOP_K)
    wts = (pg_top * top_p / jnp.sum(top_p, axis=-1, keepdims=True)).reshape(-1)
    eid = (grp[:, None] * EXPERTS_PER_GROUP + top_i).reshape(-1).astype(jnp.int32)
    tok = jnp.repeat(jnp.arange(n, dtype=jnp.int32), TOP_K)
    n_assign = n * TOP_K
    order = jnp.argsort(eid)
    e_sorted = eid[order]
    counts = jnp.bincount(eid, length=n_exp)
    start = jnp.cumsum(counts) - counts
    padded = (counts + rows_per_blk - 1) // rows_per_blk * rows_per_blk
    pend = jnp.cumsum(padded)
    pstart = pend - padded
    dest = (pstart[e_sorted] + jnp.arange(n_assign) - start[e_sorted]).astype(jnp.int32)
    n_blk = (n_assign + n_exp * (rows_per_blk - 1) + rows_per_blk - 1) // rows_per_blk
    rows = n_blk * rows_per_blk
    tok_buf = (jnp.arange(rows, dtype=jnp.int32) % n).at[dest].set(tok[order])
    blk_e = jnp.minimum(jnp.searchsorted(pend, jnp.arange(n_blk) * rows_per_blk, side='right'), n_exp - 1)
    filled = jnp.clip(pstart[blk_e] + counts[blk_e] - jnp.arange(n_blk) * rows_per_blk, 0, rows_per_blk)
    nsub = ((filled + MOE_SUB - 1) // MOE_SUB).astype(jnp.int32)
    y_buf = moe_ffn(hf[tok_buf], blk_e.astype(jnp.int32), nsub, w_gate, w_up, w_down, rows_per_blk)
    slot = jnp.zeros((n_assign,), jnp.int32).at[order].set(dest).reshape(n, TOP_K)
    wts = wts.reshape(n, TOP_K)
    return h2 + wts[:, 0:1] * y_buf[slot[:, 0]] + wts[:, 1:2] * y_buf[slot[:, 1]]


COL_DT = ZX_COLS
COL_QKV = COL_DT + SSD_HEADS
COL_NSA_GATE = COL_QKV + QKV_COLS
COL_MERGE_GATE = COL_NSA_GATE + 3 * NSA_HEADS


def _small_w_in(w_in):
    pad = jnp.zeros((w_in.shape[0], SMALL_COLS - SSD_HEADS - 3 * NSA_HEADS), w_in.dtype)
    return jnp.concatenate([w_in[:, COL_DT:COL_QKV], w_in[:, COL_NSA_GATE:COL_MERGE_GATE], pad], axis=1)


def _layer(x, mem, norm1_w, w_in, ssd_conv_w, ssd_conv_b, ssd_dt_bias, ssd_a_log, ssd_d, ssd_norm_w,
           nsa_q_norm_w, nsa_k_norm_w, cmp_pe_k, cmp_w1_k, cmp_w2_k, cmp_pe_v, cmp_w1_v, cmp_w2_v,
           w_up_ssd, w_up_nsa, w_out, norm2_w, mem_norm_w, xq_w, xkv_w, x_q_norm_w, x_k_norm_w, xo_w,
           norm3_w, router_g_w, router_g_b, router_e_w, router_e_b, moe_w_gate, moe_w_up, moe_w_down):
    batch, seq, d = x.shape
    n = batch * seq
    xf = x.reshape(n, d)
    hn = rmsnorm_rows(xf, norm1_w)
    zx = matmul_wcast(hn, w_in, 0, ZX_COLS)
    qkv = matmul_wcast(hn, w_in, COL_QKV, QKV_COLS)
    mgates = matmul_wcast(hn, w_in, COL_MERGE_GATE, GATE_COLS)
    small = matmul_wcast(hn, _small_w_in(w_in), 0, SMALL_COLS)
    y_ssd = ssd_branch(zx, small, ssd_conv_w, ssd_conv_b, ssd_dt_bias, ssd_a_log, ssd_d, ssd_norm_w, batch, seq)
    qh, kcn, vcr, ksn, vsn, kwn, vwn = nsa_prep(qkv, nsa_q_norm_w, nsa_k_norm_w, batch, seq)
    kcc = compress(kcn, cmp_pe_k, cmp_w1_k, cmp_w2_k, transpose_out=False)
    vcct = compress(vcr, cmp_pe_v, cmp_w1_v, cmp_w2_v, transpose_out=True)
    gl = small[:, SSD_HEADS:SSD_HEADS + 3 * NSA_HEADS].reshape(batch, seq, 3, NSA_KV_HEADS, NSA_REP)
    gl = gl.transpose(0, 3, 2, 4, 1).reshape(batch, NSA_KV_HEADS, 3 * NSA_REP, seq)
    y_nsa = nsa_attention(qh, kcc, vcct, ksn, vsn, kwn, vwn, gl, batch, seq)
    merged = merge_mixers(y_ssd, y_nsa, w_up_ssd.astype(BF16), w_up_nsa.astype(BF16), mgates)
    h1 = residual_matmul(xf, merged, w_out.astype(BF16))
    mlen = mem.shape[1]
    mn = rmsnorm_rows(mem.reshape(batch * mlen, d), mem_norm_w, tm=min(512, batch * mlen))
    kv = matmul(mn, xkv_w.astype(BF16))
    n_exp = router_e_w.shape[1]
    rpad = LANES - N_GROUPS - n_exp
    router_w = jnp.concatenate([router_g_w, router_e_w, jnp.zeros((d, rpad), F32)], axis=1).astype(BF16)
    router_b = jnp.concatenate([router_g_b, router_e_b, jnp.zeros((rpad,), F32)]).reshape(1, LANES)
    h2, hf, logits = xattn_router(h1, kv, norm2_w, xq_w.astype(BF16), x_q_norm_w, x_k_norm_w, xo_w.astype(BF16),
                                  norm3_w, router_w, router_b, batch, seq)
    out = hier_moe(h2, hf, logits, moe_w_gate, moe_w_up, moe_w_down)
    return out.reshape(batch, seq, d)


def kernel(x, mem, norm1_w, w_in, ssd_conv_w, ssd_conv_b, ssd_dt_bias, ssd_a_log, ssd_d, ssd_norm_w, nsa_q_norm_w, nsa_k_norm_w, cmp_pe_k, cmp_w1_k, cmp_w2_k, cmp_pe_v, cmp_w1_v, cmp_w2_v, w_up_ssd, w_up_nsa, w_out, norm2_w, mem_norm_w, xq_w, xkv_w, x_q_norm_w, x_k_norm_w, xo_w, norm3_w, router_g_w, router_g_b, router_e_w, router_e_b, moe_w_gate, moe_w_up, moe_w_down):
    h = x
    for l in range(norm1_w.shape[0]):
        h = _layer(h, mem, norm1_w[l], w_in[l], ssd_conv_w[l], ssd_conv_b[l], ssd_dt_bias[l], ssd_a_log[l], ssd_d[l],
                   ssd_norm_w[l], nsa_q_norm_w[l], nsa_k_norm_w[l], cmp_pe_k[l], cmp_w1_k[l], cmp_w2_k[l], cmp_pe_v[l],
                   cmp_w1_v[l], cmp_w2_v[l], w_up_ssd[l], w_up_nsa[l], w_out[l], norm2_w[l], mem_norm_w[l], xq_w[l],
                   xkv_w[l], x_q_norm_w[l], x_k_norm_w[l], xo_w[l], norm3_w[l], router_g_w[l], router_g_b[l],
                   router_e_w[l], router_e_b[l], moe_w_gate[l], moe_w_up[l], moe_w_down[l])
    return h.astype(x.dtype)
```

```python
import functools

import numpy as np
import jax
import jax.numpy as jnp
from jax import lax
from jax.experimental import pallas as pl
from jax.experimental.pallas import tpu as pltpu

F32 = jnp.float32
BF16 = jnp.bfloat16

D_MODEL = 2048
SSD_INNER = 4096
SSD_HEAD_DIM = 64
SSD_HEADS = 64
SSD_GROUPS = 8
SSD_STATE = 128
SSD_CONV = 4
SSD_CHUNK = 256
HEADS_PER_GROUP = SSD_HEADS // SSD_GROUPS
GROUP_WIDTH = SSD_INNER // SSD_GROUPS
HEAD_DIM = 128
NSA_HEADS = 16
NSA_KV_HEADS = 4
NSA_REP = NSA_HEADS // NSA_KV_HEADS
NSA_WIDTH = NSA_HEADS * HEAD_DIM
NSA_KV_WIDTH = NSA_KV_HEADS * HEAD_DIM
CMP_BLK = 32
CMP_STRIDE = 16
CMP_HID = 256
SEL_BLK = 64
SEL_SHIFT = 6
N_SEL = 16
WINDOW = 512
Q_BLK = 128
SEL_KEY_TILE = 512
FORCE_SCORE = 1.0e4
X_HEADS = 4
X_WIDTH = X_HEADS * HEAD_DIM
N_GROUPS = 8
EXPERTS_PER_GROUP = 8
EPG_SHIFT = 3
N_EXPERTS = 64
TOP_K = 2
EXPERT_HIDDEN = 1408
MOE_SUB = 256
MOE_ITEM_SUBS = 4
MOE_HID_TILE = 256
ROPE_THETA = 10000.0
EPS = 1e-6
NEG_INF = -1e30
Q_PRESCALE = HEAD_DIM ** -0.5 * float(np.log2(np.e))
LANES = 128
VMEM_LIMIT = 56 * 1024 * 1024

ZX_COLS = SSD_INNER + SSD_INNER + 2 * SSD_GROUPS * SSD_STATE
QKV_COLS = NSA_WIDTH + 6 * NSA_KV_WIDTH
GATE_COLS = 2 * D_MODEL
SMALL_COLS = LANES


def _params(*sem):
    return pltpu.CompilerParams(dimension_semantics=sem, vmem_limit_bytes=VMEM_LIMIT)


def _rms(x, w):
    ms = jnp.mean(x * x, axis=-1, keepdims=True)
    return x * lax.rsqrt(ms + EPS) * w


def _sigmoid(x):
    return 1.0 / (1.0 + jnp.exp(-x))


def _silu(x):
    return x * _sigmoid(x)


def _softplus(x):
    return jnp.maximum(x, 0.0) + jnp.log1p(jnp.exp(-jnp.abs(x)))


def _dot(a, b):
    return jnp.dot(a, b, preferred_element_type=F32)


def _dot_nt(a, b):
    return lax.dot_general(a, b, (((1,), (1,)), ((), ())), preferred_element_type=F32)


def _masked_softmax(s, mask):
    s = jnp.where(mask, s, NEG_INF)
    e = jnp.exp(s - jnp.max(s, axis=-1, keepdims=True))
    p = e / jnp.sum(e, axis=-1, keepdims=True)
    return jnp.where(mask, p, 0.0)


def _rmsnorm_kernel(x_ref, w_ref, o_ref):
    o_ref[...] = _rms(x_ref[...], w_ref[...]).astype(o_ref.dtype)


def rmsnorm_rows(x, w, tm=512):
    m, d = x.shape
    return pl.pallas_call(
        _rmsnorm_kernel,
        grid=(m // tm,),
        in_specs=[pl.BlockSpec((tm, d), lambda i: (i, 0)), pl.BlockSpec((1, d), lambda i: (0, 0))],
        out_specs=pl.BlockSpec((tm, d), lambda i: (i, 0)),
        out_shape=jax.ShapeDtypeStruct((m, d), BF16),
        compiler_params=_params("parallel"),
        name="rmsnorm_rows",
    )(x, w.reshape(1, d))


def _matmul_kernel(x_ref, w_ref, o_ref):
    o_ref[...] = _dot(x_ref[...], w_ref[...]).astype(o_ref.dtype)


def matmul(x, w, out_dtype=F32, tm=512, tn=512):
    m, k = x.shape
    n = w.shape[1]
    tm, tn = min(tm, m), min(tn, n)
    return pl.pallas_call(
        _matmul_kernel,
        grid=(m // tm, n // tn),
        in_specs=[pl.BlockSpec((tm, k), lambda i, j: (i, 0)), pl.BlockSpec((k, tn), lambda i, j: (0, j))],
        out_specs=pl.BlockSpec((tm, tn), lambda i, j: (i, j)),
        out_shape=jax.ShapeDtypeStruct((m, n), out_dtype),
        compiler_params=_params("parallel", "arbitrary"),
        name="matmul",
    )(x, w)


def _matmul_wcast_kernel(x_ref, w_ref, *rest, shift):
    if shift:
        wx_ref, o_ref, wb_ref = rest
    else:
        o_ref, wb_ref = rest

    @pl.when(pl.program_id(1) == 0)
    def _():
        if shift:
            tn = w_ref.shape[1]
            w = jnp.concatenate([w_ref[...], wx_ref[...]], axis=1)[:, shift:shift + tn]
        else:
            w = w_ref[...]
        wb_ref[...] = w.astype(BF16)

    o_ref[...] = _dot(x_ref[...], wb_ref[...]).astype(o_ref.dtype)


def matmul_wcast(x, w, col0, ncols, out_dtype=F32, tm=1024, tn=512):
    m, k = x.shape
    tn = min(tn, ncols)
    shift = col0 % tn
    base = (col0 - shift) // tn
    assert ncols % tn == 0 and shift <= LANES
    in_specs = [pl.BlockSpec((tm, k), lambda j, i: (i, 0)), pl.BlockSpec((k, tn), lambda j, i: (0, base + j))]
    args = [x, w]
    if shift:
        per = tn // LANES
        in_specs.append(pl.BlockSpec((k, LANES), lambda j, i: (0, (base + j + 1) * per)))
        args.append(w)
    return pl.pallas_call(
        functools.partial(_matmul_wcast_kernel, shift=shift),
        grid=(ncols // tn, m // tm),
        in_specs=in_specs,
        out_specs=pl.BlockSpec((tm, tn), lambda j, i: (i, j)),
        out_shape=jax.ShapeDtypeStruct((m, ncols), out_dtype),
        scratch_shapes=[pltpu.VMEM((k, tn), BF16)],
        compiler_params=_params("parallel", "arbitrary"),
        name="matmul_wcast",
    )(*args)


def _split3(a):
    h = a.astype(BF16)
    r = a - h.astype(F32)
    m = r.astype(BF16)
    return h, m, (r - m.astype(F32)).astype(BF16)


def _ssd_kernel(z_ref, x_ref, b_ref, c_ref, dt_ref, dtt_ref, cwx_ref, cwb_ref, cwc_ref, cbx_ref, cbb_ref, cbc_ref,
                hp_ref, hpt_ref, nw_ref, y_ref, state_ref, ext_x_ref, ext_b_ref, ext_c_ref):
    L = x_ref.shape[0]
    H = L // 2
    chunk = pl.program_id(2)

    @pl.when(chunk == 0)
    def _():
        state_ref[...] = jnp.zeros_like(state_ref)
        ext_x_ref[...] = jnp.zeros_like(ext_x_ref)
        ext_b_ref[...] = jnp.zeros_like(ext_b_ref)
        ext_c_ref[...] = jnp.zeros_like(ext_c_ref)

    def conv_silu(u_ref, ext_ref, w, b):
        ext_ref[0:8, :] = ext_ref[L:L + 8, :]
        u = u_ref[...]
        ext_ref[8:8 + L, :] = u
        acc = b + w[SSD_CONV - 1:SSD_CONV, :] * u
        for k in range(SSD_CONV - 1):
            acc = acc + w[k:k + 1, :] * ext_ref[pl.ds(8 - (SSD_CONV - 1) + k, L), :]
        return _silu(acc)

    xs = conv_silu(x_ref, ext_x_ref, cwx_ref[...], cbx_ref[...])
    bm = conv_silu(b_ref, ext_b_ref, cwb_ref[...], cbb_ref[...])
    cm = conv_silu(c_ref, ext_c_ref, cwc_ref[...], cbc_ref[...])
    bm16, cm16 = bm.astype(BF16), cm.astype(BF16)

    hp = hp_ref[0]
    hpt = hpt_ref[0]
    dt = _softplus(dt_ref[0] + hp[0:1, :])
    dtt = _softplus(dtt_ref[0] + hpt[:, 0:1])
    da = dt * (-jnp.exp(hp[1:2, :]))
    dat = dtt * (-jnp.exp(hpt[:, 1:2]))
    row = lax.broadcasted_iota(jnp.int32, (L, L), 0)
    col = lax.broadcasted_iota(jnp.int32, (L, L), 1)
    tril = jnp.where(row >= col, 1.0, 0.0).astype(BF16)
    triu = jnp.where(row <= col, 1.0, 0.0).astype(BF16)
    acum = sum(_dot(tril, part) for part in _split3(da))
    acumt = sum(_dot(part, triu) for part in _split3(dat))
    last = acum[L - 1:L, :]
    dte = jnp.exp(last - acum)
    eac = jnp.exp(acum)
    cdec = jnp.exp(last)

    cb = _dot_nt(cm16, bm16)
    cb00, cb10, cb11 = cb[:H, :H], cb[H:, :H], cb[H:, H:]
    tri = lax.broadcasted_iota(jnp.int32, (H, H), 0) >= lax.broadcasted_iota(jnp.int32, (H, H), 1)
    lane = lax.broadcasted_iota(jnp.int32, (L, LANES), 1)
    first = lane < SSD_HEAD_DIM
    srow = lax.broadcasted_iota(jnp.int32, (LANES, SSD_STATE), 0) < SSD_HEAD_DIM
    z = z_ref[...]
    d_skip = hp[2:3, :]
    outs = []
    for p in range(HEADS_PER_GROUP // 2):
        h0, h1 = 2 * p, 2 * p + 1

        def pick(a, rows=first):
            return jnp.where(rows, a[:, h0:h0 + 1], a[:, h1:h1 + 1])

        xp = xs[:, p * LANES:(p + 1) * LANES]
        xdt = xp * pick(dt)
        xdt16 = xdt.astype(BF16)
        y_top = jnp.zeros((H, LANES), F32)
        y_bot = jnp.zeros((H, LANES), F32)
        for h, msk in ((h0, first), (h1, jnp.logical_not(first))):
            a_col, a_row = acum[:, h:h + 1], acumt[h:h + 1, :]
            d00 = jnp.exp(jnp.where(tri, a_col[:H] - a_row[:, :H], -jnp.inf))
            d10 = jnp.exp(a_col[H:] - a_row[:, :H])
            d11 = jnp.exp(jnp.where(tri, a_col[H:] - a_row[:, H:], -jnp.inf))
            xh = jnp.where(msk, xdt16, jnp.zeros_like(xdt16))
            y_top = y_top + _dot((cb00 * d00).astype(BF16), xh[:H])
            y_bot = y_bot + _dot((cb10 * d10).astype(BF16), xh[:H]) + _dot((cb11 * d11).astype(BF16), xh[H:])
        y = jnp.concatenate([y_top, y_bot], axis=0)
        hprev = state_ref[p]
        y = y + _dot_nt(cm16, hprev.astype(BF16)) * pick(eac)
        xe_t = (xdt * pick(dte)).T.astype(BF16)
        state_ref[p] = hprev * jnp.where(srow, cdec[:, h0:h0 + 1], cdec[:, h1:h1 + 1]) + _dot(xe_t, bm16)
        outs.append(y + pick(d_skip, first[0:1, :]) * xp)
    y = jnp.concatenate(outs, axis=1) * _silu(z)
    y_ref[...] = _rms(y, nw_ref[...]).astype(y_ref.dtype)


def ssd_branch(zx, small, conv_w, conv_b, dt_bias, a_log, d_skip, norm_w, batch, seq):
    n = batch * seq
    L = min(SSD_CHUNK, seq)
    nc = seq // L
    g, hpg = SSD_GROUPS, HEADS_PER_GROUP
    dt_raw = small[:, :SSD_HEADS].reshape(n, g, hpg)
    dt_g = dt_raw.transpose(1, 0, 2)
    dtt_g = dt_raw.transpose(1, 2, 0)
    hp = jnp.stack([dt_bias, a_log, d_skip], axis=0).reshape(3, g, hpg).transpose(1, 0, 2)
    hp = jnp.pad(hp, ((0, 0), (0, 8 - 3), (0, 0)))
    hpt = hp.transpose(0, 2, 1)
    cw = conv_w
    cbias = conv_b.reshape(1, -1)
    nxb = SSD_INNER // GROUP_WIDTH
    rowblk = lambda b, gi, c: b * nc + c
    bc0 = SSD_INNER // SSD_STATE
    grid = (batch, g, nc)
    in_specs = [
        pl.BlockSpec((L, GROUP_WIDTH), lambda b, gi, c: (rowblk(b, gi, c), gi)),
        pl.BlockSpec((L, GROUP_WIDTH), lambda b, gi, c: (rowblk(b, gi, c), nxb + gi)),
        pl.BlockSpec((L, SSD_STATE), lambda b, gi, c: (rowblk(b, gi, c), 2 * bc0 + gi)),
        pl.BlockSpec((L, SSD_STATE), lambda b, gi, c: (rowblk(b, gi, c), 2 * bc0 + g + gi)),
        pl.BlockSpec((1, L, hpg), lambda b, gi, c: (gi, rowblk(b, gi, c), 0)),
        pl.BlockSpec((1, hpg, L), lambda b, gi, c: (gi, 0, rowblk(b, gi, c))),
        pl.BlockSpec((SSD_CONV, GROUP_WIDTH), lambda b, gi, c: (0, gi)),
        pl.BlockSpec((SSD_CONV, SSD_STATE), lambda b, gi, c: (0, bc0 + gi)),
        pl.BlockSpec((SSD_CONV, SSD_STATE), lambda b, gi, c: (0, bc0 + g + gi)),
        pl.BlockSpec((1, GROUP_WIDTH), lambda b, gi, c: (0, gi)),
        pl.BlockSpec((1, SSD_STATE), lambda b, gi, c: (0, bc0 + gi)),
        pl.BlockSpec((1, SSD_STATE), lambda b, gi, c: (0, bc0 + g + gi)),
        pl.BlockSpec((1, 8, hpg), lambda b, gi, c: (gi, 0, 0)),
        pl.BlockSpec((1, hpg, 8), lambda b, gi, c: (gi, 0, 0)),
        pl.BlockSpec((1, GROUP_WIDTH), lambda b, gi, c: (0, gi)),
    ]
    return pl.pallas_call(
        _ssd_kernel,
        grid=grid,
        in_specs=in_specs,
        out_specs=pl.BlockSpec((L, GROUP_WIDTH), lambda b, gi, c: (rowblk(b, gi, c), gi)),
        out_shape=jax.ShapeDtypeStruct((n, SSD_INNER), BF16),
        scratch_shapes=[pltpu.VMEM((HEADS_PER_GROUP // 2, LANES, SSD_STATE), F32),
                        pltpu.VMEM((L + 8, GROUP_WIDTH), F32), pltpu.VMEM((L + 8, SSD_STATE), F32),
                        pltpu.VMEM((L + 8, SSD_STATE), F32)],
        compiler_params=_params("parallel", "parallel", "arbitrary"),
        name="ssd_branch",
    )(zx, zx, zx, zx, dt_g, dtt_g, cw, cw, cw, cbias, cbias, cbias, hp, hpt, norm_w.reshape(1, -1))


def _nsa_prep_kernel(q_ref, kc_ref, vc_ref, ks_ref, vs_ref, kw_ref, vw_ref, cos_ref, sin_ref, qw_ref, kw3_ref,
                     qo_ref, kco_ref, vco_ref, kso_ref, vso_ref, kwo_ref, vwo_ref):
    cos, sin = cos_ref[...], sin_ref[...]

    def norm_rope(x, w):
        y = _rms(x, w)
        return y * cos + pltpu.roll(y, HEAD_DIM // 2, 1) * sin

    qw = qw_ref[...]
    tt = q_ref.shape[0]
    for g in range(NSA_KV_HEADS):
        sl = slice(g * HEAD_DIM, (g + 1) * HEAD_DIM)
        for r in range(NSA_REP):
            h = g * NSA_REP + r
            qh = norm_rope(q_ref[:, h * HEAD_DIM:(h + 1) * HEAD_DIM], qw) * Q_PRESCALE
            qo_ref[0, g, r] = qh.astype(qo_ref.dtype)
        kco_ref[0, g] = norm_rope(kc_ref[:, sl], kw3_ref[0:1, :])
        vco_ref[0, g] = vc_ref[:, sl]
        kso_ref[0, g] = norm_rope(ks_ref[:, sl], kw3_ref[1:2, :]).astype(kso_ref.dtype)
        kwo_ref[0, g] = norm_rope(kw_ref[:, sl], kw3_ref[2:3, :]).astype(kwo_ref.dtype)
        vst = vs_ref[:, sl].T.astype(vso_ref.dtype)
        for j in range(tt // SEL_KEY_TILE):
            vso_ref[0, g, j] = vst[:, j * SEL_KEY_TILE:(j + 1) * SEL_KEY_TILE]
        vwt = vw_ref[:, sl].T.astype(vwo_ref.dtype)
        for j in range(tt // Q_BLK):
            vwo_ref[0, g, j] = vwt[:, j * Q_BLK:(j + 1) * Q_BLK]


def nsa_prep(qkv, q_norm_w, k_norm_w, batch, seq, tt=SEL_KEY_TILE):
    half = HEAD_DIM // 2
    inv = ROPE_THETA ** (-jnp.arange(half, dtype=F32) / half)
    ang = jnp.arange(seq).astype(F32)[:, None] * inv[None, :]
    cos = jnp.concatenate([jnp.cos(ang), jnp.cos(ang)], axis=-1)
    sin = jnp.concatenate([-jnp.sin(ang), jnp.sin(ang)], axis=-1)
    nt = seq // tt
    kvb = NSA_WIDTH // NSA_KV_WIDTH
    kv_spec = lambda j: pl.BlockSpec((tt, NSA_KV_WIDTH), lambda b, t: (b * nt + t, kvb + j))
    head_spec = pl.BlockSpec((1, NSA_KV_HEADS, tt, HEAD_DIM), lambda b, t: (b, 0, t, 0))
    head_shape = lambda dt: jax.ShapeDtypeStruct((batch, NSA_KV_HEADS, seq, HEAD_DIM), dt)
    return pl.pallas_call(
        _nsa_prep_kernel,
        grid=(batch, nt),
        in_specs=[pl.BlockSpec((tt, NSA_WIDTH), lambda b, t: (b * nt + t, 0))] + [kv_spec(j) for j in range(6)] + [
            pl.BlockSpec((tt, HEAD_DIM), lambda b, t: (t, 0)), pl.BlockSpec((tt, HEAD_DIM), lambda b, t: (t, 0)),
            pl.BlockSpec((1, HEAD_DIM), lambda b, t: (0, 0)), pl.BlockSpec((3, HEAD_DIM), lambda b, t: (0, 0))],
        out_specs=[pl.BlockSpec((1, NSA_KV_HEADS, NSA_REP, tt, HEAD_DIM), lambda b, t: (b, 0, 0, t, 0)),
                   head_spec, head_spec, head_spec,
                   pl.BlockSpec((1, NSA_KV_HEADS, tt // SEL_KEY_TILE, HEAD_DIM, SEL_KEY_TILE),
                                lambda b, t: (b, 0, t, 0, 0)),
                   head_spec,
                   pl.BlockSpec((1, NSA_KV_HEADS, tt // Q_BLK, HEAD_DIM, Q_BLK), lambda b, t: (b, 0, t, 0, 0))],
        out_shape=[jax.ShapeDtypeStruct((batch, NSA_KV_HEADS, NSA_REP, seq, HEAD_DIM), BF16),
                   head_shape(F32), head_shape(F32), head_shape(BF16),
                   jax.ShapeDtypeStruct((batch, NSA_KV_HEADS, seq // SEL_KEY_TILE, HEAD_DIM, SEL_KEY_TILE), BF16),
                   head_shape(BF16),
                   jax.ShapeDtypeStruct((batch, NSA_KV_HEADS, seq // Q_BLK, HEAD_DIM, Q_BLK), BF16)],
        compiler_params=_params("parallel", "parallel"),
        name="nsa_prep",
    )(qkv, qkv, qkv, qkv, qkv, qkv, qkv, cos, sin, q_norm_w.reshape(1, -1), k_norm_w)


def _compress_kernel(x_ref, pe_ref, w1_ref, w2_ref, o_ref, *, ncmp, transpose_out):
    nseg = x_ref.shape[2] // CMP_STRIDE
    u = jnp.zeros((nseg, CMP_HID), F32)
    v = jnp.zeros((nseg, CMP_HID), F32)
    for j in range(CMP_STRIDE):
        xj = x_ref[0, 0, pl.ds(j, nseg, stride=CMP_STRIDE), :]
        u = u + _dot((xj + pe_ref[j:j + 1, :]).astype(BF16), w1_ref[j])
        v = v + _dot((xj + pe_ref[CMP_STRIDE + j:CMP_STRIDE + j + 1, :]).astype(BF16), w1_ref[CMP_STRIDE + j])
    hid = _silu(u + pltpu.roll(v, nseg - 1, 0))
    comp = _dot(hid.astype(BF16), w2_ref[...])
    rowi = lax.broadcasted_iota(jnp.int32, comp.shape, 0)
    comp = jnp.where(rowi < ncmp, comp, 0.0)
    o_ref[0, 0] = (comp.T if transpose_out else comp).astype(o_ref.dtype)


def compress(raw, pe, w1, w2, transpose_out):
    b, g, t, hd = raw.shape
    nseg = t // CMP_STRIDE
    full = lambda shape: pl.BlockSpec(shape, lambda i, j: (0,) * len(shape))
    out_dims = (hd, nseg) if transpose_out else (nseg, hd)
    return pl.pallas_call(
        functools.partial(_compress_kernel, ncmp=nseg - 1, transpose_out=transpose_out),
        grid=(b, g),
        in_specs=[pl.BlockSpec((1, 1, t, hd), lambda i, j: (i, j, 0, 0)), full((CMP_BLK, hd)),
                  full((CMP_BLK, hd, CMP_HID)), full((CMP_HID, hd))],
        out_specs=pl.BlockSpec((1, 1) + out_dims, lambda i, j: (i, j, 0, 0)),
        out_shape=jax.ShapeDtypeStruct((b, g) + out_dims, BF16),
        compiler_params=_params("parallel", "parallel"),
        name="nsa_compress",
    )(raw, pe, w1.astype(BF16), w2.astype(BF16))


def _nsa_attn_kernel(q_ref, kc_ref, vct_ref, ks_ref, vst_ref, kw_ref, vwt_ref, glt_ref, ovt_ref, o_ref,
                     selb_ref, acc_ref, ocw_ref, *, seq, n_sel):
    qb = pl.program_id(2)
    q0 = qb * Q_BLK
    rows = NSA_REP * Q_BLK
    q = q_ref[0, 0].reshape(rows, HEAD_DIM)
    tq = q0 + lax.broadcasted_iota(jnp.int32, (1, Q_BLK), 1)
    head = lambda r: slice(r * Q_BLK, (r + 1) * Q_BLK)

    wk = WINDOW + Q_BLK
    start = pl.multiple_of(jnp.clip(q0 - WINDOW, 0, seq - wk), Q_BLK)
    s_w = _dot_nt(kw_ref[0, 0, pl.ds(start, wk), :], q)

    ncp = kc_ref.shape[2]
    s_c = _dot_nt(kc_ref[0, 0], q)
    cend = lax.broadcasted_iota(jnp.int32, (ncp, 1), 0) * CMP_STRIDE + (CMP_BLK - 1)
    m_c = cend <= tq
    ps = []
    for r in range(NSA_REP):
        s = jnp.where(m_c, s_c[:, head(r)], NEG_INF)
        e = jnp.exp2(s - jnp.max(s, axis=0, keepdims=True))
        p = jnp.where(m_c, e * (1.0 / jnp.sum(e, axis=0, keepdims=True)), 0.0)
        ps.append(p.astype(BF16))
    p_c = jnp.concatenate(ps, axis=1)
    o_c = _dot(vct_ref[0, 0], p_c)

    nsb = ovt_ref.shape[0]
    imp4 = _dot(ovt_ref[...], p_c)
    imp = imp4[:, head(0)]
    for r in range(1, NSA_REP):
        imp = imp + imp4[:, head(r)]
    blk_t = (q0 + lax.broadcasted_iota(jnp.int32, (nsb, Q_BLK), 1)) >> SEL_SHIFT
    sb = lax.broadcasted_iota(jnp.int32, (nsb, Q_BLK), 0)
    forced = (sb == 0) | (sb == blk_t) | (sb == blk_t - 1)
    imp = jnp.where(sb <= blk_t, jnp.where(forced, FORCE_SCORE, imp), -jnp.inf)
    rank = jnp.zeros((nsb, Q_BLK), F32)
    for i in range(nsb):
        ri = imp[i:i + 1, :]
        beats = (ri > imp) | ((ri == imp) & (sb > i))
        rank = rank + jnp.where(beats, 1.0, 0.0)
    selb_ref[...] = jnp.where(rank < n_sel, 0.0, NEG_INF)

    dist = tq - (start + lax.broadcasted_iota(jnp.int32, (wk, 1), 0))
    bias_w = jnp.where((dist >= 0) & (dist < WINDOW), 0.0, NEG_INF)
    ps, ls = [], []
    for r in range(NSA_REP):
        s = s_w[:, head(r)] + bias_w
        p = jnp.exp2(s - jnp.max(s, axis=0, keepdims=True))
        ls.append(jnp.sum(p, axis=0, keepdims=True))
        ps.append(p.astype(BF16))
    t0 = start // Q_BLK
    vw_t = jnp.concatenate([vwt_ref[0, 0, t0 + j] for j in range(wk // Q_BLK)], axis=1)
    o_w = _dot(vw_t, jnp.concatenate(ps, axis=1)) * (1.0 / jnp.concatenate(ls, axis=1))

    gate = _sigmoid(glt_ref[0, 0])
    for r in range(NSA_REP):
        ocw_ref[:, head(r)] = (gate[r:r + 1, :] * o_c[:, head(r)]
                               + gate[2 * NSA_REP + r:2 * NSA_REP + r + 1, :] * o_w[:, head(r)])

    kt_sz = vst_ref.shape[4]
    blocks_per_tile = kt_sz // SEL_BLK
    acc_ref[...] = jnp.zeros_like(acc_ref)

    def sel_tile(kt, carry):
        m, l = carry
        k0 = pl.multiple_of(kt * kt_sz, kt_sz)
        s_all = _dot_nt(ks_ref[0, 0, pl.ds(k0, kt_sz), :], q)
        kpos = k0 + lax.broadcasted_iota(jnp.int32, (kt_sz, 1), 0)
        bias = jnp.concatenate(
            [jnp.broadcast_to(selb_ref[pl.ds(kt * blocks_per_tile + i, 1), :], (SEL_BLK, Q_BLK))
             for i in range(blocks_per_tile)], axis=0)
        bias = jnp.where(kpos <= tq, bias, NEG_INF)
        ps, ms, ls, alphas = [], [], [], []
        for r in range(NSA_REP):
            s = s_all[:, head(r)] + bias
            m_old = m[:, head(r)]
            m_new = jnp.maximum(m_old, jnp.max(s, axis=0, keepdims=True))
            p = jnp.exp2(s - m_new)
            alpha = jnp.exp2(m_old - m_new)
            ls.append(alpha * l[:, head(r)] + jnp.sum(p, axis=0, keepdims=True))
            ms.append(m_new)
            alphas.append(alpha)
            ps.append(p.astype(BF16))
        pv = _dot(vst_ref[0, 0, kt], jnp.concatenate(ps, axis=1))
        acc_ref[...] = acc_ref[...] * jnp.concatenate(alphas, axis=1) + pv
        return jnp.concatenate(ms, axis=1), jnp.concatenate(ls, axis=1)

    n_tiles = (q0 + Q_BLK + kt_sz - 1) // kt_sz
    init = (jnp.full((1, rows), NEG_INF, F32), jnp.zeros((1, rows), F32))
    _, l_s = lax.fori_loop(0, n_tiles, sel_tile, init)
    o_s = acc_ref[...] * (1.0 / l_s)

    for r in range(NSA_REP):
        out_t = ocw_ref[:, head(r)] + gate[NSA_REP + r:NSA_REP + r + 1, :] * o_s[:, head(r)]
        o_ref[:, r * HEAD_DIM:(r + 1) * HEAD_DIM] = out_t.T.astype(o_ref.dtype)


def nsa_attention(qh, kcc, vcct, ksn, vst, kwn, vwt, gate_logits_t, batch, seq):
    assert seq % SEL_KEY_TILE == 0 and seq >= WINDOW + Q_BLK
    nq = seq // Q_BLK
    nsb = seq // SEL_BLK
    n_sel = min(N_SEL, nsb)
    ncp = kcc.shape[2]
    ncmp = (seq - CMP_BLK) // CMP_STRIDE + 1
    ci = np.arange(ncp)[None, :]
    sj = np.arange(nsb)[:, None]
    ovt = ((ci * CMP_STRIDE < (sj + 1) * SEL_BLK) & (ci * CMP_STRIDE + CMP_BLK > sj * SEL_BLK) & (ci < ncmp))
    ovt = jnp.asarray(ovt, BF16)
    seq_spec = pl.BlockSpec((1, 1, seq, HEAD_DIM), lambda b, g, i: (b, g, 0, 0))
    cmp_spec = pl.BlockSpec((1, 1, ncp, HEAD_DIM), lambda b, g, i: (b, g, 0, 0))
    tiled = lambda a: pl.BlockSpec((1, 1) + a.shape[2:], lambda b, g, i: (b, g, 0, 0, 0))
    return pl.pallas_call(
        functools.partial(_nsa_attn_kernel, seq=seq, n_sel=n_sel),
        grid=(batch, NSA_KV_HEADS, nq),
        in_specs=[pl.BlockSpec((1, 1, NSA_REP, Q_BLK, HEAD_DIM), lambda b, g, i: (b, g, 0, i, 0)),
                  cmp_spec, pl.BlockSpec((1, 1, HEAD_DIM, ncp), lambda b, g, i: (b, g, 0, 0)),
                  seq_spec, tiled(vst), seq_spec, tiled(vwt),
                  pl.BlockSpec((1, 1, 3 * NSA_REP, Q_BLK), lambda b, g, i: (b, g, 0, i)),
                  pl.BlockSpec((nsb, ncp), lambda b, g, i: (0, 0))],
        out_specs=pl.BlockSpec((Q_BLK, NSA_REP * HEAD_DIM), lambda b, g, i: (b * nq + i, g)),
        out_shape=jax.ShapeDtypeStruct((batch * seq, NSA_WIDTH), BF16),
        scratch_shapes=[pltpu.VMEM((nsb, Q_BLK), F32), pltpu.VMEM((HEAD_DIM, NSA_REP * Q_BLK), F32),
                        pltpu.VMEM((HEAD_DIM, NSA_REP * Q_BLK), F32)],
        compiler_params=_params("parallel", "parallel", "arbitrary"),
        name="nsa_attention",
    )(qh, kcc, vcct, ksn, vst, kwn, vwt, gate_logits_t, ovt)


def _merge_kernel(ys_ref, yn_ref, ws_ref, wn_ref, gs_ref, gn_ref, o_ref):
    up_s = _dot(ys_ref[...], ws_ref[...])
    up_n = _dot(yn_ref[...], wn_ref[...])
    o_ref[...] = (_sigmoid(gs_ref[...]) * up_s + _sigmoid(gn_ref[...]) * up_n).astype(o_ref.dtype)


def merge_mixers(y_ssd, y_nsa, w_up_ssd, w_up_nsa, gates, tm=512, tn=512):
    m = y_ssd.shape[0]
    nb = D_MODEL // tn
    return pl.pallas_call(
        _merge_kernel,
        grid=(m // tm, nb),
        in_specs=[pl.BlockSpec((tm, SSD_INNER), lambda i, j: (i, 0)), pl.BlockSpec((tm, NSA_WIDTH), lambda i, j: (i, 0)),
                  pl.BlockSpec((SSD_INNER, tn), lambda i, j: (0, j)), pl.BlockSpec((NSA_WIDTH, tn), lambda i, j: (0, j)),
                  pl.BlockSpec((tm, tn), lambda i, j: (i, j)), pl.BlockSpec((tm, tn), lambda i, j: (i, nb + j))],
        out_specs=pl.BlockSpec((tm, tn), lambda i, j: (i, j)),
        out_shape=jax.ShapeDtypeStruct((m, D_MODEL), BF16),
        compiler_params=_params("parallel", "arbitrary"),
        name="merge_mixers",
    )(y_ssd, y_nsa, w_up_ssd, w_up_nsa, gates, gates)


def _residual_matmul_kernel(h_ref, x_ref, w_ref, o_ref):
    o_ref[...] = h_ref[...] + _dot(x_ref[...], w_ref[...])


def residual_matmul(h, x, w, tm=1024, tn=1024):
    m, k = x.shape
    n = w.shape[1]
    return pl.pallas_call(
        _residual_matmul_kernel,
        grid=(m // tm, n // tn),
        in_specs=[pl.BlockSpec((tm, tn), lambda i, j: (i, j)), pl.BlockSpec((tm, k), lambda i, j: (i, 0)),
                  pl.BlockSpec((k, tn), lambda i, j: (0, j))],
        out_specs=pl.BlockSpec((tm, tn), lambda i, j: (i, j)),
        out_shape=jax.ShapeDtypeStruct((m, n), F32),
        compiler_params=_params("parallel", "arbitrary"),
        name="residual_matmul",
    )(h, x, w)


def _xattn_router_kernel(h_ref, kv_ref, n2_ref, wq_ref, qn_ref, kn_ref, wo_ref, n3_ref, rw_ref, rb_ref,
                         h2_ref, hf_ref, rt_ref):
    h = h_ref[...]
    qp = _dot(_rms(h, n2_ref[...]).astype(BF16), wq_ref[...])
    kv = kv_ref[...]
    scale = HEAD_DIM ** -0.5
    heads = []
    for hd in range(X_HEADS):
        sl = slice(hd * HEAD_DIM, (hd + 1) * HEAD_DIM)
        qh = _rms(qp[:, sl], qn_ref[...]).astype(BF16)
        kh = _rms(kv[:, sl], kn_ref[...]).astype(BF16)
        vh = kv[:, X_WIDTH + hd * HEAD_DIM:X_WIDTH + (hd + 1) * HEAD_DIM].astype(BF16)
        s = _dot_nt(qh, kh) * scale
        e = jnp.exp(s - jnp.max(s, axis=-1, keepdims=True))
        p = e / jnp.sum(e, axis=-1, keepdims=True)
        heads.append(_dot(p.astype(BF16), vh))
    o = jnp.concatenate(heads, axis=1).astype(BF16)
    h2 = h + _dot(o, wo_ref[...])
    h2_ref[...] = h2
    hf = _rms(h2, n3_ref[...]).astype(BF16)
    hf_ref[...] = hf
    lg = _dot(hf, rw_ref[...]) + rb_ref[...]

    lane = lax.broadcasted_iota(jnp.int32, lg.shape, 1)
    rmax = lambda v: jnp.max(v, axis=-1, keepdims=True)
    rsum = lambda v: jnp.sum(v, axis=-1, keepdims=True)
    first_lane = lambda hit: jnp.min(jnp.where(hit, lane, LANES), axis=-1, keepdims=True)
    is_g = lane < N_GROUPS
    eg = jnp.where(is_g, jnp.exp(lg - rmax(jnp.where(is_g, lg, -jnp.inf))), 0.0)
    pg = eg / rsum(eg)
    pg_top = rmax(pg)
    grp = first_lane(is_g & (pg == pg_top))
    in_grp = (lane >= N_GROUPS) & (((lane - N_GROUPS) >> EPG_SHIFT) == grp)
    ee = jnp.where(in_grp, jnp.exp(lg - rmax(jnp.where(in_grp, lg, -jnp.inf))), 0.0)
    pe = jnp.where(in_grp, ee / rsum(ee), -1.0)
    p1 = rmax(pe)
    i1 = first_lane(pe == p1)
    pe2 = jnp.where(lane == i1, -1.0, pe)
    p2 = rmax(pe2)
    i2 = first_lane(pe2 == p2)
    psum = p1 + p2
    route = jnp.where(lane == 0, (i1 - N_GROUPS).astype(F32),
                      jnp.where(lane == 1, (i2 - N_GROUPS).astype(F32),
                                jnp.where(lane == 2, pg_top * p1 / psum,
                                          jnp.where(lane == 3, pg_top * p2 / psum, 0.0))))
    rt_ref[...] = route


def xattn_router(h1, kv, norm2_w, wq, q_norm_w, k_norm_w, wo, norm3_w, router_w, router_b, batch, seq, tm=512):
    n = batch * seq
    tm = min(tm, seq)
    per_seq = seq // tm
    mlen = kv.shape[0] // batch
    full = lambda shape: pl.BlockSpec(shape, lambda i: (0,) * len(shape))
    return pl.pallas_call(
        _xattn_router_kernel,
        grid=(n // tm,),
        in_specs=[pl.BlockSpec((tm, D_MODEL), lambda i: (i, 0)),
                  pl.BlockSpec((mlen, 2 * X_WIDTH), lambda i: (i // per_seq, 0)),
                  full((1, D_MODEL)), full((D_MODEL, X_WIDTH)), full((1, HEAD_DIM)), full((1, HEAD_DIM)),
                  full((X_WIDTH, D_MODEL)), full((1, D_MODEL)), full((D_MODEL, LANES)), full((1, LANES))],
        out_specs=[pl.BlockSpec((tm, D_MODEL), lambda i: (i, 0)), pl.BlockSpec((tm, D_MODEL), lambda i: (i, 0)),
                   pl.BlockSpec((tm, LANES), lambda i: (i, 0))],
        out_shape=[jax.ShapeDtypeStruct((n, D_MODEL), F32), jax.ShapeDtypeStruct((n, D_MODEL), BF16),
                   jax.ShapeDtypeStruct((n, LANES), F32)],
        compiler_params=_params("parallel"),
        name="xattn_router",
    )(h1, kv, norm2_w.reshape(1, -1), wq, q_norm_w.reshape(1, -1), k_norm_w.reshape(1, -1), wo,
      norm3_w.reshape(1, -1), router_w, router_b)


def _moe_rank_kernel(e_ref, rank_ref, cnt_ref, carry_ref):
    @pl.when(pl.program_id(0) == 0)
    def _():
        carry_ref[...] = jnp.zeros_like(carry_ref)

    t = e_ref.shape[0]
    hit = e_ref[...] == lax.broadcasted_iota(jnp.int32, (t, LANES), 1)
    onehot = jnp.where(hit, 1.0, 0.0).astype(BF16)
    earlier = lax.broadcasted_iota(jnp.int32, (t, t), 0) > lax.broadcasted_iota(jnp.int32, (t, t), 1)
    before = _dot(jnp.where(earlier, 1.0, 0.0).astype(BF16), onehot) + carry_ref[...]
    rank_ref[...] = jnp.sum(jnp.where(hit, before, 0.0), axis=-1, keepdims=True).astype(jnp.int32)
    carry_ref[...] += jnp.sum(jnp.where(hit, 1.0, 0.0), axis=0, keepdims=True)
    cnt_ref[...] = carry_ref[...]


def moe_rank(eid, t=512):
    a = eid.shape[0]
    return pl.pallas_call(
        _moe_rank_kernel,
        grid=(a // t,),
        in_specs=[pl.BlockSpec((t, 1), lambda i: (i, 0))],
        out_specs=[pl.BlockSpec((t, 1), lambda i: (i, 0)), pl.BlockSpec((1, LANES), lambda i: (0, 0))],
        out_shape=[jax.ShapeDtypeStruct((a, 1), jnp.int32), jax.ShapeDtypeStruct((1, LANES), F32)],
        scratch_shapes=[pltpu.VMEM((1, LANES), F32)],
        compiler_params=_params("arbitrary"),
        name="moe_rank",
    )(eid)


def _moe_ffn_kernel(item_e_ref, item_row_ref, item_nsub_ref, item_nout_ref, x_hbm, wg_ref, wu_ref, wd_ref, y_hbm,
                    x_vmem, y_vmem, sem_x, sem_y, *, hid):
    i, c = pl.program_id(0), pl.program_id(1)
    last_c = pl.num_programs(1) - 1
    th = wg_ref.shape[2]
    nsub, nout = item_nsub_ref[i], item_nout_ref[i]
    row0 = pl.multiple_of(item_row_ref[i], MOE_SUB)
    sub = lambda s: pl.ds(s * MOE_SUB, MOE_SUB)

    def x_copy(s):
        return pltpu.make_async_copy(x_hbm.at[pl.ds(row0 + s * MOE_SUB, MOE_SUB)], x_vmem.at[sub(s)], sem_x.at[s])

    def y_copy(row, s):
        return pltpu.make_async_copy(y_vmem.at[sub(s)], y_hbm.at[pl.ds(row + s * MOE_SUB, MOE_SUB)], sem_y.at[s])

    def for_subs(count, fn):
        for s in range(MOE_ITEM_SUBS):
            pl.when(s < count)(functools.partial(fn, s))

    @pl.when(c == 0)
    def _():
        @pl.when(i > 0)
        def _():
            prev_row = pl.multiple_of(item_row_ref[i - 1], MOE_SUB)
            for_subs(item_nout_ref[i - 1], lambda s: y_copy(prev_row, s).wait())

        for_subs(nsub, lambda s: x_copy(s).start())
        for_subs(nsub, lambda s: x_copy(s).wait())

        @pl.when(nsub == 0)
        def _():
            y_vmem[...] = jnp.zeros_like(y_vmem)
            for_subs(nout, lambda s: y_copy(row0, s).start())

    @pl.when(nsub > 0)
    def _():
        col_ok = c * th + lax.broadcasted_iota(jnp.int32, (1, th), 1) < hid
        row_ok = c * th + lax.broadcasted_iota(jnp.int32, (th, 1), 0) < hid
        wgu = jnp.concatenate([wg_ref[0], wu_ref[0]], axis=1).astype(BF16)
        wd = jnp.where(row_ok, wd_ref[0], 0.0).astype(BF16)

        def ffn(rows):
            gu = _dot(x_vmem[rows, :], wgu)
            act = jnp.where(col_ok, _silu(gu[:, :th]) * gu[:, th:], 0.0)
            part = _dot(act.astype(BF16), wd)

            @pl.when(c == 0)
            def _():
                y_vmem[rows, :] = part

            @pl.when(c > 0)
            def _():
                y_vmem[rows, :] += part

        for pair in range(MOE_ITEM_SUBS // 2):
            pl.when(2 * pair + 1 < nsub)(functools.partial(ffn, pl.ds(2 * pair * MOE_SUB, 2 * MOE_SUB)))

        @pl.when(nsub % 2 == 1)
        def _():
            ffn(pl.ds(pl.multiple_of((nsub - 1) * MOE_SUB, MOE_SUB), MOE_SUB))

        @pl.when(c == last_c)
        def _():
            for_subs(nsub, lambda s: y_copy(row0, s).start())

    @pl.when((i == pl.num_programs(0) - 1) & (c == last_c))
    def _():
        for_subs(nout, lambda s: y_copy(row0, s).wait())


def moe_ffn(x_buf, item_e, item_row, item_nsub, item_nout, w_gate, w_up, w_down):
    rows, d = x_buf.shape
    hid = w_gate.shape[2]
    th = MOE_HID_TILE
    nc = pl.cdiv(hid, th)
    tile = lambda i, c, ns: jnp.where(ns[i] > 0, c, nc - 1)
    grid_spec = pltpu.PrefetchScalarGridSpec(
        num_scalar_prefetch=4,
        grid=(item_e.shape[0], nc),
        in_specs=[pl.BlockSpec(memory_space=pl.ANY),
                  pl.BlockSpec((1, d, th), lambda i, c, ie, ir, ns, no: (ie[i], 0, tile(i, c, ns))),
                  pl.BlockSpec((1, d, th), lambda i, c, ie, ir, ns, no: (ie[i], 0, tile(i, c, ns))),
                  pl.BlockSpec((1, th, d), lambda i, c, ie, ir, ns, no: (ie[i], tile(i, c, ns), 0))],
        out_specs=pl.BlockSpec(memory_space=pl.ANY),
        scratch_shapes=[pltpu.VMEM((MOE_ITEM_SUBS * MOE_SUB, d), BF16), pltpu.VMEM((MOE_ITEM_SUBS * MOE_SUB, d), F32),
                        pltpu.SemaphoreType.DMA((MOE_ITEM_SUBS,)), pltpu.SemaphoreType.DMA((MOE_ITEM_SUBS,))],
    )
    return pl.pallas_call(
        functools.partial(_moe_ffn_kernel, hid=hid),
        grid_spec=grid_spec,
        out_shape=jax.ShapeDtypeStruct((rows, d), F32),
        compiler_params=_params("arbitrary", "arbitrary"),
        name="moe_ffn",
    )(item_e, item_row, item_nsub, item_nout, x_buf, w_gate, w_up, w_down)


def hier_moe(h2, hf, route, w_gate, w_up, w_down):
    n, d = h2.shape
    n_exp = w_gate.shape[0]
    i32 = jnp.int32
    eid = route[:, 0:TOP_K].astype(i32).reshape(-1)
    wts = route[:, TOP_K:2 * TOP_K]
    n_assign = n * TOP_K
    rank, counts = moe_rank(eid.reshape(n_assign, 1))
    counts = counts[0, :n_exp].astype(i32)
    subs_e = (counts + MOE_SUB - 1) // MOE_SUB
    sub_end = jnp.cumsum(subs_e)
    row_start = (sub_end - subs_e) * MOE_SUB
    is_e = eid[:, None] == jnp.arange(n_exp, dtype=i32)[None, :]
    dest = rank[:, 0] + jnp.sum(jnp.where(is_e, row_start[None, :], 0), axis=1)
    n_subs = (n_assign + n_exp * (MOE_SUB - 1) + MOE_SUB - 1) // MOE_SUB
    rows = n_subs * MOE_SUB
    tok = jnp.repeat(jnp.arange(n, dtype=i32), TOP_K)
    tok_buf = (jnp.arange(rows, dtype=i32) % n).at[dest].set(tok)
    per = MOE_ITEM_SUBS
    n_items = (n_exp * (per - 1) + n_subs + per - 1) // per
    items_e = (subs_e + per - 1) // per
    item_end = jnp.cumsum(items_e)
    n_used = item_end[-1]
    idx = jnp.arange(n_items, dtype=i32)
    used = idx < n_used
    e_of = jnp.minimum(jnp.searchsorted(item_end, idx, side='right'), n_exp - 1).astype(i32)
    j = idx - (item_end - items_e)[e_of]
    item_nsub = jnp.where(used, jnp.clip(subs_e[e_of] - per * j, 0, per), 0)
    fill_first = jnp.minimum(sub_end[-1] + per * (idx - n_used), n_subs)
    item_fill = jnp.where(used, 0, jnp.minimum(n_subs - fill_first, per))
    item_row = jnp.where(used, row_start[e_of] + j * per * MOE_SUB, jnp.minimum(fill_first, n_subs - 1) * MOE_SUB)
    item_e = jnp.where(used, e_of, e_of[jnp.maximum(n_used - 1, 0)])
    y_buf = moe_ffn(hf[tok_buf], item_e.astype(i32), item_row.astype(i32), item_nsub.astype(i32),
                    (item_nsub + item_fill).astype(i32), w_gate, w_up, w_down)
    slot = dest.reshape(n, TOP_K)
    return h2 + wts[:, 0:1] * y_buf[slot[:, 0]] + wts[:, 1:2] * y_buf[slot[:, 1]]


COL_DT = ZX_COLS
COL_QKV = COL_DT + SSD_HEADS
COL_NSA_GATE = COL_QKV + QKV_COLS
COL_MERGE_GATE = COL_NSA_GATE + 3 * NSA_HEADS


def _small_w_in(w_in):
    pad = jnp.zeros((w_in.shape[0], SMALL_COLS - SSD_HEADS - 3 * NSA_HEADS), w_in.dtype)
    return jnp.concatenate([w_in[:, COL_DT:COL_QKV], w_in[:, COL_NSA_GATE:COL_MERGE_GATE], pad], axis=1)


def _layer(x, mem, norm1_w, w_in, ssd_conv_w, ssd_conv_b, ssd_dt_bias, ssd_a_log, ssd_d, ssd_norm_w,
           nsa_q_norm_w, nsa_k_norm_w, cmp_pe_k, cmp_w1_k, cmp_w2_k, cmp_pe_v, cmp_w1_v, cmp_w2_v,
           w_up_ssd, w_up_nsa, w_out, norm2_w, mem_norm_w, xq_w, xkv_w, x_q_norm_w, x_k_norm_w, xo_w,
           norm3_w, router_g_w, router_g_b, router_e_w, router_e_b, moe_w_gate, moe_w_up, moe_w_down):
    batch, seq, d = x.shape
    n = batch * seq
    xf = x.reshape(n, d)
    hn = rmsnorm_rows(xf, norm1_w)
    zx = matmul_wcast(hn, w_in, 0, ZX_COLS)
    qkv = matmul_wcast(hn, w_in, COL_QKV, QKV_COLS)
    mgates = matmul_wcast(hn, w_in, COL_MERGE_GATE, GATE_COLS)
    small = matmul_wcast(hn, _small_w_in(w_in), 0, SMALL_COLS)
    y_ssd = ssd_branch(zx, small, ssd_conv_w, ssd_conv_b, ssd_dt_bias, ssd_a_log, ssd_d, ssd_norm_w, batch, seq)
    qh, kcn, vcr, ksn, vsn, kwn, vwn = nsa_prep(qkv, nsa_q_norm_w, nsa_k_norm_w, batch, seq)
    kcc = compress(kcn, cmp_pe_k, cmp_w1_k, cmp_w2_k, transpose_out=False)
    vcct = compress(vcr, cmp_pe_v, cmp_w1_v, cmp_w2_v, transpose_out=True)
    gl = small[:, SSD_HEADS:SSD_HEADS + 3 * NSA_HEADS].reshape(batch, seq, 3, NSA_KV_HEADS, NSA_REP)
    gl = gl.transpose(0, 3, 2, 4, 1).reshape(batch, NSA_KV_HEADS, 3 * NSA_REP, seq)
    y_nsa = nsa_attention(qh, kcc, vcct, ksn, vsn, kwn, vwn, gl, batch, seq)
    merged = merge_mixers(y_ssd, y_nsa, w_up_ssd.astype(BF16), w_up_nsa.astype(BF16), mgates)
    h1 = residual_matmul(xf, merged, w_out.astype(BF16))
    mlen = mem.shape[1]
    mn = rmsnorm_rows(mem.reshape(batch * mlen, d), mem_norm_w, tm=min(512, batch * mlen))
    kv = matmul(mn, xkv_w.astype(BF16))
    n_exp = router_e_w.shape[1]
    rpad = LANES - N_GROUPS - n_exp
    router_w = jnp.concatenate([router_g_w, router_e_w, jnp.zeros((d, rpad), F32)], axis=1).astype(BF16)
    router_b = jnp.concatenate([router_g_b, router_e_b, jnp.zeros((rpad,), F32)]).reshape(1, LANES)
    h2, hf, route = xattn_router(h1, kv, norm2_w, xq_w.astype(BF16), x_q_norm_w, x_k_norm_w, xo_w.astype(BF16),
                                 norm3_w, router_w, router_b, batch, seq)
    out = hier_moe(h2, hf, route, moe_w_gate, moe_w_up, moe_w_down)
    return out.reshape(batch, seq, d)


def kernel(x, mem, norm1_w, w_in, ssd_conv_w, ssd_conv_b, ssd_dt_bias, ssd_a_log, ssd_d, ssd_norm_w, nsa_q_norm_w, nsa_k_norm_w, cmp_pe_k, cmp_w1_k, cmp_w2_k, cmp_pe_v, cmp_w1_v, cmp_w2_v, w_up_ssd, w_up_nsa, w_out, norm2_w, mem_norm_w, xq_w, xkv_w, x_q_norm_w, x_k_norm_w, xo_w, norm3_w, router_g_w, router_g_b, router_e_w, router_e_b, moe_w_gate, moe_w_up, moe_w_down):
    h = x
    for l in range(norm1_w.shape[0]):
        h = _layer(h, mem, norm1_w[l], w_in[l], ssd_conv_w[l], ssd_conv_b[l], ssd_dt_bias[l], ssd_a_log[l], ssd_d[l],
                   ssd_norm_w[l], nsa_q_norm_w[l], nsa_k_norm_w[l], cmp_pe_k[l], cmp_w1_k[l], cmp_w2_k[l], cmp_pe_v[l],
                   cmp_w1_v[l], cmp_w2_v[l], w_up_ssd[l], w_up_nsa[l], w_out[l], norm2_w[l], mem_norm_w[l], xq_w[l],
                   xkv_w[l], x_q_norm_w[l], x_k_norm_w[l], xo_w[l], norm3_w[l], router_g_w[l], router_g_b[l],
                   router_e_w[l], router_e_b[l], moe_w_gate[l], moe_w_up[l], moe_w_down[l])
    return h.astype(x.dtype)
```

```python
import functools

import numpy as np
import jax
import jax.numpy as jnp
from jax import lax
from jax.experimental import pallas as pl
from jax.experimental.pallas import tpu as pltpu

F32 = jnp.float32
BF16 = jnp.bfloat16

D_MODEL = 2048
SSD_INNER = 4096
SSD_HEAD_DIM = 64
SSD_HEADS = 64
SSD_GROUPS = 8
SSD_STATE = 128
SSD_CONV = 4
SSD_CHUNK = 256
HEADS_PER_GROUP = SSD_HEADS // SSD_GROUPS
GROUP_WIDTH = SSD_INNER // SSD_GROUPS
HEAD_DIM = 128
NSA_HEADS = 16
NSA_KV_HEADS = 4
NSA_REP = NSA_HEADS // NSA_KV_HEADS
NSA_WIDTH = NSA_HEADS * HEAD_DIM
NSA_KV_WIDTH = NSA_KV_HEADS * HEAD_DIM
CMP_BLK = 32
CMP_STRIDE = 16
CMP_HID = 256
SEL_BLK = 64
SEL_SHIFT = 6
N_SEL = 16
WINDOW = 512
Q_BLK = 128
SEL_KEY_TILE = 512
FORCE_SCORE = 1.0e4
X_HEADS = 4
X_WIDTH = X_HEADS * HEAD_DIM
N_GROUPS = 8
EXPERTS_PER_GROUP = 8
EPG_SHIFT = 3
N_EXPERTS = 64
TOP_K = 2
EXPERT_HIDDEN = 1408
MOE_SUB = 256
MOE_ITEM_SUBS = 4
MOE_HID_TILE = 256
ROPE_THETA = 10000.0
EPS = 1e-6
NEG_INF = -1e30
Q_PRESCALE = HEAD_DIM ** -0.5 * float(np.log2(np.e))
LANES = 128
VMEM_LIMIT = 56 * 1024 * 1024

ZX_COLS = SSD_INNER + SSD_INNER + 2 * SSD_GROUPS * SSD_STATE
QKV_COLS = NSA_WIDTH + 6 * NSA_KV_WIDTH
GATE_COLS = 2 * D_MODEL
SMALL_COLS = LANES
IN_PROJ_TM, IN_PROJ_TN = 2048, 1024


def _params(*sem):
    return pltpu.CompilerParams(dimension_semantics=sem, vmem_limit_bytes=VMEM_LIMIT)


def _rms(x, w):
    ms = jnp.mean(x * x, axis=-1, keepdims=True)
    return x * lax.rsqrt(ms + EPS) * w


def _sigmoid(x):
    return 1.0 / (1.0 + jnp.exp(-x))


def _silu(x):
    return x * _sigmoid(x)


def _softplus(x):
    return jnp.maximum(x, 0.0) + jnp.log1p(jnp.exp(-jnp.abs(x)))


def _dot(a, b):
    return jnp.dot(a, b, preferred_element_type=F32)


def _dot_nt(a, b):
    return lax.dot_general(a, b, (((1,), (1,)), ((), ())), preferred_element_type=F32)


def _masked_softmax(s, mask):
    s = jnp.where(mask, s, NEG_INF)
    e = jnp.exp(s - jnp.max(s, axis=-1, keepdims=True))
    p = e / jnp.sum(e, axis=-1, keepdims=True)
    return jnp.where(mask, p, 0.0)


def _rmsnorm_kernel(x_ref, w_ref, o_ref):
    o_ref[...] = _rms(x_ref[...], w_ref[...]).astype(o_ref.dtype)


def rmsnorm_rows(x, w, tm=512):
    m, d = x.shape
    return pl.pallas_call(
        _rmsnorm_kernel,
        grid=(m // tm,),
        in_specs=[pl.BlockSpec((tm, d), lambda i: (i, 0)), pl.BlockSpec((1, d), lambda i: (0, 0))],
        out_specs=pl.BlockSpec((tm, d), lambda i: (i, 0)),
        out_shape=jax.ShapeDtypeStruct((m, d), BF16),
        compiler_params=_params("parallel"),
        name="rmsnorm_rows",
    )(x, w.reshape(1, d))


def _matmul_kernel(x_ref, w_ref, o_ref):
    o_ref[...] = _dot(x_ref[...], w_ref[...]).astype(o_ref.dtype)


def matmul(x, w, out_dtype=F32, tm=512, tn=512):
    m, k = x.shape
    n = w.shape[1]
    tm, tn = min(tm, m), min(tn, n)
    return pl.pallas_call(
        _matmul_kernel,
        grid=(m // tm, n // tn),
        in_specs=[pl.BlockSpec((tm, k), lambda i, j: (i, 0)), pl.BlockSpec((k, tn), lambda i, j: (0, j))],
        out_specs=pl.BlockSpec((tm, tn), lambda i, j: (i, j)),
        out_shape=jax.ShapeDtypeStruct((m, n), out_dtype),
        compiler_params=_params("parallel", "arbitrary"),
        name="matmul",
    )(x, w)


def _split3(a):
    h = a.astype(BF16)
    r = a - h.astype(F32)
    m = r.astype(BF16)
    return h, m, (r - m.astype(F32)).astype(BF16)


def _ssd_kernel(z_ref, x_ref, b_ref, c_ref, dt_ref, dtt_ref, cwx_ref, cwb_ref, cwc_ref, cbx_ref, cbb_ref, cbc_ref,
                hp_ref, hpt_ref, nw_ref, y_ref, state_ref, ext_x_ref, ext_b_ref, ext_c_ref):
    L = x_ref.shape[0]
    H = L // 2
    chunk = pl.program_id(2)

    @pl.when(chunk == 0)
    def _():
        state_ref[...] = jnp.zeros_like(state_ref)
        ext_x_ref[...] = jnp.zeros_like(ext_x_ref)
        ext_b_ref[...] = jnp.zeros_like(ext_b_ref)
        ext_c_ref[...] = jnp.zeros_like(ext_c_ref)

    def conv_silu(u_ref, ext_ref, w, b):
        ext_ref[0:8, :] = ext_ref[L:L + 8, :]
        u = u_ref[...]
        ext_ref[8:8 + L, :] = u
        acc = b + w[SSD_CONV - 1:SSD_CONV, :] * u
        for k in range(SSD_CONV - 1):
            acc = acc + w[k:k + 1, :] * ext_ref[pl.ds(8 - (SSD_CONV - 1) + k, L), :]
        return _silu(acc)

    xs = conv_silu(x_ref, ext_x_ref, cwx_ref[...], cbx_ref[...])
    bm = conv_silu(b_ref, ext_b_ref, cwb_ref[...], cbb_ref[...])
    cm = conv_silu(c_ref, ext_c_ref, cwc_ref[...], cbc_ref[...])
    bm16, cm16 = bm.astype(BF16), cm.astype(BF16)

    hp = hp_ref[0]
    hpt = hpt_ref[0]
    dt = _softplus(dt_ref[0] + hp[0:1, :])
    dtt = _softplus(dtt_ref[0] + hpt[:, 0:1])
    da = dt * (-jnp.exp(hp[1:2, :]))
    dat = dtt * (-jnp.exp(hpt[:, 1:2]))
    row = lax.broadcasted_iota(jnp.int32, (L, L), 0)
    col = lax.broadcasted_iota(jnp.int32, (L, L), 1)
    tril = jnp.where(row >= col, 1.0, 0.0).astype(BF16)
    triu = jnp.where(row <= col, 1.0, 0.0).astype(BF16)
    acum = sum(_dot(tril, part) for part in _split3(da))
    acumt = sum(_dot(part, triu) for part in _split3(dat))
    last = acum[L - 1:L, :]
    dte = jnp.exp(last - acum)
    eac = jnp.exp(acum)
    cdec = jnp.exp(last)

    cb = _dot_nt(cm16, bm16)
    cb00, cb10, cb11 = cb[:H, :H], cb[H:, :H], cb[H:, H:]
    tri = lax.broadcasted_iota(jnp.int32, (H, H), 0) >= lax.broadcasted_iota(jnp.int32, (H, H), 1)
    lane = lax.broadcasted_iota(jnp.int32, (L, LANES), 1)
    first = lane < SSD_HEAD_DIM
    srow = lax.broadcasted_iota(jnp.int32, (LANES, SSD_STATE), 0) < SSD_HEAD_DIM
    z = z_ref[...]
    d_skip = hp[2:3, :]
    outs = []
    for p in range(HEADS_PER_GROUP // 2):
        h0, h1 = 2 * p, 2 * p + 1

        def pick(a, rows=first):
            return jnp.where(rows, a[:, h0:h0 + 1], a[:, h1:h1 + 1])

        xp = xs[:, p * LANES:(p + 1) * LANES]
        xdt = xp * pick(dt)
        xdt16 = xdt.astype(BF16)
        y_top = jnp.zeros((H, LANES), F32)
        y_bot = jnp.zeros((H, LANES), F32)
        for h, msk in ((h0, first), (h1, jnp.logical_not(first))):
            a_col, a_row = acum[:, h:h + 1], acumt[h:h + 1, :]
            d00 = jnp.exp(jnp.where(tri, a_col[:H] - a_row[:, :H], -jnp.inf))
            d10 = jnp.exp(a_col[H:] - a_row[:, :H])
            d11 = jnp.exp(jnp.where(tri, a_col[H:] - a_row[:, H:], -jnp.inf))
            xh = jnp.where(msk, xdt16, jnp.zeros_like(xdt16))
            y_top = y_top + _dot((cb00 * d00).astype(BF16), xh[:H])
            y_bot = y_bot + _dot((cb10 * d10).astype(BF16), xh[:H]) + _dot((cb11 * d11).astype(BF16), xh[H:])
        y = jnp.concatenate([y_top, y_bot], axis=0)
        hprev = state_ref[p]
        y = y + _dot_nt(cm16, hprev.astype(BF16)) * pick(eac)
        xe_t = (xdt * pick(dte)).T.astype(BF16)
        state_ref[p] = hprev * jnp.where(srow, cdec[:, h0:h0 + 1], cdec[:, h1:h1 + 1]) + _dot(xe_t, bm16)
        outs.append(y + pick(d_skip, first[0:1, :]) * xp)
    y = jnp.concatenate(outs, axis=1) * _silu(z)
    y_ref[...] = _rms(y, nw_ref[...]).astype(y_ref.dtype)


def ssd_branch(zx, small, conv_w, conv_b, dt_bias, a_log, d_skip, norm_w, batch, seq):
    n = batch * seq
    L = min(SSD_CHUNK, seq)
    nc = seq // L
    g, hpg = SSD_GROUPS, HEADS_PER_GROUP
    dt_raw = small[:, :SSD_HEADS].reshape(n, g, hpg)
    dt_g = dt_raw.transpose(1, 0, 2)
    dtt_g = dt_raw.transpose(1, 2, 0)
    hp = jnp.stack([dt_bias, a_log, d_skip], axis=0).reshape(3, g, hpg).transpose(1, 0, 2)
    hp = jnp.pad(hp, ((0, 0), (0, 8 - 3), (0, 0)))
    hpt = hp.transpose(0, 2, 1)
    cw = conv_w
    cbias = conv_b.reshape(1, -1)
    nxb = SSD_INNER // GROUP_WIDTH
    rowblk = lambda b, gi, c: b * nc + c
    bc0 = SSD_INNER // SSD_STATE
    grid = (batch, g, nc)
    in_specs = [
        pl.BlockSpec((L, GROUP_WIDTH), lambda b, gi, c: (rowblk(b, gi, c), gi)),
        pl.BlockSpec((L, GROUP_WIDTH), lambda b, gi, c: (rowblk(b, gi, c), nxb + gi)),
        pl.BlockSpec((L, SSD_STATE), lambda b, gi, c: (rowblk(b, gi, c), 2 * bc0 + gi)),
        pl.BlockSpec((L, SSD_STATE), lambda b, gi, c: (rowblk(b, gi, c), 2 * bc0 + g + gi)),
        pl.BlockSpec((1, L, hpg), lambda b, gi, c: (gi, rowblk(b, gi, c), 0)),
        pl.BlockSpec((1, hpg, L), lambda b, gi, c: (gi, 0, rowblk(b, gi, c))),
        pl.BlockSpec((SSD_CONV, GROUP_WIDTH), lambda b, gi, c: (0, gi)),
        pl.BlockSpec((SSD_CONV, SSD_STATE), lambda b, gi, c: (0, bc0 + gi)),
        pl.BlockSpec((SSD_CONV, SSD_STATE), lambda b, gi, c: (0, bc0 + g + gi)),
        pl.BlockSpec((1, GROUP_WIDTH), lambda b, gi, c: (0, gi)),
        pl.BlockSpec((1, SSD_STATE), lambda b, gi, c: (0, bc0 + gi)),
        pl.BlockSpec((1, SSD_STATE), lambda b, gi, c: (0, bc0 + g + gi)),
        pl.BlockSpec((1, 8, hpg), lambda b, gi, c: (gi, 0, 0)),
        pl.BlockSpec((1, hpg, 8), lambda b, gi, c: (gi, 0, 0)),
        pl.BlockSpec((1, GROUP_WIDTH), lambda b, gi, c: (0, gi)),
    ]
    return pl.pallas_call(
        _ssd_kernel,
        grid=grid,
        in_specs=in_specs,
        out_specs=pl.BlockSpec((L, GROUP_WIDTH), lambda b, gi, c: (rowblk(b, gi, c), gi)),
        out_shape=jax.ShapeDtypeStruct((n, SSD_INNER), BF16),
        scratch_shapes=[pltpu.VMEM((HEADS_PER_GROUP // 2, LANES, SSD_STATE), F32),
                        pltpu.VMEM((L + 8, GROUP_WIDTH), F32), pltpu.VMEM((L + 8, SSD_STATE), F32),
                        pltpu.VMEM((L + 8, SSD_STATE), F32)],
        compiler_params=_params("parallel", "parallel", "arbitrary"),
        name="ssd_branch",
    )(zx, zx, zx, zx, dt_g, dtt_g, cw, cw, cw, cbias, cbias, cbias, hp, hpt, norm_w.reshape(1, -1))


def _nsa_prep_kernel(q_ref, kc_ref, vc_ref, ks_ref, vs_ref, kw_ref, vw_ref, cos_ref, sin_ref, qw_ref, kw3_ref,
                     qo_ref, kco_ref, vco_ref, kso_ref, vso_ref, kwo_ref, vwo_ref):
    cos, sin = cos_ref[...], sin_ref[...]

    def norm_rope(x, w):
        y = _rms(x, w)
        return y * cos + pltpu.roll(y, HEAD_DIM // 2, 1) * sin

    qw = qw_ref[...]
    tt = q_ref.shape[0]
    for g in range(NSA_KV_HEADS):
        sl = slice(g * HEAD_DIM, (g + 1) * HEAD_DIM)
        for r in range(NSA_REP):
            h = g * NSA_REP + r
            qh = norm_rope(q_ref[:, h * HEAD_DIM:(h + 1) * HEAD_DIM], qw) * Q_PRESCALE
            qo_ref[0, g, r] = qh.astype(qo_ref.dtype)
        kco_ref[0, g] = norm_rope(kc_ref[:, sl], kw3_ref[0:1, :])
        vco_ref[0, g] = vc_ref[:, sl]
        kso_ref[0, g] = norm_rope(ks_ref[:, sl], kw3_ref[1:2, :]).astype(kso_ref.dtype)
        kwo_ref[0, g] = norm_rope(kw_ref[:, sl], kw3_ref[2:3, :]).astype(kwo_ref.dtype)
        vst = vs_ref[:, sl].T.astype(vso_ref.dtype)
        for j in range(tt // SEL_KEY_TILE):
            vso_ref[0, g, j] = vst[:, j * SEL_KEY_TILE:(j + 1) * SEL_KEY_TILE]
        vwt = vw_ref[:, sl].T.astype(vwo_ref.dtype)
        for j in range(tt // Q_BLK):
            vwo_ref[0, g, j] = vwt[:, j * Q_BLK:(j + 1) * Q_BLK]


def nsa_prep(qkv, q_norm_w, k_norm_w, batch, seq, tt=SEL_KEY_TILE):
    half = HEAD_DIM // 2
    inv = ROPE_THETA ** (-jnp.arange(half, dtype=F32) / half)
    ang = jnp.arange(seq).astype(F32)[:, None] * inv[None, :]
    cos = jnp.concatenate([jnp.cos(ang), jnp.cos(ang)], axis=-1)
    sin = jnp.concatenate([-jnp.sin(ang), jnp.sin(ang)], axis=-1)
    nt = seq // tt
    kvb = NSA_WIDTH // NSA_KV_WIDTH
    kv_spec = lambda j: pl.BlockSpec((tt, NSA_KV_WIDTH), lambda b, t: (b * nt + t, kvb + j))
    head_spec = pl.BlockSpec((1, NSA_KV_HEADS, tt, HEAD_DIM), lambda b, t: (b, 0, t, 0))
    head_shape = lambda dt: jax.ShapeDtypeStruct((batch, NSA_KV_HEADS, seq, HEAD_DIM), dt)
    return pl.pallas_call(
        _nsa_prep_kernel,
        grid=(batch, nt),
        in_specs=[pl.BlockSpec((tt, NSA_WIDTH), lambda b, t: (b * nt + t, 0))] + [kv_spec(j) for j in range(6)] + [
            pl.BlockSpec((tt, HEAD_DIM), lambda b, t: (t, 0)), pl.BlockSpec((tt, HEAD_DIM), lambda b, t: (t, 0)),
            pl.BlockSpec((1, HEAD_DIM), lambda b, t: (0, 0)), pl.BlockSpec((3, HEAD_DIM), lambda b, t: (0, 0))],
        out_specs=[pl.BlockSpec((1, NSA_KV_HEADS, NSA_REP, tt, HEAD_DIM), lambda b, t: (b, 0, 0, t, 0)),
                   head_spec, head_spec, head_spec,
                   pl.BlockSpec((1, NSA_KV_HEADS, tt // SEL_KEY_TILE, HEAD_DIM, SEL_KEY_TILE),
                                lambda b, t: (b, 0, t, 0, 0)),
                   head_spec,
                   pl.BlockSpec((1, NSA_KV_HEADS, tt // Q_BLK, HEAD_DIM, Q_BLK), lambda b, t: (b, 0, t, 0, 0))],
        out_shape=[jax.ShapeDtypeStruct((batch, NSA_KV_HEADS, NSA_REP, seq, HEAD_DIM), BF16),
                   head_shape(F32), head_shape(F32), head_shape(BF16),
                   jax.ShapeDtypeStruct((batch, NSA_KV_HEADS, seq // SEL_KEY_TILE, HEAD_DIM, SEL_KEY_TILE), BF16),
                   head_shape(BF16),
                   jax.ShapeDtypeStruct((batch, NSA_KV_HEADS, seq // Q_BLK, HEAD_DIM, Q_BLK), BF16)],
        compiler_params=_params("parallel", "parallel"),
        name="nsa_prep",
    )(qkv, qkv, qkv, qkv, qkv, qkv, qkv, cos, sin, q_norm_w.reshape(1, -1), k_norm_w)


def _compress_kernel(x_ref, pe_ref, w1_ref, w2_ref, o_ref, *, ncmp, transpose_out):
    nseg = x_ref.shape[2] // CMP_STRIDE
    u = jnp.zeros((nseg, CMP_HID), F32)
    v = jnp.zeros((nseg, CMP_HID), F32)
    for j in range(CMP_STRIDE):
        xj = x_ref[0, 0, pl.ds(j, nseg, stride=CMP_STRIDE), :]
        u = u + _dot((xj + pe_ref[j:j + 1, :]).astype(BF16), w1_ref[j])
        v = v + _dot((xj + pe_ref[CMP_STRIDE + j:CMP_STRIDE + j + 1, :]).astype(BF16), w1_ref[CMP_STRIDE + j])
    hid = _silu(u + pltpu.roll(v, nseg - 1, 0))
    comp = _dot(hid.astype(BF16), w2_ref[...])
    rowi = lax.broadcasted_iota(jnp.int32, comp.shape, 0)
    comp = jnp.where(rowi < ncmp, comp, 0.0)
    o_ref[0, 0] = (comp.T if transpose_out else comp).astype(o_ref.dtype)


def compress(raw, pe, w1, w2, transpose_out):
    b, g, t, hd = raw.shape
    nseg = t // CMP_STRIDE
    full = lambda shape: pl.BlockSpec(shape, lambda i, j: (0,) * len(shape))
    out_dims = (hd, nseg) if transpose_out else (nseg, hd)
    return pl.pallas_call(
        functools.partial(_compress_kernel, ncmp=nseg - 1, transpose_out=transpose_out),
        grid=(b, g),
        in_specs=[pl.BlockSpec((1, 1, t, hd), lambda i, j: (i, j, 0, 0)), full((CMP_BLK, hd)),
                  full((CMP_BLK, hd, CMP_HID)), full((CMP_HID, hd))],
        out_specs=pl.BlockSpec((1, 1) + out_dims, lambda i, j: (i, j, 0, 0)),
        out_shape=jax.ShapeDtypeStruct((b, g) + out_dims, BF16),
        compiler_params=_params("parallel", "parallel"),
        name="nsa_compress",
    )(raw, pe, w1.astype(BF16), w2.astype(BF16))


def _nsa_attn_kernel(q_ref, kc_ref, vct_ref, ks_ref, vst_ref, kw_ref, vwt_ref, glt_ref, ovt_ref, o_ref,
                     selb_ref, acc_ref, ocw_ref, *, seq, n_sel):
    qb = pl.program_id(2)
    q0 = qb * Q_BLK
    rows = NSA_REP * Q_BLK
    q = q_ref[0, 0].reshape(rows, HEAD_DIM)
    tq = q0 + lax.broadcasted_iota(jnp.int32, (1, Q_BLK), 1)
    head = lambda r: slice(r * Q_BLK, (r + 1) * Q_BLK)

    wk = WINDOW + Q_BLK
    start = pl.multiple_of(jnp.clip(q0 - WINDOW, 0, seq - wk), Q_BLK)
    s_w = _dot_nt(kw_ref[0, 0, pl.ds(start, wk), :], q)

    ncp = kc_ref.shape[2]
    s_c = _dot_nt(kc_ref[0, 0], q)
    cend = lax.broadcasted_iota(jnp.int32, (ncp, 1), 0) * CMP_STRIDE + (CMP_BLK - 1)
    m_c = cend <= tq
    ps = []
    for r in range(NSA_REP):
        s = jnp.where(m_c, s_c[:, head(r)], NEG_INF)
        e = jnp.exp2(s - jnp.max(s, axis=0, keepdims=True))
        p = jnp.where(m_c, e * (1.0 / jnp.sum(e, axis=0, keepdims=True)), 0.0)
        ps.append(p.astype(BF16))
    p_c = jnp.concatenate(ps, axis=1)
    o_c = _dot(vct_ref[0, 0], p_c)

    nsb = ovt_ref.shape[0]
    imp4 = _dot(ovt_ref[...], p_c)
    imp = imp4[:, head(0)]
    for r in range(1, NSA_REP):
        imp = imp + imp4[:, head(r)]
    blk_t = (q0 + lax.broadcasted_iota(jnp.int32, (nsb, Q_BLK), 1)) >> SEL_SHIFT
    sb = lax.broadcasted_iota(jnp.int32, (nsb, Q_BLK), 0)
    forced = (sb == 0) | (sb == blk_t) | (sb == blk_t - 1)
    imp = jnp.where(sb <= blk_t, jnp.where(forced, FORCE_SCORE, imp), -jnp.inf)
    rank = jnp.zeros((nsb, Q_BLK), F32)
    for i in range(nsb):
        ri = imp[i:i + 1, :]
        beats = (ri > imp) | ((ri == imp) & (sb > i))
        rank = rank + jnp.where(beats, 1.0, 0.0)
    selb_ref[...] = jnp.where(rank < n_sel, 0.0, NEG_INF)

    dist = tq - (start + lax.broadcasted_iota(jnp.int32, (wk, 1), 0))
    bias_w = jnp.where((dist >= 0) & (dist < WINDOW), 0.0, NEG_INF)
    ps, ls = [], []
    for r in range(NSA_REP):
        s = s_w[:, head(r)] + bias_w
        p = jnp.exp2(s - jnp.max(s, axis=0, keepdims=True))
        ls.append(jnp.sum(p, axis=0, keepdims=True))
        ps.append(p.astype(BF16))
    t0 = start // Q_BLK
    vw_t = jnp.concatenate([vwt_ref[0, 0, t0 + j] for j in range(wk // Q_BLK)], axis=1)
    o_w = _dot(vw_t, jnp.concatenate(ps, axis=1)) * (1.0 / jnp.concatenate(ls, axis=1))

    gate = _sigmoid(glt_ref[0, 0])
    for r in range(NSA_REP):
        ocw_ref[:, head(r)] = (gate[r:r + 1, :] * o_c[:, head(r)]
                               + gate[2 * NSA_REP + r:2 * NSA_REP + r + 1, :] * o_w[:, head(r)])

    kt_sz = vst_ref.shape[4]
    blocks_per_tile = kt_sz // SEL_BLK
    acc_ref[...] = jnp.zeros_like(acc_ref)

    def sel_tile(kt, carry):
        m, l = carry
        k0 = pl.multiple_of(kt * kt_sz, kt_sz)
        s_all = _dot_nt(ks_ref[0, 0, pl.ds(k0, kt_sz), :], q)
        kpos = k0 + lax.broadcasted_iota(jnp.int32, (kt_sz, 1), 0)
        bias = jnp.concatenate(
            [jnp.broadcast_to(selb_ref[pl.ds(kt * blocks_per_tile + i, 1), :], (SEL_BLK, Q_BLK))
             for i in range(blocks_per_tile)], axis=0)
        bias = jnp.where(kpos <= tq, bias, NEG_INF)
        ps, ms, ls, alphas = [], [], [], []
        for r in range(NSA_REP):
            s = s_all[:, head(r)] + bias
            m_old = m[:, head(r)]
            m_new = jnp.maximum(m_old, jnp.max(s, axis=0, keepdims=True))
            p = jnp.exp2(s - m_new)
            alpha = jnp.exp2(m_old - m_new)
            ls.append(alpha * l[:, head(r)] + jnp.sum(p, axis=0, keepdims=True))
            ms.append(m_new)
            alphas.append(alpha)
            ps.append(p.astype(BF16))
        pv = _dot(vst_ref[0, 0, kt], jnp.concatenate(ps, axis=1))
        acc_ref[...] = acc_ref[...] * jnp.concatenate(alphas, axis=1) + pv
        return jnp.concatenate(ms, axis=1), jnp.concatenate(ls, axis=1)

    n_tiles = (q0 + Q_BLK + kt_sz - 1) // kt_sz
    init = (jnp.full((1, rows), NEG_INF, F32), jnp.zeros((1, rows), F32))
    _, l_s = lax.fori_loop(0, n_tiles, sel_tile, init)
    o_s = acc_ref[...] * (1.0 / l_s)

    for r in range(NSA_REP):
        out_t = ocw_ref[:, head(r)] + gate[NSA_REP + r:NSA_REP + r + 1, :] * o_s[:, head(r)]
        o_ref[:, r * HEAD_DIM:(r + 1) * HEAD_DIM] = out_t.T.astype(o_ref.dtype)


def nsa_attention(qh, kcc, vcct, ksn, vst, kwn, vwt, gate_logits_t, batch, seq):
    assert seq % SEL_KEY_TILE == 0 and seq >= WINDOW + Q_BLK
    nq = seq // Q_BLK
    nsb = seq // SEL_BLK
    n_sel = min(N_SEL, nsb)
    ncp = kcc.shape[2]
    ncmp = (seq - CMP_BLK) // CMP_STRIDE + 1
    ci = np.arange(ncp)[None, :]
    sj = np.arange(nsb)[:, None]
    ovt = ((ci * CMP_STRIDE < (sj + 1) * SEL_BLK) & (ci * CMP_STRIDE + CMP_BLK > sj * SEL_BLK) & (ci < ncmp))
    ovt = jnp.asarray(ovt, BF16)
    seq_spec = pl.BlockSpec((1, 1, seq, HEAD_DIM), lambda b, g, i: (b, g, 0, 0))
    cmp_spec = pl.BlockSpec((1, 1, ncp, HEAD_DIM), lambda b, g, i: (b, g, 0, 0))
    tiled = lambda a: pl.BlockSpec((1, 1) + a.shape[2:], lambda b, g, i: (b, g, 0, 0, 0))
    return pl.pallas_call(
        functools.partial(_nsa_attn_kernel, seq=seq, n_sel=n_sel),
        grid=(batch, NSA_KV_HEADS, nq),
        in_specs=[pl.BlockSpec((1, 1, NSA_REP, Q_BLK, HEAD_DIM), lambda b, g, i: (b, g, 0, i, 0)),
                  cmp_spec, pl.BlockSpec((1, 1, HEAD_DIM, ncp), lambda b, g, i: (b, g, 0, 0)),
                  seq_spec, tiled(vst), seq_spec, tiled(vwt),
                  pl.BlockSpec((1, 1, 3 * NSA_REP, Q_BLK), lambda b, g, i: (b, g, 0, i)),
                  pl.BlockSpec((nsb, ncp), lambda b, g, i: (0, 0))],
        out_specs=pl.BlockSpec((Q_BLK, NSA_REP * HEAD_DIM), lambda b, g, i: (b * nq + i, g)),
        out_shape=jax.ShapeDtypeStruct((batch * seq, NSA_WIDTH), BF16),
        scratch_shapes=[pltpu.VMEM((nsb, Q_BLK), F32), pltpu.VMEM((HEAD_DIM, NSA_REP * Q_BLK), F32),
                        pltpu.VMEM((HEAD_DIM, NSA_REP * Q_BLK), F32)],
        compiler_params=_params("parallel", "parallel", "arbitrary"),
        name="nsa_attention",
    )(qh, kcc, vcct, ksn, vst, kwn, vwt, gate_logits_t, ovt)


def _merge_kernel(ys_ref, yn_ref, ws_ref, wn_ref, gs_ref, gn_ref, o_ref):
    up_s = _dot(ys_ref[...], ws_ref[...])
    up_n = _dot(yn_ref[...], wn_ref[...])
    o_ref[...] = (_sigmoid(gs_ref[...]) * up_s + _sigmoid(gn_ref[...]) * up_n).astype(o_ref.dtype)


def merge_mixers(y_ssd, y_nsa, w_up_ssd, w_up_nsa, gates, tm=512, tn=512):
    m = y_ssd.shape[0]
    nb = D_MODEL // tn
    return pl.pallas_call(
        _merge_kernel,
        grid=(m // tm, nb),
        in_specs=[pl.BlockSpec((tm, SSD_INNER), lambda i, j: (i, 0)), pl.BlockSpec((tm, NSA_WIDTH), lambda i, j: (i, 0)),
                  pl.BlockSpec((SSD_INNER, tn), lambda i, j: (0, j)), pl.BlockSpec((NSA_WIDTH, tn), lambda i, j: (0, j)),
                  pl.BlockSpec((tm, tn), lambda i, j: (i, j)), pl.BlockSpec((tm, tn), lambda i, j: (i, nb + j))],
        out_specs=pl.BlockSpec((tm, tn), lambda i, j: (i, j)),
        out_shape=jax.ShapeDtypeStruct((m, D_MODEL), BF16),
        compiler_params=_params("parallel", "arbitrary"),
        name="merge_mixers",
    )(y_ssd, y_nsa, w_up_ssd, w_up_nsa, gates, gates)


def _residual_matmul_kernel(h_ref, x_ref, w_ref, o_ref):
    o_ref[...] = h_ref[...] + _dot(x_ref[...], w_ref[...])


def residual_matmul(h, x, w, tm=1024, tn=1024):
    m, k = x.shape
    n = w.shape[1]
    return pl.pallas_call(
        _residual_matmul_kernel,
        grid=(m // tm, n // tn),
        in_specs=[pl.BlockSpec((tm, tn), lambda i, j: (i, j)), pl.BlockSpec((tm, k), lambda i, j: (i, 0)),
                  pl.BlockSpec((k, tn), lambda i, j: (0, j))],
        out_specs=pl.BlockSpec((tm, tn), lambda i, j: (i, j)),
        out_shape=jax.ShapeDtypeStruct((m, n), F32),
        compiler_params=_params("parallel", "arbitrary"),
        name="residual_matmul",
    )(h, x, w)


def _xattn_router_kernel(h_ref, kv_ref, n2_ref, wq_ref, qn_ref, kn_ref, wo_ref, n3_ref, rw_ref, rb_ref,
                         h2_ref, hf_ref, rt_ref):
    h = h_ref[...]
    qp = _dot(_rms(h, n2_ref[...]).astype(BF16), wq_ref[...])
    kv = kv_ref[...]
    scale = HEAD_DIM ** -0.5
    heads = []
    for hd in range(X_HEADS):
        sl = slice(hd * HEAD_DIM, (hd + 1) * HEAD_DIM)
        qh = _rms(qp[:, sl], qn_ref[...]).astype(BF16)
        kh = _rms(kv[:, sl], kn_ref[...]).astype(BF16)
        vh = kv[:, X_WIDTH + hd * HEAD_DIM:X_WIDTH + (hd + 1) * HEAD_DIM].astype(BF16)
        s = _dot_nt(qh, kh) * scale
        e = jnp.exp(s - jnp.max(s, axis=-1, keepdims=True))
        p = e / jnp.sum(e, axis=-1, keepdims=True)
        heads.append(_dot(p.astype(BF16), vh))
    o = jnp.concatenate(heads, axis=1).astype(BF16)
    h2 = h + _dot(o, wo_ref[...])
    h2_ref[...] = h2
    hf = _rms(h2, n3_ref[...]).astype(BF16)
    hf_ref[...] = hf
    lg = _dot(hf, rw_ref[...]) + rb_ref[...]

    lane = lax.broadcasted_iota(jnp.int32, lg.shape, 1)
    rmax = lambda v: jnp.max(v, axis=-1, keepdims=True)
    rsum = lambda v: jnp.sum(v, axis=-1, keepdims=True)
    first_lane = lambda hit: jnp.min(jnp.where(hit, lane, LANES), axis=-1, keepdims=True)
    is_g = lane < N_GROUPS
    eg = jnp.where(is_g, jnp.exp(lg - rmax(jnp.where(is_g, lg, -jnp.inf))), 0.0)
    pg = eg / rsum(eg)
    pg_top = rmax(pg)
    grp = first_lane(is_g & (pg == pg_top))
    in_grp = (lane >= N_GROUPS) & (((lane - N_GROUPS) >> EPG_SHIFT) == grp)
    ee = jnp.where(in_grp, jnp.exp(lg - rmax(jnp.where(in_grp, lg, -jnp.inf))), 0.0)
    pe = jnp.where(in_grp, ee / rsum(ee), -1.0)
    p1 = rmax(pe)
    i1 = first_lane(pe == p1)
    pe2 = jnp.where(lane == i1, -1.0, pe)
    p2 = rmax(pe2)
    i2 = first_lane(pe2 == p2)
    psum = p1 + p2
    route = jnp.where(lane == 0, (i1 - N_GROUPS).astype(F32),
                      jnp.where(lane == 1, (i2 - N_GROUPS).astype(F32),
                                jnp.where(lane == 2, pg_top * p1 / psum,
                                          jnp.where(lane == 3, pg_top * p2 / psum, 0.0))))
    rt_ref[...] = route


def xattn_router(h1, kv, norm2_w, wq, q_norm_w, k_norm_w, wo, norm3_w, router_w, router_b, batch, seq, tm=512):
    n = batch * seq
    tm = min(tm, seq)
    per_seq = seq // tm
    mlen = kv.shape[0] // batch
    full = lambda shape: pl.BlockSpec(shape, lambda i: (0,) * len(shape))
    return pl.pallas_call(
        _xattn_router_kernel,
        grid=(n // tm,),
        in_specs=[pl.BlockSpec((tm, D_MODEL), lambda i: (i, 0)),
                  pl.BlockSpec((mlen, 2 * X_WIDTH), lambda i: (i // per_seq, 0)),
                  full((1, D_MODEL)), full((D_MODEL, X_WIDTH)), full((1, HEAD_DIM)), full((1, HEAD_DIM)),
                  full((X_WIDTH, D_MODEL)), full((1, D_MODEL)), full((D_MODEL, LANES)), full((1, LANES))],
        out_specs=[pl.BlockSpec((tm, D_MODEL), lambda i: (i, 0)), pl.BlockSpec((tm, D_MODEL), lambda i: (i, 0)),
                   pl.BlockSpec((tm, LANES), lambda i: (i, 0))],
        out_shape=[jax.ShapeDtypeStruct((n, D_MODEL), F32), jax.ShapeDtypeStruct((n, D_MODEL), BF16),
                   jax.ShapeDtypeStruct((n, LANES), F32)],
        compiler_params=_params("parallel"),
        name="xattn_router",
    )(h1, kv, norm2_w.reshape(1, -1), wq, q_norm_w.reshape(1, -1), k_norm_w.reshape(1, -1), wo,
      norm3_w.reshape(1, -1), router_w, router_b)


def _moe_rank_kernel(e_ref, rank_ref, cnt_ref, carry_ref):
    @pl.when(pl.program_id(0) == 0)
    def _():
        carry_ref[...] = jnp.zeros_like(carry_ref)

    t = e_ref.shape[0]
    hit = e_ref[...] == lax.broadcasted_iota(jnp.int32, (t, LANES), 1)
    onehot = jnp.where(hit, 1.0, 0.0).astype(BF16)
    earlier = lax.broadcasted_iota(jnp.int32, (t, t), 0) > lax.broadcasted_iota(jnp.int32, (t, t), 1)
    before = _dot(jnp.where(earlier, 1.0, 0.0).astype(BF16), onehot) + carry_ref[...]
    rank_ref[...] = jnp.sum(jnp.where(hit, before, 0.0), axis=-1, keepdims=True).astype(jnp.int32)
    carry_ref[...] += jnp.sum(jnp.where(hit, 1.0, 0.0), axis=0, keepdims=True)
    cnt_ref[...] = carry_ref[...]


def moe_rank(eid, t=512):
    a = eid.shape[0]
    return pl.pallas_call(
        _moe_rank_kernel,
        grid=(a // t,),
        in_specs=[pl.BlockSpec((t, 1), lambda i: (i, 0))],
        out_specs=[pl.BlockSpec((t, 1), lambda i: (i, 0)), pl.BlockSpec((1, LANES), lambda i: (0, 0))],
        out_shape=[jax.ShapeDtypeStruct((a, 1), jnp.int32), jax.ShapeDtypeStruct((1, LANES), F32)],
        scratch_shapes=[pltpu.VMEM((1, LANES), F32)],
        compiler_params=_params("arbitrary"),
        name="moe_rank",
    )(eid)


def _moe_ffn_kernel(item_e_ref, item_row_ref, item_nsub_ref, item_nout_ref, x_hbm, wg_ref, wu_ref, wd_ref, y_hbm,
                    x_vmem, y_vmem, sem_x, sem_y, *, hid):
    i, c = pl.program_id(0), pl.program_id(1)
    last_c = pl.num_programs(1) - 1
    th = wg_ref.shape[2]
    nsub, nout = item_nsub_ref[i], item_nout_ref[i]
    row0 = pl.multiple_of(item_row_ref[i], MOE_SUB)
    sub = lambda s: pl.ds(s * MOE_SUB, MOE_SUB)

    def x_copy(s):
        return pltpu.make_async_copy(x_hbm.at[pl.ds(row0 + s * MOE_SUB, MOE_SUB)], x_vmem.at[sub(s)], sem_x.at[s])

    def y_copy(row, s):
        return pltpu.make_async_copy(y_vmem.at[sub(s)], y_hbm.at[pl.ds(row + s * MOE_SUB, MOE_SUB)], sem_y.at[s])

    def for_subs(count, fn):
        for s in range(MOE_ITEM_SUBS):
            pl.when(s < count)(functools.partial(fn, s))

    @pl.when(c == 0)
    def _():
        @pl.when(i > 0)
        def _():
            prev_row = pl.multiple_of(item_row_ref[i - 1], MOE_SUB)
            for_subs(item_nout_ref[i - 1], lambda s: y_copy(prev_row, s).wait())

        for_subs(nsub, lambda s: x_copy(s).start())
        for_subs(nsub, lambda s: x_copy(s).wait())

        @pl.when(nsub == 0)
        def _():
            y_vmem[...] = jnp.zeros_like(y_vmem)
            for_subs(nout, lambda s: y_copy(row0, s).start())

    def ffn(count):
        rows = pl.ds(0, count * MOE_SUB)
        col_ok = c * th + lax.broadcasted_iota(jnp.int32, (1, th), 1) < hid
        row_ok = c * th + lax.broadcasted_iota(jnp.int32, (th, 1), 0) < hid
        wgu = jnp.concatenate([wg_ref[0], wu_ref[0]], axis=1).astype(BF16)
        wd = jnp.where(row_ok, wd_ref[0], 0.0).astype(BF16)
        gu = _dot(x_vmem[rows, :], wgu)
        act = jnp.where(col_ok, _silu(gu[:, :th]) * gu[:, th:], 0.0)
        part = _dot(act.astype(BF16), wd)

        @pl.when(c == 0)
        def _():
            y_vmem[rows, :] = part

        @pl.when(c > 0)
        def _():
            y_vmem[rows, :] += part

    for count in range(1, MOE_ITEM_SUBS + 1):
        pl.when(nsub == count)(functools.partial(ffn, count))

    @pl.when((nsub > 0) & (c == last_c))
    def _():
        for_subs(nsub, lambda s: y_copy(row0, s).start())

    @pl.when((i == pl.num_programs(0) - 1) & (c == last_c))
    def _():
        for_subs(nout, lambda s: y_copy(row0, s).wait())


def moe_ffn(x_buf, item_e, item_row, item_nsub, item_nout, w_gate, w_up, w_down):
    rows, d = x_buf.shape
    hid = w_gate.shape[2]
    th = MOE_HID_TILE
    nc = pl.cdiv(hid, th)
    tile = lambda i, c, ns: jnp.where(ns[i] > 0, c, nc - 1)
    grid_spec = pltpu.PrefetchScalarGridSpec(
        num_scalar_prefetch=4,
        grid=(item_e.shape[0], nc),
        in_specs=[pl.BlockSpec(memory_space=pl.ANY),
                  pl.BlockSpec((1, d, th), lambda i, c, ie, ir, ns, no: (ie[i], 0, tile(i, c, ns))),
                  pl.BlockSpec((1, d, th), lambda i, c, ie, ir, ns, no: (ie[i], 0, tile(i, c, ns))),
                  pl.BlockSpec((1, th, d), lambda i, c, ie, ir, ns, no: (ie[i], tile(i, c, ns), 0))],
        out_specs=pl.BlockSpec(memory_space=pl.ANY),
        scratch_shapes=[pltpu.VMEM((MOE_ITEM_SUBS * MOE_SUB, d), BF16), pltpu.VMEM((MOE_ITEM_SUBS * MOE_SUB, d), F32),
                        pltpu.SemaphoreType.DMA((MOE_ITEM_SUBS,)), pltpu.SemaphoreType.DMA((MOE_ITEM_SUBS,))],
    )
    return pl.pallas_call(
        functools.partial(_moe_ffn_kernel, hid=hid),
        grid_spec=grid_spec,
        out_shape=jax.ShapeDtypeStruct((rows, d), F32),
        compiler_params=_params("arbitrary", "arbitrary"),
        name="moe_ffn",
    )(item_e, item_row, item_nsub, item_nout, x_buf, w_gate, w_up, w_down)


def hier_moe(h2, hf, route, w_gate, w_up, w_down):
    n, d = h2.shape
    n_exp = w_gate.shape[0]
    i32 = jnp.int32
    eid = route[:, 0:TOP_K].astype(i32).reshape(-1)
    wts = route[:, TOP_K:2 * TOP_K]
    n_assign = n * TOP_K
    rank, counts = moe_rank(eid.reshape(n_assign, 1))
    counts = counts[0, :n_exp].astype(i32)
    subs_e = (counts + MOE_SUB - 1) // MOE_SUB
    sub_end = jnp.cumsum(subs_e)
    row_start = (sub_end - subs_e) * MOE_SUB
    is_e = eid[:, None] == jnp.arange(n_exp, dtype=i32)[None, :]
    dest = rank[:, 0] + jnp.sum(jnp.where(is_e, row_start[None, :], 0), axis=1)
    n_subs = (n_assign + n_exp * (MOE_SUB - 1) + MOE_SUB - 1) // MOE_SUB
    rows = n_subs * MOE_SUB
    tok = jnp.repeat(jnp.arange(n, dtype=i32), TOP_K)
    tok_buf = (jnp.arange(rows, dtype=i32) % n).at[dest].set(tok)
    per = MOE_ITEM_SUBS
    n_items = (n_exp * (per - 1) + n_subs + per - 1) // per
    items_e = (subs_e + per - 1) // per
    item_end = jnp.cumsum(items_e)
    n_used = item_end[-1]
    idx = jnp.arange(n_items, dtype=i32)
    used = idx < n_used
    e_of = jnp.minimum(jnp.searchsorted(item_end, idx, side='right'), n_exp - 1).astype(i32)
    j = idx - (item_end - items_e)[e_of]
    item_nsub = jnp.where(used, jnp.clip(subs_e[e_of] - per * j, 0, per), 0)
    fill_first = jnp.minimum(sub_end[-1] + per * (idx - n_used), n_subs)
    item_fill = jnp.where(used, 0, jnp.minimum(n_subs - fill_first, per))
    item_row = jnp.where(used, row_start[e_of] + j * per * MOE_SUB, jnp.minimum(fill_first, n_subs - 1) * MOE_SUB)
    item_e = jnp.where(used, e_of, e_of[jnp.maximum(n_used - 1, 0)])
    y_buf = moe_ffn(hf[tok_buf], item_e.astype(i32), item_row.astype(i32), item_nsub.astype(i32),
                    (item_nsub + item_fill).astype(i32), w_gate, w_up, w_down)
    slot = dest.reshape(n, TOP_K)
    return h2 + wts[:, 0:1] * y_buf[slot[:, 0]] + wts[:, 1:2] * y_buf[slot[:, 1]]


COL_DT = ZX_COLS
COL_QKV = COL_DT + SSD_HEADS
COL_NSA_GATE = COL_QKV + QKV_COLS
COL_MERGE_GATE = COL_NSA_GATE + 3 * NSA_HEADS


def _small_w_in(w_in):
    pad = jnp.zeros(w_in.shape[:2] + (SMALL_COLS - SSD_HEADS - 3 * NSA_HEADS,), w_in.dtype)
    return jnp.concatenate([w_in[:, :, COL_DT:COL_QKV], w_in[:, :, COL_NSA_GATE:COL_MERGE_GATE], pad], axis=2)


def _layer(x, mem, norm1_w, w_in, ssd_conv_w, ssd_conv_b, ssd_dt_bias, ssd_a_log, ssd_d, ssd_norm_w,
           nsa_q_norm_w, nsa_k_norm_w, cmp_pe_k, cmp_w1_k, cmp_w2_k, cmp_pe_v, cmp_w1_v, cmp_w2_v,
           w_up_ssd, w_up_nsa, w_out, norm2_w, mem_norm_w, xq_w, xkv_w, x_q_norm_w, x_k_norm_w, xo_w,
           norm3_w, router_g_w, router_g_b, router_e_w, router_e_b, moe_w_gate, moe_w_up, moe_w_down):
    batch, seq, d = x.shape
    n = batch * seq
    xf = x.reshape(n, d)
    hn = rmsnorm_rows(xf, norm1_w)
    w_cols = lambda c0, nc: w_in[0, :, c0:c0 + nc].astype(BF16)
    zx = matmul(hn, w_cols(0, ZX_COLS), tm=IN_PROJ_TM, tn=IN_PROJ_TN)
    qkv = matmul(hn, w_cols(COL_QKV, QKV_COLS), tm=IN_PROJ_TM, tn=IN_PROJ_TN)
    mgates = matmul(hn, w_cols(COL_MERGE_GATE, GATE_COLS), tm=IN_PROJ_TM, tn=IN_PROJ_TN)
    small = matmul(hn, _small_w_in(w_in)[0].astype(BF16), tm=IN_PROJ_TM)
    y_ssd = ssd_branch(zx, small, ssd_conv_w, ssd_conv_b, ssd_dt_bias, ssd_a_log, ssd_d, ssd_norm_w, batch, seq)
    qh, kcn, vcr, ksn, vsn, kwn, vwn = nsa_prep(qkv, nsa_q_norm_w, nsa_k_norm_w, batch, seq)
    kcc = compress(kcn, cmp_pe_k, cmp_w1_k, cmp_w2_k, transpose_out=False)
    vcct = compress(vcr, cmp_pe_v, cmp_w1_v, cmp_w2_v, transpose_out=True)
    gl = small[:, SSD_HEADS:SSD_HEADS + 3 * NSA_HEADS].reshape(batch, seq, 3, NSA_KV_HEADS, NSA_REP)
    gl = gl.transpose(0, 3, 2, 4, 1).reshape(batch, NSA_KV_HEADS, 3 * NSA_REP, seq)
    y_nsa = nsa_attention(qh, kcc, vcct, ksn, vsn, kwn, vwn, gl, batch, seq)
    merged = merge_mixers(y_ssd, y_nsa, w_up_ssd.astype(BF16), w_up_nsa.astype(BF16), mgates)
    h1 = residual_matmul(xf, merged, w_out.astype(BF16))
    mlen = mem.shape[1]
    mn = rmsnorm_rows(mem.reshape(batch * mlen, d), mem_norm_w, tm=min(512, batch * mlen))
    kv = matmul(mn, xkv_w.astype(BF16))
    n_exp = router_e_w.shape[1]
    rpad = LANES - N_GROUPS - n_exp
    router_w = jnp.concatenate([router_g_w, router_e_w, jnp.zeros((d, rpad), F32)], axis=1).astype(BF16)
    router_b = jnp.concatenate([router_g_b, router_e_b, jnp.zeros((rpad,), F32)]).reshape(1, LANES)
    h2, hf, route = xattn_router(h1, kv, norm2_w, xq_w.astype(BF16), x_q_norm_w, x_k_norm_w, xo_w.astype(BF16),
                                 norm3_w, router_w, router_b, batch, seq)
    out = hier_moe(h2, hf, route, moe_w_gate, moe_w_up, moe_w_down)
    return out.reshape(batch, seq, d)


def kernel(x, mem, norm1_w, w_in, ssd_conv_w, ssd_conv_b, ssd_dt_bias, ssd_a_log, ssd_d, ssd_norm_w, nsa_q_norm_w, nsa_k_norm_w, cmp_pe_k, cmp_w1_k, cmp_w2_k, cmp_pe_v, cmp_w1_v, cmp_w2_v, w_up_ssd, w_up_nsa, w_out, norm2_w, mem_norm_w, xq_w, xkv_w, x_q_norm_w, x_k_norm_w, xo_w, norm3_w, router_g_w, router_g_b, router_e_w, router_e_b, moe_w_gate, moe_w_up, moe_w_down):
    h = x
    for l in range(norm1_w.shape[0]):
        h = _layer(h, mem, norm1_w[l], w_in[l:l + 1], ssd_conv_w[l], ssd_conv_b[l], ssd_dt_bias[l], ssd_a_log[l], ssd_d[l],
                   ssd_norm_w[l], nsa_q_norm_w[l], nsa_k_norm_w[l], cmp_pe_k[l], cmp_w1_k[l], cmp_w2_k[l], cmp_pe_v[l],
                   cmp_w1_v[l], cmp_w2_v[l], w_up_ssd[l], w_up_nsa[l], w_out[l], norm2_w[l], mem_norm_w[l], xq_w[l],
                   xkv_w[l], x_q_norm_w[l], x_k_norm_w[l], xo_w[l], norm3_w[l], router_g_w[l], router_g_b[l],
                   router_e_w[l], router_e_b[l], moe_w_gate[l], moe_w_up[l], moe_w_down[l])
    return h.astype(x.dtype)
```

```python
import functools

import numpy as np
import jax
import jax.numpy as jnp
from jax import lax
from jax.experimental import pallas as pl
from jax.experimental.pallas import tpu as pltpu

F32 = jnp.float32
BF16 = jnp.bfloat16

D_MODEL = 2048
SSD_INNER = 4096
SSD_HEAD_DIM = 64
SSD_HEADS = 64
SSD_GROUPS = 8
SSD_STATE = 128
SSD_CONV = 4
SSD_CHUNK = 256
HEADS_PER_GROUP = SSD_HEADS // SSD_GROUPS
GROUP_WIDTH = SSD_INNER // SSD_GROUPS
HEAD_DIM = 128
NSA_HEADS = 16
NSA_KV_HEADS = 4
NSA_REP = NSA_HEADS // NSA_KV_HEADS
NSA_WIDTH = NSA_HEADS * HEAD_DIM
NSA_KV_WIDTH = NSA_KV_HEADS * HEAD_DIM
CMP_BLK = 32
CMP_STRIDE = 16
CMP_HID = 256
SEL_BLK = 64
SEL_SHIFT = 6
N_SEL = 16
WINDOW = 512
Q_BLK = 128
SEL_KEY_TILE = 256
FORCE_SCORE = 1.0e4
X_HEADS = 4
X_WIDTH = X_HEADS * HEAD_DIM
N_GROUPS = 8
EXPERTS_PER_GROUP = 8
EPG_SHIFT = 3
N_EXPERTS = 64
TOP_K = 2
EXPERT_HIDDEN = 1408
MOE_SUB = 128
MOE_ITEM_SUBS = 8
MOE_HID_TILE = 256
ROPE_THETA = 10000.0
EPS = 1e-6
NEG_INF = -1e30
Q_PRESCALE = HEAD_DIM ** -0.5 * float(np.log2(np.e))
LANES = 128
VMEM_LIMIT = 56 * 1024 * 1024

ZX_COLS = SSD_INNER + SSD_INNER + 2 * SSD_GROUPS * SSD_STATE
QKV_COLS = NSA_WIDTH + 6 * NSA_KV_WIDTH
GATE_COLS = 2 * D_MODEL
SMALL_COLS = LANES
IN_PROJ_TM, IN_PROJ_TN = 2048, 1024


def _params(*sem):
    return pltpu.CompilerParams(dimension_semantics=sem, vmem_limit_bytes=VMEM_LIMIT)


def _rms(x, w):
    ms = jnp.mean(x * x, axis=-1, keepdims=True)
    return x * lax.rsqrt(ms + EPS) * w


def _sigmoid(x):
    return 1.0 / (1.0 + jnp.exp(-x))


def _silu(x):
    return x * _sigmoid(x)


def _softplus(x):
    return jnp.maximum(x, 0.0) + jnp.log1p(jnp.exp(-jnp.abs(x)))


def _dot(a, b):
    return jnp.dot(a, b, preferred_element_type=F32)


def _dot_nt(a, b):
    return lax.dot_general(a, b, (((1,), (1,)), ((), ())), preferred_element_type=F32)


def _masked_softmax(s, mask):
    s = jnp.where(mask, s, NEG_INF)
    e = jnp.exp(s - jnp.max(s, axis=-1, keepdims=True))
    p = e / jnp.sum(e, axis=-1, keepdims=True)
    return jnp.where(mask, p, 0.0)


def _rmsnorm_kernel(x_ref, w_ref, o_ref):
    o_ref[...] = _rms(x_ref[...], w_ref[...]).astype(o_ref.dtype)


def rmsnorm_rows(x, w, tm=512):
    m, d = x.shape
    return pl.pallas_call(
        _rmsnorm_kernel,
        grid=(m // tm,),
        in_specs=[pl.BlockSpec((tm, d), lambda i: (i, 0)), pl.BlockSpec((1, d), lambda i: (0, 0))],
        out_specs=pl.BlockSpec((tm, d), lambda i: (i, 0)),
        out_shape=jax.ShapeDtypeStruct((m, d), BF16),
        compiler_params=_params("parallel"),
        name="rmsnorm_rows",
    )(x, w.reshape(1, d))


def _matmul_kernel(x_ref, w_ref, o_ref):
    o_ref[...] = _dot(x_ref[...], w_ref[...]).astype(o_ref.dtype)


def matmul(x, w, out_dtype=F32, tm=512, tn=512):
    m, k = x.shape
    n = w.shape[1]
    tm, tn = min(tm, m), min(tn, n)
    return pl.pallas_call(
        _matmul_kernel,
        grid=(m // tm, n // tn),
        in_specs=[pl.BlockSpec((tm, k), lambda i, j: (i, 0)), pl.BlockSpec((k, tn), lambda i, j: (0, j))],
        out_specs=pl.BlockSpec((tm, tn), lambda i, j: (i, j)),
        out_shape=jax.ShapeDtypeStruct((m, n), out_dtype),
        compiler_params=_params("parallel", "arbitrary"),
        name="matmul",
    )(x, w)


def _split3(a):
    h = a.astype(BF16)
    r = a - h.astype(F32)
    m = r.astype(BF16)
    return h, m, (r - m.astype(F32)).astype(BF16)


def _ssd_kernel(z_ref, x_ref, b_ref, c_ref, dt_ref, dtt_ref, cwx_ref, cwb_ref, cwc_ref, cbx_ref, cbb_ref, cbc_ref,
                hp_ref, hpt_ref, nw_ref, y_ref, state_ref, ext_x_ref, ext_b_ref, ext_c_ref):
    L = x_ref.shape[0]
    H = L // 2
    chunk = pl.program_id(2)

    @pl.when(chunk == 0)
    def _():
        state_ref[...] = jnp.zeros_like(state_ref)
        ext_x_ref[...] = jnp.zeros_like(ext_x_ref)
        ext_b_ref[...] = jnp.zeros_like(ext_b_ref)
        ext_c_ref[...] = jnp.zeros_like(ext_c_ref)

    def conv_silu(u_ref, ext_ref, w, b):
        ext_ref[0:8, :] = ext_ref[L:L + 8, :]
        u = u_ref[...]
        ext_ref[8:8 + L, :] = u
        acc = b + w[SSD_CONV - 1:SSD_CONV, :] * u
        for k in range(SSD_CONV - 1):
            acc = acc + w[k:k + 1, :] * ext_ref[pl.ds(8 - (SSD_CONV - 1) + k, L), :]
        return _silu(acc)

    xs = conv_silu(x_ref, ext_x_ref, cwx_ref[...], cbx_ref[...])
    bm = conv_silu(b_ref, ext_b_ref, cwb_ref[...], cbb_ref[...])
    cm = conv_silu(c_ref, ext_c_ref, cwc_ref[...], cbc_ref[...])
    bm16, cm16 = bm.astype(BF16), cm.astype(BF16)

    hp = hp_ref[0]
    hpt = hpt_ref[0]
    dt = _softplus(dt_ref[0] + hp[0:1, :])
    dtt = _softplus(dtt_ref[0] + hpt[:, 0:1])
    da = dt * (-jnp.exp(hp[1:2, :]))
    dat = dtt * (-jnp.exp(hpt[:, 1:2]))
    row = lax.broadcasted_iota(jnp.int32, (L, L), 0)
    col = lax.broadcasted_iota(jnp.int32, (L, L), 1)
    tril = jnp.where(row >= col, 1.0, 0.0).astype(BF16)
    triu = jnp.where(row <= col, 1.0, 0.0).astype(BF16)
    acum = sum(_dot(tril, part) for part in _split3(da))
    acumt = sum(_dot(part, triu) for part in _split3(dat))
    last = acum[L - 1:L, :]
    dte = jnp.exp(last - acum)
    eac = jnp.exp(acum)
    cdec = jnp.exp(last)

    cb = _dot_nt(cm16, bm16)
    cb00, cb10, cb11 = cb[:H, :H], cb[H:, :H], cb[H:, H:]
    tri = lax.broadcasted_iota(jnp.int32, (H, H), 0) >= lax.broadcasted_iota(jnp.int32, (H, H), 1)
    lane = lax.broadcasted_iota(jnp.int32, (L, LANES), 1)
    first = lane < SSD_HEAD_DIM
    srow = lax.broadcasted_iota(jnp.int32, (LANES, SSD_STATE), 0) < SSD_HEAD_DIM
    z = z_ref[...]
    d_skip = hp[2:3, :]
    outs = []
    for p in range(HEADS_PER_GROUP // 2):
        h0, h1 = 2 * p, 2 * p + 1

        def pick(a, rows=first):
            return jnp.where(rows, a[:, h0:h0 + 1], a[:, h1:h1 + 1])

        xp = xs[:, p * LANES:(p + 1) * LANES]
        xdt = xp * pick(dt)
        xdt16 = xdt.astype(BF16)
        y_top = jnp.zeros((H, LANES), F32)
        y_bot = jnp.zeros((H, LANES), F32)
        for h, msk in ((h0, first), (h1, jnp.logical_not(first))):
            a_col, a_row = acum[:, h:h + 1], acumt[h:h + 1, :]
            d00 = jnp.exp(jnp.where(tri, a_col[:H] - a_row[:, :H], -jnp.inf))
            d10 = jnp.exp(a_col[H:] - a_row[:, :H])
            d11 = jnp.exp(jnp.where(tri, a_col[H:] - a_row[:, H:], -jnp.inf))
            xh = jnp.where(msk, xdt16, jnp.zeros_like(xdt16))
            y_top = y_top + _dot((cb00 * d00).astype(BF16), xh[:H])
            y_bot = y_bot + _dot((cb10 * d10).astype(BF16), xh[:H]) + _dot((cb11 * d11).astype(BF16), xh[H:])
        y = jnp.concatenate([y_top, y_bot], axis=0)
        hprev = state_ref[p]
        y = y + _dot_nt(cm16, hprev.astype(BF16)) * pick(eac)
        xe_t = (xdt * pick(dte)).T.astype(BF16)
        state_ref[p] = hprev * jnp.where(srow, cdec[:, h0:h0 + 1], cdec[:, h1:h1 + 1]) + _dot(xe_t, bm16)
        outs.append(y + pick(d_skip, first[0:1, :]) * xp)
    y = jnp.concatenate(outs, axis=1) * _silu(z)
    y_ref[...] = _rms(y, nw_ref[...]).astype(y_ref.dtype)


def ssd_branch(zx, small, conv_w, conv_b, dt_bias, a_log, d_skip, norm_w, batch, seq):
    n = batch * seq
    L = min(SSD_CHUNK, seq)
    nc = seq // L
    g, hpg = SSD_GROUPS, HEADS_PER_GROUP
    dt_raw = small[:, :SSD_HEADS].reshape(n, g, hpg)
    dt_g = dt_raw.transpose(1, 0, 2)
    dtt_g = dt_raw.transpose(1, 2, 0)
    hp = jnp.stack([dt_bias, a_log, d_skip], axis=0).reshape(3, g, hpg).transpose(1, 0, 2)
    hp = jnp.pad(hp, ((0, 0), (0, 8 - 3), (0, 0)))
    hpt = hp.transpose(0, 2, 1)
    cw = conv_w
    cbias = conv_b.reshape(1, -1)
    nxb = SSD_INNER // GROUP_WIDTH
    rowblk = lambda b, gi, c: b * nc + c
    bc0 = SSD_INNER // SSD_STATE
    grid = (batch, g, nc)
    in_specs = [
        pl.BlockSpec((L, GROUP_WIDTH), lambda b, gi, c: (rowblk(b, gi, c), gi)),
        pl.BlockSpec((L, GROUP_WIDTH), lambda b, gi, c: (rowblk(b, gi, c), nxb + gi)),
        pl.BlockSpec((L, SSD_STATE), lambda b, gi, c: (rowblk(b, gi, c), 2 * bc0 + gi)),
        pl.BlockSpec((L, SSD_STATE), lambda b, gi, c: (rowblk(b, gi, c), 2 * bc0 + g + gi)),
        pl.BlockSpec((1, L, hpg), lambda b, gi, c: (gi, rowblk(b, gi, c), 0)),
        pl.BlockSpec((1, hpg, L), lambda b, gi, c: (gi, 0, rowblk(b, gi, c))),
        pl.BlockSpec((SSD_CONV, GROUP_WIDTH), lambda b, gi, c: (0, gi)),
        pl.BlockSpec((SSD_CONV, SSD_STATE), lambda b, gi, c: (0, bc0 + gi)),
        pl.BlockSpec((SSD_CONV, SSD_STATE), lambda b, gi, c: (0, bc0 + g + gi)),
        pl.BlockSpec((1, GROUP_WIDTH), lambda b, gi, c: (0, gi)),
        pl.BlockSpec((1, SSD_STATE), lambda b, gi, c: (0, bc0 + gi)),
        pl.BlockSpec((1, SSD_STATE), lambda b, gi, c: (0, bc0 + g + gi)),
        pl.BlockSpec((1, 8, hpg), lambda b, gi, c: (gi, 0, 0)),
        pl.BlockSpec((1, hpg, 8), lambda b, gi, c: (gi, 0, 0)),
        pl.BlockSpec((1, GROUP_WIDTH), lambda b, gi, c: (0, gi)),
    ]
    return pl.pallas_call(
        _ssd_kernel,
        grid=grid,
        in_specs=in_specs,
        out_specs=pl.BlockSpec((L, GROUP_WIDTH), lambda b, gi, c: (rowblk(b, gi, c), gi)),
        out_shape=jax.ShapeDtypeStruct((n, SSD_INNER), BF16),
        scratch_shapes=[pltpu.VMEM((HEADS_PER_GROUP // 2, LANES, SSD_STATE), F32),
                        pltpu.VMEM((L + 8, GROUP_WIDTH), F32), pltpu.VMEM((L + 8, SSD_STATE), F32),
                        pltpu.VMEM((L + 8, SSD_STATE), F32)],
        compiler_params=_params("parallel", "parallel", "arbitrary"),
        name="ssd_branch",
    )(zx, zx, zx, zx, dt_g, dtt_g, cw, cw, cw, cbias, cbias, cbias, hp, hpt, norm_w.reshape(1, -1))


def _nsa_prep_kernel(q_ref, kc_ref, vc_ref, ks_ref, vs_ref, kw_ref, vw_ref, cos_ref, sin_ref, qw_ref, kw3_ref,
                     qo_ref, kco_ref, vco_ref, kso_ref, vso_ref, kwo_ref, vwo_ref):
    cos, sin = cos_ref[...], sin_ref[...]

    def norm_rope(x, w):
        y = _rms(x, w)
        return y * cos + pltpu.roll(y, HEAD_DIM // 2, 1) * sin

    qw = qw_ref[...]
    tt = q_ref.shape[0]
    for g in range(NSA_KV_HEADS):
        sl = slice(g * HEAD_DIM, (g + 1) * HEAD_DIM)
        for r in range(NSA_REP):
            h = g * NSA_REP + r
            qh = norm_rope(q_ref[:, h * HEAD_DIM:(h + 1) * HEAD_DIM], qw) * Q_PRESCALE
            qo_ref[0, g, r] = qh.astype(qo_ref.dtype)
        kco_ref[0, g] = norm_rope(kc_ref[:, sl], kw3_ref[0:1, :])
        vco_ref[0, g] = vc_ref[:, sl]
        kso_ref[0, g] = norm_rope(ks_ref[:, sl], kw3_ref[1:2, :]).astype(kso_ref.dtype)
        kwo_ref[0, g] = norm_rope(kw_ref[:, sl], kw3_ref[2:3, :]).astype(kwo_ref.dtype)
        vst = vs_ref[:, sl].T.astype(vso_ref.dtype)
        for j in range(tt // SEL_KEY_TILE):
            vso_ref[0, g, j] = vst[:, j * SEL_KEY_TILE:(j + 1) * SEL_KEY_TILE]
        vwt = vw_ref[:, sl].T.astype(vwo_ref.dtype)
        for j in range(tt // Q_BLK):
            vwo_ref[0, g, j] = vwt[:, j * Q_BLK:(j + 1) * Q_BLK]


def nsa_prep(qkv, q_norm_w, k_norm_w, batch, seq, tt=2 * SEL_KEY_TILE):
    half = HEAD_DIM // 2
    inv = ROPE_THETA ** (-jnp.arange(half, dtype=F32) / half)
    ang = jnp.arange(seq).astype(F32)[:, None] * inv[None, :]
    cos = jnp.concatenate([jnp.cos(ang), jnp.cos(ang)], axis=-1)
    sin = jnp.concatenate([-jnp.sin(ang), jnp.sin(ang)], axis=-1)
    nt = seq // tt
    kvb = NSA_WIDTH // NSA_KV_WIDTH
    kv_spec = lambda j: pl.BlockSpec((tt, NSA_KV_WIDTH), lambda b, t: (b * nt + t, kvb + j))
    head_spec = pl.BlockSpec((1, NSA_KV_HEADS, tt, HEAD_DIM), lambda b, t: (b, 0, t, 0))
    head_shape = lambda dt: jax.ShapeDtypeStruct((batch, NSA_KV_HEADS, seq, HEAD_DIM), dt)
    return pl.pallas_call(
        _nsa_prep_kernel,
        grid=(batch, nt),
        in_specs=[pl.BlockSpec((tt, NSA_WIDTH), lambda b, t: (b * nt + t, 0))] + [kv_spec(j) for j in range(6)] + [
            pl.BlockSpec((tt, HEAD_DIM), lambda b, t: (t, 0)), pl.BlockSpec((tt, HEAD_DIM), lambda b, t: (t, 0)),
            pl.BlockSpec((1, HEAD_DIM), lambda b, t: (0, 0)), pl.BlockSpec((3, HEAD_DIM), lambda b, t: (0, 0))],
        out_specs=[pl.BlockSpec((1, NSA_KV_HEADS, NSA_REP, tt, HEAD_DIM), lambda b, t: (b, 0, 0, t, 0)),
                   head_spec, head_spec, head_spec,
                   pl.BlockSpec((1, NSA_KV_HEADS, tt // SEL_KEY_TILE, HEAD_DIM, SEL_KEY_TILE),
                                lambda b, t: (b, 0, t, 0, 0)),
                   head_spec,
                   pl.BlockSpec((1, NSA_KV_HEADS, tt // Q_BLK, HEAD_DIM, Q_BLK), lambda b, t: (b, 0, t, 0, 0))],
        out_shape=[jax.ShapeDtypeStruct((batch, NSA_KV_HEADS, NSA_REP, seq, HEAD_DIM), BF16),
                   head_shape(F32), head_shape(F32), head_shape(BF16),
                   jax.ShapeDtypeStruct((batch, NSA_KV_HEADS, seq // SEL_KEY_TILE, HEAD_DIM, SEL_KEY_TILE), BF16),
                   head_shape(BF16),
                   jax.ShapeDtypeStruct((batch, NSA_KV_HEADS, seq // Q_BLK, HEAD_DIM, Q_BLK), BF16)],
        compiler_params=_params("parallel", "parallel"),
        name="nsa_prep",
    )(qkv, qkv, qkv, qkv, qkv, qkv, qkv, cos, sin, q_norm_w.reshape(1, -1), k_norm_w)


def _compress_kernel(x_ref, pe_ref, w1_ref, w2_ref, o_ref, *, ncmp, transpose_out):
    nseg = x_ref.shape[2] // CMP_STRIDE
    u = jnp.zeros((nseg, CMP_HID), F32)
    v = jnp.zeros((nseg, CMP_HID), F32)
    for j in range(CMP_STRIDE):
        xj = x_ref[0, 0, pl.ds(j, nseg, stride=CMP_STRIDE), :]
        u = u + _dot((xj + pe_ref[j:j + 1, :]).astype(BF16), w1_ref[j])
        v = v + _dot((xj + pe_ref[CMP_STRIDE + j:CMP_STRIDE + j + 1, :]).astype(BF16), w1_ref[CMP_STRIDE + j])
    hid = _silu(u + pltpu.roll(v, nseg - 1, 0))
    comp = _dot(hid.astype(BF16), w2_ref[...])
    rowi = lax.broadcasted_iota(jnp.int32, comp.shape, 0)
    comp = jnp.where(rowi < ncmp, comp, 0.0)
    o_ref[0, 0] = (comp.T if transpose_out else comp).astype(o_ref.dtype)


def compress(raw, pe, w1, w2, transpose_out):
    b, g, t, hd = raw.shape
    nseg = t // CMP_STRIDE
    full = lambda shape: pl.BlockSpec(shape, lambda i, j: (0,) * len(shape))
    out_dims = (hd, nseg) if transpose_out else (nseg, hd)
    return pl.pallas_call(
        functools.partial(_compress_kernel, ncmp=nseg - 1, transpose_out=transpose_out),
        grid=(b, g),
        in_specs=[pl.BlockSpec((1, 1, t, hd), lambda i, j: (i, j, 0, 0)), full((CMP_BLK, hd)),
                  full((CMP_BLK, hd, CMP_HID)), full((CMP_HID, hd))],
        out_specs=pl.BlockSpec((1, 1) + out_dims, lambda i, j: (i, j, 0, 0)),
        out_shape=jax.ShapeDtypeStruct((b, g) + out_dims, BF16),
        compiler_params=_params("parallel", "parallel"),
        name="nsa_compress",
    )(raw, pe, w1.astype(BF16), w2.astype(BF16))


def _nsa_attn_kernel(q_ref, kc_ref, vct_ref, ks_ref, vst_ref, kw_ref, vwt_ref, glt_ref, ovt_ref, o_ref,
                     selb_ref, acc_ref, ocw_ref, s_ref, p_ref, *, seq, n_sel):
    qb = pl.program_id(2)
    q0 = qb * Q_BLK
    rows = NSA_REP * Q_BLK
    q = q_ref[0, 0].reshape(rows, HEAD_DIM)
    tq = q0 + lax.broadcasted_iota(jnp.int32, (1, Q_BLK), 1)
    head = lambda r: slice(r * Q_BLK, (r + 1) * Q_BLK)

    ncp = kc_ref.shape[2]
    wk = WINDOW + Q_BLK
    kt_sz = vst_ref.shape[4]
    start = pl.multiple_of(jnp.clip(q0 - WINDOW, 0, seq - wk), Q_BLK)
    s_cw = _dot_nt(jnp.concatenate([kc_ref[0, 0], kw_ref[0, 0, pl.ds(start, wk), :], ks_ref[0, 0, 0:kt_sz, :]],
                                   axis=0), q)
    s_c, s_w = s_cw[:ncp], s_cw[ncp:ncp + wk]
    s_ref[0] = s_cw[ncp + wk:]

    cend = lax.broadcasted_iota(jnp.int32, (ncp, 1), 0) * CMP_STRIDE + (CMP_BLK - 1)
    m_c = cend <= tq
    ps = []
    for r in range(NSA_REP):
        s = jnp.where(m_c, s_c[:, head(r)], NEG_INF)
        e = jnp.exp2(s - jnp.max(s, axis=0, keepdims=True))
        p = jnp.where(m_c, e * (1.0 / jnp.sum(e, axis=0, keepdims=True)), 0.0)
        ps.append(p.astype(BF16))
    p_c = jnp.concatenate(ps, axis=1)
    nsb = ovt_ref.shape[0]
    oc_imp = _dot(jnp.concatenate([vct_ref[0, 0], ovt_ref[...]], axis=0), p_c)
    o_c, imp4 = oc_imp[:HEAD_DIM], oc_imp[HEAD_DIM:]

    imp = imp4[:, head(0)]
    for r in range(1, NSA_REP):
        imp = imp + imp4[:, head(r)]
    blk_t = (q0 + lax.broadcasted_iota(jnp.int32, (nsb, Q_BLK), 1)) >> SEL_SHIFT
    sb = lax.broadcasted_iota(jnp.int32, (nsb, Q_BLK), 0)
    forced = (sb == 0) | (sb == blk_t) | (sb == blk_t - 1)
    imp = jnp.where(sb <= blk_t, jnp.where(forced, FORCE_SCORE, imp), -jnp.inf)
    rank = jnp.zeros((nsb, Q_BLK), F32)
    for i in range(nsb):
        ri = imp[i:i + 1, :]
        beats = (ri > imp) | ((ri == imp) & (sb > i))
        rank = rank + jnp.where(beats, 1.0, 0.0)
    selb_ref[...] = jnp.where(rank < n_sel, 0.0, NEG_INF)

    dist = tq - (start + lax.broadcasted_iota(jnp.int32, (wk, 1), 0))
    bias_w = jnp.where((dist >= 0) & (dist < WINDOW), 0.0, NEG_INF)
    ps, ls = [], []
    for r in range(NSA_REP):
        s = s_w[:, head(r)] + bias_w
        p = jnp.exp2(s - jnp.max(s, axis=0, keepdims=True))
        ls.append(jnp.sum(p, axis=0, keepdims=True))
        ps.append(p.astype(BF16))
    t0 = start // Q_BLK
    vw_t = jnp.concatenate([vwt_ref[0, 0, t0 + j] for j in range(wk // Q_BLK)], axis=1)
    o_w = _dot(vw_t, jnp.concatenate(ps, axis=1)) * (1.0 / jnp.concatenate(ls, axis=1))

    gate = _sigmoid(glt_ref[0, 0])
    for r in range(NSA_REP):
        ocw_ref[:, head(r)] = (gate[r:r + 1, :] * o_c[:, head(r)]
                               + gate[2 * NSA_REP + r:2 * NSA_REP + r + 1, :] * o_w[:, head(r)])

    blocks_per_tile = kt_sz // SEL_BLK
    n_all = seq // kt_sz
    acc_ref[...] = jnp.zeros_like(acc_ref)
    p_ref[1] = jnp.zeros(p_ref.shape[1:], p_ref.dtype)

    def sel_tile(kt, carry):
        m, l, alpha_prev = carry
        slot = kt & 1
        s_all = s_ref[slot]
        p_prev = p_ref[1 - slot]
        k_next = pl.multiple_of(jnp.minimum(kt + 1, n_all - 1) * kt_sz, kt_sz)
        s_ref[1 - slot] = _dot_nt(ks_ref[0, 0, pl.ds(k_next, kt_sz), :], q)
        pv_prev = _dot(vst_ref[0, 0, jnp.maximum(kt - 1, 0)], p_prev)
        acc_ref[...] = acc_ref[...] * alpha_prev + pv_prev
        k0 = kt * kt_sz
        kpos = k0 + lax.broadcasted_iota(jnp.int32, (kt_sz, 1), 0)
        bias = jnp.concatenate(
            [jnp.broadcast_to(selb_ref[pl.ds(kt * blocks_per_tile + i, 1), :], (SEL_BLK, Q_BLK))
             for i in range(blocks_per_tile)], axis=0)
        bias = jnp.where(kpos <= tq, bias, NEG_INF)
        ps, ms, ls, alphas = [], [], [], []
        for r in range(NSA_REP):
            s = s_all[:, head(r)] + bias
            m_old = m[:, head(r)]
            m_new = jnp.maximum(m_old, jnp.max(s, axis=0, keepdims=True))
            p = jnp.exp2(s - m_new)
            alpha = jnp.exp2(m_old - m_new)
            ls.append(alpha * l[:, head(r)] + jnp.sum(p, axis=0, keepdims=True))
            ms.append(m_new)
            alphas.append(alpha)
            ps.append(p.astype(BF16))
        p_ref[slot] = jnp.concatenate(ps, axis=1)
        return jnp.concatenate(ms, axis=1), jnp.concatenate(ls, axis=1), jnp.concatenate(alphas, axis=1)

    n_tiles = (q0 + Q_BLK + kt_sz - 1) // kt_sz
    init = (jnp.full((1, rows), NEG_INF, F32), jnp.zeros((1, rows), F32), jnp.ones((1, rows), F32))
    _, l_s, alpha_last = lax.fori_loop(0, n_tiles, sel_tile, init)
    pv_last = _dot(vst_ref[0, 0, n_tiles - 1], p_ref[(n_tiles - 1) & 1])
    o_s = (acc_ref[...] * alpha_last + pv_last) * (1.0 / l_s)

    for r in range(NSA_REP):
        out_t = ocw_ref[:, head(r)] + gate[NSA_REP + r:NSA_REP + r + 1, :] * o_s[:, head(r)]
        o_ref[:, r * HEAD_DIM:(r + 1) * HEAD_DIM] = out_t.T.astype(o_ref.dtype)


def nsa_attention(qh, kcc, vcct, ksn, vst, kwn, vwt, gate_logits_t, batch, seq):
    assert seq % SEL_KEY_TILE == 0 and seq >= WINDOW + Q_BLK
    nq = seq // Q_BLK
    nsb = seq // SEL_BLK
    n_sel = min(N_SEL, nsb)
    ncp = kcc.shape[2]
    ncmp = (seq - CMP_BLK) // CMP_STRIDE + 1
    ci = np.arange(ncp)[None, :]
    sj = np.arange(nsb)[:, None]
    ovt = ((ci * CMP_STRIDE < (sj + 1) * SEL_BLK) & (ci * CMP_STRIDE + CMP_BLK > sj * SEL_BLK) & (ci < ncmp))
    ovt = jnp.asarray(ovt, BF16)
    seq_spec = pl.BlockSpec((1, 1, seq, HEAD_DIM), lambda b, g, i: (b, g, 0, 0))
    cmp_spec = pl.BlockSpec((1, 1, ncp, HEAD_DIM), lambda b, g, i: (b, g, 0, 0))
    tiled = lambda a: pl.BlockSpec((1, 1) + a.shape[2:], lambda b, g, i: (b, g, 0, 0, 0))
    return pl.pallas_call(
        functools.partial(_nsa_attn_kernel, seq=seq, n_sel=n_sel),
        grid=(batch, NSA_KV_HEADS, nq),
        in_specs=[pl.BlockSpec((1, 1, NSA_REP, Q_BLK, HEAD_DIM), lambda b, g, i: (b, g, 0, i, 0)),
                  cmp_spec, pl.BlockSpec((1, 1, HEAD_DIM, ncp), lambda b, g, i: (b, g, 0, 0)),
                  seq_spec, tiled(vst), seq_spec, tiled(vwt),
                  pl.BlockSpec((1, 1, 3 * NSA_REP, Q_BLK), lambda b, g, i: (b, g, 0, i)),
                  pl.BlockSpec((nsb, ncp), lambda b, g, i: (0, 0))],
        out_specs=pl.BlockSpec((Q_BLK, NSA_REP * HEAD_DIM), lambda b, g, i: (b * nq + i, g)),
        out_shape=jax.ShapeDtypeStruct((batch * seq, NSA_WIDTH), BF16),
        scratch_shapes=[pltpu.VMEM((nsb, Q_BLK), F32), pltpu.VMEM((HEAD_DIM, NSA_REP * Q_BLK), F32),
                        pltpu.VMEM((HEAD_DIM, NSA_REP * Q_BLK), F32),
                        pltpu.VMEM((2, SEL_KEY_TILE, NSA_REP * Q_BLK), F32),
                        pltpu.VMEM((2, SEL_KEY_TILE, NSA_REP * Q_BLK), BF16)],
        compiler_params=_params("parallel", "parallel", "arbitrary"),
        name="nsa_attention",
    )(qh, kcc, vcct, ksn, vst, kwn, vwt, gate_logits_t, ovt)


def _merge_kernel(ys_ref, yn_ref, ws_ref, wn_ref, gs_ref, gn_ref, o_ref):
    up_s = _dot(ys_ref[...], ws_ref[...])
    up_n = _dot(yn_ref[...], wn_ref[...])
    o_ref[...] = (_sigmoid(gs_ref[...]) * up_s + _sigmoid(gn_ref[...]) * up_n).astype(o_ref.dtype)


def merge_mixers(y_ssd, y_nsa, w_up_ssd, w_up_nsa, gates, tm=1024, tn=512):
    m = y_ssd.shape[0]
    nb = D_MODEL // tn
    return pl.pallas_call(
        _merge_kernel,
        grid=(m // tm, nb),
        in_specs=[pl.BlockSpec((tm, SSD_INNER), lambda i, j: (i, 0)), pl.BlockSpec((tm, NSA_WIDTH), lambda i, j: (i, 0)),
                  pl.BlockSpec((SSD_INNER, tn), lambda i, j: (0, j)), pl.BlockSpec((NSA_WIDTH, tn), lambda i, j: (0, j)),
                  pl.BlockSpec((tm, tn), lambda i, j: (i, j)), pl.BlockSpec((tm, tn), lambda i, j: (i, nb + j))],
        out_specs=pl.BlockSpec((tm, tn), lambda i, j: (i, j)),
        out_shape=jax.ShapeDtypeStruct((m, D_MODEL), BF16),
        compiler_params=_params("parallel", "arbitrary"),
        name="merge_mixers",
    )(y_ssd, y_nsa, w_up_ssd, w_up_nsa, gates, gates)


def _residual_matmul_kernel(h_ref, x_ref, w_ref, o_ref):
    o_ref[...] = h_ref[...] + _dot(x_ref[...], w_ref[...])


def residual_matmul(h, x, w, tm=1024, tn=1024):
    m, k = x.shape
    n = w.shape[1]
    return pl.pallas_call(
        _residual_matmul_kernel,
        grid=(m // tm, n // tn),
        in_specs=[pl.BlockSpec((tm, tn), lambda i, j: (i, j)), pl.BlockSpec((tm, k), lambda i, j: (i, 0)),
                  pl.BlockSpec((k, tn), lambda i, j: (0, j))],
        out_specs=pl.BlockSpec((tm, tn), lambda i, j: (i, j)),
        out_shape=jax.ShapeDtypeStruct((m, n), F32),
        compiler_params=_params("parallel", "arbitrary"),
        name="residual_matmul",
    )(h, x, w)


def _xattn_router_kernel(h_ref, kv_ref, n2_ref, wq_ref, qn_ref, kn_ref, wo_ref, n3_ref, rw_ref, rb_ref,
                         h2_ref, hf_ref, rt_ref):
    h = h_ref[...]
    qp = _dot(_rms(h, n2_ref[...]).astype(BF16), wq_ref[...])
    kv = kv_ref[...]
    scale = HEAD_DIM ** -0.5
    heads = []
    for hd in range(X_HEADS):
        sl = slice(hd * HEAD_DIM, (hd + 1) * HEAD_DIM)
        qh = _rms(qp[:, sl], qn_ref[...]).astype(BF16)
        kh = _rms(kv[:, sl], kn_ref[...]).astype(BF16)
        vh = kv[:, X_WIDTH + hd * HEAD_DIM:X_WIDTH + (hd + 1) * HEAD_DIM].astype(BF16)
        s = _dot_nt(qh, kh) * scale
        e = jnp.exp(s - jnp.max(s, axis=-1, keepdims=True))
        p = e / jnp.sum(e, axis=-1, keepdims=True)
        heads.append(_dot(p.astype(BF16), vh))
    o = jnp.concatenate(heads, axis=1).astype(BF16)
    h2 = h + _dot(o, wo_ref[...])
    h2_ref[...] = h2
    hf = _rms(h2, n3_ref[...]).astype(BF16)
    hf_ref[...] = hf
    lg = _dot(hf, rw_ref[...]) + rb_ref[...]

    lane = lax.broadcasted_iota(jnp.int32, lg.shape, 1)
    rmax = lambda v: jnp.max(v, axis=-1, keepdims=True)
    rsum = lambda v: jnp.sum(v, axis=-1, keepdims=True)
    first_lane = lambda hit: jnp.min(jnp.where(hit, lane, LANES), axis=-1, keepdims=True)
    is_g = lane < N_GROUPS
    eg = jnp.where(is_g, jnp.exp(lg - rmax(jnp.where(is_g, lg, -jnp.inf))), 0.0)
    pg = eg / rsum(eg)
    pg_top = rmax(pg)
    grp = first_lane(is_g & (pg == pg_top))
    in_grp = (lane >= N_GROUPS) & (((lane - N_GROUPS) >> EPG_SHIFT) == grp)
    ee = jnp.where(in_grp, jnp.exp(lg - rmax(jnp.where(in_grp, lg, -jnp.inf))), 0.0)
    pe = jnp.where(in_grp, ee / rsum(ee), -1.0)
    p1 = rmax(pe)
    i1 = first_lane(pe == p1)
    pe2 = jnp.where(lane == i1, -1.0, pe)
    p2 = rmax(pe2)
    i2 = first_lane(pe2 == p2)
    psum = p1 + p2
    route = jnp.where(lane == 0, (i1 - N_GROUPS).astype(F32),
                      jnp.where(lane == 1, (i2 - N_GROUPS).astype(F32),
                                jnp.where(lane == 2, pg_top * p1 / psum,
                                          jnp.where(lane == 3, pg_top * p2 / psum, 0.0))))
    rt_ref[...] = route


def xattn_router(h1, kv, norm2_w, wq, q_norm_w, k_norm_w, wo, norm3_w, router_w, router_b, batch, seq, tm=512):
    n = batch * seq
    tm = min(tm, seq)
    per_seq = seq // tm
    mlen = kv.shape[0] // batch
    full = lambda shape: pl.BlockSpec(shape, lambda i: (0,) * len(shape))
    return pl.pallas_call(
        _xattn_router_kernel,
        grid=(n // tm,),
        in_specs=[pl.BlockSpec((tm, D_MODEL), lambda i: (i, 0)),
                  pl.BlockSpec((mlen, 2 * X_WIDTH), lambda i: (i // per_seq, 0)),
                  full((1, D_MODEL)), full((D_MODEL, X_WIDTH)), full((1, HEAD_DIM)), full((1, HEAD_DIM)),
                  full((X_WIDTH, D_MODEL)), full((1, D_MODEL)), full((D_MODEL, LANES)), full((1, LANES))],
        out_specs=[pl.BlockSpec((tm, D_MODEL), lambda i: (i, 0)), pl.BlockSpec((tm, D_MODEL), lambda i: (i, 0)),
                   pl.BlockSpec((tm, LANES), lambda i: (i, 0))],
        out_shape=[jax.ShapeDtypeStruct((n, D_MODEL), F32), jax.ShapeDtypeStruct((n, D_MODEL), BF16),
                   jax.ShapeDtypeStruct((n, LANES), F32)],
        compiler_params=_params("parallel"),
        name="xattn_router",
    )(h1, kv, norm2_w.reshape(1, -1), wq, q_norm_w.reshape(1, -1), k_norm_w.reshape(1, -1), wo,
      norm3_w.reshape(1, -1), router_w, router_b)


def _moe_rank_kernel(e_ref, rank_ref, cnt_ref, carry_ref):
    @pl.when(pl.program_id(0) == 0)
    def _():
        carry_ref[...] = jnp.zeros_like(carry_ref)

    t = e_ref.shape[0]
    hit = e_ref[...] == lax.broadcasted_iota(jnp.int32, (t, LANES), 1)
    onehot = jnp.where(hit, 1.0, 0.0).astype(BF16)
    earlier = lax.broadcasted_iota(jnp.int32, (t, t), 0) > lax.broadcasted_iota(jnp.int32, (t, t), 1)
    before = _dot(jnp.where(earlier, 1.0, 0.0).astype(BF16), onehot) + carry_ref[...]
    rank_ref[...] = jnp.sum(jnp.where(hit, before, 0.0), axis=-1, keepdims=True).astype(jnp.int32)
    carry_ref[...] += jnp.sum(jnp.where(hit, 1.0, 0.0), axis=0, keepdims=True)
    cnt_ref[...] = carry_ref[...]


def moe_rank(eid, t=512):
    a = eid.shape[0]
    return pl.pallas_call(
        _moe_rank_kernel,
        grid=(a // t,),
        in_specs=[pl.BlockSpec((t, 1), lambda i: (i, 0))],
        out_specs=[pl.BlockSpec((t, 1), lambda i: (i, 0)), pl.BlockSpec((1, LANES), lambda i: (0, 0))],
        out_shape=[jax.ShapeDtypeStruct((a, 1), jnp.int32), jax.ShapeDtypeStruct((1, LANES), F32)],
        scratch_shapes=[pltpu.VMEM((1, LANES), F32)],
        compiler_params=_params("arbitrary"),
        name="moe_rank",
    )(eid)


def _moe_ffn_kernel(item_e_ref, item_row_ref, item_nsub_ref, item_nout_ref, x_hbm, wg_ref, wu_ref, wd_ref, y_hbm,
                    x_vmem, y_vmem, sem_x, sem_y, *, hid):
    i, c = pl.program_id(0), pl.program_id(1)
    last_c = pl.num_programs(1) - 1
    th = wg_ref.shape[2]
    nsub, nout = item_nsub_ref[i], item_nout_ref[i]
    row0 = pl.multiple_of(item_row_ref[i], MOE_SUB)
    sub = lambda s: pl.ds(s * MOE_SUB, MOE_SUB)

    def x_copy(s):
        return pltpu.make_async_copy(x_hbm.at[pl.ds(row0 + s * MOE_SUB, MOE_SUB)], x_vmem.at[sub(s)], sem_x.at[s])

    def y_copy(row, s):
        return pltpu.make_async_copy(y_vmem.at[sub(s)], y_hbm.at[pl.ds(row + s * MOE_SUB, MOE_SUB)], sem_y.at[s])

    def for_subs(count, fn):
        for s in range(MOE_ITEM_SUBS):
            pl.when(s < count)(functools.partial(fn, s))

    @pl.when(c == 0)
    def _():
        @pl.when(i > 0)
        def _():
            prev_row = pl.multiple_of(item_row_ref[i - 1], MOE_SUB)
            for_subs(item_nout_ref[i - 1], lambda s: y_copy(prev_row, s).wait())

        for_subs(nsub, lambda s: x_copy(s).start())
        for_subs(nsub, lambda s: x_copy(s).wait())

        @pl.when(nsub == 0)
        def _():
            y_vmem[...] = jnp.zeros_like(y_vmem)
            for_subs(nout, lambda s: y_copy(row0, s).start())

    def ffn(count):
        rows = pl.ds(0, count * MOE_SUB)
        col_ok = c * th + lax.broadcasted_iota(jnp.int32, (1, th), 1) < hid
        row_ok = c * th + lax.broadcasted_iota(jnp.int32, (th, 1), 0) < hid
        wgu = jnp.concatenate([wg_ref[0], wu_ref[0]], axis=1).astype(BF16)
        wd = jnp.where(row_ok, wd_ref[0], 0.0).astype(BF16)
        gu = _dot(x_vmem[rows, :], wgu)
        act = jnp.where(col_ok, _silu(gu[:, :th]) * gu[:, th:], 0.0)
        part = _dot(act.astype(BF16), wd)

        @pl.when(c == 0)
        def _():
            y_vmem[rows, :] = part

        @pl.when(c > 0)
        def _():
            y_vmem[rows, :] += part

    for count in range(1, MOE_ITEM_SUBS + 1):
        pl.when(nsub == count)(functools.partial(ffn, count))

    @pl.when((nsub > 0) & (c == last_c))
    def _():
        for_subs(nsub, lambda s: y_copy(row0, s).start())

    @pl.when((i == pl.num_programs(0) - 1) & (c == last_c))
    def _():
        for_subs(nout, lambda s: y_copy(row0, s).wait())


def moe_ffn(x_buf, item_e, item_row, item_nsub, item_nout, w_gate, w_up, w_down):
    rows, d = x_buf.shape
    hid = w_gate.shape[2]
    th = MOE_HID_TILE
    nc = pl.cdiv(hid, th)
    tile = lambda i, c, ns: jnp.where(ns[i] > 0, c, nc - 1)
    grid_spec = pltpu.PrefetchScalarGridSpec(
        num_scalar_prefetch=4,
        grid=(item_e.shape[0], nc),
        in_specs=[pl.BlockSpec(memory_space=pl.ANY),
                  pl.BlockSpec((1, d, th), lambda i, c, ie, ir, ns, no: (ie[i], 0, tile(i, c, ns))),
                  pl.BlockSpec((1, d, th), lambda i, c, ie, ir, ns, no: (ie[i], 0, tile(i, c, ns))),
                  pl.BlockSpec((1, th, d), lambda i, c, ie, ir, ns, no: (ie[i], tile(i, c, ns), 0))],
        out_specs=pl.BlockSpec(memory_space=pl.ANY),
        scratch_shapes=[pltpu.VMEM((MOE_ITEM_SUBS * MOE_SUB, d), BF16), pltpu.VMEM((MOE_ITEM_SUBS * MOE_SUB, d), F32),
                        pltpu.SemaphoreType.DMA((MOE_ITEM_SUBS,)), pltpu.SemaphoreType.DMA((MOE_ITEM_SUBS,))],
    )
    return pl.pallas_call(
        functools.partial(_moe_ffn_kernel, hid=hid),
        grid_spec=grid_spec,
        out_shape=jax.ShapeDtypeStruct((rows, d), F32),
        compiler_params=_params("arbitrary", "arbitrary"),
        name="moe_ffn",
    )(item_e, item_row, item_nsub, item_nout, x_buf, w_gate, w_up, w_down)


def hier_moe(h2, hf, route, w_gate, w_up, w_down):
    n, d = h2.shape
    n_exp = w_gate.shape[0]
    i32 = jnp.int32
    eid = route[:, 0:TOP_K].astype(i32).reshape(-1)
    wts = route[:, TOP_K:2 * TOP_K]
    n_assign = n * TOP_K
    rank, counts = moe_rank(eid.reshape(n_assign, 1))
    counts = counts[0, :n_exp].astype(i32)
    subs_e = (counts + MOE_SUB - 1) // MOE_SUB
    sub_end = jnp.cumsum(subs_e)
    row_start = (sub_end - subs_e) * MOE_SUB
    is_e = eid[:, None] == jnp.arange(n_exp, dtype=i32)[None, :]
    dest = rank[:, 0] + jnp.sum(jnp.where(is_e, row_start[None, :], 0), axis=1)
    n_subs = (n_assign + n_exp * (MOE_SUB - 1) + MOE_SUB - 1) // MOE_SUB
    rows = n_subs * MOE_SUB
    tok = jnp.repeat(jnp.arange(n, dtype=i32), TOP_K)
    tok_buf = (jnp.arange(rows, dtype=i32) % n).at[dest].set(tok)
    per = MOE_ITEM_SUBS
    n_items = (n_exp * (per - 1) + n_subs + per - 1) // per
    items_e = (subs_e + per - 1) // per
    item_end = jnp.cumsum(items_e)
    n_used = item_end[-1]
    idx = jnp.arange(n_items, dtype=i32)
    used = idx < n_used
    e_of = jnp.minimum(jnp.searchsorted(item_end, idx, side='right'), n_exp - 1).astype(i32)
    j = idx - (item_end - items_e)[e_of]
    item_nsub = jnp.where(used, jnp.clip(subs_e[e_of] - per * j, 0, per), 0)
    fill_first = jnp.minimum(sub_end[-1] + per * (idx - n_used), n_subs)
    item_fill = jnp.where(used, 0, jnp.minimum(n_subs - fill_first, per))
    item_row = jnp.where(used, row_start[e_of] + j * per * MOE_SUB, jnp.minimum(fill_first, n_subs - 1) * MOE_SUB)
    item_e = jnp.where(used, e_of, e_of[jnp.maximum(n_used - 1, 0)])
    y_buf = moe_ffn(hf[tok_buf], item_e.astype(i32), item_row.astype(i32), item_nsub.astype(i32),
                    (item_nsub + item_fill).astype(i32), w_gate, w_up, w_down)
    slot = dest.reshape(n, TOP_K)
    return h2 + wts[:, 0:1] * y_buf[slot[:, 0]] + wts[:, 1:2] * y_buf[slot[:, 1]]


COL_DT = ZX_COLS
COL_QKV = COL_DT + SSD_HEADS
COL_NSA_GATE = COL_QKV + QKV_COLS
COL_MERGE_GATE = COL_NSA_GATE + 3 * NSA_HEADS


def _small_w_in(w_in):
    pad = jnp.zeros(w_in.shape[:2] + (SMALL_COLS - SSD_HEADS - 3 * NSA_HEADS,), w_in.dtype)
    return jnp.concatenate([w_in[:, :, COL_DT:COL_QKV], w_in[:, :, COL_NSA_GATE:COL_MERGE_GATE], pad], axis=2)


def _layer(x, mem, norm1_w, w_in, ssd_conv_w, ssd_conv_b, ssd_dt_bias, ssd_a_log, ssd_d, ssd_norm_w,
           nsa_q_norm_w, nsa_k_norm_w, cmp_pe_k, cmp_w1_k, cmp_w2_k, cmp_pe_v, cmp_w1_v, cmp_w2_v,
           w_up_ssd, w_up_nsa, w_out, norm2_w, mem_norm_w, xq_w, xkv_w, x_q_norm_w, x_k_norm_w, xo_w,
           norm3_w, router_g_w, router_g_b, router_e_w, router_e_b, moe_w_gate, moe_w_up, moe_w_down):
    batch, seq, d = x.shape
    n = batch * seq
    xf = x.reshape(n, d)
    hn = rmsnorm_rows(xf, norm1_w)
    w_cols = lambda c0, nc: w_in[0, :, c0:c0 + nc].astype(BF16)
    zx = matmul(hn, w_cols(0, ZX_COLS), tm=IN_PROJ_TM, tn=IN_PROJ_TN)
    qkv = matmul(hn, w_cols(COL_QKV, QKV_COLS), tm=IN_PROJ_TM, tn=IN_PROJ_TN)
    mgates = matmul(hn, w_cols(COL_MERGE_GATE, GATE_COLS), tm=IN_PROJ_TM, tn=IN_PROJ_TN)
    small = matmul(hn, _small_w_in(w_in)[0].astype(BF16), tm=IN_PROJ_TM)
    y_ssd = ssd_branch(zx, small, ssd_conv_w, ssd_conv_b, ssd_dt_bias, ssd_a_log, ssd_d, ssd_norm_w, batch, seq)
    qh, kcn, vcr, ksn, vsn, kwn, vwn = nsa_prep(qkv, nsa_q_norm_w, nsa_k_norm_w, batch, seq)
    kcc = compress(kcn, cmp_pe_k, cmp_w1_k, cmp_w2_k, transpose_out=False)
    vcct = compress(vcr, cmp_pe_v, cmp_w1_v, cmp_w2_v, transpose_out=True)
    gl = small[:, SSD_HEADS:SSD_HEADS + 3 * NSA_HEADS].reshape(batch, seq, 3, NSA_KV_HEADS, NSA_REP)
    gl = gl.transpose(0, 3, 2, 4, 1).reshape(batch, NSA_KV_HEADS, 3 * NSA_REP, seq)
    y_nsa = nsa_attention(qh, kcc, vcct, ksn, vsn, kwn, vwn, gl, batch, seq)
    merged = merge_mixers(y_ssd, y_nsa, w_up_ssd.astype(BF16), w_up_nsa.astype(BF16), mgates)
    h1 = residual_matmul(xf, merged, w_out.astype(BF16))
    mlen = mem.shape[1]
    mn = rmsnorm_rows(mem.reshape(batch * mlen, d), mem_norm_w, tm=min(512, batch * mlen))
    kv = matmul(mn, xkv_w.astype(BF16))
    n_exp = router_e_w.shape[1]
    rpad = LANES - N_GROUPS - n_exp
    router_w = jnp.concatenate([router_g_w, router_e_w, jnp.zeros((d, rpad), F32)], axis=1).astype(BF16)
    router_b = jnp.concatenate([router_g_b, router_e_b, jnp.zeros((rpad,), F32)]).reshape(1, LANES)
    h2, hf, route = xattn_router(h1, kv, norm2_w, xq_w.astype(BF16), x_q_norm_w, x_k_norm_w, xo_w.astype(BF16),
                                 norm3_w, router_w, router_b, batch, seq)
    out = hier_moe(h2, hf, route, moe_w_gate, moe_w_up, moe_w_down)
    return out.reshape(batch, seq, d)


def kernel(x, mem, norm1_w, w_in, ssd_conv_w, ssd_conv_b, ssd_dt_bias, ssd_a_log, ssd_d, ssd_norm_w, nsa_q_norm_w, nsa_k_norm_w, cmp_pe_k, cmp_w1_k, cmp_w2_k, cmp_pe_v, cmp_w1_v, cmp_w2_v, w_up_ssd, w_up_nsa, w_out, norm2_w, mem_norm_w, xq_w, xkv_w, x_q_norm_w, x_k_norm_w, xo_w, norm3_w, router_g_w, router_g_b, router_e_w, router_e_b, moe_w_gate, moe_w_up, moe_w_down):
    h = x
    for l in range(norm1_w.shape[0]):
        h = _layer(h, mem, norm1_w[l], w_in[l:l + 1], ssd_conv_w[l], ssd_conv_b[l], ssd_dt_bias[l], ssd_a_log[l], ssd_d[l],
                   ssd_norm_w[l], nsa_q_norm_w[l], nsa_k_norm_w[l], cmp_pe_k[l], cmp_w1_k[l], cmp_w2_k[l], cmp_pe_v[l],
                   cmp_w1_v[l], cmp_w2_v[l], w_up_ssd[l], w_up_nsa[l], w_out[l], norm2_w[l], mem_norm_w[l], xq_w[l],
                   xkv_w[l], x_q_norm_w[l], x_k_norm_w[l], xo_w[l], norm3_w[l], router_g_w[l], router_g_b[l],
                   router_e_w[l], router_e_b[l], moe_w_gate[l], moe_w_up[l], moe_w_down[l])
    return h.astype(x.dtype)
```

```python
import functools

import numpy as np
import jax
import jax.numpy as jnp
from jax import lax
from jax.experimental import pallas as pl
from jax.experimental.pallas import tpu as pltpu

F32 = jnp.float32
BF16 = jnp.bfloat16

D_MODEL = 2048
SSD_INNER = 4096
SSD_HEAD_DIM = 64
SSD_HEADS = 64
SSD_GROUPS = 8
SSD_STATE = 128
SSD_CONV = 4
SSD_CHUNK = 256
HEADS_PER_GROUP = SSD_HEADS // SSD_GROUPS
GROUP_WIDTH = SSD_INNER // SSD_GROUPS
HEAD_DIM = 128
NSA_HEADS = 16
NSA_KV_HEADS = 4
NSA_REP = NSA_HEADS // NSA_KV_HEADS
NSA_WIDTH = NSA_HEADS * HEAD_DIM
NSA_KV_WIDTH = NSA_KV_HEADS * HEAD_DIM
CMP_BLK = 32
CMP_STRIDE = 16
CMP_HID = 256
SEL_BLK = 64
SEL_SHIFT = 6
N_SEL = 16
WINDOW = 512
NSA_Q_BLK = 128
WIN_TILE = 128
SEL_KEY_TILE = 256
V_ROWS = HEAD_DIM
FORCE_SCORE = 1.0e4
X_HEADS = 4
X_WIDTH = X_HEADS * HEAD_DIM
N_GROUPS = 8
EXPERTS_PER_GROUP = 8
EPG_SHIFT = 3
N_EXPERTS = 64
TOP_K = 2
EXPERT_HIDDEN = 1408
MOE_SUB = 128
MOE_ITEM_SUBS = 8
MOE_HID_TILE = 256
ROPE_THETA = 10000.0
EPS = 1e-6
NEG_INF = -1e30
LOG2E = float(np.log2(np.e))
Q_PRESCALE = HEAD_DIM ** -0.5 * LOG2E
LANES = 128
VMEM_LIMIT = 56 * 1024 * 1024

ZX_COLS = SSD_INNER + SSD_INNER + 2 * SSD_GROUPS * SSD_STATE
QKV_COLS = NSA_WIDTH + 6 * NSA_KV_WIDTH
GATE_COLS = 2 * D_MODEL
SMALL_COLS = LANES
IN_PROJ_TM, IN_PROJ_TN = 2048, 1024


def _params(*sem):
    return pltpu.CompilerParams(dimension_semantics=sem, vmem_limit_bytes=VMEM_LIMIT)


def _rms(x, w):
    ms = jnp.mean(x * x, axis=-1, keepdims=True)
    return x * lax.rsqrt(ms + EPS) * w


def _sigmoid(x):
    return 0.5 + 0.5 * jnp.tanh(0.5 * x)


def _silu(x):
    h = 0.5 * x
    return h + h * jnp.tanh(h)


def _softplus(x):
    return jnp.maximum(x, 0.0) + jnp.log1p(jnp.exp(-jnp.abs(x)))


def _dot(a, b):
    return jnp.dot(a, b, preferred_element_type=F32)


def _dot_nt(a, b):
    return lax.dot_general(a, b, (((1,), (1,)), ((), ())), preferred_element_type=F32)


def _masked_softmax(s, mask):
    s = jnp.where(mask, s, NEG_INF)
    e = jnp.exp(s - jnp.max(s, axis=-1, keepdims=True))
    p = e / jnp.sum(e, axis=-1, keepdims=True)
    return jnp.where(mask, p, 0.0)


def _rmsnorm_kernel(x_ref, w_ref, o_ref):
    o_ref[...] = _rms(x_ref[...], w_ref[...]).astype(o_ref.dtype)


def rmsnorm_rows(x, w, tm=512):
    m, d = x.shape
    return pl.pallas_call(
        _rmsnorm_kernel,
        grid=(m // tm,),
        in_specs=[pl.BlockSpec((tm, d), lambda i: (i, 0)), pl.BlockSpec((1, d), lambda i: (0, 0))],
        out_specs=pl.BlockSpec((tm, d), lambda i: (i, 0)),
        out_shape=jax.ShapeDtypeStruct((m, d), BF16),
        compiler_params=_params("parallel"),
        name="rmsnorm_rows",
    )(x, w.reshape(1, d))


def _matmul_kernel(x_ref, w_ref, o_ref):
    o_ref[...] = _dot(x_ref[...], w_ref[...]).astype(o_ref.dtype)


def matmul(x, w, out_dtype=F32, tm=512, tn=512):
    m, k = x.shape
    n = w.shape[1]
    tm, tn = min(tm, m), min(tn, n)
    return pl.pallas_call(
        _matmul_kernel,
        grid=(m // tm, n // tn),
        in_specs=[pl.BlockSpec((tm, k), lambda i, j: (i, 0)), pl.BlockSpec((k, tn), lambda i, j: (0, j))],
        out_specs=pl.BlockSpec((tm, tn), lambda i, j: (i, j)),
        out_shape=jax.ShapeDtypeStruct((m, n), out_dtype),
        compiler_params=_params("parallel", "arbitrary"),
        name="matmul",
    )(x, w)


def _split3(a):
    h = a.astype(BF16)
    r = a - h.astype(F32)
    m = r.astype(BF16)
    return h, m, (r - m.astype(F32)).astype(BF16)


def _ssd_kernel(z_ref, x_ref, b_ref, c_ref, dt_ref, dtt_ref, cwx_ref, cwb_ref, cwc_ref, cbx_ref, cbb_ref, cbc_ref,
                hp_ref, hpt_ref, nw_ref, y_ref, state_ref, ext_x_ref, ext_b_ref, ext_c_ref):
    L = x_ref.shape[0]
    H = L // 2
    chunk = pl.program_id(2)

    @pl.when(chunk == 0)
    def _():
        state_ref[...] = jnp.zeros_like(state_ref)
        ext_x_ref[...] = jnp.zeros_like(ext_x_ref)
        ext_b_ref[...] = jnp.zeros_like(ext_b_ref)
        ext_c_ref[...] = jnp.zeros_like(ext_c_ref)

    def conv_silu(u_ref, ext_ref, w, b):
        ext_ref[0:8, :] = ext_ref[L:L + 8, :]
        u = u_ref[...]
        ext_ref[8:8 + L, :] = u
        acc = b + w[SSD_CONV - 1:SSD_CONV, :] * u
        for k in range(SSD_CONV - 1):
            acc = acc + w[k:k + 1, :] * ext_ref[pl.ds(8 - (SSD_CONV - 1) + k, L), :]
        return _silu(acc)

    xs = conv_silu(x_ref, ext_x_ref, cwx_ref[...], cbx_ref[...])
    bm = conv_silu(b_ref, ext_b_ref, cwb_ref[...], cbb_ref[...])
    cm = conv_silu(c_ref, ext_c_ref, cwc_ref[...], cbc_ref[...])
    bm16, cm16 = bm.astype(BF16), cm.astype(BF16)

    hp = hp_ref[0]
    hpt = hpt_ref[0]
    dt = _softplus(dt_ref[0] + hp[0:1, :])
    dtt = _softplus(dtt_ref[0] + hpt[:, 0:1])
    da = dt * (-jnp.exp(hp[1:2, :]))
    dat = dtt * (-jnp.exp(hpt[:, 1:2]))
    row = lax.broadcasted_iota(jnp.int32, (L, L), 0)
    col = lax.broadcasted_iota(jnp.int32, (L, L), 1)
    tril = jnp.where(row >= col, 1.0, 0.0).astype(BF16)
    triu = jnp.where(row <= col, 1.0, 0.0).astype(BF16)
    acum = sum(_dot(tril, part) for part in _split3(da)) * LOG2E
    acumt = sum(_dot(part, triu) for part in _split3(dat)) * LOG2E
    last = acum[L - 1:L, :]
    dte = jnp.exp2(last - acum)
    eac = jnp.exp2(acum)
    cdec = jnp.exp2(last)

    cb = _dot_nt(cm16, bm16)
    cb00, cb10, cb11 = cb[:H, :H], cb[H:, :H], cb[H:, H:]
    tri = lax.broadcasted_iota(jnp.int32, (H, H), 0) >= lax.broadcasted_iota(jnp.int32, (H, H), 1)
    lane = lax.broadcasted_iota(jnp.int32, (L, LANES), 1)
    first = lane < SSD_HEAD_DIM
    srow = lax.broadcasted_iota(jnp.int32, (LANES, SSD_STATE), 0) < SSD_HEAD_DIM
    z = z_ref[...]
    d_skip = hp[2:3, :]
    outs = []
    for p in range(HEADS_PER_GROUP // 2):
        h0, h1 = 2 * p, 2 * p + 1

        def pick(a, rows=first):
            return jnp.where(rows, a[:, h0:h0 + 1], a[:, h1:h1 + 1])

        xp = xs[:, p * LANES:(p + 1) * LANES]
        xdt = xp * pick(dt)
        xdt16 = xdt.astype(BF16)
        y_top = jnp.zeros((H, LANES), F32)
        y_bot = jnp.zeros((H, LANES), F32)
        for h, msk in ((h0, first), (h1, jnp.logical_not(first))):
            a_col, a_row = acum[:, h:h + 1], acumt[h:h + 1, :]
            d00 = jnp.exp2(jnp.where(tri, a_col[:H] - a_row[:, :H], -jnp.inf))
            d10 = jnp.exp2(a_col[H:] - a_row[:, :H])
            d11 = jnp.exp2(jnp.where(tri, a_col[H:] - a_row[:, H:], -jnp.inf))
            xh = jnp.where(msk, xdt16, jnp.zeros_like(xdt16))
            y_top = y_top + _dot((cb00 * d00).astype(BF16), xh[:H])
            y_bot = y_bot + _dot((cb10 * d10).astype(BF16), xh[:H]) + _dot((cb11 * d11).astype(BF16), xh[H:])
        y = jnp.concatenate([y_top, y_bot], axis=0)
        hprev = state_ref[p]
        y = y + _dot_nt(cm16, hprev.astype(BF16)) * pick(eac)
        xe_t = (xdt * pick(dte)).T.astype(BF16)
        state_ref[p] = hprev * jnp.where(srow, cdec[:, h0:h0 + 1], cdec[:, h1:h1 + 1]) + _dot(xe_t, bm16)
        outs.append(y + pick(d_skip, first[0:1, :]) * xp)
    y = jnp.concatenate(outs, axis=1) * _silu(z)
    y_ref[...] = _rms(y, nw_ref[...]).astype(y_ref.dtype)


def ssd_branch(zx, small, conv_w, conv_b, dt_bias, a_log, d_skip, norm_w, batch, seq):
    n = batch * seq
    L = min(SSD_CHUNK, seq)
    nc = seq // L
    g, hpg = SSD_GROUPS, HEADS_PER_GROUP
    dt_raw = small[:, :SSD_HEADS].reshape(n, g, hpg)
    dt_g = dt_raw.transpose(1, 0, 2)
    dtt_g = dt_raw.transpose(1, 2, 0)
    hp = jnp.stack([dt_bias, a_log, d_skip], axis=0).reshape(3, g, hpg).transpose(1, 0, 2)
    hp = jnp.pad(hp, ((0, 0), (0, 8 - 3), (0, 0)))
    hpt = hp.transpose(0, 2, 1)
    cw = conv_w
    cbias = conv_b.reshape(1, -1)
    nxb = SSD_INNER // GROUP_WIDTH
    rowblk = lambda b, gi, c: b * nc + c
    bc0 = SSD_INNER // SSD_STATE
    grid = (g, batch, nc)
    in_specs = [
        pl.BlockSpec((L, GROUP_WIDTH), lambda gi, b, c: (rowblk(b, gi, c), gi)),
        pl.BlockSpec((L, GROUP_WIDTH), lambda gi, b, c: (rowblk(b, gi, c), nxb + gi)),
        pl.BlockSpec((L, SSD_STATE), lambda gi, b, c: (rowblk(b, gi, c), 2 * bc0 + gi)),
        pl.BlockSpec((L, SSD_STATE), lambda gi, b, c: (rowblk(b, gi, c), 2 * bc0 + g + gi)),
        pl.BlockSpec((1, L, hpg), lambda gi, b, c: (gi, rowblk(b, gi, c), 0)),
        pl.BlockSpec((1, hpg, L), lambda gi, b, c: (gi, 0, rowblk(b, gi, c))),
        pl.BlockSpec((SSD_CONV, GROUP_WIDTH), lambda gi, b, c: (0, gi)),
        pl.BlockSpec((SSD_CONV, SSD_STATE), lambda gi, b, c: (0, bc0 + gi)),
        pl.BlockSpec((SSD_CONV, SSD_STATE), lambda gi, b, c: (0, bc0 + g + gi)),
        pl.BlockSpec((1, GROUP_WIDTH), lambda gi, b, c: (0, gi)),
        pl.BlockSpec((1, SSD_STATE), lambda gi, b, c: (0, bc0 + gi)),
        pl.BlockSpec((1, SSD_STATE), lambda gi, b, c: (0, bc0 + g + gi)),
        pl.BlockSpec((1, 8, hpg), lambda gi, b, c: (gi, 0, 0)),
        pl.BlockSpec((1, hpg, 8), lambda gi, b, c: (gi, 0, 0)),
        pl.BlockSpec((1, GROUP_WIDTH), lambda gi, b, c: (0, gi)),
    ]
    return pl.pallas_call(
        _ssd_kernel,
        grid=grid,
        in_specs=in_specs,
        out_specs=pl.BlockSpec((L, GROUP_WIDTH), lambda gi, b, c: (rowblk(b, gi, c), gi)),
        out_shape=jax.ShapeDtypeStruct((n, SSD_INNER), BF16),
        scratch_shapes=[pltpu.VMEM((HEADS_PER_GROUP // 2, LANES, SSD_STATE), F32),
                        pltpu.VMEM((L + 8, GROUP_WIDTH), F32), pltpu.VMEM((L + 8, SSD_STATE), F32),
                        pltpu.VMEM((L + 8, SSD_STATE), F32)],
        compiler_params=_params("parallel", "parallel", "arbitrary"),
        name="ssd_branch",
    )(zx, zx, zx, zx, dt_g, dtt_g, cw, cw, cw, cbias, cbias, cbias, hp, hpt, norm_w.reshape(1, -1))


def _nsa_prep_kernel(q_ref, kc_ref, vc_ref, ks_ref, vs_ref, kw_ref, vw_ref, cos_ref, sin_ref, qw_ref, kw3_ref,
                     qo_ref, kco_ref, vco_ref, kso_ref, vso_ref, kwo_ref, vwo_ref):
    cos, sin = cos_ref[...], sin_ref[...]

    def norm_rope(x, w):
        y = _rms(x, w)
        return y * cos + pltpu.roll(y, HEAD_DIM // 2, 1) * sin

    qw = qw_ref[...]
    tt = q_ref.shape[0]
    for g in range(NSA_KV_HEADS):
        sl = slice(g * HEAD_DIM, (g + 1) * HEAD_DIM)
        for r in range(NSA_REP):
            h = g * NSA_REP + r
            qh = norm_rope(q_ref[:, h * HEAD_DIM:(h + 1) * HEAD_DIM], qw) * Q_PRESCALE
            qo_ref[0, g, r] = qh.astype(qo_ref.dtype)
        kco_ref[0, g] = norm_rope(kc_ref[:, sl], kw3_ref[0:1, :])
        vco_ref[0, g] = vc_ref[:, sl]
        kso_ref[0, g] = norm_rope(ks_ref[:, sl], kw3_ref[1:2, :]).astype(kso_ref.dtype)
        kwo_ref[0, g] = norm_rope(kw_ref[:, sl], kw3_ref[2:3, :]).astype(kwo_ref.dtype)
        vst = vs_ref[:, sl].T.astype(vso_ref.dtype)
        for j in range(tt // SEL_KEY_TILE):
            vso_ref[0, g, j] = vst[:, j * SEL_KEY_TILE:(j + 1) * SEL_KEY_TILE]
        vwt = vw_ref[:, sl].T.astype(vwo_ref.dtype)
        for j in range(tt // WIN_TILE):
            vwo_ref[0, g, j] = vwt[:, j * WIN_TILE:(j + 1) * WIN_TILE]


def nsa_prep(qkv, q_norm_w, k_norm_w, batch, seq, tt=2 * SEL_KEY_TILE):
    half = HEAD_DIM // 2
    inv = ROPE_THETA ** (-jnp.arange(half, dtype=F32) / half)
    ang = jnp.arange(seq).astype(F32)[:, None] * inv[None, :]
    cos = jnp.concatenate([jnp.cos(ang), jnp.cos(ang)], axis=-1)
    sin = jnp.concatenate([-jnp.sin(ang), jnp.sin(ang)], axis=-1)
    nt = seq // tt
    kvb = NSA_WIDTH // NSA_KV_WIDTH
    kv_spec = lambda j: pl.BlockSpec((tt, NSA_KV_WIDTH), lambda b, t: (b * nt + t, kvb + j))
    head_spec = pl.BlockSpec((1, NSA_KV_HEADS, tt, HEAD_DIM), lambda b, t: (b, 0, t, 0))
    head_shape = lambda dt: jax.ShapeDtypeStruct((batch, NSA_KV_HEADS, seq, HEAD_DIM), dt)
    return pl.pallas_call(
        _nsa_prep_kernel,
        grid=(batch, nt),
        in_specs=[pl.BlockSpec((tt, NSA_WIDTH), lambda b, t: (b * nt + t, 0))] + [kv_spec(j) for j in range(6)] + [
            pl.BlockSpec((tt, HEAD_DIM), lambda b, t: (t, 0)), pl.BlockSpec((tt, HEAD_DIM), lambda b, t: (t, 0)),
            pl.BlockSpec((1, HEAD_DIM), lambda b, t: (0, 0)), pl.BlockSpec((3, HEAD_DIM), lambda b, t: (0, 0))],
        out_specs=[pl.BlockSpec((1, NSA_KV_HEADS, NSA_REP, tt, HEAD_DIM), lambda b, t: (b, 0, 0, t, 0)),
                   head_spec, head_spec, head_spec,
                   pl.BlockSpec((1, NSA_KV_HEADS, tt // SEL_KEY_TILE, V_ROWS, SEL_KEY_TILE),
                                lambda b, t: (b, 0, t, 0, 0)),
                   head_spec,
                   pl.BlockSpec((1, NSA_KV_HEADS, tt // WIN_TILE, V_ROWS, WIN_TILE), lambda b, t: (b, 0, t, 0, 0))],
        out_shape=[jax.ShapeDtypeStruct((batch, NSA_KV_HEADS, NSA_REP, seq, HEAD_DIM), BF16),
                   head_shape(F32), head_shape(F32), head_shape(BF16),
                   jax.ShapeDtypeStruct((batch, NSA_KV_HEADS, seq // SEL_KEY_TILE, V_ROWS, SEL_KEY_TILE), BF16),
                   head_shape(BF16),
                   jax.ShapeDtypeStruct((batch, NSA_KV_HEADS, seq // WIN_TILE, V_ROWS, WIN_TILE), BF16)],
        compiler_params=_params("parallel", "parallel"),
        name="nsa_prep",
    )(qkv, qkv, qkv, qkv, qkv, qkv, qkv, cos, sin, q_norm_w.reshape(1, -1), k_norm_w)


def _compress_kernel(x_ref, pe_ref, w1_ref, w2_ref, o_ref, *, ncmp, transpose_out):
    nseg = x_ref.shape[2] // CMP_STRIDE
    u = jnp.zeros((nseg, CMP_HID), F32)
    v = jnp.zeros((nseg, CMP_HID), F32)
    for j in range(CMP_STRIDE):
        xj = x_ref[0, 0, pl.ds(j, nseg, stride=CMP_STRIDE), :]
        u = u + _dot((xj + pe_ref[j:j + 1, :]).astype(BF16), w1_ref[j])
        v = v + _dot((xj + pe_ref[CMP_STRIDE + j:CMP_STRIDE + j + 1, :]).astype(BF16), w1_ref[CMP_STRIDE + j])
    hid = _silu(u + pltpu.roll(v, nseg - 1, 0))
    comp = _dot(hid.astype(BF16), w2_ref[...])
    rowi = lax.broadcasted_iota(jnp.int32, comp.shape, 0)
    comp = jnp.where(rowi < ncmp, comp, 0.0)
    o_ref[0, 0] = (comp.T if transpose_out else comp).astype(o_ref.dtype)


def compress(raw, pe, w1, w2, transpose_out):
    b, g, t, hd = raw.shape
    nseg = t // CMP_STRIDE
    full = lambda shape: pl.BlockSpec(shape, lambda i, j: (0,) * len(shape))
    out_dims = (hd, nseg) if transpose_out else (nseg, hd)
    return pl.pallas_call(
        functools.partial(_compress_kernel, ncmp=nseg - 1, transpose_out=transpose_out),
        grid=(b, g),
        in_specs=[pl.BlockSpec((1, 1, t, hd), lambda i, j: (i, j, 0, 0)), full((CMP_BLK, hd)),
                  full((CMP_BLK, hd, CMP_HID)), full((CMP_HID, hd))],
        out_specs=pl.BlockSpec((1, 1) + out_dims, lambda i, j: (i, j, 0, 0)),
        out_shape=jax.ShapeDtypeStruct((b, g) + out_dims, BF16),
        compiler_params=_params("parallel", "parallel"),
        name="nsa_compress",
    )(raw, pe, w1.astype(BF16), w2.astype(BF16))


def _nsa_attn_kernel(q_ref, kc_ref, vct_ref, ks_ref, vst_ref, kw_ref, vwt_ref, glt_ref, ovt_ref, o_ref,
                     selb_ref, acc_ref, ocw_ref, s_ref, p_ref, *, seq, n_sel):
    Q = q_ref.shape[3]
    q0 = pl.program_id(2) * Q
    rows = NSA_REP * Q
    q = q_ref[0, 0].reshape(rows, HEAD_DIM)
    tq = q0 + lax.broadcasted_iota(jnp.int32, (1, Q), 1)
    head = lambda r: slice(r * Q, (r + 1) * Q)

    ncp = kc_ref.shape[2]
    wk = WINDOW + Q
    kt_sz = vst_ref.shape[4]
    start = pl.multiple_of(jnp.clip(q0 - WINDOW, 0, seq - wk), WIN_TILE)
    s_cw = _dot_nt(jnp.concatenate([kc_ref[0, 0], kw_ref[0, 0, pl.ds(start, wk), :], ks_ref[0, 0, 0:kt_sz, :]],
                                   axis=0), q)
    s_c, s_w = s_cw[:ncp], s_cw[ncp:ncp + wk]
    s_ref[0] = s_cw[ncp + wk:]

    cend = lax.broadcasted_iota(jnp.int32, (ncp, 1), 0) * CMP_STRIDE + (CMP_BLK - 1)
    m_c = cend <= tq
    ps = []
    for r in range(NSA_REP):
        s = jnp.where(m_c, s_c[:, head(r)], NEG_INF)
        e = jnp.exp2(s - jnp.max(s, axis=0, keepdims=True))
        p = jnp.where(m_c, e * (1.0 / jnp.sum(e, axis=0, keepdims=True)), 0.0)
        ps.append(p.astype(BF16))
    p_c = jnp.concatenate(ps, axis=1)
    nsb = ovt_ref.shape[0]
    oc_imp = _dot(jnp.concatenate([vct_ref[0, 0], ovt_ref[...]], axis=0), p_c)
    o_c, imp4 = oc_imp[:HEAD_DIM], oc_imp[HEAD_DIM:]

    imp = imp4[:, head(0)]
    for r in range(1, NSA_REP):
        imp = imp + imp4[:, head(r)]
    blk_t = (q0 + lax.broadcasted_iota(jnp.int32, (nsb, Q), 1)) >> SEL_SHIFT
    sb = lax.broadcasted_iota(jnp.int32, (nsb, Q), 0)
    forced = (sb == 0) | (sb == blk_t) | (sb == blk_t - 1)
    imp = jnp.where(sb <= blk_t, jnp.where(forced, FORCE_SCORE, imp), -jnp.inf)
    rank = jnp.zeros((nsb, Q), F32)
    for i in range(nsb):
        ri = imp[i:i + 1, :]
        beats = (ri > imp) | ((ri == imp) & (sb > i))
        rank = rank + jnp.where(beats, 1.0, 0.0)
    selb_ref[...] = jnp.where(rank < n_sel, 0.0, NEG_INF)

    dist = tq - (start + lax.broadcasted_iota(jnp.int32, (wk, 1), 0))
    bias_w = jnp.where((dist >= 0) & (dist < WINDOW), 0.0, NEG_INF)
    ps, ls = [], []
    for r in range(NSA_REP):
        s = s_w[:, head(r)] + bias_w
        p = jnp.exp2(s - jnp.max(s, axis=0, keepdims=True))
        ls.append(jnp.sum(p, axis=0, keepdims=True))
        ps.append(p.astype(BF16))
    t0 = start // WIN_TILE
    vw_t = jnp.concatenate([vwt_ref[0, 0, t0 + j] for j in range(wk // WIN_TILE)], axis=1)
    o_w = _dot(vw_t, jnp.concatenate(ps, axis=1)) * (1.0 / jnp.concatenate(ls, axis=1))

    gate = _sigmoid(glt_ref[0, 0])
    for r in range(NSA_REP):
        ocw_ref[:, head(r)] = (gate[r:r + 1, :] * o_c[:, head(r)]
                               + gate[2 * NSA_REP + r:2 * NSA_REP + r + 1, :] * o_w[:, head(r)])

    blocks_per_tile = kt_sz // SEL_BLK
    n_all = seq // kt_sz
    acc_ref[...] = jnp.zeros_like(acc_ref)
    p_ref[1] = jnp.zeros(p_ref.shape[1:], p_ref.dtype)

    def sel_tile(kt, carry):
        m, l, alpha_prev = carry
        slot = kt & 1
        s_all = s_ref[slot]
        p_prev = p_ref[1 - slot]
        k_next = pl.multiple_of(jnp.minimum(kt + 1, n_all - 1) * kt_sz, kt_sz)
        s_ref[1 - slot] = _dot_nt(ks_ref[0, 0, pl.ds(k_next, kt_sz), :], q)
        pv_prev = _dot(vst_ref[0, 0, jnp.maximum(kt - 1, 0)], p_prev)
        acc_ref[...] = acc_ref[...] * alpha_prev + pv_prev
        k0 = kt * kt_sz
        kpos = k0 + lax.broadcasted_iota(jnp.int32, (kt_sz, 1), 0)
        bias = jnp.concatenate(
            [jnp.broadcast_to(selb_ref[pl.ds(kt * blocks_per_tile + i, 1), :], (SEL_BLK, Q))
             for i in range(blocks_per_tile)], axis=0)
        bias = jnp.where(kpos <= tq, bias, NEG_INF)
        ps, ms, ls, alphas = [], [], [], []
        for r in range(NSA_REP):
            s = s_all[:, head(r)] + bias
            m_old = m[:, head(r)]
            m_new = jnp.maximum(m_old, jnp.max(s, axis=0, keepdims=True))
            p = jnp.exp2(s - m_new)
            alpha = jnp.exp2(m_old - m_new)
            ls.append(alpha * l[:, head(r)] + jnp.sum(p, axis=0, keepdims=True))
            ms.append(m_new)
            alphas.append(alpha)
            ps.append(p.astype(BF16))
        p_ref[slot] = jnp.concatenate(ps, axis=1)
        return jnp.concatenate(ms, axis=1), jnp.concatenate(ls, axis=1), jnp.concatenate(alphas, axis=1)

    n_tiles = (q0 + Q + kt_sz - 1) // kt_sz
    init = (jnp.full((1, rows), NEG_INF, F32), jnp.zeros((1, rows), F32), jnp.ones((1, rows), F32))
    _, l_s, alpha_last = lax.fori_loop(0, n_tiles, sel_tile, init)
    pv_last = _dot(vst_ref[0, 0, n_tiles - 1], p_ref[(n_tiles - 1) & 1])
    o_s = (acc_ref[...] * alpha_last + pv_last) * (1.0 / l_s)

    for r in range(NSA_REP):
        out_t = ocw_ref[:, head(r)] + gate[NSA_REP + r:NSA_REP + r + 1, :] * o_s[:, head(r)]
        o_ref[:, r * HEAD_DIM:(r + 1) * HEAD_DIM] = out_t.T.astype(o_ref.dtype)


def nsa_attention(qh, kcc, vcct, ksn, vst, kwn, vwt, gate_logits_t, batch, seq):
    Q_BLK = NSA_Q_BLK
    assert seq % SEL_KEY_TILE == 0 and seq % Q_BLK == 0 and seq >= WINDOW + Q_BLK
    nq = seq // Q_BLK
    nsb = seq // SEL_BLK
    n_sel = min(N_SEL, nsb)
    ncp = kcc.shape[2]
    ncmp = (seq - CMP_BLK) // CMP_STRIDE + 1
    ci = np.arange(ncp)[None, :]
    sj = np.arange(nsb)[:, None]
    ovt = ((ci * CMP_STRIDE < (sj + 1) * SEL_BLK) & (ci * CMP_STRIDE + CMP_BLK > sj * SEL_BLK) & (ci < ncmp))
    ovt = jnp.asarray(ovt, BF16)
    seq_spec = pl.BlockSpec((1, 1, seq, HEAD_DIM), lambda b, g, i: (b, g, 0, 0))
    cmp_spec = pl.BlockSpec((1, 1, ncp, HEAD_DIM), lambda b, g, i: (b, g, 0, 0))
    tiled = lambda a: pl.BlockSpec((1, 1) + a.shape[2:], lambda b, g, i: (b, g, 0, 0, 0))
    return pl.pallas_call(
        functools.partial(_nsa_attn_kernel, seq=seq, n_sel=n_sel),
        grid=(batch, NSA_KV_HEADS, nq),
        in_specs=[pl.BlockSpec((1, 1, NSA_REP, Q_BLK, HEAD_DIM), lambda b, g, i: (b, g, 0, i, 0)),
                  cmp_spec, pl.BlockSpec((1, 1, HEAD_DIM, ncp), lambda b, g, i: (b, g, 0, 0)),
                  seq_spec, tiled(vst), seq_spec, tiled(vwt),
                  pl.BlockSpec((1, 1, 3 * NSA_REP, Q_BLK), lambda b, g, i: (b, g, 0, i)),
                  pl.BlockSpec((nsb, ncp), lambda b, g, i: (0, 0))],
        out_specs=pl.BlockSpec((Q_BLK, NSA_REP * HEAD_DIM), lambda b, g, i: (b * nq + i, g)),
        out_shape=jax.ShapeDtypeStruct((batch * seq, NSA_WIDTH), BF16),
        scratch_shapes=[pltpu.VMEM((nsb, Q_BLK), F32), pltpu.VMEM((HEAD_DIM, NSA_REP * Q_BLK), F32),
                        pltpu.VMEM((HEAD_DIM, NSA_REP * Q_BLK), F32),
                        pltpu.VMEM((2, SEL_KEY_TILE, NSA_REP * Q_BLK), F32),
                        pltpu.VMEM((2, SEL_KEY_TILE, NSA_REP * Q_BLK), BF16)],
        compiler_params=_params("parallel", "parallel", "arbitrary"),
        name="nsa_attention",
    )(qh, kcc, vcct, ksn, vst, kwn, vwt, gate_logits_t, ovt)


def _merge_kernel(ys_ref, yn_ref, ws_ref, wn_ref, gs_ref, gn_ref, o_ref):
    up_s = _dot(ys_ref[...], ws_ref[...])
    up_n = _dot(yn_ref[...], wn_ref[...])
    o_ref[...] = (_sigmoid(gs_ref[...]) * up_s + _sigmoid(gn_ref[...]) * up_n).astype(o_ref.dtype)


def merge_mixers(y_ssd, y_nsa, w_up_ssd, w_up_nsa, gates, tm=1024, tn=512):
    m = y_ssd.shape[0]
    nb = D_MODEL // tn
    return pl.pallas_call(
        _merge_kernel,
        grid=(m // tm, nb),
        in_specs=[pl.BlockSpec((tm, SSD_INNER), lambda i, j: (i, 0)), pl.BlockSpec((tm, NSA_WIDTH), lambda i, j: (i, 0)),
                  pl.BlockSpec((SSD_INNER, tn), lambda i, j: (0, j)), pl.BlockSpec((NSA_WIDTH, tn), lambda i, j: (0, j)),
                  pl.BlockSpec((tm, tn), lambda i, j: (i, j)), pl.BlockSpec((tm, tn), lambda i, j: (i, nb + j))],
        out_specs=pl.BlockSpec((tm, tn), lambda i, j: (i, j)),
        out_shape=jax.ShapeDtypeStruct((m, D_MODEL), BF16),
        compiler_params=_params("parallel", "arbitrary"),
        name="merge_mixers",
    )(y_ssd, y_nsa, w_up_ssd, w_up_nsa, gates, gates)


def _residual_matmul_kernel(h_ref, x_ref, w_ref, o_ref):
    o_ref[...] = h_ref[...] + _dot(x_ref[...], w_ref[...])


def residual_matmul(h, x, w, tm=1024, tn=1024):
    m, k = x.shape
    n = w.shape[1]
    return pl.pallas_call(
        _residual_matmul_kernel,
        grid=(m // tm, n // tn),
        in_specs=[pl.BlockSpec((tm, tn), lambda i, j: (i, j)), pl.BlockSpec((tm, k), lambda i, j: (i, 0)),
                  pl.BlockSpec((k, tn), lambda i, j: (0, j))],
        out_specs=pl.BlockSpec((tm, tn), lambda i, j: (i, j)),
        out_shape=jax.ShapeDtypeStruct((m, n), F32),
        compiler_params=_params("parallel", "arbitrary"),
        name="residual_matmul",
    )(h, x, w)


def _xattn_router_kernel(h_ref, kv_ref, n2_ref, wq_ref, qn_ref, kn_ref, wo_ref, n3_ref, rw_ref, rb_ref,
                         h2_ref, hf_ref, rt_ref):
    h = h_ref[...]
    qp = _dot(_rms(h, n2_ref[...]).astype(BF16), wq_ref[...])
    kv = kv_ref[...]
    scale = HEAD_DIM ** -0.5
    heads = []
    for hd in range(X_HEADS):
        sl = slice(hd * HEAD_DIM, (hd + 1) * HEAD_DIM)
        qh = _rms(qp[:, sl], qn_ref[...]).astype(BF16)
        kh = _rms(kv[:, sl], kn_ref[...]).astype(BF16)
        vh = kv[:, X_WIDTH + hd * HEAD_DIM:X_WIDTH + (hd + 1) * HEAD_DIM].astype(BF16)
        s = _dot_nt(qh, kh) * scale
        e = jnp.exp(s - jnp.max(s, axis=-1, keepdims=True))
        p = e / jnp.sum(e, axis=-1, keepdims=True)
        heads.append(_dot(p.astype(BF16), vh))
    o = jnp.concatenate(heads, axis=1).astype(BF16)
    h2 = h + _dot(o, wo_ref[...])
    h2_ref[...] = h2
    hf = _rms(h2, n3_ref[...]).astype(BF16)
    hf_ref[...] = hf
    lg = _dot(hf, rw_ref[...]) + rb_ref[...]

    lane = lax.broadcasted_iota(jnp.int32, lg.shape, 1)
    rmax = lambda v: jnp.max(v, axis=-1, keepdims=True)
    rsum = lambda v: jnp.sum(v, axis=-1, keepdims=True)
    first_lane = lambda hit: jnp.min(jnp.where(hit, lane, LANES), axis=-1, keepdims=True)
    is_g = lane < N_GROUPS
    eg = jnp.where(is_g, jnp.exp(lg - rmax(jnp.where(is_g, lg, -jnp.inf))), 0.0)
    pg = eg / rsum(eg)
    pg_top = rmax(pg)
    grp = first_lane(is_g & (pg == pg_top))
    in_grp = (lane >= N_GROUPS) & (((lane - N_GROUPS) >> EPG_SHIFT) == grp)
    ee = jnp.where(in_grp, jnp.exp(lg - rmax(jnp.where(in_grp, lg, -jnp.inf))), 0.0)
    pe = jnp.where(in_grp, ee / rsum(ee), -1.0)
    p1 = rmax(pe)
    i1 = first_lane(pe == p1)
    pe2 = jnp.where(lane == i1, -1.0, pe)
    p2 = rmax(pe2)
    i2 = first_lane(pe2 == p2)
    psum = p1 + p2
    route = jnp.where(lane == 0, (i1 - N_GROUPS).astype(F32),
                      jnp.where(lane == 1, (i2 - N_GROUPS).astype(F32),
                                jnp.where(lane == 2, pg_top * p1 / psum,
                                          jnp.where(lane == 3, pg_top * p2 / psum, 0.0))))
    rt_ref[...] = route


def xattn_router(h1, kv, norm2_w, wq, q_norm_w, k_norm_w, wo, norm3_w, router_w, router_b, batch, seq, tm=512):
    n = batch * seq
    tm = min(tm, seq)
    per_seq = seq // tm
    mlen = kv.shape[0] // batch
    full = lambda shape: pl.BlockSpec(shape, lambda i: (0,) * len(shape))
    return pl.pallas_call(
        _xattn_router_kernel,
        grid=(n // tm,),
        in_specs=[pl.BlockSpec((tm, D_MODEL), lambda i: (i, 0)),
                  pl.BlockSpec((mlen, 2 * X_WIDTH), lambda i: (i // per_seq, 0)),
                  full((1, D_MODEL)), full((D_MODEL, X_WIDTH)), full((1, HEAD_DIM)), full((1, HEAD_DIM)),
                  full((X_WIDTH, D_MODEL)), full((1, D_MODEL)), full((D_MODEL, LANES)), full((1, LANES))],
        out_specs=[pl.BlockSpec((tm, D_MODEL), lambda i: (i, 0)), pl.BlockSpec((tm, D_MODEL), lambda i: (i, 0)),
                   pl.BlockSpec((tm, LANES), lambda i: (i, 0))],
        out_shape=[jax.ShapeDtypeStruct((n, D_MODEL), F32), jax.ShapeDtypeStruct((n, D_MODEL), BF16),
                   jax.ShapeDtypeStruct((n, LANES), F32)],
        compiler_params=_params("parallel"),
        name="xattn_router",
    )(h1, kv, norm2_w.reshape(1, -1), wq, q_norm_w.reshape(1, -1), k_norm_w.reshape(1, -1), wo,
      norm3_w.reshape(1, -1), router_w, router_b)


def _moe_rank_kernel(e_ref, rank_ref, cnt_ref, carry_ref):
    @pl.when(pl.program_id(0) == 0)
    def _():
        carry_ref[...] = jnp.zeros_like(carry_ref)

    t = e_ref.shape[0]
    hit = e_ref[...] == lax.broadcasted_iota(jnp.int32, (t, LANES), 1)
    onehot = jnp.where(hit, 1.0, 0.0).astype(BF16)
    earlier = lax.broadcasted_iota(jnp.int32, (t, t), 0) > lax.broadcasted_iota(jnp.int32, (t, t), 1)
    before = _dot(jnp.where(earlier, 1.0, 0.0).astype(BF16), onehot) + carry_ref[...]
    rank_ref[...] = jnp.sum(jnp.where(hit, before, 0.0), axis=-1, keepdims=True).astype(jnp.int32)
    carry_ref[...] += jnp.sum(jnp.where(hit, 1.0, 0.0), axis=0, keepdims=True)
    cnt_ref[...] = carry_ref[...]


def moe_rank(eid, t=1024):
    a = eid.shape[0]
    return pl.pallas_call(
        _moe_rank_kernel,
        grid=(a // t,),
        in_specs=[pl.BlockSpec((t, 1), lambda i: (i, 0))],
        out_specs=[pl.BlockSpec((t, 1), lambda i: (i, 0)), pl.BlockSpec((1, LANES), lambda i: (0, 0))],
        out_shape=[jax.ShapeDtypeStruct((a, 1), jnp.int32), jax.ShapeDtypeStruct((1, LANES), F32)],
        scratch_shapes=[pltpu.VMEM((1, LANES), F32)],
        compiler_params=_params("arbitrary"),
        name="moe_rank",
    )(eid)


def _moe_ffn_kernel(item_e_ref, item_row_ref, item_nsub_ref, item_nout_ref, x_hbm, wg_ref, wu_ref, wd_ref, y_hbm,
                    x_vmem, y_vmem, sem_x, sem_y, *, hid):
    i, c = pl.program_id(0), pl.program_id(1)
    last_c = pl.num_programs(1) - 1
    th = wg_ref.shape[2]
    nsub, nout = item_nsub_ref[i], item_nout_ref[i]
    row0 = pl.multiple_of(item_row_ref[i], MOE_SUB)
    sub = lambda s: pl.ds(s * MOE_SUB, MOE_SUB)

    def x_copy(item, s):
        row = pl.multiple_of(item_row_ref[item], MOE_SUB)
        return pltpu.make_async_copy(x_hbm.at[pl.ds(row + s * MOE_SUB, MOE_SUB)], x_vmem.at[item & 1, sub(s)],
                                     sem_x.at[item & 1, s])

    def y_copy(row, s):
        return pltpu.make_async_copy(y_vmem.at[sub(s)], y_hbm.at[pl.ds(row + s * MOE_SUB, MOE_SUB)], sem_y.at[s])

    def for_subs(count, fn):
        for s in range(MOE_ITEM_SUBS):
            pl.when(s < count)(functools.partial(fn, s))

    @pl.when(c == 0)
    def _():
        @pl.when(i > 0)
        def _():
            prev_row = pl.multiple_of(item_row_ref[i - 1], MOE_SUB)
            for_subs(item_nout_ref[i - 1], lambda s: y_copy(prev_row, s).wait())

        @pl.when(i == 0)
        def _():
            for_subs(nsub, lambda s: x_copy(i, s).start())

        for_subs(nsub, lambda s: x_copy(i, s).wait())

        @pl.when(i + 1 < pl.num_programs(0))
        def _():
            nxt = jnp.minimum(i + 1, pl.num_programs(0) - 1)
            for_subs(item_nsub_ref[nxt], lambda s: x_copy(nxt, s).start())

        @pl.when(nsub == 0)
        def _():
            y_vmem[...] = jnp.zeros_like(y_vmem)
            for_subs(nout, lambda s: y_copy(row0, s).start())

    def ffn(count):
        rows = pl.ds(0, count * MOE_SUB)
        col_ok = c * th + lax.broadcasted_iota(jnp.int32, (1, th), 1) < hid
        row_ok = c * th + lax.broadcasted_iota(jnp.int32, (th, 1), 0) < hid
        wgu = jnp.concatenate([wg_ref[0], wu_ref[0]], axis=1).astype(BF16)
        wd = jnp.where(row_ok, wd_ref[0], 0.0).astype(BF16)
        gu = _dot(x_vmem[i & 1, rows, :], wgu)
        act = jnp.where(col_ok, _silu(gu[:, :th]) * gu[:, th:], 0.0)
        part = _dot(act.astype(BF16), wd)

        @pl.when(c == 0)
        def _():
            y_vmem[rows, :] = part

        @pl.when(c > 0)
        def _():
            y_vmem[rows, :] += part

    for count in range(1, MOE_ITEM_SUBS + 1):
        pl.when(nsub == count)(functools.partial(ffn, count))

    @pl.when((nsub > 0) & (c == last_c))
    def _():
        for_subs(nsub, lambda s: y_copy(row0, s).start())

    @pl.when((i == pl.num_programs(0) - 1) & (c == last_c))
    def _():
        for_subs(nout, lambda s: y_copy(row0, s).wait())


def moe_ffn(x_buf, item_e, item_row, item_nsub, item_nout, w_gate, w_up, w_down):
    rows, d = x_buf.shape
    hid = w_gate.shape[2]
    th = MOE_HID_TILE
    nc = pl.cdiv(hid, th)
    tile = lambda i, c, ns: jnp.where(ns[i] > 0, c, nc - 1)
    grid_spec = pltpu.PrefetchScalarGridSpec(
        num_scalar_prefetch=4,
        grid=(item_e.shape[0], nc),
        in_specs=[pl.BlockSpec(memory_space=pl.ANY),
                  pl.BlockSpec((1, d, th), lambda i, c, ie, ir, ns, no: (ie[i], 0, tile(i, c, ns))),
                  pl.BlockSpec((1, d, th), lambda i, c, ie, ir, ns, no: (ie[i], 0, tile(i, c, ns))),
                  pl.BlockSpec((1, th, d), lambda i, c, ie, ir, ns, no: (ie[i], tile(i, c, ns), 0))],
        out_specs=pl.BlockSpec(memory_space=pl.ANY),
        scratch_shapes=[pltpu.VMEM((2, MOE_ITEM_SUBS * MOE_SUB, d), BF16),
                        pltpu.VMEM((MOE_ITEM_SUBS * MOE_SUB, d), F32),
                        pltpu.SemaphoreType.DMA((2, MOE_ITEM_SUBS)), pltpu.SemaphoreType.DMA((MOE_ITEM_SUBS,))],
    )
    return pl.pallas_call(
        functools.partial(_moe_ffn_kernel, hid=hid),
        grid_spec=grid_spec,
        out_shape=jax.ShapeDtypeStruct((rows, d), F32),
        compiler_params=_params("arbitrary", "arbitrary"),
        name="moe_ffn",
    )(item_e, item_row, item_nsub, item_nout, x_buf, w_gate, w_up, w_down)


def hier_moe(h2, hf, route, w_gate, w_up, w_down):
    n, d = h2.shape
    n_exp = w_gate.shape[0]
    i32 = jnp.int32
    eid = route[:, 0:TOP_K].astype(i32).reshape(-1)
    wts = route[:, TOP_K:2 * TOP_K]
    n_assign = n * TOP_K
    rank, counts = moe_rank(eid.reshape(n_assign, 1))
    counts = counts[0, :n_exp].astype(i32)
    subs_e = (counts + MOE_SUB - 1) // MOE_SUB
    sub_end = jnp.cumsum(subs_e)
    row_start = (sub_end - subs_e) * MOE_SUB
    is_e = eid[:, None] == jnp.arange(n_exp, dtype=i32)[None, :]
    dest = rank[:, 0] + jnp.sum(jnp.where(is_e, row_start[None, :], 0), axis=1)
    n_subs = (n_assign + n_exp * (MOE_SUB - 1) + MOE_SUB - 1) // MOE_SUB
    rows = n_subs * MOE_SUB
    tok = jnp.repeat(jnp.arange(n, dtype=i32), TOP_K)
    tok_buf = (jnp.arange(rows, dtype=i32) % n).at[dest].set(tok)
    per = MOE_ITEM_SUBS
    n_items = (n_exp * (per - 1) + n_subs + per - 1) // per
    items_e = (subs_e + per - 1) // per
    item_end = jnp.cumsum(items_e)
    n_used = item_end[-1]
    idx = jnp.arange(n_items, dtype=i32)
    used = idx < n_used
    e_of = jnp.minimum(jnp.searchsorted(item_end, idx, side='right'), n_exp - 1).astype(i32)
    j = idx - (item_end - items_e)[e_of]
    item_nsub = jnp.where(used, jnp.clip(subs_e[e_of] - per * j, 0, per), 0)
    fill_first = jnp.minimum(sub_end[-1] + per * (idx - n_used), n_subs)
    item_fill = jnp.where(used, 0, jnp.minimum(n_subs - fill_first, per))
    item_row = jnp.where(used, row_start[e_of] + j * per * MOE_SUB, jnp.minimum(fill_first, n_subs - 1) * MOE_SUB)
    item_e = jnp.where(used, e_of, e_of[jnp.maximum(n_used - 1, 0)])
    y_buf = moe_ffn(hf[tok_buf], item_e.astype(i32), item_row.astype(i32), item_nsub.astype(i32),
                    (item_nsub + item_fill).astype(i32), w_gate, w_up, w_down)
    slot = dest.reshape(n, TOP_K)
    return h2 + wts[:, 0:1] * y_buf[slot[:, 0]] + wts[:, 1:2] * y_buf[slot[:, 1]]


COL_DT = ZX_COLS
COL_QKV = COL_DT + SSD_HEADS
COL_NSA_GATE = COL_QKV + QKV_COLS
COL_MERGE_GATE = COL_NSA_GATE + 3 * NSA_HEADS


def _small_w_in(w_in):
    pad = jnp.zeros(w_in.shape[:2] + (SMALL_COLS - SSD_HEADS - 3 * NSA_HEADS,), w_in.dtype)
    return jnp.concatenate([w_in[:, :, COL_DT:COL_QKV], w_in[:, :, COL_NSA_GATE:COL_MERGE_GATE], pad], axis=2)


def _layer(x, mem, norm1_w, w_in, ssd_conv_w, ssd_conv_b, ssd_dt_bias, ssd_a_log, ssd_d, ssd_norm_w,
           nsa_q_norm_w, nsa_k_norm_w, cmp_pe_k, cmp_w1_k, cmp_w2_k, cmp_pe_v, cmp_w1_v, cmp_w2_v,
           w_up_ssd, w_up_nsa, w_out, norm2_w, mem_norm_w, xq_w, xkv_w, x_q_norm_w, x_k_norm_w, xo_w,
           norm3_w, router_g_w, router_g_b, router_e_w, router_e_b, moe_w_gate, moe_w_up, moe_w_down):
    batch, seq, d = x.shape
    n = batch * seq
    xf = x.reshape(n, d)
    hn = rmsnorm_rows(xf, norm1_w)
    w_cols = lambda c0, nc: w_in[0, :, c0:c0 + nc].astype(BF16)
    zx = matmul(hn, w_cols(0, ZX_COLS), tm=IN_PROJ_TM, tn=IN_PROJ_TN)
    qkv = matmul(hn, w_cols(COL_QKV, QKV_COLS), tm=IN_PROJ_TM, tn=IN_PROJ_TN)
    mgates = matmul(hn, w_cols(COL_MERGE_GATE, GATE_COLS), tm=IN_PROJ_TM, tn=IN_PROJ_TN)
    small = matmul(hn, _small_w_in(w_in)[0].astype(BF16), tm=IN_PROJ_TM)
    y_ssd = ssd_branch(zx, small, ssd_conv_w, ssd_conv_b, ssd_dt_bias, ssd_a_log, ssd_d, ssd_norm_w, batch, seq)
    qh, kcn, vcr, ksn, vsn, kwn, vwn = nsa_prep(qkv, nsa_q_norm_w, nsa_k_norm_w, batch, seq)
    kcc = compress(kcn, cmp_pe_k, cmp_w1_k, cmp_w2_k, transpose_out=False)
    vcct = compress(vcr, cmp_pe_v, cmp_w1_v, cmp_w2_v, transpose_out=True)
    gl = small[:, SSD_HEADS:SSD_HEADS + 3 * NSA_HEADS].reshape(batch, seq, 3, NSA_KV_HEADS, NSA_REP)
    gl = gl.transpose(0, 3, 2, 4, 1).reshape(batch, NSA_KV_HEADS, 3 * NSA_REP, seq)
    y_nsa = nsa_attention(qh, kcc, vcct, ksn, vsn, kwn, vwn, gl, batch, seq)
    merged = merge_mixers(y_ssd, y_nsa, w_up_ssd.astype(BF16), w_up_nsa.astype(BF16), mgates)
    h1 = residual_matmul(xf, merged, w_out.astype(BF16))
    mlen = mem.shape[1]
    mn = rmsnorm_rows(mem.reshape(batch * mlen, d), mem_norm_w, tm=min(512, batch * mlen))
    kv = matmul(mn, xkv_w.astype(BF16))
    n_exp = router_e_w.shape[1]
    rpad = LANES - N_GROUPS - n_exp
    router_w = jnp.concatenate([router_g_w, router_e_w, jnp.zeros((d, rpad), F32)], axis=1).astype(BF16)
    router_b = jnp.concatenate([router_g_b, router_e_b, jnp.zeros((rpad,), F32)]).reshape(1, LANES)
    h2, hf, route = xattn_router(h1, kv, norm2_w, xq_w.astype(BF16), x_q_norm_w, x_k_norm_w, xo_w.astype(BF16),
                                 norm3_w, router_w, router_b, batch, seq)
    out = hier_moe(h2, hf, route, moe_w_gate, moe_w_up, moe_w_down)
    return out.reshape(batch, seq, d)


def kernel(x, mem, norm1_w, w_in, ssd_conv_w, ssd_conv_b, ssd_dt_bias, ssd_a_log, ssd_d, ssd_norm_w, nsa_q_norm_w, nsa_k_norm_w, cmp_pe_k, cmp_w1_k, cmp_w2_k, cmp_pe_v, cmp_w1_v, cmp_w2_v, w_up_ssd, w_up_nsa, w_out, norm2_w, mem_norm_w, xq_w, xkv_w, x_q_norm_w, x_k_norm_w, xo_w, norm3_w, router_g_w, router_g_b, router_e_w, router_e_b, moe_w_gate, moe_w_up, moe_w_down):
    h = x
    for l in range(norm1_w.shape[0]):
        h = _layer(h, mem, norm1_w[l], w_in[l:l + 1], ssd_conv_w[l], ssd_conv_b[l], ssd_dt_bias[l], ssd_a_log[l], ssd_d[l],
                   ssd_norm_w[l], nsa_q_norm_w[l], nsa_k_norm_w[l], cmp_pe_k[l], cmp_w1_k[l], cmp_w2_k[l], cmp_pe_v[l],
                   cmp_w1_v[l], cmp_w2_v[l], w_up_ssd[l], w_up_nsa[l], w_out[l], norm2_w[l], mem_norm_w[l], xq_w[l],
                   xkv_w[l], x_q_norm_w[l], x_k_norm_w[l], xo_w[l], norm3_w[l], router_g_w[l], router_g_b[l],
                   router_e_w[l], router_e_b[l], moe_w_gate[l], moe_w_up[l], moe_w_down[l])
    return h.astype(x.dtype)
```

```python
import functools

import numpy as np
import jax
import jax.numpy as jnp
from jax import lax
from jax.experimental import pallas as pl
from jax.experimental.pallas import tpu as pltpu

F32 = jnp.float32
BF16 = jnp.bfloat16

D_MODEL = 2048
SSD_INNER = 4096
SSD_HEAD_DIM = 64
SSD_HEADS = 64
SSD_GROUPS = 8
SSD_STATE = 128
SSD_CONV = 4
SSD_CHUNK = 256
SSD_CHUNKS_PER_STEP = 2
HEADS_PER_GROUP = SSD_HEADS // SSD_GROUPS
GROUP_WIDTH = SSD_INNER // SSD_GROUPS
HEAD_DIM = 128
NSA_HEADS = 16
NSA_KV_HEADS = 4
NSA_REP = NSA_HEADS // NSA_KV_HEADS
NSA_WIDTH = NSA_HEADS * HEAD_DIM
NSA_KV_WIDTH = NSA_KV_HEADS * HEAD_DIM
CMP_BLK = 32
CMP_STRIDE = 16
CMP_HID = 256
SEL_BLK = 64
SEL_SHIFT = 6
N_SEL = 16
WINDOW = 512
NSA_Q_BLK = 256
WIN_TILE = 128
SEL_KEY_TILE = 256
V_ROWS = HEAD_DIM
FORCE_SCORE = 1.0e4
X_HEADS = 4
X_WIDTH = X_HEADS * HEAD_DIM
N_GROUPS = 8
EXPERTS_PER_GROUP = 8
EPG_SHIFT = 3
N_EXPERTS = 64
TOP_K = 2
EXPERT_HIDDEN = 1408
MOE_SUB = 128
MOE_ITEM_SUBS = 8
MOE_HID_TILE = 256
ROPE_THETA = 10000.0
EPS = 1e-6
NEG_INF = -1e30
LOG2E = float(np.log2(np.e))
Q_PRESCALE = HEAD_DIM ** -0.5 * LOG2E
LANES = 128
VMEM_LIMIT = 56 * 1024 * 1024

ZX_COLS = SSD_INNER + SSD_INNER + 2 * SSD_GROUPS * SSD_STATE
QKV_COLS = NSA_WIDTH + 6 * NSA_KV_WIDTH
GATE_COLS = 2 * D_MODEL
SMALL_COLS = LANES
IN_PROJ_TM, IN_PROJ_TN = 2048, 1024


def _params(*sem):
    return pltpu.CompilerParams(dimension_semantics=sem, vmem_limit_bytes=VMEM_LIMIT)


def _rms(x, w):
    ms = jnp.mean(x * x, axis=-1, keepdims=True)
    return x * lax.rsqrt(ms + EPS) * w


def _sigmoid(x):
    return 0.5 + 0.5 * jnp.tanh(0.5 * x)


def _silu(x):
    h = 0.5 * x
    return h + h * jnp.tanh(h)


def _softplus(x):
    return jnp.maximum(x, 0.0) + jnp.log1p(jnp.exp(-jnp.abs(x)))


def _dot(a, b):
    return jnp.dot(a, b, preferred_element_type=F32)


def _dot_nt(a, b):
    return lax.dot_general(a, b, (((1,), (1,)), ((), ())), preferred_element_type=F32)


def _masked_softmax(s, mask):
    s = jnp.where(mask, s, NEG_INF)
    e = jnp.exp(s - jnp.max(s, axis=-1, keepdims=True))
    p = e / jnp.sum(e, axis=-1, keepdims=True)
    return jnp.where(mask, p, 0.0)


def _rmsnorm_kernel(x_ref, w_ref, o_ref):
    o_ref[...] = _rms(x_ref[...], w_ref[...]).astype(o_ref.dtype)


def rmsnorm_rows(x, w, tm=512):
    m, d = x.shape
    return pl.pallas_call(
        _rmsnorm_kernel,
        grid=(m // tm,),
        in_specs=[pl.BlockSpec((tm, d), lambda i: (i, 0)), pl.BlockSpec((1, d), lambda i: (0, 0))],
        out_specs=pl.BlockSpec((tm, d), lambda i: (i, 0)),
        out_shape=jax.ShapeDtypeStruct((m, d), BF16),
        compiler_params=_params("parallel"),
        name="rmsnorm_rows",
    )(x, w.reshape(1, d))


def _matmul_kernel(x_ref, w_ref, o_ref):
    o_ref[...] = _dot(x_ref[...], w_ref[...]).astype(o_ref.dtype)


def matmul(x, w, out_dtype=F32, tm=512, tn=512):
    m, k = x.shape
    n = w.shape[1]
    tm, tn = min(tm, m), min(tn, n)
    return pl.pallas_call(
        _matmul_kernel,
        grid=(m // tm, n // tn),
        in_specs=[pl.BlockSpec((tm, k), lambda i, j: (i, 0)), pl.BlockSpec((k, tn), lambda i, j: (0, j))],
        out_specs=pl.BlockSpec((tm, tn), lambda i, j: (i, j)),
        out_shape=jax.ShapeDtypeStruct((m, n), out_dtype),
        compiler_params=_params("parallel", "arbitrary"),
        name="matmul",
    )(x, w)


def _split3(a):
    h = a.astype(BF16)
    r = a - h.astype(F32)
    m = r.astype(BF16)
    return h, m, (r - m.astype(F32)).astype(BF16)


def _ssd_kernel(z_ref, x_ref, b_ref, c_ref, dt_ref, dtt_ref, *rest):
    @pl.when(pl.program_id(2) == 0)
    def _():
        for ref in rest[-4:]:
            ref[...] = jnp.zeros_like(ref)

    L = x_ref.shape[0] // SSD_CHUNKS_PER_STEP
    for cc in range(SSD_CHUNKS_PER_STEP):
        rows = pl.ds(cc * L, L)
        _ssd_chunk(z_ref.at[rows], x_ref.at[rows], b_ref.at[rows], c_ref.at[rows], dt_ref.at[:, rows],
                   dtt_ref.at[:, :, rows], *rest[:9], rest[9].at[rows], *rest[10:])


def _ssd_chunk(z_ref, x_ref, b_ref, c_ref, dt_ref, dtt_ref, cwx_ref, cwb_ref, cwc_ref, cbx_ref, cbb_ref, cbc_ref,
               hp_ref, hpt_ref, nw_ref, y_ref, state_ref, ext_x_ref, ext_b_ref, ext_c_ref):
    L = x_ref.shape[0]
    H = L // 2

    def conv_silu(u_ref, ext_ref, w, b):
        ext_ref[0:8, :] = ext_ref[L:L + 8, :]
        u = u_ref[...]
        ext_ref[8:8 + L, :] = u
        acc = b + w[SSD_CONV - 1:SSD_CONV, :] * u
        for k in range(SSD_CONV - 1):
            acc = acc + w[k:k + 1, :] * ext_ref[pl.ds(8 - (SSD_CONV - 1) + k, L), :]
        return _silu(acc)

    xs = conv_silu(x_ref, ext_x_ref, cwx_ref[...], cbx_ref[...])
    bm = conv_silu(b_ref, ext_b_ref, cwb_ref[...], cbb_ref[...])
    cm = conv_silu(c_ref, ext_c_ref, cwc_ref[...], cbc_ref[...])
    bm16, cm16 = bm.astype(BF16), cm.astype(BF16)

    hp = hp_ref[0]
    hpt = hpt_ref[0]
    dt = _softplus(dt_ref[0] + hp[0:1, :])
    dtt = _softplus(dtt_ref[0] + hpt[:, 0:1])
    da = dt * (-jnp.exp(hp[1:2, :]))
    dat = dtt * (-jnp.exp(hpt[:, 1:2]))
    row = lax.broadcasted_iota(jnp.int32, (L, L), 0)
    col = lax.broadcasted_iota(jnp.int32, (L, L), 1)
    tril = jnp.where(row >= col, 1.0, 0.0).astype(BF16)
    triu = jnp.where(row <= col, 1.0, 0.0).astype(BF16)
    acum = sum(_dot(tril, part) for part in _split3(da)) * LOG2E
    acumt = sum(_dot(part, triu) for part in _split3(dat)) * LOG2E
    last = acum[L - 1:L, :]
    dte = jnp.exp2(last - acum)
    eac = jnp.exp2(acum)
    cdec = jnp.exp2(last)

    cb = _dot_nt(cm16, bm16)
    cb00, cb10, cb11 = cb[:H, :H], cb[H:, :H], cb[H:, H:]
    tri = lax.broadcasted_iota(jnp.int32, (H, H), 0) >= lax.broadcasted_iota(jnp.int32, (H, H), 1)
    lane = lax.broadcasted_iota(jnp.int32, (L, LANES), 1)
    first = lane < SSD_HEAD_DIM
    srow = lax.broadcasted_iota(jnp.int32, (LANES, SSD_STATE), 0) < SSD_HEAD_DIM
    z = z_ref[...]
    d_skip = hp[2:3, :]
    outs = []
    for p in range(HEADS_PER_GROUP // 2):
        h0, h1 = 2 * p, 2 * p + 1

        def pick(a, rows=first):
            return jnp.where(rows, a[:, h0:h0 + 1], a[:, h1:h1 + 1])

        xp = xs[:, p * LANES:(p + 1) * LANES]
        xdt = xp * pick(dt)
        xdt16 = xdt.astype(BF16)
        y_top = jnp.zeros((H, LANES), F32)
        y_bot = jnp.zeros((H, LANES), F32)
        for h, msk in ((h0, first), (h1, jnp.logical_not(first))):
            a_col, a_row = acum[:, h:h + 1], acumt[h:h + 1, :]
            d00 = jnp.exp2(jnp.where(tri, a_col[:H] - a_row[:, :H], -jnp.inf))
            d10 = jnp.exp2(a_col[H:] - a_row[:, :H])
            d11 = jnp.exp2(jnp.where(tri, a_col[H:] - a_row[:, H:], -jnp.inf))
            xh = jnp.where(msk, xdt16, jnp.zeros_like(xdt16))
            y_top = y_top + _dot((cb00 * d00).astype(BF16), xh[:H])
            y_bot = y_bot + _dot((cb10 * d10).astype(BF16), xh[:H]) + _dot((cb11 * d11).astype(BF16), xh[H:])
        y = jnp.concatenate([y_top, y_bot], axis=0)
        hprev = state_ref[p]
        y = y + _dot_nt(cm16, hprev.astype(BF16)) * pick(eac)
        xe_t = (xdt * pick(dte)).T.astype(BF16)
        state_ref[p] = hprev * jnp.where(srow, cdec[:, h0:h0 + 1], cdec[:, h1:h1 + 1]) + _dot(xe_t, bm16)
        outs.append(y + pick(d_skip, first[0:1, :]) * xp)
    y = jnp.concatenate(outs, axis=1) * _silu(z)
    y_ref[...] = _rms(y, nw_ref[...]).astype(y_ref.dtype)


def ssd_branch(zx, small, conv_w, conv_b, dt_bias, a_log, d_skip, norm_w, batch, seq):
    n = batch * seq
    L = min(SSD_CHUNK, seq)
    nc = seq // L
    g, hpg = SSD_GROUPS, HEADS_PER_GROUP
    dt_raw = small[:, :SSD_HEADS].reshape(n, g, hpg)
    dt_g = dt_raw.transpose(1, 0, 2)
    dtt_g = dt_raw.transpose(1, 2, 0)
    hp = jnp.stack([dt_bias, a_log, d_skip], axis=0).reshape(3, g, hpg).transpose(1, 0, 2)
    hp = jnp.pad(hp, ((0, 0), (0, 8 - 3), (0, 0)))
    hpt = hp.transpose(0, 2, 1)
    cw = conv_w
    cbias = conv_b.reshape(1, -1)
    nxb = SSD_INNER // GROUP_WIDTH
    assert nc % SSD_CHUNKS_PER_STEP == 0
    lb, nb = L * SSD_CHUNKS_PER_STEP, nc // SSD_CHUNKS_PER_STEP
    rowblk = lambda b, gi, c: b * nb + c
    bc0 = SSD_INNER // SSD_STATE
    grid = (g, batch, nb)
    in_specs = [
        pl.BlockSpec((lb, GROUP_WIDTH), lambda gi, b, c: (rowblk(b, gi, c), gi)),
        pl.BlockSpec((lb, GROUP_WIDTH), lambda gi, b, c: (rowblk(b, gi, c), nxb + gi)),
        pl.BlockSpec((lb, SSD_STATE), lambda gi, b, c: (rowblk(b, gi, c), 2 * bc0 + gi)),
        pl.BlockSpec((lb, SSD_STATE), lambda gi, b, c: (rowblk(b, gi, c), 2 * bc0 + g + gi)),
        pl.BlockSpec((1, lb, hpg), lambda gi, b, c: (gi, rowblk(b, gi, c), 0)),
        pl.BlockSpec((1, hpg, lb), lambda gi, b, c: (gi, 0, rowblk(b, gi, c))),
        pl.BlockSpec((SSD_CONV, GROUP_WIDTH), lambda gi, b, c: (0, gi)),
        pl.BlockSpec((SSD_CONV, SSD_STATE), lambda gi, b, c: (0, bc0 + gi)),
        pl.BlockSpec((SSD_CONV, SSD_STATE), lambda gi, b, c: (0, bc0 + g + gi)),
        pl.BlockSpec((1, GROUP_WIDTH), lambda gi, b, c: (0, gi)),
        pl.BlockSpec((1, SSD_STATE), lambda gi, b, c: (0, bc0 + gi)),
        pl.BlockSpec((1, SSD_STATE), lambda gi, b, c: (0, bc0 + g + gi)),
        pl.BlockSpec((1, 8, hpg), lambda gi, b, c: (gi, 0, 0)),
        pl.BlockSpec((1, hpg, 8), lambda gi, b, c: (gi, 0, 0)),
        pl.BlockSpec((1, GROUP_WIDTH), lambda gi, b, c: (0, gi)),
    ]
    return pl.pallas_call(
        _ssd_kernel,
        grid=grid,
        in_specs=in_specs,
        out_specs=pl.BlockSpec((lb, GROUP_WIDTH), lambda gi, b, c: (rowblk(b, gi, c), gi)),
        out_shape=jax.ShapeDtypeStruct((n, SSD_INNER), BF16),
        scratch_shapes=[pltpu.VMEM((HEADS_PER_GROUP // 2, LANES, SSD_STATE), F32),
                        pltpu.VMEM((L + 8, GROUP_WIDTH), F32), pltpu.VMEM((L + 8, SSD_STATE), F32),
                        pltpu.VMEM((L + 8, SSD_STATE), F32)],
        compiler_params=_params("parallel", "parallel", "arbitrary"),
        name="ssd_branch",
    )(zx, zx, zx, zx, dt_g, dtt_g, cw, cw, cw, cbias, cbias, cbias, hp, hpt, norm_w.reshape(1, -1))


def _nsa_prep_kernel(q_ref, kc_ref, vc_ref, ks_ref, vs_ref, kw_ref, vw_ref, cos_ref, sin_ref, qw_ref, kw3_ref,
                     qo_ref, kco_ref, vco_ref, kso_ref, vso_ref, kwo_ref, vwo_ref):
    cos, sin = cos_ref[...], sin_ref[...]

    def norm_rope(x, w):
        y = _rms(x, w)
        return y * cos + pltpu.roll(y, HEAD_DIM // 2, 1) * sin

    qw = qw_ref[...]
    tt = q_ref.shape[0]
    for g in range(NSA_KV_HEADS):
        sl = slice(g * HEAD_DIM, (g + 1) * HEAD_DIM)
        for r in range(NSA_REP):
            h = g * NSA_REP + r
            qh = norm_rope(q_ref[:, h * HEAD_DIM:(h + 1) * HEAD_DIM], qw) * Q_PRESCALE
            qo_ref[0, g, r] = qh.astype(qo_ref.dtype)
        kco_ref[0, g] = norm_rope(kc_ref[:, sl], kw3_ref[0:1, :])
        vco_ref[0, g] = vc_ref[:, sl]
        kso_ref[0, g] = norm_rope(ks_ref[:, sl], kw3_ref[1:2, :]).astype(kso_ref.dtype)
        kwo_ref[0, g] = norm_rope(kw_ref[:, sl], kw3_ref[2:3, :]).astype(kwo_ref.dtype)
        vst = vs_ref[:, sl].T.astype(vso_ref.dtype)
        for j in range(tt // SEL_KEY_TILE):
            vso_ref[0, g, j] = vst[:, j * SEL_KEY_TILE:(j + 1) * SEL_KEY_TILE]
        vwt = vw_ref[:, sl].T.astype(vwo_ref.dtype)
        for j in range(tt // WIN_TILE):
            vwo_ref[0, g, j] = vwt[:, j * WIN_TILE:(j + 1) * WIN_TILE]


def nsa_prep(qkv, q_norm_w, k_norm_w, batch, seq, tt=2 * SEL_KEY_TILE):
    half = HEAD_DIM // 2
    inv = ROPE_THETA ** (-jnp.arange(half, dtype=F32) / half)
    ang = jnp.arange(seq).astype(F32)[:, None] * inv[None, :]
    cos = jnp.concatenate([jnp.cos(ang), jnp.cos(ang)], axis=-1)
    sin = jnp.concatenate([-jnp.sin(ang), jnp.sin(ang)], axis=-1)
    nt = seq // tt
    kvb = NSA_WIDTH // NSA_KV_WIDTH
    kv_spec = lambda j: pl.BlockSpec((tt, NSA_KV_WIDTH), lambda b, t: (b * nt + t, kvb + j))
    head_spec = pl.BlockSpec((1, NSA_KV_HEADS, tt, HEAD_DIM), lambda b, t: (b, 0, t, 0))
    head_shape = lambda dt: jax.ShapeDtypeStruct((batch, NSA_KV_HEADS, seq, HEAD_DIM), dt)
    return pl.pallas_call(
        _nsa_prep_kernel,
        grid=(batch, nt),
        in_specs=[pl.BlockSpec((tt, NSA_WIDTH), lambda b, t: (b * nt + t, 0))] + [kv_spec(j) for j in range(6)] + [
            pl.BlockSpec((tt, HEAD_DIM), lambda b, t: (t, 0)), pl.BlockSpec((tt, HEAD_DIM), lambda b, t: (t, 0)),
            pl.BlockSpec((1, HEAD_DIM), lambda b, t: (0, 0)), pl.BlockSpec((3, HEAD_DIM), lambda b, t: (0, 0))],
        out_specs=[pl.BlockSpec((1, NSA_KV_HEADS, NSA_REP, tt, HEAD_DIM), lambda b, t: (b, 0, 0, t, 0)),
                   head_spec, head_spec, head_spec,
                   pl.BlockSpec((1, NSA_KV_HEADS, tt // SEL_KEY_TILE, V_ROWS, SEL_KEY_TILE),
                                lambda b, t: (b, 0, t, 0, 0)),
                   head_spec,
                   pl.BlockSpec((1, NSA_KV_HEADS, tt // WIN_TILE, V_ROWS, WIN_TILE), lambda b, t: (b, 0, t, 0, 0))],
        out_shape=[jax.ShapeDtypeStruct((batch, NSA_KV_HEADS, NSA_REP, seq, HEAD_DIM), BF16),
                   head_shape(F32), head_shape(F32), head_shape(BF16),
                   jax.ShapeDtypeStruct((batch, NSA_KV_HEADS, seq // SEL_KEY_TILE, V_ROWS, SEL_KEY_TILE), BF16),
                   head_shape(BF16),
                   jax.ShapeDtypeStruct((batch, NSA_KV_HEADS, seq // WIN_TILE, V_ROWS, WIN_TILE), BF16)],
        compiler_params=_params("parallel", "parallel"),
        name="nsa_prep",
    )(qkv, qkv, qkv, qkv, qkv, qkv, qkv, cos, sin, q_norm_w.reshape(1, -1), k_norm_w)


def _compress_kernel(x_ref, pe_ref, w1_ref, w2_ref, o_ref, *, ncmp, transpose_out):
    nseg = x_ref.shape[2] // CMP_STRIDE
    u = jnp.zeros((nseg, CMP_HID), F32)
    v = jnp.zeros((nseg, CMP_HID), F32)
    for j in range(CMP_STRIDE):
        xj = x_ref[0, 0, pl.ds(j, nseg, stride=CMP_STRIDE), :]
        u = u + _dot((xj + pe_ref[j:j + 1, :]).astype(BF16), w1_ref[j])
        v = v + _dot((xj + pe_ref[CMP_STRIDE + j:CMP_STRIDE + j + 1, :]).astype(BF16), w1_ref[CMP_STRIDE + j])
    hid = _silu(u + pltpu.roll(v, nseg - 1, 0))
    comp = _dot(hid.astype(BF16), w2_ref[...])
    rowi = lax.broadcasted_iota(jnp.int32, comp.shape, 0)
    comp = jnp.where(rowi < ncmp, comp, 0.0)
    o_ref[0, 0] = (comp.T if transpose_out else comp).astype(o_ref.dtype)


def compress(raw, pe, w1, w2, transpose_out):
    b, g, t, hd = raw.shape
    nseg = t // CMP_STRIDE
    full = lambda shape: pl.BlockSpec(shape, lambda i, j: (0,) * len(shape))
    out_dims = (hd, nseg) if transpose_out else (nseg, hd)
    return pl.pallas_call(
        functools.partial(_compress_kernel, ncmp=nseg - 1, transpose_out=transpose_out),
        grid=(b, g),
        in_specs=[pl.BlockSpec((1, 1, t, hd), lambda i, j: (i, j, 0, 0)), full((CMP_BLK, hd)),
                  full((CMP_BLK, hd, CMP_HID)), full((CMP_HID, hd))],
        out_specs=pl.BlockSpec((1, 1) + out_dims, lambda i, j: (i, j, 0, 0)),
        out_shape=jax.ShapeDtypeStruct((b, g) + out_dims, BF16),
        compiler_params=_params("parallel", "parallel"),
        name="nsa_compress",
    )(raw, pe, w1.astype(BF16), w2.astype(BF16))


def _nsa_attn_kernel(q_ref, kc_ref, vct_ref, ks_ref, vst_ref, kw_ref, vwt_ref, glt_ref, ovt_ref, o_ref,
                     selb_ref, acc_ref, ocw_ref, s_ref, p_ref, *, seq, n_sel):
    Q = q_ref.shape[3]
    q0 = pl.program_id(2) * Q
    rows = NSA_REP * Q
    q = q_ref[0, 0].reshape(rows, HEAD_DIM)
    tq = q0 + lax.broadcasted_iota(jnp.int32, (1, Q), 1)
    head = lambda r: slice(r * Q, (r + 1) * Q)

    ncp = kc_ref.shape[2]
    wk = WINDOW + Q
    kt_sz = vst_ref.shape[4]
    start = pl.multiple_of(jnp.clip(q0 - WINDOW, 0, seq - wk), WIN_TILE)
    s_cw = _dot_nt(jnp.concatenate([kc_ref[0, 0], kw_ref[0, 0, pl.ds(start, wk), :], ks_ref[0, 0, 0:kt_sz, :]],
                                   axis=0), q)
    s_c, s_w = s_cw[:ncp], s_cw[ncp:ncp + wk]
    s_ref[0] = s_cw[ncp + wk:]

    cend = lax.broadcasted_iota(jnp.int32, (ncp, 1), 0) * CMP_STRIDE + (CMP_BLK - 1)
    m_c = cend <= tq
    ps = []
    for r in range(NSA_REP):
        s = jnp.where(m_c, s_c[:, head(r)], NEG_INF)
        e = jnp.exp2(s - jnp.max(s, axis=0, keepdims=True))
        p = jnp.where(m_c, e * (1.0 / jnp.sum(e, axis=0, keepdims=True)), 0.0)
        ps.append(p.astype(BF16))
    p_c = jnp.concatenate(ps, axis=1)
    nsb = ovt_ref.shape[0]
    oc_imp = _dot(jnp.concatenate([vct_ref[0, 0], ovt_ref[...]], axis=0), p_c)
    o_c, imp4 = oc_imp[:HEAD_DIM], oc_imp[HEAD_DIM:]

    imp = imp4[:, head(0)]
    for r in range(1, NSA_REP):
        imp = imp + imp4[:, head(r)]
    blk_t = (q0 + lax.broadcasted_iota(jnp.int32, (nsb, Q), 1)) >> SEL_SHIFT
    sb = lax.broadcasted_iota(jnp.int32, (nsb, Q), 0)
    forced = (sb == 0) | (sb == blk_t) | (sb == blk_t - 1)
    imp = jnp.where(sb <= blk_t, jnp.where(forced, FORCE_SCORE, imp), -jnp.inf)
    rank = jnp.zeros((nsb, Q), F32)
    for i in range(nsb):
        ri = imp[i:i + 1, :]
        beats = (ri > imp) | ((ri == imp) & (sb > i))
        rank = rank + jnp.where(beats, 1.0, 0.0)
    selb_ref[...] = jnp.where(rank < n_sel, 0.0, NEG_INF)

    dist = tq - (start + lax.broadcasted_iota(jnp.int32, (wk, 1), 0))
    bias_w = jnp.where((dist >= 0) & (dist < WINDOW), 0.0, NEG_INF)
    ps, ls = [], []
    for r in range(NSA_REP):
        s = s_w[:, head(r)] + bias_w
        p = jnp.exp2(s - jnp.max(s, axis=0, keepdims=True))
        ls.append(jnp.sum(p, axis=0, keepdims=True))
        ps.append(p.astype(BF16))
    t0 = start // WIN_TILE
    vw_t = jnp.concatenate([vwt_ref[0, 0, t0 + j] for j in range(wk // WIN_TILE)], axis=1)
    o_w = _dot(vw_t, jnp.concatenate(ps, axis=1)) * (1.0 / jnp.concatenate(ls, axis=1))

    gate = _sigmoid(glt_ref[0, 0])
    for r in range(NSA_REP):
        ocw_ref[:, head(r)] = (gate[r:r + 1, :] * o_c[:, head(r)]
                               + gate[2 * NSA_REP + r:2 * NSA_REP + r + 1, :] * o_w[:, head(r)])

    blocks_per_tile = kt_sz // SEL_BLK
    n_all = seq // kt_sz
    acc_ref[...] = jnp.zeros_like(acc_ref)
    p_ref[1] = jnp.zeros(p_ref.shape[1:], p_ref.dtype)

    def sel_tile(kt, carry):
        m, l, alpha_prev = carry
        slot = kt & 1
        s_all = s_ref[slot]
        p_prev = p_ref[1 - slot]
        k_next = pl.multiple_of(jnp.minimum(kt + 1, n_all - 1) * kt_sz, kt_sz)
        s_ref[1 - slot] = _dot_nt(ks_ref[0, 0, pl.ds(k_next, kt_sz), :], q)
        pv_prev = _dot(vst_ref[0, 0, jnp.maximum(kt - 1, 0)], p_prev)
        acc_ref[...] = acc_ref[...] * alpha_prev + pv_prev
        k0 = kt * kt_sz
        kpos = k0 + lax.broadcasted_iota(jnp.int32, (kt_sz, 1), 0)
        bias = jnp.concatenate(
            [jnp.broadcast_to(selb_ref[pl.ds(kt * blocks_per_tile + i, 1), :], (SEL_BLK, Q))
             for i in range(blocks_per_tile)], axis=0)
        bias = jnp.where(kpos <= tq, bias, NEG_INF)
        ps, ms, ls, alphas = [], [], [], []
        for r in range(NSA_REP):
            s = s_all[:, head(r)] + bias
            m_old = m[:, head(r)]
            m_new = jnp.maximum(m_old, jnp.max(s, axis=0, keepdims=True))
            p = jnp.exp2(s - m_new)
            alpha = jnp.exp2(m_old - m_new)
            ls.append(alpha * l[:, head(r)] + jnp.sum(p, axis=0, keepdims=True))
            ms.append(m_new)
            alphas.append(alpha)
            ps.append(p.astype(BF16))
        p_ref[slot] = jnp.concatenate(ps, axis=1)
        return jnp.concatenate(ms, axis=1), jnp.concatenate(ls, axis=1), jnp.concatenate(alphas, axis=1)

    n_tiles = (q0 + Q + kt_sz - 1) // kt_sz
    init = (jnp.full((1, rows), NEG_INF, F32), jnp.zeros((1, rows), F32), jnp.ones((1, rows), F32))
    _, l_s, alpha_last = lax.fori_loop(0, n_tiles, sel_tile, init)
    pv_last = _dot(vst_ref[0, 0, n_tiles - 1], p_ref[(n_tiles - 1) & 1])
    o_s = (acc_ref[...] * alpha_last + pv_last) * (1.0 / l_s)

    for r in range(NSA_REP):
        out_t = ocw_ref[:, head(r)] + gate[NSA_REP + r:NSA_REP + r + 1, :] * o_s[:, head(r)]
        o_ref[:, r * HEAD_DIM:(r + 1) * HEAD_DIM] = out_t.T.astype(o_ref.dtype)


def nsa_attention(qh, kcc, vcct, ksn, vst, kwn, vwt, gate_logits_t, batch, seq):
    Q_BLK = NSA_Q_BLK
    assert seq % SEL_KEY_TILE == 0 and seq % Q_BLK == 0 and seq >= WINDOW + Q_BLK
    nq = seq // Q_BLK
    nsb = seq // SEL_BLK
    n_sel = min(N_SEL, nsb)
    ncp = kcc.shape[2]
    ncmp = (seq - CMP_BLK) // CMP_STRIDE + 1
    ci = np.arange(ncp)[None, :]
    sj = np.arange(nsb)[:, None]
    ovt = ((ci * CMP_STRIDE < (sj + 1) * SEL_BLK) & (ci * CMP_STRIDE + CMP_BLK > sj * SEL_BLK) & (ci < ncmp))
    ovt = jnp.asarray(ovt, BF16)
    seq_spec = pl.BlockSpec((1, 1, seq, HEAD_DIM), lambda b, g, i: (b, g, 0, 0))
    cmp_spec = pl.BlockSpec((1, 1, ncp, HEAD_DIM), lambda b, g, i: (b, g, 0, 0))
    tiled = lambda a: pl.BlockSpec((1, 1) + a.shape[2:], lambda b, g, i: (b, g, 0, 0, 0))
    return pl.pallas_call(
        functools.partial(_nsa_attn_kernel, seq=seq, n_sel=n_sel),
        grid=(batch, NSA_KV_HEADS, nq),
        in_specs=[pl.BlockSpec((1, 1, NSA_REP, Q_BLK, HEAD_DIM), lambda b, g, i: (b, g, 0, i, 0)),
                  cmp_spec, pl.BlockSpec((1, 1, HEAD_DIM, ncp), lambda b, g, i: (b, g, 0, 0)),
                  seq_spec, tiled(vst), seq_spec, tiled(vwt),
                  pl.BlockSpec((1, 1, 3 * NSA_REP, Q_BLK), lambda b, g, i: (b, g, 0, i)),
                  pl.BlockSpec((nsb, ncp), lambda b, g, i: (0, 0))],
        out_specs=pl.BlockSpec((Q_BLK, NSA_REP * HEAD_DIM), lambda b, g, i: (b * nq + i, g)),
        out_shape=jax.ShapeDtypeStruct((batch * seq, NSA_WIDTH), BF16),
        scratch_shapes=[pltpu.VMEM((nsb, Q_BLK), F32), pltpu.VMEM((HEAD_DIM, NSA_REP * Q_BLK), F32),
                        pltpu.VMEM((HEAD_DIM, NSA_REP * Q_BLK), F32),
                        pltpu.VMEM((2, SEL_KEY_TILE, NSA_REP * Q_BLK), F32),
                        pltpu.VMEM((2, SEL_KEY_TILE, NSA_REP * Q_BLK), BF16)],
        compiler_params=_params("parallel", "parallel", "arbitrary"),
        name="nsa_attention",
    )(qh, kcc, vcct, ksn, vst, kwn, vwt, gate_logits_t, ovt)


def _merge_kernel(ys_ref, yn_ref, ws_ref, wn_ref, gs_ref, gn_ref, o_ref):
    up_s = _dot(ys_ref[...], ws_ref[...])
    up_n = _dot(yn_ref[...], wn_ref[...])
    o_ref[...] = (_sigmoid(gs_ref[...]) * up_s + _sigmoid(gn_ref[...]) * up_n).astype(o_ref.dtype)


def merge_mixers(y_ssd, y_nsa, w_up_ssd, w_up_nsa, gates, tm=1024, tn=512):
    m = y_ssd.shape[0]
    nb = D_MODEL // tn
    return pl.pallas_call(
        _merge_kernel,
        grid=(m // tm, nb),
        in_specs=[pl.BlockSpec((tm, SSD_INNER), lambda i, j: (i, 0)), pl.BlockSpec((tm, NSA_WIDTH), lambda i, j: (i, 0)),
                  pl.BlockSpec((SSD_INNER, tn), lambda i, j: (0, j)), pl.BlockSpec((NSA_WIDTH, tn), lambda i, j: (0, j)),
                  pl.BlockSpec((tm, tn), lambda i, j: (i, j)), pl.BlockSpec((tm, tn), lambda i, j: (i, nb + j))],
        out_specs=pl.BlockSpec((tm, tn), lambda i, j: (i, j)),
        out_shape=jax.ShapeDtypeStruct((m, D_MODEL), BF16),
        compiler_params=_params("parallel", "arbitrary"),
        name="merge_mixers",
    )(y_ssd, y_nsa, w_up_ssd, w_up_nsa, gates, gates)


def _residual_matmul_kernel(h_ref, x_ref, w_ref, o_ref):
    o_ref[...] = h_ref[...] + _dot(x_ref[...], w_ref[...])


def residual_matmul(h, x, w, tm=1024, tn=1024):
    m, k = x.shape
    n = w.shape[1]
    return pl.pallas_call(
        _residual_matmul_kernel,
        grid=(m // tm, n // tn),
        in_specs=[pl.BlockSpec((tm, tn), lambda i, j: (i, j)), pl.BlockSpec((tm, k), lambda i, j: (i, 0)),
                  pl.BlockSpec((k, tn), lambda i, j: (0, j))],
        out_specs=pl.BlockSpec((tm, tn), lambda i, j: (i, j)),
        out_shape=jax.ShapeDtypeStruct((m, n), F32),
        compiler_params=_params("parallel", "arbitrary"),
        name="residual_matmul",
    )(h, x, w)


def _xattn_router_kernel(h_ref, kv_ref, n2_ref, wq_ref, qn_ref, kn_ref, wo_ref, n3_ref, rw_ref, rb_ref,
                         h2_ref, hf_ref, rt_ref):
    h = h_ref[...]
    qp = _dot(_rms(h, n2_ref[...]).astype(BF16), wq_ref[...])
    kv = kv_ref[...]
    scale = HEAD_DIM ** -0.5
    heads = []
    for hd in range(X_HEADS):
        sl = slice(hd * HEAD_DIM, (hd + 1) * HEAD_DIM)
        qh = _rms(qp[:, sl], qn_ref[...]).astype(BF16)
        kh = _rms(kv[:, sl], kn_ref[...]).astype(BF16)
        vh = kv[:, X_WIDTH + hd * HEAD_DIM:X_WIDTH + (hd + 1) * HEAD_DIM].astype(BF16)
        s = _dot_nt(qh, kh) * scale
        e = jnp.exp(s - jnp.max(s, axis=-1, keepdims=True))
        p = e / jnp.sum(e, axis=-1, keepdims=True)
        heads.append(_dot(p.astype(BF16), vh))
    o = jnp.concatenate(heads, axis=1).astype(BF16)
    h2 = h + _dot(o, wo_ref[...])
    h2_ref[...] = h2
    hf = _rms(h2, n3_ref[...]).astype(BF16)
    hf_ref[...] = hf
    lg = _dot(hf, rw_ref[...]) + rb_ref[...]

    lane = lax.broadcasted_iota(jnp.int32, lg.shape, 1)
    rmax = lambda v: jnp.max(v, axis=-1, keepdims=True)
    rsum = lambda v: jnp.sum(v, axis=-1, keepdims=True)
    first_lane = lambda hit: jnp.min(jnp.where(hit, lane, LANES), axis=-1, keepdims=True)
    is_g = lane < N_GROUPS
    eg = jnp.where(is_g, jnp.exp(lg - rmax(jnp.where(is_g, lg, -jnp.inf))), 0.0)
    pg = eg / rsum(eg)
    pg_top = rmax(pg)
    grp = first_lane(is_g & (pg == pg_top))
    in_grp = (lane >= N_GROUPS) & (((lane - N_GROUPS) >> EPG_SHIFT) == grp)
    ee = jnp.where(in_grp, jnp.exp(lg - rmax(jnp.where(in_grp, lg, -jnp.inf))), 0.0)
    pe = jnp.where(in_grp, ee / rsum(ee), -1.0)
    p1 = rmax(pe)
    i1 = first_lane(pe == p1)
    pe2 = jnp.where(lane == i1, -1.0, pe)
    p2 = rmax(pe2)
    i2 = first_lane(pe2 == p2)
    psum = p1 + p2
    route = jnp.where(lane == 0, (i1 - N_GROUPS).astype(F32),
                      jnp.where(lane == 1, (i2 - N_GROUPS).astype(F32),
                                jnp.where(lane == 2, pg_top * p1 / psum,
                                          jnp.where(lane == 3, pg_top * p2 / psum, 0.0))))
    rt_ref[...] = route


def xattn_router(h1, kv, norm2_w, wq, q_norm_w, k_norm_w, wo, norm3_w, router_w, router_b, batch, seq, tm=512):
    n = batch * seq
    tm = min(tm, seq)
    per_seq = seq // tm
    mlen = kv.shape[0] // batch
    full = lambda shape: pl.BlockSpec(shape, lambda i: (0,) * len(shape))
    return pl.pallas_call(
        _xattn_router_kernel,
        grid=(n // tm,),
        in_specs=[pl.BlockSpec((tm, D_MODEL), lambda i: (i, 0)),
                  pl.BlockSpec((mlen, 2 * X_WIDTH), lambda i: (i // per_seq, 0)),
                  full((1, D_MODEL)), full((D_MODEL, X_WIDTH)), full((1, HEAD_DIM)), full((1, HEAD_DIM)),
                  full((X_WIDTH, D_MODEL)), full((1, D_MODEL)), full((D_MODEL, LANES)), full((1, LANES))],
        out_specs=[pl.BlockSpec((tm, D_MODEL), lambda i: (i, 0)), pl.BlockSpec((tm, D_MODEL), lambda i: (i, 0)),
                   pl.BlockSpec((tm, LANES), lambda i: (i, 0))],
        out_shape=[jax.ShapeDtypeStruct((n, D_MODEL), F32), jax.ShapeDtypeStruct((n, D_MODEL), BF16),
                   jax.ShapeDtypeStruct((n, LANES), F32)],
        compiler_params=_params("parallel"),
        name="xattn_router",
    )(h1, kv, norm2_w.reshape(1, -1), wq, q_norm_w.reshape(1, -1), k_norm_w.reshape(1, -1), wo,
      norm3_w.reshape(1, -1), router_w, router_b)


def _moe_rank_kernel(e_ref, rank_ref, cnt_ref, carry_ref):
    @pl.when(pl.program_id(0) == 0)
    def _():
        carry_ref[...] = jnp.zeros_like(carry_ref)

    t = e_ref.shape[0]
    hit = e_ref[...] == lax.broadcasted_iota(jnp.int32, (t, LANES), 1)
    onehot = jnp.where(hit, 1.0, 0.0).astype(BF16)
    earlier = lax.broadcasted_iota(jnp.int32, (t, t), 0) > lax.broadcasted_iota(jnp.int32, (t, t), 1)
    before = _dot(jnp.where(earlier, 1.0, 0.0).astype(BF16), onehot) + carry_ref[...]
    rank_ref[...] = jnp.sum(jnp.where(hit, before, 0.0), axis=-1, keepdims=True).astype(jnp.int32)
    carry_ref[...] += jnp.sum(jnp.where(hit, 1.0, 0.0), axis=0, keepdims=True)
    cnt_ref[...] = carry_ref[...]


def moe_rank(eid, t=1024):
    a = eid.shape[0]
    return pl.pallas_call(
        _moe_rank_kernel,
        grid=(a // t,),
        in_specs=[pl.BlockSpec((t, 1), lambda i: (i, 0))],
        out_specs=[pl.BlockSpec((t, 1), lambda i: (i, 0)), pl.BlockSpec((1, LANES), lambda i: (0, 0))],
        out_shape=[jax.ShapeDtypeStruct((a, 1), jnp.int32), jax.ShapeDtypeStruct((1, LANES), F32)],
        scratch_shapes=[pltpu.VMEM((1, LANES), F32)],
        compiler_params=_params("arbitrary"),
        name="moe_rank",
    )(eid)


def _moe_ffn_kernel(item_e_ref, item_row_ref, item_nsub_ref, item_nout_ref, x_hbm, wg_ref, wu_ref, wd_ref, y_hbm,
                    x_vmem, y_vmem, sem_x, sem_y, *, hid):
    i, c = pl.program_id(0), pl.program_id(1)
    last_c = pl.num_programs(1) - 1
    th = wg_ref.shape[2]
    nsub, nout = item_nsub_ref[i], item_nout_ref[i]
    row0 = pl.multiple_of(item_row_ref[i], MOE_SUB)
    sub = lambda s: pl.ds(s * MOE_SUB, MOE_SUB)

    def x_copy(item, s):
        row = pl.multiple_of(item_row_ref[item], MOE_SUB)
        return pltpu.make_async_copy(x_hbm.at[pl.ds(row + s * MOE_SUB, MOE_SUB)], x_vmem.at[item & 1, sub(s)],
                                     sem_x.at[item & 1, s])

    def y_copy(row, s):
        return pltpu.make_async_copy(y_vmem.at[sub(s)], y_hbm.at[pl.ds(row + s * MOE_SUB, MOE_SUB)], sem_y.at[s])

    def for_subs(count, fn):
        for s in range(MOE_ITEM_SUBS):
            pl.when(s < count)(functools.partial(fn, s))

    @pl.when(c == 0)
    def _():
        @pl.when(i > 0)
        def _():
            prev_row = pl.multiple_of(item_row_ref[i - 1], MOE_SUB)
            for_subs(item_nout_ref[i - 1], lambda s: y_copy(prev_row, s).wait())

        @pl.when(i == 0)
        def _():
            for_subs(nsub, lambda s: x_copy(i, s).start())

        for_subs(nsub, lambda s: x_copy(i, s).wait())

        @pl.when(i + 1 < pl.num_programs(0))
        def _():
            nxt = jnp.minimum(i + 1, pl.num_programs(0) - 1)
            for_subs(item_nsub_ref[nxt], lambda s: x_copy(nxt, s).start())

        @pl.when(nsub == 0)
        def _():
            y_vmem[...] = jnp.zeros_like(y_vmem)
            for_subs(nout, lambda s: y_copy(row0, s).start())

    def ffn(count):
        rows = pl.ds(0, count * MOE_SUB)
        col_ok = c * th + lax.broadcasted_iota(jnp.int32, (1, th), 1) < hid
        row_ok = c * th + lax.broadcasted_iota(jnp.int32, (th, 1), 0) < hid
        wgu = jnp.concatenate([wg_ref[0], wu_ref[0]], axis=1).astype(BF16)
        wd = jnp.where(row_ok, wd_ref[0], 0.0).astype(BF16)
        gu = _dot(x_vmem[i & 1, rows, :], wgu)
        act = jnp.where(col_ok, _silu(gu[:, :th]) * gu[:, th:], 0.0)
        part = _dot(act.astype(BF16), wd)

        @pl.when(c == 0)
        def _():
            y_vmem[rows, :] = part

        @pl.when(c > 0)
        def _():
            y_vmem[rows, :] += part

    for count in range(1, MOE_ITEM_SUBS + 1):
        pl.when(nsub == count)(functools.partial(ffn, count))

    @pl.when((nsub > 0) & (c == last_c))
    def _():
        for_subs(nsub, lambda s: y_copy(row0, s).start())

    @pl.when((i == pl.num_programs(0) - 1) & (c == last_c))
    def _():
        for_subs(nout, lambda s: y_copy(row0, s).wait())


def moe_ffn(x_buf, item_e, item_row, item_nsub, item_nout, w_gate, w_up, w_down):
    rows, d = x_buf.shape
    hid = w_gate.shape[2]
    th = MOE_HID_TILE
    nc = pl.cdiv(hid, th)
    tile = lambda i, c, ns: jnp.where(ns[i] > 0, c, nc - 1)
    grid_spec = pltpu.PrefetchScalarGridSpec(
        num_scalar_prefetch=4,
        grid=(item_e.shape[0], nc),
        in_specs=[pl.BlockSpec(memory_space=pl.ANY),
                  pl.BlockSpec((1, d, th), lambda i, c, ie, ir, ns, no: (ie[i], 0, tile(i, c, ns))),
                  pl.BlockSpec((1, d, th), lambda i, c, ie, ir, ns, no: (ie[i], 0, tile(i, c, ns))),
                  pl.BlockSpec((1, th, d), lambda i, c, ie, ir, ns, no: (ie[i], tile(i, c, ns), 0))],
        out_specs=pl.BlockSpec(memory_space=pl.ANY),
        scratch_shapes=[pltpu.VMEM((2, MOE_ITEM_SUBS * MOE_SUB, d), BF16),
                        pltpu.VMEM((MOE_ITEM_SUBS * MOE_SUB, d), F32),
                        pltpu.SemaphoreType.DMA((2, MOE_ITEM_SUBS)), pltpu.SemaphoreType.DMA((MOE_ITEM_SUBS,))],
    )
    return pl.pallas_call(
        functools.partial(_moe_ffn_kernel, hid=hid),
        grid_spec=grid_spec,
        out_shape=jax.ShapeDtypeStruct((rows, d), F32),
        compiler_params=_params("arbitrary", "arbitrary"),
        name="moe_ffn",
    )(item_e, item_row, item_nsub, item_nout, x_buf, w_gate, w_up, w_down)


def hier_moe(h2, hf, route, w_gate, w_up, w_down):
    n, d = h2.shape
    n_exp = w_gate.shape[0]
    i32 = jnp.int32
    eid = route[:, 0:TOP_K].astype(i32).reshape(-1)
    wts = route[:, TOP_K:2 * TOP_K]
    n_assign = n * TOP_K
    rank, counts = moe_rank(eid.reshape(n_assign, 1))
    counts = counts[0, :n_exp].astype(i32)
    subs_e = (counts + MOE_SUB - 1) // MOE_SUB
    sub_end = jnp.cumsum(subs_e)
    row_start = (sub_end - subs_e) * MOE_SUB
    is_e = eid[:, None] == jnp.arange(n_exp, dtype=i32)[None, :]
    dest = rank[:, 0] + jnp.sum(jnp.where(is_e, row_start[None, :], 0), axis=1)
    n_subs = (n_assign + n_exp * (MOE_SUB - 1) + MOE_SUB - 1) // MOE_SUB
    rows = n_subs * MOE_SUB
    tok = jnp.repeat(jnp.arange(n, dtype=i32), TOP_K)
    tok_buf = (jnp.arange(rows, dtype=i32) % n).at[dest].set(tok)
    per = MOE_ITEM_SUBS
    n_items = (n_exp * (per - 1) + n_subs + per - 1) // per
    items_e = (subs_e + per - 1) // per
    item_end = jnp.cumsum(items_e)
    n_used = item_end[-1]
    idx = jnp.arange(n_items, dtype=i32)
    used = idx < n_used
    e_of = jnp.minimum(jnp.searchsorted(item_end, idx, side='right'), n_exp - 1).astype(i32)
    j = idx - (item_end - items_e)[e_of]
    item_nsub = jnp.where(used, jnp.clip(subs_e[e_of] - per * j, 0, per), 0)
    fill_first = jnp.minimum(sub_end[-1] + per * (idx - n_used), n_subs)
    item_fill = jnp.where(used, 0, jnp.minimum(n_subs - fill_first, per))
    item_row = jnp.where(used, row_start[e_of] + j * per * MOE_SUB, jnp.minimum(fill_first, n_subs - 1) * MOE_SUB)
    item_e = jnp.where(used, e_of, e_of[jnp.maximum(n_used - 1, 0)])
    y_buf = moe_ffn(hf[tok_buf], item_e.astype(i32), item_row.astype(i32), item_nsub.astype(i32),
                    (item_nsub + item_fill).astype(i32), w_gate, w_up, w_down)
    slot = dest.reshape(n, TOP_K)
    return h2 + wts[:, 0:1] * y_buf[slot[:, 0]] + wts[:, 1:2] * y_buf[slot[:, 1]]


COL_DT = ZX_COLS
COL_QKV = COL_DT + SSD_HEADS
COL_NSA_GATE = COL_QKV + QKV_COLS
COL_MERGE_GATE = COL_NSA_GATE + 3 * NSA_HEADS


def _small_w_in(w_in):
    pad = jnp.zeros(w_in.shape[:2] + (SMALL_COLS - SSD_HEADS - 3 * NSA_HEADS,), w_in.dtype)
    return jnp.concatenate([w_in[:, :, COL_DT:COL_QKV], w_in[:, :, COL_NSA_GATE:COL_MERGE_GATE], pad], axis=2)


def _layer(x, mem, norm1_w, w_in, ssd_conv_w, ssd_conv_b, ssd_dt_bias, ssd_a_log, ssd_d, ssd_norm_w,
           nsa_q_norm_w, nsa_k_norm_w, cmp_pe_k, cmp_w1_k, cmp_w2_k, cmp_pe_v, cmp_w1_v, cmp_w2_v,
           w_up_ssd, w_up_nsa, w_out, norm2_w, mem_norm_w, xq_w, xkv_w, x_q_norm_w, x_k_norm_w, xo_w,
           norm3_w, router_g_w, router_g_b, router_e_w, router_e_b, moe_w_gate, moe_w_up, moe_w_down):
    batch, seq, d = x.shape
    n = batch * seq
    xf = x.reshape(n, d)
    hn = rmsnorm_rows(xf, norm1_w)
    w_cols = lambda c0, nc: w_in[0, :, c0:c0 + nc].astype(BF16)
    zx = matmul(hn, w_cols(0, ZX_COLS), tm=IN_PROJ_TM, tn=IN_PROJ_TN)
    qkv = matmul(hn, w_cols(COL_QKV, QKV_COLS), tm=IN_PROJ_TM, tn=IN_PROJ_TN)
    mgates = matmul(hn, w_cols(COL_MERGE_GATE, GATE_COLS), tm=IN_PROJ_TM, tn=IN_PROJ_TN)
    small = matmul(hn, _small_w_in(w_in)[0].astype(BF16), tm=IN_PROJ_TM)
    y_ssd = ssd_branch(zx, small, ssd_conv_w, ssd_conv_b, ssd_dt_bias, ssd_a_log, ssd_d, ssd_norm_w, batch, seq)
    qh, kcn, vcr, ksn, vsn, kwn, vwn = nsa_prep(qkv, nsa_q_norm_w, nsa_k_norm_w, batch, seq)
    kcc = compress(kcn, cmp_pe_k, cmp_w1_k, cmp_w2_k, transpose_out=False)
    vcct = compress(vcr, cmp_pe_v, cmp_w1_v, cmp_w2_v, transpose_out=True)
    gl = small[:, SSD_HEADS:SSD_HEADS + 3 * NSA_HEADS].reshape(batch, seq, 3, NSA_KV_HEADS, NSA_REP)
    gl = gl.transpose(0, 3, 2, 4, 1).reshape(batch, NSA_KV_HEADS, 3 * NSA_REP, seq)
    y_nsa = nsa_attention(qh, kcc, vcct, ksn, vsn, kwn, vwn, gl, batch, seq)
    merged = merge_mixers(y_ssd, y_nsa, w_up_ssd.astype(BF16), w_up_nsa.astype(BF16), mgates)
    h1 = residual_matmul(xf, merged, w_out.astype(BF16))
    mlen = mem.shape[1]
    mn = rmsnorm_rows(mem.reshape(batch * mlen, d), mem_norm_w, tm=min(512, batch * mlen))
    kv = matmul(mn, xkv_w.astype(BF16))
    n_exp = router_e_w.shape[1]
    rpad = LANES - N_GROUPS - n_exp
    router_w = jnp.concatenate([router_g_w, router_e_w, jnp.zeros((d, rpad), F32)], axis=1).astype(BF16)
    router_b = jnp.concatenate([router_g_b, router_e_b, jnp.zeros((rpad,), F32)]).reshape(1, LANES)
    h2, hf, route = xattn_router(h1, kv, norm2_w, xq_w.astype(BF16), x_q_norm_w, x_k_norm_w, xo_w.astype(BF16),
                                 norm3_w, router_w, router_b, batch, seq)
    out = hier_moe(h2, hf, route, moe_w_gate, moe_w_up, moe_w_down)
    return out.reshape(batch, seq, d)


def kernel(x, mem, norm1_w, w_in, ssd_conv_w, ssd_conv_b, ssd_dt_bias, ssd_a_log, ssd_d, ssd_norm_w, nsa_q_norm_w, nsa_k_norm_w, cmp_pe_k, cmp_w1_k, cmp_w2_k, cmp_pe_v, cmp_w1_v, cmp_w2_v, w_up_ssd, w_up_nsa, w_out, norm2_w, mem_norm_w, xq_w, xkv_w, x_q_norm_w, x_k_norm_w, xo_w, norm3_w, router_g_w, router_g_b, router_e_w, router_e_b, moe_w_gate, moe_w_up, moe_w_down):
    h = x
    for l in range(norm1_w.shape[0]):
        h = _layer(h, mem, norm1_w[l], w_in[l:l + 1], ssd_conv_w[l], ssd_conv_b[l], ssd_dt_bias[l], ssd_a_log[l], ssd_d[l],
                   ssd_norm_w[l], nsa_q_norm_w[l], nsa_k_norm_w[l], cmp_pe_k[l], cmp_w1_k[l], cmp_w2_k[l], cmp_pe_v[l],
                   cmp_w1_v[l], cmp_w2_v[l], w_up_ssd[l], w_up_nsa[l], w_out[l], norm2_w[l], mem_norm_w[l], xq_w[l],
                   xkv_w[l], x_q_norm_w[l], x_k_norm_w[l], xo_w[l], norm3_w[l], router_g_w[l], router_g_b[l],
                   router_e_w[l], router_e_b[l], moe_w_gate[l], moe_w_up[l], moe_w_down[l])
    return h.astype(x.dtype)
```

```python
import functools

import numpy as np
import jax
import jax.numpy as jnp
from jax import lax
from jax.experimental import pallas as pl
from jax.experimental.pallas import tpu as pltpu

F32 = jnp.float32
BF16 = jnp.bfloat16

D_MODEL = 2048
SSD_INNER = 4096
SSD_HEAD_DIM = 64
SSD_HEADS = 64
SSD_GROUPS = 8
SSD_STATE = 128
SSD_CONV = 4
SSD_CHUNK = 256
SSD_CHUNKS_PER_STEP = 2
HEADS_PER_GROUP = SSD_HEADS // SSD_GROUPS
GROUP_WIDTH = SSD_INNER // SSD_GROUPS
HEAD_DIM = 128
NSA_HEADS = 16
NSA_KV_HEADS = 4
NSA_REP = NSA_HEADS // NSA_KV_HEADS
NSA_WIDTH = NSA_HEADS * HEAD_DIM
NSA_KV_WIDTH = NSA_KV_HEADS * HEAD_DIM
CMP_BLK = 32
CMP_STRIDE = 16
CMP_HID = 256
SEL_BLK = 64
SEL_SHIFT = 6
N_SEL = 16
WINDOW = 512
NSA_Q_BLK = 256
WIN_TILE = 128
SEL_KEY_TILE = 256
FORCE_SCORE = 1.0e4
X_HEADS = 4
X_WIDTH = X_HEADS * HEAD_DIM
N_GROUPS = 8
EXPERTS_PER_GROUP = 8
EPG_SHIFT = 3
N_EXPERTS = 64
TOP_K = 2
EXPERT_HIDDEN = 1408
MOE_SUB = 128
MOE_ITEM_SUBS = 8
MOE_HID_TILE = 256
ROPE_THETA = 10000.0
EPS = 1e-6
NEG_INF = -1e30
LOG2E = float(np.log2(np.e))
Q_PRESCALE = HEAD_DIM ** -0.5 * LOG2E
SUBLANES = 8
LANES = 128
VMEM_LIMIT = 56 * 1024 * 1024

ZX_COLS = SSD_INNER + SSD_INNER + 2 * SSD_GROUPS * SSD_STATE
QKV_COLS = NSA_WIDTH + 6 * NSA_KV_WIDTH
GATE_COLS = 2 * D_MODEL
SMALL_COLS = LANES
IN_PROJ_TM, IN_PROJ_TN = 2048, 1024


def _params(*sem):
    return pltpu.CompilerParams(dimension_semantics=sem, vmem_limit_bytes=VMEM_LIMIT)


def _rms(x, w):
    ms = jnp.mean(x * x, axis=-1, keepdims=True)
    return x * lax.rsqrt(ms + EPS) * w


def _sigmoid(x):
    return 0.5 + 0.5 * jnp.tanh(0.5 * x)


def _silu(x):
    h = 0.5 * x
    return h + h * jnp.tanh(h)


def _softplus(x):
    return jnp.maximum(x, 0.0) + jnp.log1p(jnp.exp(-jnp.abs(x)))


def _dot(a, b):
    return jnp.dot(a, b, preferred_element_type=F32)


def _dot_nt(a, b):
    return lax.dot_general(a, b, (((1,), (1,)), ((), ())), preferred_element_type=F32)


def _rmsnorm_kernel(x_ref, w_ref, o_ref):
    o_ref[...] = _rms(x_ref[...], w_ref[...]).astype(o_ref.dtype)


def rmsnorm_rows(x, w, tm=512):
    m, d = x.shape
    return pl.pallas_call(
        _rmsnorm_kernel,
        grid=(m // tm,),
        in_specs=[pl.BlockSpec((tm, d), lambda i: (i, 0)), pl.BlockSpec((1, d), lambda i: (0, 0))],
        out_specs=pl.BlockSpec((tm, d), lambda i: (i, 0)),
        out_shape=jax.ShapeDtypeStruct((m, d), BF16),
        compiler_params=_params("parallel"),
        name="rmsnorm_rows",
    )(x, w.reshape(1, d))


def _matmul_kernel(x_ref, w_ref, o_ref):
    o_ref[...] = _dot(x_ref[...], w_ref[...]).astype(o_ref.dtype)


def matmul(x, w, out_dtype=F32, tm=512, tn=512):
    m, k = x.shape
    n = w.shape[1]
    tm, tn = min(tm, m), min(tn, n)
    return pl.pallas_call(
        _matmul_kernel,
        grid=(m // tm, n // tn),
        in_specs=[pl.BlockSpec((tm, k), lambda i, j: (i, 0)), pl.BlockSpec((k, tn), lambda i, j: (0, j))],
        out_specs=pl.BlockSpec((tm, tn), lambda i, j: (i, j)),
        out_shape=jax.ShapeDtypeStruct((m, n), out_dtype),
        compiler_params=_params("parallel", "arbitrary"),
        name="matmul",
    )(x, w)


def _split3(a):
    h = a.astype(BF16)
    r = a - h.astype(F32)
    m = r.astype(BF16)
    return h, m, (r - m.astype(F32)).astype(BF16)


def _ssd_kernel(z_ref, x_ref, b_ref, c_ref, dt_ref, dtt_ref, *rest):
    @pl.when(pl.program_id(2) == 0)
    def _():
        for ref in rest[-4:]:
            ref[...] = jnp.zeros_like(ref)

    L = x_ref.shape[0] // SSD_CHUNKS_PER_STEP
    for cc in range(SSD_CHUNKS_PER_STEP):
        rows = pl.ds(cc * L, L)
        _ssd_chunk(z_ref.at[rows], x_ref.at[rows], b_ref.at[rows], c_ref.at[rows], dt_ref.at[:, rows],
                   dtt_ref.at[:, :, rows], *rest[:9], rest[9].at[rows], *rest[10:])


def _ssd_chunk(z_ref, x_ref, b_ref, c_ref, dt_ref, dtt_ref, cwx_ref, cwb_ref, cwc_ref, cbx_ref, cbb_ref, cbc_ref,
               hp_ref, hpt_ref, nw_ref, y_ref, state_ref, ext_x_ref, ext_b_ref, ext_c_ref):
    L = x_ref.shape[0]
    H = L // 2

    def conv_silu(u_ref, ext_ref, w, b):
        ext_ref[0:SUBLANES, :] = ext_ref[L:L + SUBLANES, :]
        u = u_ref[...]
        ext_ref[SUBLANES:SUBLANES + L, :] = u
        acc = b + w[SSD_CONV - 1:SSD_CONV, :] * u
        for k in range(SSD_CONV - 1):
            acc = acc + w[k:k + 1, :] * ext_ref[pl.ds(SUBLANES - (SSD_CONV - 1) + k, L), :]
        return _silu(acc)

    xs = conv_silu(x_ref, ext_x_ref, cwx_ref[...], cbx_ref[...])
    bm = conv_silu(b_ref, ext_b_ref, cwb_ref[...], cbb_ref[...])
    cm = conv_silu(c_ref, ext_c_ref, cwc_ref[...], cbc_ref[...])
    bm16, cm16 = bm.astype(BF16), cm.astype(BF16)

    hp = hp_ref[0]
    hpt = hpt_ref[0]
    dt = _softplus(dt_ref[0] + hp[0:1, :])
    dtt = _softplus(dtt_ref[0] + hpt[:, 0:1])
    da = dt * (-jnp.exp(hp[1:2, :]))
    dat = dtt * (-jnp.exp(hpt[:, 1:2]))
    row = lax.broadcasted_iota(jnp.int32, (L, L), 0)
    col = lax.broadcasted_iota(jnp.int32, (L, L), 1)
    tril = jnp.where(row >= col, 1.0, 0.0).astype(BF16)
    triu = jnp.where(row <= col, 1.0, 0.0).astype(BF16)
    acum = sum(_dot(tril, part) for part in _split3(da)) * LOG2E
    acumt = sum(_dot(part, triu) for part in _split3(dat)) * LOG2E
    last = acum[L - 1:L, :]
    dte = jnp.exp2(last - acum)
    eac = jnp.exp2(acum)
    cdec = jnp.exp2(last)

    cb = _dot_nt(cm16, bm16)
    cb00, cb10, cb11 = cb[:H, :H], cb[H:, :H], cb[H:, H:]
    tri = lax.broadcasted_iota(jnp.int32, (H, H), 0) >= lax.broadcasted_iota(jnp.int32, (H, H), 1)
    lane = lax.broadcasted_iota(jnp.int32, (L, LANES), 1)
    first = lane < SSD_HEAD_DIM
    srow = lax.broadcasted_iota(jnp.int32, (LANES, SSD_STATE), 0) < SSD_HEAD_DIM
    z = z_ref[...]
    d_skip = hp[2:3, :]
    outs = []
    for p in range(HEADS_PER_GROUP // 2):
        h0, h1 = 2 * p, 2 * p + 1

        def pick(a, rows=first):
            return jnp.where(rows, a[:, h0:h0 + 1], a[:, h1:h1 + 1])

        xp = xs[:, p * LANES:(p + 1) * LANES]
        xdt = xp * pick(dt)
        xdt16 = xdt.astype(BF16)
        y_top = jnp.zeros((H, LANES), F32)
        y_bot = jnp.zeros((H, LANES), F32)
        for h, msk in ((h0, first), (h1, jnp.logical_not(first))):
            a_col, a_row = acum[:, h:h + 1], acumt[h:h + 1, :]
            d00 = jnp.exp2(jnp.where(tri, a_col[:H] - a_row[:, :H], -jnp.inf))
            d10 = jnp.exp2(a_col[H:] - a_row[:, :H])
            d11 = jnp.exp2(jnp.where(tri, a_col[H:] - a_row[:, H:], -jnp.inf))
            xh = jnp.where(msk, xdt16, jnp.zeros_like(xdt16))
            y_top = y_top + _dot((cb00 * d00).astype(BF16), xh[:H])
            y_bot = y_bot + _dot((cb10 * d10).astype(BF16), xh[:H]) + _dot((cb11 * d11).astype(BF16), xh[H:])
        y = jnp.concatenate([y_top, y_bot], axis=0)
        hprev = state_ref[p]
        y = y + _dot_nt(cm16, hprev.astype(BF16)) * pick(eac)
        xe_t = (xdt * pick(dte)).T.astype(BF16)
        state_ref[p] = hprev * jnp.where(srow, cdec[:, h0:h0 + 1], cdec[:, h1:h1 + 1]) + _dot(xe_t, bm16)
        outs.append(y + pick(d_skip, first[0:1, :]) * xp)
    y = jnp.concatenate(outs, axis=1) * _silu(z)
    y_ref[...] = _rms(y, nw_ref[...]).astype(y_ref.dtype)


def ssd_branch(zx, small, conv_w, conv_b, dt_bias, a_log, d_skip, norm_w, batch, seq):
    n = batch * seq
    L = min(SSD_CHUNK, seq)
    nc = seq // L
    g, hpg = SSD_GROUPS, HEADS_PER_GROUP
    dt_raw = small[:, :SSD_HEADS].reshape(n, g, hpg)
    dt_g = dt_raw.transpose(1, 0, 2)
    dtt_g = dt_raw.transpose(1, 2, 0)
    hp = jnp.stack([dt_bias, a_log, d_skip], axis=0).reshape(3, g, hpg).transpose(1, 0, 2)
    hp = jnp.pad(hp, ((0, 0), (0, SUBLANES - 3), (0, 0)))
    hpt = hp.transpose(0, 2, 1)
    cw = conv_w
    cbias = conv_b.reshape(1, -1)
    nxb = SSD_INNER // GROUP_WIDTH
    assert nc % SSD_CHUNKS_PER_STEP == 0
    lb, nb = L * SSD_CHUNKS_PER_STEP, nc // SSD_CHUNKS_PER_STEP
    rowblk = lambda b, gi, c: b * nb + c
    bc0 = SSD_INNER // SSD_STATE
    grid = (g, batch, nb)
    in_specs = [
        pl.BlockSpec((lb, GROUP_WIDTH), lambda gi, b, c: (rowblk(b, gi, c), gi)),
        pl.BlockSpec((lb, GROUP_WIDTH), lambda gi, b, c: (rowblk(b, gi, c), nxb + gi)),
        pl.BlockSpec((lb, SSD_STATE), lambda gi, b, c: (rowblk(b, gi, c), 2 * bc0 + gi)),
        pl.BlockSpec((lb, SSD_STATE), lambda gi, b, c: (rowblk(b, gi, c), 2 * bc0 + g + gi)),
        pl.BlockSpec((1, lb, hpg), lambda gi, b, c: (gi, rowblk(b, gi, c), 0)),
        pl.BlockSpec((1, hpg, lb), lambda gi, b, c: (gi, 0, rowblk(b, gi, c))),
        pl.BlockSpec((SSD_CONV, GROUP_WIDTH), lambda gi, b, c: (0, gi)),
        pl.BlockSpec((SSD_CONV, SSD_STATE), lambda gi, b, c: (0, bc0 + gi)),
        pl.BlockSpec((SSD_CONV, SSD_STATE), lambda gi, b, c: (0, bc0 + g + gi)),
        pl.BlockSpec((1, GROUP_WIDTH), lambda gi, b, c: (0, gi)),
        pl.BlockSpec((1, SSD_STATE), lambda gi, b, c: (0, bc0 + gi)),
        pl.BlockSpec((1, SSD_STATE), lambda gi, b, c: (0, bc0 + g + gi)),
        pl.BlockSpec((1, SUBLANES, hpg), lambda gi, b, c: (gi, 0, 0)),
        pl.BlockSpec((1, hpg, SUBLANES), lambda gi, b, c: (gi, 0, 0)),
        pl.BlockSpec((1, GROUP_WIDTH), lambda gi, b, c: (0, gi)),
    ]
    return pl.pallas_call(
        _ssd_kernel,
        grid=grid,
        in_specs=in_specs,
        out_specs=pl.BlockSpec((lb, GROUP_WIDTH), lambda gi, b, c: (rowblk(b, gi, c), gi)),
        out_shape=jax.ShapeDtypeStruct((n, SSD_INNER), BF16),
        scratch_shapes=[pltpu.VMEM((HEADS_PER_GROUP // 2, LANES, SSD_STATE), F32),
                        pltpu.VMEM((L + SUBLANES, GROUP_WIDTH), F32), pltpu.VMEM((L + SUBLANES, SSD_STATE), F32),
                        pltpu.VMEM((L + SUBLANES, SSD_STATE), F32)],
        compiler_params=_params("parallel", "parallel", "arbitrary"),
        name="ssd_branch",
    )(zx, zx, zx, zx, dt_g, dtt_g, cw, cw, cw, cbias, cbias, cbias, hp, hpt, norm_w.reshape(1, -1))


def _nsa_prep_kernel(q_ref, kc_ref, vc_ref, ks_ref, vs_ref, kw_ref, vw_ref, cos_ref, sin_ref, qw_ref, kw3_ref,
                     qo_ref, kco_ref, vco_ref, kso_ref, vso_ref, kwo_ref, vwo_ref):
    cos, sin = cos_ref[...], sin_ref[...]

    def norm_rope(x, w):
        y = _rms(x, w)
        return y * cos + pltpu.roll(y, HEAD_DIM // 2, 1) * sin

    qw = qw_ref[...]
    tt = q_ref.shape[0]
    for g in range(NSA_KV_HEADS):
        sl = slice(g * HEAD_DIM, (g + 1) * HEAD_DIM)
        for r in range(NSA_REP):
            h = g * NSA_REP + r
            qh = norm_rope(q_ref[:, h * HEAD_DIM:(h + 1) * HEAD_DIM], qw) * Q_PRESCALE
            qo_ref[0, g, r] = qh.astype(qo_ref.dtype)
        kco_ref[0, g] = norm_rope(kc_ref[:, sl], kw3_ref[0:1, :])
        vco_ref[0, g] = vc_ref[:, sl]
        kso_ref[0, g] = norm_rope(ks_ref[:, sl], kw3_ref[1:2, :]).astype(kso_ref.dtype)
        kwo_ref[0, g] = norm_rope(kw_ref[:, sl], kw3_ref[2:3, :]).astype(kwo_ref.dtype)
        vst = vs_ref[:, sl].T.astype(vso_ref.dtype)
        for j in range(tt // SEL_KEY_TILE):
            vso_ref[0, g, j] = vst[:, j * SEL_KEY_TILE:(j + 1) * SEL_KEY_TILE]
        vwt = vw_ref[:, sl].T.astype(vwo_ref.dtype)
        for j in range(tt // WIN_TILE):
            vwo_ref[0, g, j] = vwt[:, j * WIN_TILE:(j + 1) * WIN_TILE]


def nsa_prep(qkv, q_norm_w, k_norm_w, batch, seq, tt=2 * SEL_KEY_TILE):
    half = HEAD_DIM // 2
    inv = ROPE_THETA ** (-jnp.arange(half, dtype=F32) / half)
    ang = jnp.arange(seq).astype(F32)[:, None] * inv[None, :]
    cos = jnp.concatenate([jnp.cos(ang), jnp.cos(ang)], axis=-1)
    sin = jnp.concatenate([-jnp.sin(ang), jnp.sin(ang)], axis=-1)
    nt = seq // tt
    kvb = NSA_WIDTH // NSA_KV_WIDTH
    kv_spec = lambda j: pl.BlockSpec((tt, NSA_KV_WIDTH), lambda b, t: (b * nt + t, kvb + j))
    head_spec = pl.BlockSpec((1, NSA_KV_HEADS, tt, HEAD_DIM), lambda b, t: (b, 0, t, 0))
    head_shape = lambda dt: jax.ShapeDtypeStruct((batch, NSA_KV_HEADS, seq, HEAD_DIM), dt)
    return pl.pallas_call(
        _nsa_prep_kernel,
        grid=(batch, nt),
        in_specs=[pl.BlockSpec((tt, NSA_WIDTH), lambda b, t: (b * nt + t, 0))] + [kv_spec(j) for j in range(6)] + [
            pl.BlockSpec((tt, HEAD_DIM), lambda b, t: (t, 0)), pl.BlockSpec((tt, HEAD_DIM), lambda b, t: (t, 0)),
            pl.BlockSpec((1, HEAD_DIM), lambda b, t: (0, 0)), pl.BlockSpec((3, HEAD_DIM), lambda b, t: (0, 0))],
        out_specs=[pl.BlockSpec((1, NSA_KV_HEADS, NSA_REP, tt, HEAD_DIM), lambda b, t: (b, 0, 0, t, 0)),
                   head_spec, head_spec, head_spec,
                   pl.BlockSpec((1, NSA_KV_HEADS, tt // SEL_KEY_TILE, HEAD_DIM, SEL_KEY_TILE),
                                lambda b, t: (b, 0, t, 0, 0)),
                   head_spec,
                   pl.BlockSpec((1, NSA_KV_HEADS, tt // WIN_TILE, HEAD_DIM, WIN_TILE), lambda b, t: (b, 0, t, 0, 0))],
        out_shape=[jax.ShapeDtypeStruct((batch, NSA_KV_HEADS, NSA_REP, seq, HEAD_DIM), BF16),
                   head_shape(F32), head_shape(F32), head_shape(BF16),
                   jax.ShapeDtypeStruct((batch, NSA_KV_HEADS, seq // SEL_KEY_TILE, HEAD_DIM, SEL_KEY_TILE), BF16),
                   head_shape(BF16),
                   jax.ShapeDtypeStruct((batch, NSA_KV_HEADS, seq // WIN_TILE, HEAD_DIM, WIN_TILE), BF16)],
        compiler_params=_params("parallel", "parallel"),
        name="nsa_prep",
    )(qkv, qkv, qkv, qkv, qkv, qkv, qkv, cos, sin, q_norm_w.reshape(1, -1), k_norm_w)


def _compress_kernel(x_ref, pe_ref, w1_ref, w2_ref, o_ref, *, ncmp, transpose_out):
    nseg = x_ref.shape[2] // CMP_STRIDE
    u = jnp.zeros((nseg, CMP_HID), F32)
    v = jnp.zeros((nseg, CMP_HID), F32)
    for j in range(CMP_STRIDE):
        xj = x_ref[0, 0, pl.ds(j, nseg, stride=CMP_STRIDE), :]
        u = u + _dot((xj + pe_ref[j:j + 1, :]).astype(BF16), w1_ref[j])
        v = v + _dot((xj + pe_ref[CMP_STRIDE + j:CMP_STRIDE + j + 1, :]).astype(BF16), w1_ref[CMP_STRIDE + j])
    hid = _silu(u + pltpu.roll(v, nseg - 1, 0))
    comp = _dot(hid.astype(BF16), w2_ref[...])
    rowi = lax.broadcasted_iota(jnp.int32, comp.shape, 0)
    comp = jnp.where(rowi < ncmp, comp, 0.0)
    o_ref[0, 0] = (comp.T if transpose_out else comp).astype(o_ref.dtype)


def compress(raw, pe, w1, w2, transpose_out):
    b, g, t, hd = raw.shape
    nseg = t // CMP_STRIDE
    full = lambda shape: pl.BlockSpec(shape, lambda i, j: (0,) * len(shape))
    out_dims = (hd, nseg) if transpose_out else (nseg, hd)
    return pl.pallas_call(
        functools.partial(_compress_kernel, ncmp=nseg - 1, transpose_out=transpose_out),
        grid=(b, g),
        in_specs=[pl.BlockSpec((1, 1, t, hd), lambda i, j: (i, j, 0, 0)), full((CMP_BLK, hd)),
                  full((CMP_BLK, hd, CMP_HID)), full((CMP_HID, hd))],
        out_specs=pl.BlockSpec((1, 1) + out_dims, lambda i, j: (i, j, 0, 0)),
        out_shape=jax.ShapeDtypeStruct((b, g) + out_dims, BF16),
        compiler_params=_params("parallel", "parallel"),
        name="nsa_compress",
    )(raw, pe, w1.astype(BF16), w2.astype(BF16))


def _nsa_attn_kernel(q_ref, kc_ref, vct_ref, ks_ref, vst_ref, kw_ref, vwt_ref, glt_ref, ovt_ref, o_ref,
                     selb_ref, acc_ref, ocw_ref, s_ref, p_ref, *, seq, n_sel):
    Q = q_ref.shape[3]
    q0 = pl.program_id(2) * Q
    rows = NSA_REP * Q
    q = q_ref[0, 0].reshape(rows, HEAD_DIM)
    tq = q0 + lax.broadcasted_iota(jnp.int32, (1, Q), 1)
    head = lambda r: slice(r * Q, (r + 1) * Q)

    ncp = kc_ref.shape[2]
    wk = WINDOW + Q
    kt_sz = vst_ref.shape[4]
    start = pl.multiple_of(jnp.clip(q0 - WINDOW, 0, seq - wk), WIN_TILE)
    s_cw = _dot_nt(jnp.concatenate([kc_ref[0, 0], kw_ref[0, 0, pl.ds(start, wk), :], ks_ref[0, 0, 0:kt_sz, :]],
                                   axis=0), q)
    s_c, s_w = s_cw[:ncp], s_cw[ncp:ncp + wk]
    s_ref[0] = s_cw[ncp + wk:]

    cend = lax.broadcasted_iota(jnp.int32, (ncp, 1), 0) * CMP_STRIDE + (CMP_BLK - 1)
    m_c = cend <= tq
    ps = []
    for r in range(NSA_REP):
        s = jnp.where(m_c, s_c[:, head(r)], NEG_INF)
        e = jnp.exp2(s - jnp.max(s, axis=0, keepdims=True))
        p = jnp.where(m_c, e * (1.0 / jnp.sum(e, axis=0, keepdims=True)), 0.0)
        ps.append(p.astype(BF16))
    p_c = jnp.concatenate(ps, axis=1)
    nsb = ovt_ref.shape[0]
    oc_imp = _dot(jnp.concatenate([vct_ref[0, 0], ovt_ref[...]], axis=0), p_c)
    o_c, imp4 = oc_imp[:HEAD_DIM], oc_imp[HEAD_DIM:]

    imp = imp4[:, head(0)]
    for r in range(1, NSA_REP):
        imp = imp + imp4[:, head(r)]
    blk_t = (q0 + lax.broadcasted_iota(jnp.int32, (nsb, Q), 1)) >> SEL_SHIFT
    sb = lax.broadcasted_iota(jnp.int32, (nsb, Q), 0)
    forced = (sb == 0) | (sb == blk_t) | (sb == blk_t - 1)
    imp = jnp.where(sb <= blk_t, jnp.where(forced, FORCE_SCORE, imp), -jnp.inf)
    rank = jnp.zeros((nsb, Q), F32)
    for i in range(nsb):
        ri = imp[i:i + 1, :]
        beats = (ri > imp) | ((ri == imp) & (sb > i))
        rank = rank + jnp.where(beats, 1.0, 0.0)
    selb_ref[...] = jnp.where(rank < n_sel, 0.0, NEG_INF)

    dist = tq - (start + lax.broadcasted_iota(jnp.int32, (wk, 1), 0))
    bias_w = jnp.where((dist >= 0) & (dist < WINDOW), 0.0, NEG_INF)
    ps, ls = [], []
    for r in range(NSA_REP):
        s = s_w[:, head(r)] + bias_w
        p = jnp.exp2(s - jnp.max(s, axis=0, keepdims=True))
        ls.append(jnp.sum(p, axis=0, keepdims=True))
        ps.append(p.astype(BF16))
    t0 = start // WIN_TILE
    vw_t = jnp.concatenate([vwt_ref[0, 0, t0 + j] for j in range(wk // WIN_TILE)], axis=1)
    o_w = _dot(vw_t, jnp.concatenate(ps, axis=1)) * (1.0 / jnp.concatenate(ls, axis=1))

    gate = _sigmoid(glt_ref[0, 0])
    for r in range(NSA_REP):
        ocw_ref[:, head(r)] = (gate[r:r + 1, :] * o_c[:, head(r)]
                               + gate[2 * NSA_REP + r:2 * NSA_REP + r + 1, :] * o_w[:, head(r)])

    blocks_per_tile = kt_sz // SEL_BLK
    n_all = seq // kt_sz
    acc_ref[...] = jnp.zeros_like(acc_ref)
    p_ref[1] = jnp.zeros(p_ref.shape[1:], p_ref.dtype)

    def sel_tile(kt, carry):
        m, l, alpha_prev = carry
        slot = kt & 1
        s_all = s_ref[slot]
        p_prev = p_ref[1 - slot]
        k_next = pl.multiple_of(jnp.minimum(kt + 1, n_all - 1) * kt_sz, kt_sz)
        s_ref[1 - slot] = _dot_nt(ks_ref[0, 0, pl.ds(k_next, kt_sz), :], q)
        pv_prev = _dot(vst_ref[0, 0, jnp.maximum(kt - 1, 0)], p_prev)
        acc_ref[...] = acc_ref[...] * alpha_prev + pv_prev
        k0 = kt * kt_sz
        kpos = k0 + lax.broadcasted_iota(jnp.int32, (kt_sz, 1), 0)
        bias = jnp.concatenate(
            [jnp.broadcast_to(selb_ref[pl.ds(kt * blocks_per_tile + i, 1), :], (SEL_BLK, Q))
             for i in range(blocks_per_tile)], axis=0)
        bias = jnp.where(kpos <= tq, bias, NEG_INF)
        ps, ms, ls, alphas = [], [], [], []
        for r in range(NSA_REP):
            s = s_all[:, head(r)] + bias
            m_old = m[:, head(r)]
            m_new = jnp.maximum(m_old, jnp.max(s, axis=0, keepdims=True))
            p = jnp.exp2(s - m_new)
            alpha = jnp.exp2(m_old - m_new)
            ls.append(alpha * l[:, head(r)] + jnp.sum(p, axis=0, keepdims=True))
            ms.append(m_new)
            alphas.append(alpha)
            ps.append(p.astype(BF16))
        p_ref[slot] = jnp.concatenate(ps, axis=1)
        return jnp.concatenate(ms, axis=1), jnp.concatenate(ls, axis=1), jnp.concatenate(alphas, axis=1)

    n_tiles = (q0 + Q + kt_sz - 1) // kt_sz
    init = (jnp.full((1, rows), NEG_INF, F32), jnp.zeros((1, rows), F32), jnp.ones((1, rows), F32))
    _, l_s, alpha_last = lax.fori_loop(0, n_tiles, sel_tile, init)
    pv_last = _dot(vst_ref[0, 0, n_tiles - 1], p_ref[(n_tiles - 1) & 1])
    o_s = (acc_ref[...] * alpha_last + pv_last) * (1.0 / l_s)

    for r in range(NSA_REP):
        out_t = ocw_ref[:, head(r)] + gate[NSA_REP + r:NSA_REP + r + 1, :] * o_s[:, head(r)]
        o_ref[:, r * HEAD_DIM:(r + 1) * HEAD_DIM] = out_t.T.astype(o_ref.dtype)


def nsa_attention(qh, kcc, vcct, ksn, vst, kwn, vwt, gate_logits_t, batch, seq):
    Q_BLK = NSA_Q_BLK
    assert seq % SEL_KEY_TILE == 0 and seq % Q_BLK == 0 and seq >= WINDOW + Q_BLK
    nq = seq // Q_BLK
    nsb = seq // SEL_BLK
    n_sel = min(N_SEL, nsb)
    ncp = kcc.shape[2]
    ncmp = (seq - CMP_BLK) // CMP_STRIDE + 1
    ci = np.arange(ncp)[None, :]
    sj = np.arange(nsb)[:, None]
    ovt = ((ci * CMP_STRIDE < (sj + 1) * SEL_BLK) & (ci * CMP_STRIDE + CMP_BLK > sj * SEL_BLK) & (ci < ncmp))
    ovt = jnp.asarray(ovt, BF16)
    seq_spec = pl.BlockSpec((1, 1, seq, HEAD_DIM), lambda b, g, i: (b, g, 0, 0))
    cmp_spec = pl.BlockSpec((1, 1, ncp, HEAD_DIM), lambda b, g, i: (b, g, 0, 0))
    tiled = lambda a: pl.BlockSpec((1, 1) + a.shape[2:], lambda b, g, i: (b, g, 0, 0, 0))
    return pl.pallas_call(
        functools.partial(_nsa_attn_kernel, seq=seq, n_sel=n_sel),
        grid=(batch, NSA_KV_HEADS, nq),
        in_specs=[pl.BlockSpec((1, 1, NSA_REP, Q_BLK, HEAD_DIM), lambda b, g, i: (b, g, 0, i, 0)),
                  cmp_spec, pl.BlockSpec((1, 1, HEAD_DIM, ncp), lambda b, g, i: (b, g, 0, 0)),
                  seq_spec, tiled(vst), seq_spec, tiled(vwt),
                  pl.BlockSpec((1, 1, 3 * NSA_REP, Q_BLK), lambda b, g, i: (b, g, 0, i)),
                  pl.BlockSpec((nsb, ncp), lambda b, g, i: (0, 0))],
        out_specs=pl.BlockSpec((Q_BLK, NSA_REP * HEAD_DIM), lambda b, g, i: (b * nq + i, g)),
        out_shape=jax.ShapeDtypeStruct((batch * seq, NSA_WIDTH), BF16),
        scratch_shapes=[pltpu.VMEM((nsb, Q_BLK), F32), pltpu.VMEM((HEAD_DIM, NSA_REP * Q_BLK), F32),
                        pltpu.VMEM((HEAD_DIM, NSA_REP * Q_BLK), F32),
                        pltpu.VMEM((2, SEL_KEY_TILE, NSA_REP * Q_BLK), F32),
                        pltpu.VMEM((2, SEL_KEY_TILE, NSA_REP * Q_BLK), BF16)],
        compiler_params=_params("parallel", "parallel", "arbitrary"),
        name="nsa_attention",
    )(qh, kcc, vcct, ksn, vst, kwn, vwt, gate_logits_t, ovt)


def _merge_kernel(ys_ref, yn_ref, ws_ref, wn_ref, gs_ref, gn_ref, o_ref):
    up_s = _dot(ys_ref[...], ws_ref[...])
    up_n = _dot(yn_ref[...], wn_ref[...])
    o_ref[...] = (_sigmoid(gs_ref[...]) * up_s + _sigmoid(gn_ref[...]) * up_n).astype(o_ref.dtype)


def merge_mixers(y_ssd, y_nsa, w_up_ssd, w_up_nsa, gates, tm=1024, tn=512):
    m = y_ssd.shape[0]
    nb = D_MODEL // tn
    return pl.pallas_call(
        _merge_kernel,
        grid=(m // tm, nb),
        in_specs=[pl.BlockSpec((tm, SSD_INNER), lambda i, j: (i, 0)), pl.BlockSpec((tm, NSA_WIDTH), lambda i, j: (i, 0)),
                  pl.BlockSpec((SSD_INNER, tn), lambda i, j: (0, j)), pl.BlockSpec((NSA_WIDTH, tn), lambda i, j: (0, j)),
                  pl.BlockSpec((tm, tn), lambda i, j: (i, j)), pl.BlockSpec((tm, tn), lambda i, j: (i, nb + j))],
        out_specs=pl.BlockSpec((tm, tn), lambda i, j: (i, j)),
        out_shape=jax.ShapeDtypeStruct((m, D_MODEL), BF16),
        compiler_params=_params("parallel", "arbitrary"),
        name="merge_mixers",
    )(y_ssd, y_nsa, w_up_ssd, w_up_nsa, gates, gates)


def _residual_matmul_kernel(h_ref, x_ref, w_ref, o_ref):
    o_ref[...] = h_ref[...] + _dot(x_ref[...], w_ref[...])


def residual_matmul(h, x, w, tm=1024, tn=1024):
    m, k = x.shape
    n = w.shape[1]
    return pl.pallas_call(
        _residual_matmul_kernel,
        grid=(m // tm, n // tn),
        in_specs=[pl.BlockSpec((tm, tn), lambda i, j: (i, j)), pl.BlockSpec((tm, k), lambda i, j: (i, 0)),
                  pl.BlockSpec((k, tn), lambda i, j: (0, j))],
        out_specs=pl.BlockSpec((tm, tn), lambda i, j: (i, j)),
        out_shape=jax.ShapeDtypeStruct((m, n), F32),
        compiler_params=_params("parallel", "arbitrary"),
        name="residual_matmul",
    )(h, x, w)


def _xattn_router_kernel(h_ref, kv_ref, n2_ref, wq_ref, qn_ref, kn_ref, wo_ref, n3_ref, rw_ref, rb_ref,
                         h2_ref, hf_ref, rt_ref):
    h = h_ref[...]
    qp = _dot(_rms(h, n2_ref[...]).astype(BF16), wq_ref[...])
    kv = kv_ref[...]
    scale = HEAD_DIM ** -0.5
    heads = []
    for hd in range(X_HEADS):
        sl = slice(hd * HEAD_DIM, (hd + 1) * HEAD_DIM)
        qh = _rms(qp[:, sl], qn_ref[...]).astype(BF16)
        kh = _rms(kv[:, sl], kn_ref[...]).astype(BF16)
        vh = kv[:, X_WIDTH + hd * HEAD_DIM:X_WIDTH + (hd + 1) * HEAD_DIM].astype(BF16)
        s = _dot_nt(qh, kh) * scale
        e = jnp.exp(s - jnp.max(s, axis=-1, keepdims=True))
        p = e / jnp.sum(e, axis=-1, keepdims=True)
        heads.append(_dot(p.astype(BF16), vh))
    o = jnp.concatenate(heads, axis=1).astype(BF16)
    h2 = h + _dot(o, wo_ref[...])
    h2_ref[...] = h2
    hf = _rms(h2, n3_ref[...]).astype(BF16)
    hf_ref[...] = hf
    lg = _dot(hf, rw_ref[...]) + rb_ref[...]

    lane = lax.broadcasted_iota(jnp.int32, lg.shape, 1)
    rmax = lambda v: jnp.max(v, axis=-1, keepdims=True)
    rsum = lambda v: jnp.sum(v, axis=-1, keepdims=True)
    first_lane = lambda hit: jnp.min(jnp.where(hit, lane, LANES), axis=-1, keepdims=True)
    is_g = lane < N_GROUPS
    eg = jnp.where(is_g, jnp.exp(lg - rmax(jnp.where(is_g, lg, -jnp.inf))), 0.0)
    pg = eg / rsum(eg)
    pg_top = rmax(pg)
    grp = first_lane(is_g & (pg == pg_top))
    in_grp = (lane >= N_GROUPS) & (((lane - N_GROUPS) >> EPG_SHIFT) == grp)
    ee = jnp.where(in_grp, jnp.exp(lg - rmax(jnp.where(in_grp, lg, -jnp.inf))), 0.0)
    pe = jnp.where(in_grp, ee / rsum(ee), -1.0)
    p1 = rmax(pe)
    i1 = first_lane(pe == p1)
    pe2 = jnp.where(lane == i1, -1.0, pe)
    p2 = rmax(pe2)
    i2 = first_lane(pe2 == p2)
    psum = p1 + p2
    route = jnp.where(lane == 0, (i1 - N_GROUPS).astype(F32),
                      jnp.where(lane == 1, (i2 - N_GROUPS).astype(F32),
                                jnp.where(lane == 2, pg_top * p1 / psum,
                                          jnp.where(lane == 3, pg_top * p2 / psum, 0.0))))
    rt_ref[...] = route


def xattn_router(h1, kv, norm2_w, wq, q_norm_w, k_norm_w, wo, norm3_w, router_w, router_b, batch, seq, tm=512):
    n = batch * seq
    tm = min(tm, seq)
    per_seq = seq // tm
    mlen = kv.shape[0] // batch
    full = lambda shape: pl.BlockSpec(shape, lambda i: (0,) * len(shape))
    return pl.pallas_call(
        _xattn_router_kernel,
        grid=(n // tm,),
        in_specs=[pl.BlockSpec((tm, D_MODEL), lambda i: (i, 0)),
                  pl.BlockSpec((mlen, 2 * X_WIDTH), lambda i: (i // per_seq, 0)),
                  full((1, D_MODEL)), full((D_MODEL, X_WIDTH)), full((1, HEAD_DIM)), full((1, HEAD_DIM)),
                  full((X_WIDTH, D_MODEL)), full((1, D_MODEL)), full((D_MODEL, LANES)), full((1, LANES))],
        out_specs=[pl.BlockSpec((tm, D_MODEL), lambda i: (i, 0)), pl.BlockSpec((tm, D_MODEL), lambda i: (i, 0)),
                   pl.BlockSpec((tm, LANES), lambda i: (i, 0))],
        out_shape=[jax.ShapeDtypeStruct((n, D_MODEL), F32), jax.ShapeDtypeStruct((n, D_MODEL), BF16),
                   jax.ShapeDtypeStruct((n, LANES), F32)],
        compiler_params=_params("parallel"),
        name="xattn_router",
    )(h1, kv, norm2_w.reshape(1, -1), wq, q_norm_w.reshape(1, -1), k_norm_w.reshape(1, -1), wo,
      norm3_w.reshape(1, -1), router_w, router_b)


def _moe_rank_kernel(e_ref, rank_ref, cnt_ref, carry_ref):
    @pl.when(pl.program_id(0) == 0)
    def _():
        carry_ref[...] = jnp.zeros_like(carry_ref)

    t = e_ref.shape[0]
    hit = e_ref[...] == lax.broadcasted_iota(jnp.int32, (t, LANES), 1)
    onehot = jnp.where(hit, 1.0, 0.0).astype(BF16)
    earlier = lax.broadcasted_iota(jnp.int32, (t, t), 0) > lax.broadcasted_iota(jnp.int32, (t, t), 1)
    before = _dot(jnp.where(earlier, 1.0, 0.0).astype(BF16), onehot) + carry_ref[...]
    rank_ref[...] = jnp.sum(jnp.where(hit, before, 0.0), axis=-1, keepdims=True).astype(jnp.int32)
    carry_ref[...] += jnp.sum(jnp.where(hit, 1.0, 0.0), axis=0, keepdims=True)
    cnt_ref[...] = carry_ref[...]


def moe_rank(eid, t=1024):
    a = eid.shape[0]
    return pl.pallas_call(
        _moe_rank_kernel,
        grid=(a // t,),
        in_specs=[pl.BlockSpec((t, 1), lambda i: (i, 0))],
        out_specs=[pl.BlockSpec((t, 1), lambda i: (i, 0)), pl.BlockSpec((1, LANES), lambda i: (0, 0))],
        out_shape=[jax.ShapeDtypeStruct((a, 1), jnp.int32), jax.ShapeDtypeStruct((1, LANES), F32)],
        scratch_shapes=[pltpu.VMEM((1, LANES), F32)],
        compiler_params=_params("arbitrary"),
        name="moe_rank",
    )(eid)


def _moe_ffn_kernel(item_e_ref, item_row_ref, item_nsub_ref, item_nout_ref, x_hbm, wg_ref, wu_ref, wd_ref, y_hbm,
                    x_vmem, y_vmem, sem_x, sem_y, *, hid):
    i, c = pl.program_id(0), pl.program_id(1)
    last_c = pl.num_programs(1) - 1
    th = wg_ref.shape[2]
    nsub, nout = item_nsub_ref[i], item_nout_ref[i]
    row0 = pl.multiple_of(item_row_ref[i], MOE_SUB)
    sub = lambda s: pl.ds(s * MOE_SUB, MOE_SUB)

    def x_copy(item, s):
        row = pl.multiple_of(item_row_ref[item], MOE_SUB)
        return pltpu.make_async_copy(x_hbm.at[pl.ds(row + s * MOE_SUB, MOE_SUB)], x_vmem.at[item & 1, sub(s)],
                                     sem_x.at[item & 1, s])

    def y_copy(row, s):
        return pltpu.make_async_copy(y_vmem.at[sub(s)], y_hbm.at[pl.ds(row + s * MOE_SUB, MOE_SUB)], sem_y.at[s])

    def for_subs(count, fn):
        for s in range(MOE_ITEM_SUBS):
            pl.when(s < count)(functools.partial(fn, s))

    @pl.when(c == 0)
    def _():
        @pl.when(i > 0)
        def _():
            prev_row = pl.multiple_of(item_row_ref[i - 1], MOE_SUB)
            for_subs(item_nout_ref[i - 1], lambda s: y_copy(prev_row, s).wait())

        @pl.when(i == 0)
        def _():
            for_subs(nsub, lambda s: x_copy(i, s).start())

        for_subs(nsub, lambda s: x_copy(i, s).wait())

        @pl.when(i + 1 < pl.num_programs(0))
        def _():
            nxt = jnp.minimum(i + 1, pl.num_programs(0) - 1)
            for_subs(item_nsub_ref[nxt], lambda s: x_copy(nxt, s).start())

        @pl.when(nsub == 0)
        def _():
            y_vmem[...] = jnp.zeros_like(y_vmem)
            for_subs(nout, lambda s: y_copy(row0, s).start())

    def ffn(count):
        rows = pl.ds(0, count * MOE_SUB)
        col_ok = c * th + lax.broadcasted_iota(jnp.int32, (1, th), 1) < hid
        row_ok = c * th + lax.broadcasted_iota(jnp.int32, (th, 1), 0) < hid
        wgu = jnp.concatenate([wg_ref[0], wu_ref[0]], axis=1).astype(BF16)
        wd = jnp.where(row_ok, wd_ref[0], 0.0).astype(BF16)
        gu = _dot(x_vmem[i & 1, rows, :], wgu)
        act = jnp.where(col_ok, _silu(gu[:, :th]) * gu[:, th:], 0.0)
        part = _dot(act.astype(BF16), wd)

        @pl.when(c == 0)
        def _():
            y_vmem[rows, :] = part

        @pl.when(c > 0)
        def _():
            y_vmem[rows, :] += part

    for count in range(1, MOE_ITEM_SUBS + 1):
        pl.when(nsub == count)(functools.partial(ffn, count))

    @pl.when((nsub > 0) & (c == last_c))
    def _():
        for_subs(nsub, lambda s: y_copy(row0, s).start())

    @pl.when((i == pl.num_programs(0) - 1) & (c == last_c))
    def _():
        for_subs(nout, lambda s: y_copy(row0, s).wait())


def moe_ffn(x_buf, item_e, item_row, item_nsub, item_nout, w_gate, w_up, w_down):
    rows, d = x_buf.shape
    hid = w_gate.shape[2]
    th = MOE_HID_TILE
    nc = pl.cdiv(hid, th)
    tile = lambda i, c, ns: jnp.where(ns[i] > 0, c, nc - 1)
    grid_spec = pltpu.PrefetchScalarGridSpec(
        num_scalar_prefetch=4,
        grid=(item_e.shape[0], nc),
        in_specs=[pl.BlockSpec(memory_space=pl.ANY),
                  pl.BlockSpec((1, d, th), lambda i, c, ie, ir, ns, no: (ie[i], 0, tile(i, c, ns))),
                  pl.BlockSpec((1, d, th), lambda i, c, ie, ir, ns, no: (ie[i], 0, tile(i, c, ns))),
                  pl.BlockSpec((1, th, d), lambda i, c, ie, ir, ns, no: (ie[i], tile(i, c, ns), 0))],
        out_specs=pl.BlockSpec(memory_space=pl.ANY),
        scratch_shapes=[pltpu.VMEM((2, MOE_ITEM_SUBS * MOE_SUB, d), BF16),
                        pltpu.VMEM((MOE_ITEM_SUBS * MOE_SUB, d), F32),
                        pltpu.SemaphoreType.DMA((2, MOE_ITEM_SUBS)), pltpu.SemaphoreType.DMA((MOE_ITEM_SUBS,))],
    )
    return pl.pallas_call(
        functools.partial(_moe_ffn_kernel, hid=hid),
        grid_spec=grid_spec,
        out_shape=jax.ShapeDtypeStruct((rows, d), F32),
        compiler_params=_params("arbitrary", "arbitrary"),
        name="moe_ffn",
    )(item_e, item_row, item_nsub, item_nout, x_buf, w_gate, w_up, w_down)


def hier_moe(h2, hf, route, w_gate, w_up, w_down):
    n, d = h2.shape
    n_exp = w_gate.shape[0]
    i32 = jnp.int32
    eid = route[:, 0:TOP_K].astype(i32).reshape(-1)
    wts = route[:, TOP_K:2 * TOP_K]
    n_assign = n * TOP_K
    rank, counts = moe_rank(eid.reshape(n_assign, 1))
    counts = counts[0, :n_exp].astype(i32)
    subs_e = (counts + MOE_SUB - 1) // MOE_SUB
    sub_end = jnp.cumsum(subs_e)
    row_start = (sub_end - subs_e) * MOE_SUB
    is_e = eid[:, None] == jnp.arange(n_exp, dtype=i32)[None, :]
    dest = rank[:, 0] + jnp.sum(jnp.where(is_e, row_start[None, :], 0), axis=1)
    n_subs = (n_assign + n_exp * (MOE_SUB - 1) + MOE_SUB - 1) // MOE_SUB
    rows = n_subs * MOE_SUB
    tok = jnp.repeat(jnp.arange(n, dtype=i32), TOP_K)
    tok_buf = (jnp.arange(rows, dtype=i32) % n).at[dest].set(tok)
    per = MOE_ITEM_SUBS
    n_items = (n_exp * (per - 1) + n_subs + per - 1) // per
    items_e = (subs_e + per - 1) // per
    item_end = jnp.cumsum(items_e)
    n_used = item_end[-1]
    idx = jnp.arange(n_items, dtype=i32)
    used = idx < n_used
    e_of = jnp.minimum(jnp.searchsorted(item_end, idx, side='right'), n_exp - 1).astype(i32)
    j = idx - (item_end - items_e)[e_of]
    item_nsub = jnp.where(used, jnp.clip(subs_e[e_of] - per * j, 0, per), 0)
    fill_first = jnp.minimum(sub_end[-1] + per * (idx - n_used), n_subs)
    item_fill = jnp.where(used, 0, jnp.minimum(n_subs - fill_first, per))
    item_row = jnp.where(used, row_start[e_of] + j * per * MOE_SUB, jnp.minimum(fill_first, n_subs - 1) * MOE_SUB)
    item_e = jnp.where(used, e_of, e_of[jnp.maximum(n_used - 1, 0)])
    y_buf = moe_ffn(hf[tok_buf], item_e.astype(i32), item_row.astype(i32), item_nsub.astype(i32),
                    (item_nsub + item_fill).astype(i32), w_gate, w_up, w_down)
    slot = dest.reshape(n, TOP_K)
    return h2 + wts[:, 0:1] * y_buf[slot[:, 0]] + wts[:, 1:2] * y_buf[slot[:, 1]]


COL_DT = ZX_COLS
COL_QKV = COL_DT + SSD_HEADS
COL_NSA_GATE = COL_QKV + QKV_COLS
COL_MERGE_GATE = COL_NSA_GATE + 3 * NSA_HEADS


def _small_w_in(w_in):
    pad = jnp.zeros(w_in.shape[:2] + (SMALL_COLS - SSD_HEADS - 3 * NSA_HEADS,), w_in.dtype)
    return jnp.concatenate([w_in[:, :, COL_DT:COL_QKV], w_in[:, :, COL_NSA_GATE:COL_MERGE_GATE], pad], axis=2)


def _layer(x, mem, norm1_w, w_in, ssd_conv_w, ssd_conv_b, ssd_dt_bias, ssd_a_log, ssd_d, ssd_norm_w,
           nsa_q_norm_w, nsa_k_norm_w, cmp_pe_k, cmp_w1_k, cmp_w2_k, cmp_pe_v, cmp_w1_v, cmp_w2_v,
           w_up_ssd, w_up_nsa, w_out, norm2_w, mem_norm_w, xq_w, xkv_w, x_q_norm_w, x_k_norm_w, xo_w,
           norm3_w, router_g_w, router_g_b, router_e_w, router_e_b, moe_w_gate, moe_w_up, moe_w_down):
    batch, seq, d = x.shape
    n = batch * seq
    xf = x.reshape(n, d)
    hn = rmsnorm_rows(xf, norm1_w)
    w_cols = lambda c0, nc: w_in[0, :, c0:c0 + nc].astype(BF16)
    zx = matmul(hn, w_cols(0, ZX_COLS), tm=IN_PROJ_TM, tn=IN_PROJ_TN)
    qkv = matmul(hn, w_cols(COL_QKV, QKV_COLS), tm=IN_PROJ_TM, tn=IN_PROJ_TN)
    mgates = matmul(hn, w_cols(COL_MERGE_GATE, GATE_COLS), tm=IN_PROJ_TM, tn=IN_PROJ_TN)
    small = matmul(hn, _small_w_in(w_in)[0].astype(BF16), tm=IN_PROJ_TM)
    y_ssd = ssd_branch(zx, small, ssd_conv_w, ssd_conv_b, ssd_dt_bias, ssd_a_log, ssd_d, ssd_norm_w, batch, seq)
    qh, kcn, vcr, ksn, vsn, kwn, vwn = nsa_prep(qkv, nsa_q_norm_w, nsa_k_norm_w, batch, seq)
    kcc = compress(kcn, cmp_pe_k, cmp_w1_k, cmp_w2_k, transpose_out=False)
    vcct = compress(vcr, cmp_pe_v, cmp_w1_v, cmp_w2_v, transpose_out=True)
    gl = small[:, SSD_HEADS:SSD_HEADS + 3 * NSA_HEADS].reshape(batch, seq, 3, NSA_KV_HEADS, NSA_REP)
    gl = gl.transpose(0, 3, 2, 4, 1).reshape(batch, NSA_KV_HEADS, 3 * NSA_REP, seq)
    y_nsa = nsa_attention(qh, kcc, vcct, ksn, vsn, kwn, vwn, gl, batch, seq)
    merged = merge_mixers(y_ssd, y_nsa, w_up_ssd.astype(BF16), w_up_nsa.astype(BF16), mgates)
    h1 = residual_matmul(xf, merged, w_out.astype(BF16))
    mlen = mem.shape[1]
    mn = rmsnorm_rows(mem.reshape(batch * mlen, d), mem_norm_w, tm=min(512, batch * mlen))
    kv = matmul(mn, xkv_w.astype(BF16))
    n_exp = router_e_w.shape[1]
    rpad = LANES - N_GROUPS - n_exp
    router_w = jnp.concatenate([router_g_w, router_e_w, jnp.zeros((d, rpad), F32)], axis=1).astype(BF16)
    router_b = jnp.concatenate([router_g_b, router_e_b, jnp.zeros((rpad,), F32)]).reshape(1, LANES)
    h2, hf, route = xattn_router(h1, kv, norm2_w, xq_w.astype(BF16), x_q_norm_w, x_k_norm_w, xo_w.astype(BF16),
                                 norm3_w, router_w, router_b, batch, seq)
    out = hier_moe(h2, hf, route, moe_w_gate, moe_w_up, moe_w_down)
    return out.reshape(batch, seq, d)


def kernel(x, mem, norm1_w, w_in, ssd_conv_w, ssd_conv_b, ssd_dt_bias, ssd_a_log, ssd_d, ssd_norm_w, nsa_q_norm_w, nsa_k_norm_w, cmp_pe_k, cmp_w1_k, cmp_w2_k, cmp_pe_v, cmp_w1_v, cmp_w2_v, w_up_ssd, w_up_nsa, w_out, norm2_w, mem_norm_w, xq_w, xkv_w, x_q_norm_w, x_k_norm_w, xo_w, norm3_w, router_g_w, router_g_b, router_e_w, router_e_b, moe_w_gate, moe_w_up, moe_w_down):
    h = x
    for l in range(norm1_w.shape[0]):
        h = _layer(h, mem, norm1_w[l], w_in[l:l + 1], ssd_conv_w[l], ssd_conv_b[l], ssd_dt_bias[l], ssd_a_log[l], ssd_d[l],
                   ssd_norm_w[l], nsa_q_norm_w[l], nsa_k_norm_w[l], cmp_pe_k[l], cmp_w1_k[l], cmp_w2_k[l], cmp_pe_v[l],
                   cmp_w1_v[l], cmp_w2_v[l], w_up_ssd[l], w_up_nsa[l], w_out[l], norm2_w[l], mem_norm_w[l], xq_w[l],
                   xkv_w[l], x_q_norm_w[l], x_k_norm_w[l], xo_w[l], norm3_w[l], router_g_w[l], router_g_b[l],
                   router_e_w[l], router_e_b[l], moe_w_gate[l], moe_w_up[l], moe_w_down[l])
    return h.astype(x.dtype)
```

```python
import functools

import numpy as np
import jax
import jax.numpy as jnp
from jax import lax
from jax.experimental import pallas as pl
from jax.experimental.pallas import tpu as pltpu

F32 = jnp.float32
BF16 = jnp.bfloat16

D_MODEL = 2048
SSD_INNER = 4096
SSD_HEAD_DIM = 64
SSD_HEADS = 64
SSD_GROUPS = 8
SSD_STATE = 128
SSD_CONV = 4
SSD_CHUNK = 256
SSD_CHUNKS_PER_STEP = 2
HEADS_PER_GROUP = SSD_HEADS // SSD_GROUPS
GROUP_WIDTH = SSD_INNER // SSD_GROUPS
HEAD_DIM = 128
NSA_HEADS = 16
NSA_KV_HEADS = 4
NSA_REP = NSA_HEADS // NSA_KV_HEADS
NSA_WIDTH = NSA_HEADS * HEAD_DIM
NSA_KV_WIDTH = NSA_KV_HEADS * HEAD_DIM
CMP_BLK = 32
CMP_STRIDE = 16
CMP_HID = 256
SEL_BLK = 64
SEL_SHIFT = 6
N_SEL = 16
WINDOW = 512
NSA_Q_BLK = 256
WIN_TILE = 128
SEL_KEY_TILE = 256
FORCE_SCORE = 1.0e4
X_HEADS = 4
X_WIDTH = X_HEADS * HEAD_DIM
N_GROUPS = 8
EXPERTS_PER_GROUP = 8
EPG_SHIFT = 3
N_EXPERTS = 64
TOP_K = 2
EXPERT_HIDDEN = 1408
MOE_SUB = 64
MOE_ITEM_SUBS = 16
MOE_HID_TILE = 256
ROPE_THETA = 10000.0
EPS = 1e-6
NEG_INF = -1e30
LOG2E = float(np.log2(np.e))
Q_PRESCALE = HEAD_DIM ** -0.5 * LOG2E
SUBLANES = 8
LANES = 128
VMEM_LIMIT = 56 * 1024 * 1024

ZX_COLS = SSD_INNER + SSD_INNER + 2 * SSD_GROUPS * SSD_STATE
QKV_COLS = NSA_WIDTH + 6 * NSA_KV_WIDTH
GATE_COLS = 2 * D_MODEL
SMALL_COLS = LANES
IN_PROJ_TM, IN_PROJ_TN = 2048, 1024


def _params(*sem):
    return pltpu.CompilerParams(dimension_semantics=sem, vmem_limit_bytes=VMEM_LIMIT)


def _rms(x, w):
    ms = jnp.mean(x * x, axis=-1, keepdims=True)
    return x * lax.rsqrt(ms + EPS) * w


def _sigmoid(x):
    return 0.5 + 0.5 * jnp.tanh(0.5 * x)


def _silu(x):
    h = 0.5 * x
    return h + h * jnp.tanh(h)


def _softplus(x):
    return jnp.maximum(x, 0.0) + jnp.log1p(jnp.exp(-jnp.abs(x)))


def _dot(a, b):
    return jnp.dot(a, b, preferred_element_type=F32)


def _dot_nt(a, b):
    return lax.dot_general(a, b, (((1,), (1,)), ((), ())), preferred_element_type=F32)


def _rmsnorm_kernel(x_ref, w_ref, o_ref):
    o_ref[...] = _rms(x_ref[...], w_ref[...]).astype(o_ref.dtype)


def rmsnorm_rows(x, w, tm=512):
    m, d = x.shape
    return pl.pallas_call(
        _rmsnorm_kernel,
        grid=(m // tm,),
        in_specs=[pl.BlockSpec((tm, d), lambda i: (i, 0)), pl.BlockSpec((1, d), lambda i: (0, 0))],
        out_specs=pl.BlockSpec((tm, d), lambda i: (i, 0)),
        out_shape=jax.ShapeDtypeStruct((m, d), BF16),
        compiler_params=_params("parallel"),
        name="rmsnorm_rows",
    )(x, w.reshape(1, d))


def _matmul_kernel(x_ref, w_ref, o_ref):
    o_ref[...] = _dot(x_ref[...], w_ref[...]).astype(o_ref.dtype)


def matmul(x, w, out_dtype=F32, tm=512, tn=512):
    m, k = x.shape
    n = w.shape[1]
    tm, tn = min(tm, m), min(tn, n)
    return pl.pallas_call(
        _matmul_kernel,
        grid=(m // tm, n // tn),
        in_specs=[pl.BlockSpec((tm, k), lambda i, j: (i, 0)), pl.BlockSpec((k, tn), lambda i, j: (0, j))],
        out_specs=pl.BlockSpec((tm, tn), lambda i, j: (i, j)),
        out_shape=jax.ShapeDtypeStruct((m, n), out_dtype),
        compiler_params=_params("parallel", "arbitrary"),
        name="matmul",
    )(x, w)


def _split3(a):
    h = a.astype(BF16)
    r = a - h.astype(F32)
    m = r.astype(BF16)
    return h, m, (r - m.astype(F32)).astype(BF16)


def _ssd_kernel(z_ref, x_ref, b_ref, c_ref, dt_ref, dtt_ref, *rest):
    @pl.when(pl.program_id(2) == 0)
    def _():
        for ref in rest[-4:]:
            ref[...] = jnp.zeros_like(ref)

    L = x_ref.shape[0] // SSD_CHUNKS_PER_STEP
    for cc in range(SSD_CHUNKS_PER_STEP):
        rows = pl.ds(cc * L, L)
        _ssd_chunk(z_ref.at[rows], x_ref.at[rows], b_ref.at[rows], c_ref.at[rows], dt_ref.at[:, rows],
                   dtt_ref.at[:, :, rows], *rest[:9], rest[9].at[rows], *rest[10:])


def _ssd_chunk(z_ref, x_ref, b_ref, c_ref, dt_ref, dtt_ref, cwx_ref, cwb_ref, cwc_ref, cbx_ref, cbb_ref, cbc_ref,
               hp_ref, hpt_ref, nw_ref, y_ref, state_ref, ext_x_ref, ext_b_ref, ext_c_ref):
    L = x_ref.shape[0]
    H = L // 2

    def conv_silu(u_ref, ext_ref, w, b):
        ext_ref[0:SUBLANES, :] = ext_ref[L:L + SUBLANES, :]
        u = u_ref[...]
        ext_ref[SUBLANES:SUBLANES + L, :] = u
        acc = b + w[SSD_CONV - 1:SSD_CONV, :] * u
        for k in range(SSD_CONV - 1):
            acc = acc + w[k:k + 1, :] * ext_ref[pl.ds(SUBLANES - (SSD_CONV - 1) + k, L), :]
        return _silu(acc)

    xs = conv_silu(x_ref, ext_x_ref, cwx_ref[...], cbx_ref[...])
    bm = conv_silu(b_ref, ext_b_ref, cwb_ref[...], cbb_ref[...])
    cm = conv_silu(c_ref, ext_c_ref, cwc_ref[...], cbc_ref[...])
    bm16, cm16 = bm.astype(BF16), cm.astype(BF16)

    hp = hp_ref[0]
    hpt = hpt_ref[0]
    dt = _softplus(dt_ref[0] + hp[0:1, :])
    dtt = _softplus(dtt_ref[0] + hpt[:, 0:1])
    da = dt * (-jnp.exp(hp[1:2, :]))
    dat = dtt * (-jnp.exp(hpt[:, 1:2]))
    row = lax.broadcasted_iota(jnp.int32, (L, L), 0)
    col = lax.broadcasted_iota(jnp.int32, (L, L), 1)
    tril = jnp.where(row >= col, 1.0, 0.0).astype(BF16)
    triu = jnp.where(row <= col, 1.0, 0.0).astype(BF16)
    acum = sum(_dot(tril, part) for part in _split3(da)) * LOG2E
    acumt = sum(_dot(part, triu) for part in _split3(dat)) * LOG2E
    dtet = jnp.exp2(acumt[:, L - 1:L] - acumt)
    eact = jnp.exp2(acumt)
    cdec = jnp.exp2(acum[L - 1:L, :])

    cbt = _dot_nt(bm16, cm16)
    cbt00, cbt01, cbt11 = cbt[:H, :H], cbt[:H, H:], cbt[H:, H:]
    tri = lax.broadcasted_iota(jnp.int32, (H, H), 0) <= lax.broadcasted_iota(jnp.int32, (H, H), 1)
    xst = xs.T
    d_skip = hp[2:3, :]
    P = SSD_HEAD_DIM
    outs = []
    for p in range(HEADS_PER_GROUP // 2):
        hprev = state_ref[p]
        yoff = _dot_nt(hprev.astype(BF16), cm16)
        new_state = []
        for hh in range(2):
            h = 2 * p + hh
            rs = slice(hh * P, (hh + 1) * P)
            xt = xst[h * P:(h + 1) * P, :]
            xdt = xt * dtt[h:h + 1, :]
            xdt16 = xdt.astype(BF16)
            a_col, a_row = acum[:, h:h + 1], acumt[h:h + 1, :]
            d00 = jnp.exp2(jnp.where(tri, a_row[:, :H] - a_col[:H], -jnp.inf))
            d01 = jnp.exp2(a_row[:, H:] - a_col[:H])
            d11 = jnp.exp2(jnp.where(tri, a_row[:, H:] - a_col[H:], -jnp.inf))
            y_l = _dot(xdt16[:, :H], (cbt00 * d00).astype(BF16))
            y_r = _dot(xdt16[:, :H], (cbt01 * d01).astype(BF16)) + _dot(xdt16[:, H:], (cbt11 * d11).astype(BF16))
            y = jnp.concatenate([y_l, y_r], axis=1)
            outs.append(y + yoff[rs] * eact[h:h + 1, :] + d_skip[:, h:h + 1] * xt)
            new_state.append(hprev[rs] * cdec[:, h:h + 1] + _dot((xdt * dtet[h:h + 1, :]).astype(BF16), bm16))
        state_ref[p] = jnp.concatenate(new_state, axis=0)
    y = jnp.concatenate(outs, axis=0).T * _silu(z_ref[...])
    y_ref[...] = _rms(y, nw_ref[...]).astype(y_ref.dtype)


def ssd_branch(zx, small, conv_w, conv_b, dt_bias, a_log, d_skip, norm_w, batch, seq):
    n = batch * seq
    L = min(SSD_CHUNK, seq)
    nc = seq // L
    g, hpg = SSD_GROUPS, HEADS_PER_GROUP
    dt_raw = small[:, :SSD_HEADS].reshape(n, g, hpg)
    dt_g = dt_raw.transpose(1, 0, 2)
    dtt_g = dt_raw.transpose(1, 2, 0)
    hp = jnp.stack([dt_bias, a_log, d_skip], axis=0).reshape(3, g, hpg).transpose(1, 0, 2)
    hp = jnp.pad(hp, ((0, 0), (0, SUBLANES - 3), (0, 0)))
    hpt = hp.transpose(0, 2, 1)
    cw = conv_w
    cbias = conv_b.reshape(1, -1)
    nxb = SSD_INNER // GROUP_WIDTH
    assert nc % SSD_CHUNKS_PER_STEP == 0
    lb, nb = L * SSD_CHUNKS_PER_STEP, nc // SSD_CHUNKS_PER_STEP
    rowblk = lambda b, gi, c: b * nb + c
    bc0 = SSD_INNER // SSD_STATE
    grid = (g, batch, nb)
    in_specs = [
        pl.BlockSpec((lb, GROUP_WIDTH), lambda gi, b, c: (rowblk(b, gi, c), gi)),
        pl.BlockSpec((lb, GROUP_WIDTH), lambda gi, b, c: (rowblk(b, gi, c), nxb + gi)),
        pl.BlockSpec((lb, SSD_STATE), lambda gi, b, c: (rowblk(b, gi, c), 2 * bc0 + gi)),
        pl.BlockSpec((lb, SSD_STATE), lambda gi, b, c: (rowblk(b, gi, c), 2 * bc0 + g + gi)),
        pl.BlockSpec((1, lb, hpg), lambda gi, b, c: (gi, rowblk(b, gi, c), 0)),
        pl.BlockSpec((1, hpg, lb), lambda gi, b, c: (gi, 0, rowblk(b, gi, c))),
        pl.BlockSpec((SSD_CONV, GROUP_WIDTH), lambda gi, b, c: (0, gi)),
        pl.BlockSpec((SSD_CONV, SSD_STATE), lambda gi, b, c: (0, bc0 + gi)),
        pl.BlockSpec((SSD_CONV, SSD_STATE), lambda gi, b, c: (0, bc0 + g + gi)),
        pl.BlockSpec((1, GROUP_WIDTH), lambda gi, b, c: (0, gi)),
        pl.BlockSpec((1, SSD_STATE), lambda gi, b, c: (0, bc0 + gi)),
        pl.BlockSpec((1, SSD_STATE), lambda gi, b, c: (0, bc0 + g + gi)),
        pl.BlockSpec((1, SUBLANES, hpg), lambda gi, b, c: (gi, 0, 0)),
        pl.BlockSpec((1, hpg, SUBLANES), lambda gi, b, c: (gi, 0, 0)),
        pl.BlockSpec((1, GROUP_WIDTH), lambda gi, b, c: (0, gi)),
    ]
    return pl.pallas_call(
        _ssd_kernel,
        grid=grid,
        in_specs=in_specs,
        out_specs=pl.BlockSpec((lb, GROUP_WIDTH), lambda gi, b, c: (rowblk(b, gi, c), gi)),
        out_shape=jax.ShapeDtypeStruct((n, SSD_INNER), BF16),
        scratch_shapes=[pltpu.VMEM((HEADS_PER_GROUP // 2, LANES, SSD_STATE), F32),
                        pltpu.VMEM((L + SUBLANES, GROUP_WIDTH), F32), pltpu.VMEM((L + SUBLANES, SSD_STATE), F32),
                        pltpu.VMEM((L + SUBLANES, SSD_STATE), F32)],
        compiler_params=_params("parallel", "parallel", "arbitrary"),
        name="ssd_branch",
    )(zx, zx, zx, zx, dt_g, dtt_g, cw, cw, cw, cbias, cbias, cbias, hp, hpt, norm_w.reshape(1, -1))


def _nsa_prep_kernel(q_ref, kc_ref, vc_ref, ks_ref, vs_ref, kw_ref, vw_ref, cos_ref, sin_ref, qw_ref, kw3_ref,
                     qo_ref, kco_ref, vco_ref, kso_ref, vso_ref, kwo_ref, vwo_ref):
    cos, sin = cos_ref[...], sin_ref[...]

    def norm_rope(x, w):
        y = _rms(x, w)
        return y * cos + pltpu.roll(y, HEAD_DIM // 2, 1) * sin

    qw = qw_ref[...]
    tt = q_ref.shape[0]
    for g in range(NSA_KV_HEADS):
        sl = slice(g * HEAD_DIM, (g + 1) * HEAD_DIM)
        for r in range(NSA_REP):
            h = g * NSA_REP + r
            qh = norm_rope(q_ref[:, h * HEAD_DIM:(h + 1) * HEAD_DIM], qw) * Q_PRESCALE
            qo_ref[0, g, r] = qh.astype(qo_ref.dtype)
        kco_ref[0, g] = norm_rope(kc_ref[:, sl], kw3_ref[0:1, :])
        vco_ref[0, g] = vc_ref[:, sl]
        kso_ref[0, g] = norm_rope(ks_ref[:, sl], kw3_ref[1:2, :]).astype(kso_ref.dtype)
        kwo_ref[0, g] = norm_rope(kw_ref[:, sl], kw3_ref[2:3, :]).astype(kwo_ref.dtype)
        vst = vs_ref[:, sl].T.astype(vso_ref.dtype)
        for j in range(tt // SEL_KEY_TILE):
            vso_ref[0, g, j] = vst[:, j * SEL_KEY_TILE:(j + 1) * SEL_KEY_TILE]
        vwt = vw_ref[:, sl].T.astype(vwo_ref.dtype)
        for j in range(tt // WIN_TILE):
            vwo_ref[0, g, j] = vwt[:, j * WIN_TILE:(j + 1) * WIN_TILE]


def nsa_prep(qkv, q_norm_w, k_norm_w, batch, seq, tt=2 * SEL_KEY_TILE):
    half = HEAD_DIM // 2
    inv = ROPE_THETA ** (-jnp.arange(half, dtype=F32) / half)
    ang = jnp.arange(seq).astype(F32)[:, None] * inv[None, :]
    cos = jnp.concatenate([jnp.cos(ang), jnp.cos(ang)], axis=-1)
    sin = jnp.concatenate([-jnp.sin(ang), jnp.sin(ang)], axis=-1)
    nt = seq // tt
    kvb = NSA_WIDTH // NSA_KV_WIDTH
    kv_spec = lambda j: pl.BlockSpec((tt, NSA_KV_WIDTH), lambda b, t: (b * nt + t, kvb + j))
    head_spec = pl.BlockSpec((1, NSA_KV_HEADS, tt, HEAD_DIM), lambda b, t: (b, 0, t, 0))
    head_shape = lambda dt: jax.ShapeDtypeStruct((batch, NSA_KV_HEADS, seq, HEAD_DIM), dt)
    return pl.pallas_call(
        _nsa_prep_kernel,
        grid=(batch, nt),
        in_specs=[pl.BlockSpec((tt, NSA_WIDTH), lambda b, t: (b * nt + t, 0))] + [kv_spec(j) for j in range(6)] + [
            pl.BlockSpec((tt, HEAD_DIM), lambda b, t: (t, 0)), pl.BlockSpec((tt, HEAD_DIM), lambda b, t: (t, 0)),
            pl.BlockSpec((1, HEAD_DIM), lambda b, t: (0, 0)), pl.BlockSpec((3, HEAD_DIM), lambda b, t: (0, 0))],
        out_specs=[pl.BlockSpec((1, NSA_KV_HEADS, NSA_REP, tt, HEAD_DIM), lambda b, t: (b, 0, 0, t, 0)),
                   head_spec, head_spec, head_spec,
                   pl.BlockSpec((1, NSA_KV_HEADS, tt // SEL_KEY_TILE, HEAD_DIM, SEL_KEY_TILE),
                                lambda b, t: (b, 0, t, 0, 0)),
                   head_spec,
                   pl.BlockSpec((1, NSA_KV_HEADS, tt // WIN_TILE, HEAD_DIM, WIN_TILE), lambda b, t: (b, 0, t, 0, 0))],
        out_shape=[jax.ShapeDtypeStruct((batch, NSA_KV_HEADS, NSA_REP, seq, HEAD_DIM), BF16),
                   head_shape(F32), head_shape(F32), head_shape(BF16),
                   jax.ShapeDtypeStruct((batch, NSA_KV_HEADS, seq // SEL_KEY_TILE, HEAD_DIM, SEL_KEY_TILE), BF16),
                   head_shape(BF16),
                   jax.ShapeDtypeStruct((batch, NSA_KV_HEADS, seq // WIN_TILE, HEAD_DIM, WIN_TILE), BF16)],
        compiler_params=_params("parallel", "parallel"),
        name="nsa_prep",
    )(qkv, qkv, qkv, qkv, qkv, qkv, qkv, cos, sin, q_norm_w.reshape(1, -1), k_norm_w)


def _compress_kernel(x_ref, pe_ref, w1_ref, w2_ref, o_ref, *, ncmp, transpose_out):
    nseg = x_ref.shape[2] // CMP_STRIDE
    u = jnp.zeros((nseg, CMP_HID), F32)
    v = jnp.zeros((nseg, CMP_HID), F32)
    for j in range(CMP_STRIDE):
        xj = x_ref[0, 0, pl.ds(j, nseg, stride=CMP_STRIDE), :]
        u = u + _dot((xj + pe_ref[j:j + 1, :]).astype(BF16), w1_ref[j])
        v = v + _dot((xj + pe_ref[CMP_STRIDE + j:CMP_STRIDE + j + 1, :]).astype(BF16), w1_ref[CMP_STRIDE + j])
    hid = _silu(u + pltpu.roll(v, nseg - 1, 0))
    comp = _dot(hid.astype(BF16), w2_ref[...])
    rowi = lax.broadcasted_iota(jnp.int32, comp.shape, 0)
    comp = jnp.where(rowi < ncmp, comp, 0.0)
    o_ref[0, 0] = (comp.T if transpose_out else comp).astype(o_ref.dtype)


def compress(raw, pe, w1, w2, transpose_out):
    b, g, t, hd = raw.shape
    nseg = t // CMP_STRIDE
    full = lambda shape: pl.BlockSpec(shape, lambda i, j: (0,) * len(shape))
    out_dims = (hd, nseg) if transpose_out else (nseg, hd)
    return pl.pallas_call(
        functools.partial(_compress_kernel, ncmp=nseg - 1, transpose_out=transpose_out),
        grid=(b, g),
        in_specs=[pl.BlockSpec((1, 1, t, hd), lambda i, j: (i, j, 0, 0)), full((CMP_BLK, hd)),
                  full((CMP_BLK, hd, CMP_HID)), full((CMP_HID, hd))],
        out_specs=pl.BlockSpec((1, 1) + out_dims, lambda i, j: (i, j, 0, 0)),
        out_shape=jax.ShapeDtypeStruct((b, g) + out_dims, BF16),
        compiler_params=_params("parallel", "parallel"),
        name="nsa_compress",
    )(raw, pe, w1.astype(BF16), w2.astype(BF16))


def _nsa_attn_kernel(q_ref, kc_ref, vct_ref, ks_ref, vst_ref, kw_ref, vwt_ref, glt_ref, ovt_ref, o_ref,
                     selb_ref, acc_ref, ocw_ref, s_ref, p_ref, *, seq, n_sel):
    Q = q_ref.shape[3]
    q0 = pl.program_id(2) * Q
    rows = NSA_REP * Q
    q = q_ref[0, 0].reshape(rows, HEAD_DIM)
    tq = q0 + lax.broadcasted_iota(jnp.int32, (1, Q), 1)
    head = lambda r: slice(r * Q, (r + 1) * Q)

    ncp = kc_ref.shape[2]
    wk = WINDOW + Q
    kt_sz = vst_ref.shape[4]
    start = pl.multiple_of(jnp.clip(q0 - WINDOW, 0, seq - wk), WIN_TILE)
    s_cw = _dot_nt(jnp.concatenate([kc_ref[0, 0], kw_ref[0, 0, pl.ds(start, wk), :], ks_ref[0, 0, 0:kt_sz, :]],
                                   axis=0), q)
    s_c, s_w = s_cw[:ncp], s_cw[ncp:ncp + wk]
    s_ref[0] = s_cw[ncp + wk:]

    cend = lax.broadcasted_iota(jnp.int32, (ncp, 1), 0) * CMP_STRIDE + (CMP_BLK - 1)
    m_c = cend <= tq
    ps = []
    for r in range(NSA_REP):
        s = jnp.where(m_c, s_c[:, head(r)], NEG_INF)
        e = jnp.exp2(s - jnp.max(s, axis=0, keepdims=True))
        p = jnp.where(m_c, e * (1.0 / jnp.sum(e, axis=0, keepdims=True)), 0.0)
        ps.append(p.astype(BF16))
    p_c = jnp.concatenate(ps, axis=1)
    nsb = ovt_ref.shape[0]
    oc_imp = _dot(jnp.concatenate([vct_ref[0, 0], ovt_ref[...]], axis=0), p_c)
    o_c, imp4 = oc_imp[:HEAD_DIM], oc_imp[HEAD_DIM:]

    imp = imp4[:, head(0)]
    for r in range(1, NSA_REP):
        imp = imp + imp4[:, head(r)]
    blk_t = (q0 + lax.broadcasted_iota(jnp.int32, (nsb, Q), 1)) >> SEL_SHIFT
    sb = lax.broadcasted_iota(jnp.int32, (nsb, Q), 0)
    forced = (sb == 0) | (sb == blk_t) | (sb == blk_t - 1)
    imp = jnp.where(sb <= blk_t, jnp.where(forced, FORCE_SCORE, imp), -jnp.inf)
    rank = jnp.zeros((nsb, Q), F32)
    for i in range(nsb):
        ri = imp[i:i + 1, :]
        beats = (ri > imp) | ((ri == imp) & (sb > i))
        rank = rank + jnp.where(beats, 1.0, 0.0)
    selb_ref[...] = jnp.where(rank < n_sel, 0.0, NEG_INF)

    dist = tq - (start + lax.broadcasted_iota(jnp.int32, (wk, 1), 0))
    bias_w = jnp.where((dist >= 0) & (dist < WINDOW), 0.0, NEG_INF)
    ps, ls = [], []
    for r in range(NSA_REP):
        s = s_w[:, head(r)] + bias_w
        p = jnp.exp2(s - jnp.max(s, axis=0, keepdims=True))
        ls.append(jnp.sum(p, axis=0, keepdims=True))
        ps.append(p.astype(BF16))
    t0 = start // WIN_TILE
    vw_t = jnp.concatenate([vwt_ref[0, 0, t0 + j] for j in range(wk // WIN_TILE)], axis=1)
    o_w = _dot(vw_t, jnp.concatenate(ps, axis=1)) * (1.0 / jnp.concatenate(ls, axis=1))

    gate = _sigmoid(glt_ref[0, 0])
    for r in range(NSA_REP):
        ocw_ref[:, head(r)] = (gate[r:r + 1, :] * o_c[:, head(r)]
                               + gate[2 * NSA_REP + r:2 * NSA_REP + r + 1, :] * o_w[:, head(r)])

    blocks_per_tile = kt_sz // SEL_BLK
    n_all = seq // kt_sz
    acc_ref[...] = jnp.zeros_like(acc_ref)
    p_ref[1] = jnp.zeros(p_ref.shape[1:], p_ref.dtype)

    def sel_tile(kt, carry):
        m, l, alpha_prev = carry
        slot = kt & 1
        s_all = s_ref[slot]
        p_prev = p_ref[1 - slot]
        k_next = pl.multiple_of(jnp.minimum(kt + 1, n_all - 1) * kt_sz, kt_sz)
        s_ref[1 - slot] = _dot_nt(ks_ref[0, 0, pl.ds(k_next, kt_sz), :], q)
        pv_prev = _dot(vst_ref[0, 0, jnp.maximum(kt - 1, 0)], p_prev)
        acc_ref[...] = acc_ref[...] * alpha_prev + pv_prev
        k0 = kt * kt_sz
        kpos = k0 + lax.broadcasted_iota(jnp.int32, (kt_sz, 1), 0)
        bias = jnp.concatenate(
            [jnp.broadcast_to(selb_ref[pl.ds(kt * blocks_per_tile + i, 1), :], (SEL_BLK, Q))
             for i in range(blocks_per_tile)], axis=0)
        bias = jnp.where(kpos <= tq, bias, NEG_INF)
        ps, ms, ls, alphas = [], [], [], []
        for r in range(NSA_REP):
            s = s_all[:, head(r)] + bias
            m_old = m[:, head(r)]
            m_new = jnp.maximum(m_old, jnp.max(s, axis=0, keepdims=True))
            p = jnp.exp2(s - m_new)
            alpha = jnp.exp2(m_old - m_new)
            ls.append(alpha * l[:, head(r)] + jnp.sum(p, axis=0, keepdims=True))
            ms.append(m_new)
            alphas.append(alpha)
            ps.append(p.astype(BF16))
        p_ref[slot] = jnp.concatenate(ps, axis=1)
        return jnp.concatenate(ms, axis=1), jnp.concatenate(ls, axis=1), jnp.concatenate(alphas, axis=1)

    n_tiles = (q0 + Q + kt_sz - 1) // kt_sz
    init = (jnp.full((1, rows), NEG_INF, F32), jnp.zeros((1, rows), F32), jnp.ones((1, rows), F32))
    _, l_s, alpha_last = lax.fori_loop(0, n_tiles, sel_tile, init)
    pv_last = _dot(vst_ref[0, 0, n_tiles - 1], p_ref[(n_tiles - 1) & 1])
    o_s = (acc_ref[...] * alpha_last + pv_last) * (1.0 / l_s)

    for r in range(NSA_REP):
        out_t = ocw_ref[:, head(r)] + gate[NSA_REP + r:NSA_REP + r + 1, :] * o_s[:, head(r)]
        o_ref[:, r * HEAD_DIM:(r + 1) * HEAD_DIM] = out_t.T.astype(o_ref.dtype)


def nsa_attention(qh, kcc, vcct, ksn, vst, kwn, vwt, gate_logits_t, batch, seq):
    Q_BLK = NSA_Q_BLK
    assert seq % SEL_KEY_TILE == 0 and seq % Q_BLK == 0 and seq >= WINDOW + Q_BLK
    nq = seq // Q_BLK
    nsb = seq // SEL_BLK
    n_sel = min(N_SEL, nsb)
    ncp = kcc.shape[2]
    ncmp = (seq - CMP_BLK) // CMP_STRIDE + 1
    ci = np.arange(ncp)[None, :]
    sj = np.arange(nsb)[:, None]
    ovt = ((ci * CMP_STRIDE < (sj + 1) * SEL_BLK) & (ci * CMP_STRIDE + CMP_BLK > sj * SEL_BLK) & (ci < ncmp))
    ovt = jnp.asarray(ovt, BF16)
    seq_spec = pl.BlockSpec((1, 1, seq, HEAD_DIM), lambda b, g, i: (b, g, 0, 0))
    cmp_spec = pl.BlockSpec((1, 1, ncp, HEAD_DIM), lambda b, g, i: (b, g, 0, 0))
    tiled = lambda a: pl.BlockSpec((1, 1) + a.shape[2:], lambda b, g, i: (b, g, 0, 0, 0))
    return pl.pallas_call(
        functools.partial(_nsa_attn_kernel, seq=seq, n_sel=n_sel),
        grid=(batch, NSA_KV_HEADS, nq),
        in_specs=[pl.BlockSpec((1, 1, NSA_REP, Q_BLK, HEAD_DIM), lambda b, g, i: (b, g, 0, i, 0)),
                  cmp_spec, pl.BlockSpec((1, 1, HEAD_DIM, ncp), lambda b, g, i: (b, g, 0, 0)),
                  seq_spec, tiled(vst), seq_spec, tiled(vwt),
                  pl.BlockSpec((1, 1, 3 * NSA_REP, Q_BLK), lambda b, g, i: (b, g, 0, i)),
                  pl.BlockSpec((nsb, ncp), lambda b, g, i: (0, 0))],
        out_specs=pl.BlockSpec((Q_BLK, NSA_REP * HEAD_DIM), lambda b, g, i: (b * nq + i, g)),
        out_shape=jax.ShapeDtypeStruct((batch * seq, NSA_WIDTH), BF16),
        scratch_shapes=[pltpu.VMEM((nsb, Q_BLK), F32), pltpu.VMEM((HEAD_DIM, NSA_REP * Q_BLK), F32),
                        pltpu.VMEM((HEAD_DIM, NSA_REP * Q_BLK), F32),
                        pltpu.VMEM((2, SEL_KEY_TILE, NSA_REP * Q_BLK), F32),
                        pltpu.VMEM((2, SEL_KEY_TILE, NSA_REP * Q_BLK), BF16)],
        compiler_params=_params("parallel", "parallel", "arbitrary"),
        name="nsa_attention",
    )(qh, kcc, vcct, ksn, vst, kwn, vwt, gate_logits_t, ovt)


def _merge_kernel(ys_ref, yn_ref, ws_ref, wn_ref, gs_ref, gn_ref, o_ref):
    up_s = _dot(ys_ref[...], ws_ref[...])
    up_n = _dot(yn_ref[...], wn_ref[...])
    o_ref[...] = (_sigmoid(gs_ref[...]) * up_s + _sigmoid(gn_ref[...]) * up_n).astype(o_ref.dtype)


def merge_mixers(y_ssd, y_nsa, w_up_ssd, w_up_nsa, gates, tm=1024, tn=512):
    m = y_ssd.shape[0]
    nb = D_MODEL // tn
    return pl.pallas_call(
        _merge_kernel,
        grid=(m // tm, nb),
        in_specs=[pl.BlockSpec((tm, SSD_INNER), lambda i, j: (i, 0)), pl.BlockSpec((tm, NSA_WIDTH), lambda i, j: (i, 0)),
                  pl.BlockSpec((SSD_INNER, tn), lambda i, j: (0, j)), pl.BlockSpec((NSA_WIDTH, tn), lambda i, j: (0, j)),
                  pl.BlockSpec((tm, tn), lambda i, j: (i, j)), pl.BlockSpec((tm, tn), lambda i, j: (i, nb + j))],
        out_specs=pl.BlockSpec((tm, tn), lambda i, j: (i, j)),
        out_shape=jax.ShapeDtypeStruct((m, D_MODEL), BF16),
        compiler_params=_params("parallel", "arbitrary"),
        name="merge_mixers",
    )(y_ssd, y_nsa, w_up_ssd, w_up_nsa, gates, gates)


def _residual_matmul_kernel(h_ref, x_ref, w_ref, o_ref):
    o_ref[...] = h_ref[...] + _dot(x_ref[...], w_ref[...])


def residual_matmul(h, x, w, tm=1024, tn=1024):
    m, k = x.shape
    n = w.shape[1]
    return pl.pallas_call(
        _residual_matmul_kernel,
        grid=(m // tm, n // tn),
        in_specs=[pl.BlockSpec((tm, tn), lambda i, j: (i, j)), pl.BlockSpec((tm, k), lambda i, j: (i, 0)),
                  pl.BlockSpec((k, tn), lambda i, j: (0, j))],
        out_specs=pl.BlockSpec((tm, tn), lambda i, j: (i, j)),
        out_shape=jax.ShapeDtypeStruct((m, n), F32),
        compiler_params=_params("parallel", "arbitrary"),
        name="residual_matmul",
    )(h, x, w)


def _xattn_router_kernel(h_ref, kv_ref, n2_ref, wq_ref, qn_ref, kn_ref, wo_ref, n3_ref, rw_ref, rb_ref,
                         h2_ref, hf_ref, rt_ref):
    h = h_ref[...]
    qp = _dot(_rms(h, n2_ref[...]).astype(BF16), wq_ref[...])
    kv = kv_ref[...]
    scale = HEAD_DIM ** -0.5
    heads = []
    for hd in range(X_HEADS):
        sl = slice(hd * HEAD_DIM, (hd + 1) * HEAD_DIM)
        qh = _rms(qp[:, sl], qn_ref[...]).astype(BF16)
        kh = _rms(kv[:, sl], kn_ref[...]).astype(BF16)
        vh = kv[:, X_WIDTH + hd * HEAD_DIM:X_WIDTH + (hd + 1) * HEAD_DIM].astype(BF16)
        s = _dot_nt(qh, kh) * scale
        e = jnp.exp(s - jnp.max(s, axis=-1, keepdims=True))
        p = e / jnp.sum(e, axis=-1, keepdims=True)
        heads.append(_dot(p.astype(BF16), vh))
    o = jnp.concatenate(heads, axis=1).astype(BF16)
    h2 = h + _dot(o, wo_ref[...])
    h2_ref[...] = h2
    hf = _rms(h2, n3_ref[...]).astype(BF16)
    hf_ref[...] = hf
    lg = _dot(hf, rw_ref[...]) + rb_ref[...]

    lane = lax.broadcasted_iota(jnp.int32, lg.shape, 1)
    rmax = lambda v: jnp.max(v, axis=-1, keepdims=True)
    rsum = lambda v: jnp.sum(v, axis=-1, keepdims=True)
    first_lane = lambda hit: jnp.min(jnp.where(hit, lane, LANES), axis=-1, keepdims=True)
    is_g = lane < N_GROUPS
    eg = jnp.where(is_g, jnp.exp(lg - rmax(jnp.where(is_g, lg, -jnp.inf))), 0.0)
    pg = eg / rsum(eg)
    pg_top = rmax(pg)
    grp = first_lane(is_g & (pg == pg_top))
    in_grp = (lane >= N_GROUPS) & (((lane - N_GROUPS) >> EPG_SHIFT) == grp)
    ee = jnp.where(in_grp, jnp.exp(lg - rmax(jnp.where(in_grp, lg, -jnp.inf))), 0.0)
    pe = jnp.where(in_grp, ee / rsum(ee), -1.0)
    p1 = rmax(pe)
    i1 = first_lane(pe == p1)
    pe2 = jnp.where(lane == i1, -1.0, pe)
    p2 = rmax(pe2)
    i2 = first_lane(pe2 == p2)
    psum = p1 + p2
    route = jnp.where(lane == 0, (i1 - N_GROUPS).astype(F32),
                      jnp.where(lane == 1, (i2 - N_GROUPS).astype(F32),
                                jnp.where(lane == 2, pg_top * p1 / psum,
                                          jnp.where(lane == 3, pg_top * p2 / psum, 0.0))))
    rt_ref[...] = route


def xattn_router(h1, kv, norm2_w, wq, q_norm_w, k_norm_w, wo, norm3_w, router_w, router_b, batch, seq, tm=512):
    n = batch * seq
    tm = min(tm, seq)
    per_seq = seq // tm
    mlen = kv.shape[0] // batch
    full = lambda shape: pl.BlockSpec(shape, lambda i: (0,) * len(shape))
    return pl.pallas_call(
        _xattn_router_kernel,
        grid=(n // tm,),
        in_specs=[pl.BlockSpec((tm, D_MODEL), lambda i: (i, 0)),
                  pl.BlockSpec((mlen, 2 * X_WIDTH), lambda i: (i // per_seq, 0)),
                  full((1, D_MODEL)), full((D_MODEL, X_WIDTH)), full((1, HEAD_DIM)), full((1, HEAD_DIM)),
                  full((X_WIDTH, D_MODEL)), full((1, D_MODEL)), full((D_MODEL, LANES)), full((1, LANES))],
        out_specs=[pl.BlockSpec((tm, D_MODEL), lambda i: (i, 0)), pl.BlockSpec((tm, D_MODEL), lambda i: (i, 0)),
                   pl.BlockSpec((tm, LANES), lambda i: (i, 0))],
        out_shape=[jax.ShapeDtypeStruct((n, D_MODEL), F32), jax.ShapeDtypeStruct((n, D_MODEL), BF16),
                   jax.ShapeDtypeStruct((n, LANES), F32)],
        compiler_params=_params("parallel"),
        name="xattn_router",
    )(h1, kv, norm2_w.reshape(1, -1), wq, q_norm_w.reshape(1, -1), k_norm_w.reshape(1, -1), wo,
      norm3_w.reshape(1, -1), router_w, router_b)


def _moe_rank_kernel(e_ref, rank_ref, cnt_ref, carry_ref):
    @pl.when(pl.program_id(0) == 0)
    def _():
        carry_ref[...] = jnp.zeros_like(carry_ref)

    t = e_ref.shape[0]
    hit = e_ref[...] == lax.broadcasted_iota(jnp.int32, (t, LANES), 1)
    onehot = jnp.where(hit, 1.0, 0.0).astype(BF16)
    earlier = lax.broadcasted_iota(jnp.int32, (t, t), 0) > lax.broadcasted_iota(jnp.int32, (t, t), 1)
    before = _dot(jnp.where(earlier, 1.0, 0.0).astype(BF16), onehot) + carry_ref[...]
    rank_ref[...] = jnp.sum(jnp.where(hit, before, 0.0), axis=-1, keepdims=True).astype(jnp.int32)
    carry_ref[...] += jnp.sum(jnp.where(hit, 1.0, 0.0), axis=0, keepdims=True)
    cnt_ref[...] = carry_ref[...]


def moe_rank(eid, t=1024):
    a = eid.shape[0]
    return pl.pallas_call(
        _moe_rank_kernel,
        grid=(a // t,),
        in_specs=[pl.BlockSpec((t, 1), lambda i: (i, 0))],
        out_specs=[pl.BlockSpec((t, 1), lambda i: (i, 0)), pl.BlockSpec((1, LANES), lambda i: (0, 0))],
        out_shape=[jax.ShapeDtypeStruct((a, 1), jnp.int32), jax.ShapeDtypeStruct((1, LANES), F32)],
        scratch_shapes=[pltpu.VMEM((1, LANES), F32)],
        compiler_params=_params("arbitrary"),
        name="moe_rank",
    )(eid)


def _moe_ffn_kernel(item_e_ref, item_row_ref, item_nsub_ref, item_nout_ref, x_hbm, wg_ref, wu_ref, wd_ref, y_hbm,
                    x_vmem, y_vmem, sem_x, sem_y, *, hid):
    i, c = pl.program_id(0), pl.program_id(1)
    last_c = pl.num_programs(1) - 1
    th = wg_ref.shape[2]
    nsub, nout = item_nsub_ref[i], item_nout_ref[i]
    row0 = pl.multiple_of(item_row_ref[i], MOE_SUB)
    sub = lambda s: pl.ds(s * MOE_SUB, MOE_SUB)

    def x_copy(item, s):
        row = pl.multiple_of(item_row_ref[item], MOE_SUB)
        return pltpu.make_async_copy(x_hbm.at[pl.ds(row + s * MOE_SUB, MOE_SUB)], x_vmem.at[item & 1, sub(s)],
                                     sem_x.at[item & 1, s])

    def y_copy(row, s):
        return pltpu.make_async_copy(y_vmem.at[sub(s)], y_hbm.at[pl.ds(row + s * MOE_SUB, MOE_SUB)], sem_y.at[s])

    def for_subs(count, fn):
        for s in range(MOE_ITEM_SUBS):
            pl.when(s < count)(functools.partial(fn, s))

    @pl.when(c == 0)
    def _():
        @pl.when(i > 0)
        def _():
            prev_row = pl.multiple_of(item_row_ref[i - 1], MOE_SUB)
            for_subs(item_nout_ref[i - 1], lambda s: y_copy(prev_row, s).wait())

        @pl.when(i == 0)
        def _():
            for_subs(nsub, lambda s: x_copy(i, s).start())

        for_subs(nsub, lambda s: x_copy(i, s).wait())

        @pl.when(i + 1 < pl.num_programs(0))
        def _():
            nxt = jnp.minimum(i + 1, pl.num_programs(0) - 1)
            for_subs(item_nsub_ref[nxt], lambda s: x_copy(nxt, s).start())

        @pl.when(nsub == 0)
        def _():
            y_vmem[...] = jnp.zeros_like(y_vmem)
            for_subs(nout, lambda s: y_copy(row0, s).start())

    def ffn(count):
        rows = pl.ds(0, count * MOE_SUB)
        col_ok = c * th + lax.broadcasted_iota(jnp.int32, (1, th), 1) < hid
        row_ok = c * th + lax.broadcasted_iota(jnp.int32, (th, 1), 0) < hid
        wgu = jnp.concatenate([wg_ref[0], wu_ref[0]], axis=1).astype(BF16)
        wd = jnp.where(row_ok, wd_ref[0], 0.0).astype(BF16)
        gu = _dot(x_vmem[i & 1, rows, :], wgu)
        act = jnp.where(col_ok, _silu(gu[:, :th]) * gu[:, th:], 0.0)
        part = _dot(act.astype(BF16), wd)

        @pl.when(c == 0)
        def _():
            y_vmem[rows, :] = part

        @pl.when(c > 0)
        def _():
            y_vmem[rows, :] += part

    for count in range(1, MOE_ITEM_SUBS + 1):
        pl.when(nsub == count)(functools.partial(ffn, count))

    @pl.when((nsub > 0) & (c == last_c))
    def _():
        for_subs(nsub, lambda s: y_copy(row0, s).start())

    @pl.when((i == pl.num_programs(0) - 1) & (c == last_c))
    def _():
        for_subs(nout, lambda s: y_copy(row0, s).wait())


def moe_ffn(x_buf, item_e, item_row, item_nsub, item_nout, w_gate, w_up, w_down):
    rows, d = x_buf.shape
    hid = w_gate.shape[2]
    th = MOE_HID_TILE
    nc = pl.cdiv(hid, th)
    tile = lambda i, c, ns: jnp.where(ns[i] > 0, c, nc - 1)
    grid_spec = pltpu.PrefetchScalarGridSpec(
        num_scalar_prefetch=4,
        grid=(item_e.shape[0], nc),
        in_specs=[pl.BlockSpec(memory_space=pl.ANY),
                  pl.BlockSpec((1, d, th), lambda i, c, ie, ir, ns, no: (ie[i], 0, tile(i, c, ns))),
                  pl.BlockSpec((1, d, th), lambda i, c, ie, ir, ns, no: (ie[i], 0, tile(i, c, ns))),
                  pl.BlockSpec((1, th, d), lambda i, c, ie, ir, ns, no: (ie[i], tile(i, c, ns), 0))],
        out_specs=pl.BlockSpec(memory_space=pl.ANY),
        scratch_shapes=[pltpu.VMEM((2, MOE_ITEM_SUBS * MOE_SUB, d), BF16),
                        pltpu.VMEM((MOE_ITEM_SUBS * MOE_SUB, d), F32),
                        pltpu.SemaphoreType.DMA((2, MOE_ITEM_SUBS)), pltpu.SemaphoreType.DMA((MOE_ITEM_SUBS,))],
    )
    return pl.pallas_call(
        functools.partial(_moe_ffn_kernel, hid=hid),
        grid_spec=grid_spec,
        out_shape=jax.ShapeDtypeStruct((rows, d), F32),
        compiler_params=_params("arbitrary", "arbitrary"),
        name="moe_ffn",
    )(item_e, item_row, item_nsub, item_nout, x_buf, w_gate, w_up, w_down)


def hier_moe(h2, hf, route, w_gate, w_up, w_down):
    n, d = h2.shape
    n_exp = w_gate.shape[0]
    i32 = jnp.int32
    eid = route[:, 0:TOP_K].astype(i32).reshape(-1)
    wts = route[:, TOP_K:2 * TOP_K]
    n_assign = n * TOP_K
    rank, counts = moe_rank(eid.reshape(n_assign, 1))
    counts = counts[0, :n_exp].astype(i32)
    subs_e = (counts + MOE_SUB - 1) // MOE_SUB
    sub_end = jnp.cumsum(subs_e)
    row_start = (sub_end - subs_e) * MOE_SUB
    is_e = eid[:, None] == jnp.arange(n_exp, dtype=i32)[None, :]
    dest = rank[:, 0] + jnp.sum(jnp.where(is_e, row_start[None, :], 0), axis=1)
    n_subs = (n_assign + n_exp * (MOE_SUB - 1) + MOE_SUB - 1) // MOE_SUB
    rows = n_subs * MOE_SUB
    tok = jnp.repeat(jnp.arange(n, dtype=i32), TOP_K)
    tok_buf = (jnp.arange(rows, dtype=i32) % n).at[dest].set(tok)
    per = MOE_ITEM_SUBS
    n_items = (n_exp * (per - 1) + n_subs + per - 1) // per
    items_e = (subs_e + per - 1) // per
    item_end = jnp.cumsum(items_e)
    n_used = item_end[-1]
    idx = jnp.arange(n_items, dtype=i32)
    used = idx < n_used
    e_of = jnp.minimum(jnp.searchsorted(item_end, idx, side='right'), n_exp - 1).astype(i32)
    j = idx - (item_end - items_e)[e_of]
    item_nsub = jnp.where(used, jnp.clip(subs_e[e_of] - per * j, 0, per), 0)
    fill_first = jnp.minimum(sub_end[-1] + per * (idx - n_used), n_subs)
    item_fill = jnp.where(used, 0, jnp.minimum(n_subs - fill_first, per))
    item_row = jnp.where(used, row_start[e_of] + j * per * MOE_SUB, jnp.minimum(fill_first, n_subs - 1) * MOE_SUB)
    item_e = jnp.where(used, e_of, e_of[jnp.maximum(n_used - 1, 0)])
    y_buf = moe_ffn(hf[tok_buf], item_e.astype(i32), item_row.astype(i32), item_nsub.astype(i32),
                    (item_nsub + item_fill).astype(i32), w_gate, w_up, w_down)
    slot = dest.reshape(n, TOP_K)
    return h2 + wts[:, 0:1] * y_buf[slot[:, 0]] + wts[:, 1:2] * y_buf[slot[:, 1]]


COL_DT = ZX_COLS
COL_QKV = COL_DT + SSD_HEADS
COL_NSA_GATE = COL_QKV + QKV_COLS
COL_MERGE_GATE = COL_NSA_GATE + 3 * NSA_HEADS


def _small_w_in(w_in):
    pad = jnp.zeros(w_in.shape[:2] + (SMALL_COLS - SSD_HEADS - 3 * NSA_HEADS,), w_in.dtype)
    return jnp.concatenate([w_in[:, :, COL_DT:COL_QKV], w_in[:, :, COL_NSA_GATE:COL_MERGE_GATE], pad], axis=2)


def _layer(x, mem, norm1_w, w_in, ssd_conv_w, ssd_conv_b, ssd_dt_bias, ssd_a_log, ssd_d, ssd_norm_w,
           nsa_q_norm_w, nsa_k_norm_w, cmp_pe_k, cmp_w1_k, cmp_w2_k, cmp_pe_v, cmp_w1_v, cmp_w2_v,
           w_up_ssd, w_up_nsa, w_out, norm2_w, mem_norm_w, xq_w, xkv_w, x_q_norm_w, x_k_norm_w, xo_w,
           norm3_w, router_g_w, router_g_b, router_e_w, router_e_b, moe_w_gate, moe_w_up, moe_w_down):
    batch, seq, d = x.shape
    n = batch * seq
    xf = x.reshape(n, d)
    hn = rmsnorm_rows(xf, norm1_w)
    w_cols = lambda c0, nc: w_in[0, :, c0:c0 + nc].astype(BF16)
    zx = matmul(hn, w_cols(0, ZX_COLS), tm=IN_PROJ_TM, tn=IN_PROJ_TN)
    qkv = matmul(hn, w_cols(COL_QKV, QKV_COLS), tm=IN_PROJ_TM, tn=IN_PROJ_TN)
    mgates = matmul(hn, w_cols(COL_MERGE_GATE, GATE_COLS), tm=IN_PROJ_TM, tn=IN_PROJ_TN)
    small = matmul(hn, _small_w_in(w_in)[0].astype(BF16), tm=IN_PROJ_TM)
    y_ssd = ssd_branch(zx, small, ssd_conv_w, ssd_conv_b, ssd_dt_bias, ssd_a_log, ssd_d, ssd_norm_w, batch, seq)
    qh, kcn, vcr, ksn, vsn, kwn, vwn = nsa_prep(qkv, nsa_q_norm_w, nsa_k_norm_w, batch, seq)
    kcc = compress(kcn, cmp_pe_k, cmp_w1_k, cmp_w2_k, transpose_out=False)
    vcct = compress(vcr, cmp_pe_v, cmp_w1_v, cmp_w2_v, transpose_out=True)
    gl = small[:, SSD_HEADS:SSD_HEADS + 3 * NSA_HEADS].reshape(batch, seq, 3, NSA_KV_HEADS, NSA_REP)
    gl = gl.transpose(0, 3, 2, 4, 1).reshape(batch, NSA_KV_HEADS, 3 * NSA_REP, seq)
    y_nsa = nsa_attention(qh, kcc, vcct, ksn, vsn, kwn, vwn, gl, batch, seq)
    merged = merge_mixers(y_ssd, y_nsa, w_up_ssd.astype(BF16), w_up_nsa.astype(BF16), mgates)
    h1 = residual_matmul(xf, merged, w_out.astype(BF16))
    mlen = mem.shape[1]
    mn = rmsnorm_rows(mem.reshape(batch * mlen, d), mem_norm_w, tm=min(512, batch * mlen))
    kv = matmul(mn, xkv_w.astype(BF16))
    n_exp = router_e_w.shape[1]
    rpad = LANES - N_GROUPS - n_exp
    router_w = jnp.concatenate([router_g_w, router_e_w, jnp.zeros((d, rpad), F32)], axis=1).astype(BF16)
    router_b = jnp.concatenate([router_g_b, router_e_b, jnp.zeros((rpad,), F32)]).reshape(1, LANES)
    h2, hf, route = xattn_router(h1, kv, norm2_w, xq_w.astype(BF16), x_q_norm_w, x_k_norm_w, xo_w.astype(BF16),
                                 norm3_w, router_w, router_b, batch, seq)
    out = hier_moe(h2, hf, route, moe_w_gate, moe_w_up, moe_w_down)
    return out.reshape(batch, seq, d)


def kernel(x, mem, norm1_w, w_in, ssd_conv_w, ssd_conv_b, ssd_dt_bias, ssd_a_log, ssd_d, ssd_norm_w, nsa_q_norm_w, nsa_k_norm_w, cmp_pe_k, cmp_w1_k, cmp_w2_k, cmp_pe_v, cmp_w1_v, cmp_w2_v, w_up_ssd, w_up_nsa, w_out, norm2_w, mem_norm_w, xq_w, xkv_w, x_q_norm_w, x_k_norm_w, xo_w, norm3_w, router_g_w, router_g_b, router_e_w, router_e_b, moe_w_gate, moe_w_up, moe_w_down):
    h = x
    for l in range(norm1_w.shape[0]):
        h = _layer(h, mem, norm1_w[l], w_in[l:l + 1], ssd_conv_w[l], ssd_conv_b[l], ssd_dt_bias[l], ssd_a_log[l], ssd_d[l],
                   ssd_norm_w[l], nsa_q_norm_w[l], nsa_k_norm_w[l], cmp_pe_k[l], cmp_w1_k[l], cmp_w2_k[l], cmp_pe_v[l],
                   cmp_w1_v[l], cmp_w2_v[l], w_up_ssd[l], w_up_nsa[l], w_out[l], norm2_w[l], mem_norm_w[l], xq_w[l],
                   xkv_w[l], x_q_norm_w[l], x_k_norm_w[l], xo_w[l], norm3_w[l], router_g_w[l], router_g_b[l],
                   router_e_w[l], router_e_b[l], moe_w_gate[l], moe_w_up[l], moe_w_down[l])
    return h.astype(x.dtype)
```

```python
import functools

import numpy as np
import jax
import jax.numpy as jnp
from jax import lax
from jax.experimental import pallas as pl
from jax.experimental.pallas import tpu as pltpu

F32 = jnp.float32
BF16 = jnp.bfloat16

D_MODEL = 2048
SSD_INNER = 4096
SSD_HEAD_DIM = 64
SSD_HEADS = 64
SSD_GROUPS = 8
SSD_STATE = 128
SSD_CONV = 4
SSD_CHUNK = 256
SSD_CHUNKS_PER_STEP = 2
HEADS_PER_GROUP = SSD_HEADS // SSD_GROUPS
GROUP_WIDTH = SSD_INNER // SSD_GROUPS
HEAD_DIM = 128
NSA_HEADS = 16
NSA_KV_HEADS = 4
NSA_REP = NSA_HEADS // NSA_KV_HEADS
NSA_WIDTH = NSA_HEADS * HEAD_DIM
NSA_KV_WIDTH = NSA_KV_HEADS * HEAD_DIM
CMP_BLK = 32
CMP_STRIDE = 16
CMP_HID = 256
SEL_BLK = 64
SEL_SHIFT = 6
N_SEL = 16
WINDOW = 512
NSA_Q_BLK = 256
WIN_TILE = 128
SEL_KEY_TILE = 256
FORCE_SCORE = 1.0e4
X_HEADS = 4
X_WIDTH = X_HEADS * HEAD_DIM
N_GROUPS = 8
EXPERTS_PER_GROUP = 8
EPG_SHIFT = 3
N_EXPERTS = 64
TOP_K = 2
EXPERT_HIDDEN = 1408
MOE_SUB = 128
MOE_ITEM_SUBS = 8
MOE_HID_TILE = 256
ROPE_THETA = 10000.0
EPS = 1e-6
NEG_INF = -1e30
LOG2E = float(np.log2(np.e))
Q_PRESCALE = HEAD_DIM ** -0.5 * LOG2E
SUBLANES = 8
LANES = 128
VMEM_LIMIT = 56 * 1024 * 1024

ZX_COLS = SSD_INNER + SSD_INNER + 2 * SSD_GROUPS * SSD_STATE
QKV_COLS = NSA_WIDTH + 6 * NSA_KV_WIDTH
GATE_COLS = 2 * D_MODEL
SMALL_COLS = LANES
IN_PROJ_TM, IN_PROJ_TN = 2048, 1024


def _params(*sem):
    return pltpu.CompilerParams(dimension_semantics=sem, vmem_limit_bytes=VMEM_LIMIT)


def _rms(x, w):
    ms = jnp.mean(x * x, axis=-1, keepdims=True)
    return x * lax.rsqrt(ms + EPS) * w


def _sigmoid(x):
    return 0.5 + 0.5 * jnp.tanh(0.5 * x)


def _silu(x):
    h = 0.5 * x
    return h + h * jnp.tanh(h)


def _softplus(x):
    return jnp.maximum(x, 0.0) + jnp.log1p(jnp.exp(-jnp.abs(x)))


def _dot(a, b):
    return jnp.dot(a, b, preferred_element_type=F32)


def _dot_nt(a, b):
    return lax.dot_general(a, b, (((1,), (1,)), ((), ())), preferred_element_type=F32)


def _rmsnorm_kernel(x_ref, w_ref, o_ref):
    o_ref[...] = _rms(x_ref[...], w_ref[...]).astype(o_ref.dtype)


def rmsnorm_rows(x, w, tm=512):
    m, d = x.shape
    return pl.pallas_call(
        _rmsnorm_kernel,
        grid=(m // tm,),
        in_specs=[pl.BlockSpec((tm, d), lambda i: (i, 0)), pl.BlockSpec((1, d), lambda i: (0, 0))],
        out_specs=pl.BlockSpec((tm, d), lambda i: (i, 0)),
        out_shape=jax.ShapeDtypeStruct((m, d), BF16),
        compiler_params=_params("parallel"),
        name="rmsnorm_rows",
    )(x, w.reshape(1, d))


def _matmul_kernel(x_ref, w_ref, o_ref):
    o_ref[...] = _dot(x_ref[...], w_ref[...]).astype(o_ref.dtype)


def matmul(x, w, out_dtype=F32, tm=512, tn=512):
    m, k = x.shape
    n = w.shape[1]
    tm, tn = min(tm, m), min(tn, n)
    return pl.pallas_call(
        _matmul_kernel,
        grid=(m // tm, n // tn),
        in_specs=[pl.BlockSpec((tm, k), lambda i, j: (i, 0)), pl.BlockSpec((k, tn), lambda i, j: (0, j))],
        out_specs=pl.BlockSpec((tm, tn), lambda i, j: (i, j)),
        out_shape=jax.ShapeDtypeStruct((m, n), out_dtype),
        compiler_params=_params("parallel", "arbitrary"),
        name="matmul",
    )(x, w)


def _split3(a):
    h = a.astype(BF16)
    r = a - h.astype(F32)
    m = r.astype(BF16)
    return h, m, (r - m.astype(F32)).astype(BF16)


def _ssd_kernel(z_ref, x_ref, b_ref, c_ref, dt_ref, dtt_ref, *rest):
    @pl.when(pl.program_id(2) == 0)
    def _():
        for ref in rest[-4:]:
            ref[...] = jnp.zeros_like(ref)

    L = x_ref.shape[0] // SSD_CHUNKS_PER_STEP
    for cc in range(SSD_CHUNKS_PER_STEP):
        rows = pl.ds(cc * L, L)
        _ssd_chunk(z_ref.at[rows], x_ref.at[rows], b_ref.at[rows], c_ref.at[rows], dt_ref.at[:, rows],
                   dtt_ref.at[:, :, rows], *rest[:9], rest[9].at[rows], *rest[10:])


def _ssd_chunk(z_ref, x_ref, b_ref, c_ref, dt_ref, dtt_ref, cwx_ref, cwb_ref, cwc_ref, cbx_ref, cbb_ref, cbc_ref,
               hp_ref, hpt_ref, nw_ref, y_ref, state_ref, ext_x_ref, ext_b_ref, ext_c_ref):
    L = x_ref.shape[0]
    H = L // 2

    def conv_silu(u_ref, ext_ref, w, b):
        ext_ref[0:SUBLANES, :] = ext_ref[L:L + SUBLANES, :]
        u = u_ref[...]
        ext_ref[SUBLANES:SUBLANES + L, :] = u
        acc = b + w[SSD_CONV - 1:SSD_CONV, :] * u
        for k in range(SSD_CONV - 1):
            acc = acc + w[k:k + 1, :] * ext_ref[pl.ds(SUBLANES - (SSD_CONV - 1) + k, L), :]
        return _silu(acc)

    xs = conv_silu(x_ref, ext_x_ref, cwx_ref[...], cbx_ref[...])
    bm = conv_silu(b_ref, ext_b_ref, cwb_ref[...], cbb_ref[...])
    cm = conv_silu(c_ref, ext_c_ref, cwc_ref[...], cbc_ref[...])
    bm16, cm16 = bm.astype(BF16), cm.astype(BF16)

    hp = hp_ref[0]
    hpt = hpt_ref[0]
    dt = _softplus(dt_ref[0] + hp[0:1, :])
    dtt = _softplus(dtt_ref[0] + hpt[:, 0:1])
    da = dt * (-jnp.exp(hp[1:2, :]))
    dat = dtt * (-jnp.exp(hpt[:, 1:2]))
    row = lax.broadcasted_iota(jnp.int32, (L, L), 0)
    col = lax.broadcasted_iota(jnp.int32, (L, L), 1)
    tril = jnp.where(row >= col, 1.0, 0.0).astype(BF16)
    triu = jnp.where(row <= col, 1.0, 0.0).astype(BF16)
    acum = sum(_dot(tril, part) for part in _split3(da)) * LOG2E
    acumt = sum(_dot(part, triu) for part in _split3(dat)) * LOG2E
    dtet = jnp.exp2(acumt[:, L - 1:L] - acumt)
    eact = jnp.exp2(acumt)
    cdec = jnp.exp2(acum[L - 1:L, :])

    cbt = _dot_nt(bm16, cm16)
    cbt00, cbt01, cbt11 = cbt[:H, :H], cbt[:H, H:], cbt[H:, H:]
    tri = lax.broadcasted_iota(jnp.int32, (H, H), 0) <= lax.broadcasted_iota(jnp.int32, (H, H), 1)
    xst = xs.T
    d_skip = hp[2:3, :]
    P = SSD_HEAD_DIM
    outs = []
    for p in range(HEADS_PER_GROUP // 2):
        hprev = state_ref[p]
        yoff = _dot_nt(hprev.astype(BF16), cm16)
        new_state = []
        for hh in range(2):
            h = 2 * p + hh
            rs = slice(hh * P, (hh + 1) * P)
            xt = xst[h * P:(h + 1) * P, :]
            xdt = xt * dtt[h:h + 1, :]
            xdt16 = xdt.astype(BF16)
            a_col, a_row = acum[:, h:h + 1], acumt[h:h + 1, :]
            d00 = jnp.exp2(jnp.where(tri, a_row[:, :H] - a_col[:H], -jnp.inf))
            d01 = jnp.exp2(a_row[:, H:] - a_col[:H])
            d11 = jnp.exp2(jnp.where(tri, a_row[:, H:] - a_col[H:], -jnp.inf))
            y_l = _dot(xdt16[:, :H], (cbt00 * d00).astype(BF16))
            y_r = _dot(xdt16[:, :H], (cbt01 * d01).astype(BF16)) + _dot(xdt16[:, H:], (cbt11 * d11).astype(BF16))
            y = jnp.concatenate([y_l, y_r], axis=1)
            outs.append(y + yoff[rs] * eact[h:h + 1, :] + d_skip[:, h:h + 1] * xt)
            new_state.append(hprev[rs] * cdec[:, h:h + 1] + _dot((xdt * dtet[h:h + 1, :]).astype(BF16), bm16))
        state_ref[p] = jnp.concatenate(new_state, axis=0)
    y = jnp.concatenate(outs, axis=0).T * _silu(z_ref[...])
    y_ref[...] = _rms(y, nw_ref[...]).astype(y_ref.dtype)


def ssd_branch(zx, small, conv_w, conv_b, dt_bias, a_log, d_skip, norm_w, batch, seq):
    n = batch * seq
    L = min(SSD_CHUNK, seq)
    nc = seq // L
    g, hpg = SSD_GROUPS, HEADS_PER_GROUP
    dt_raw = small[:, :SSD_HEADS].reshape(n, g, hpg)
    dt_g = dt_raw.transpose(1, 0, 2)
    dtt_g = dt_raw.transpose(1, 2, 0)
    hp = jnp.stack([dt_bias, a_log, d_skip], axis=0).reshape(3, g, hpg).transpose(1, 0, 2)
    hp = jnp.pad(hp, ((0, 0), (0, SUBLANES - 3), (0, 0)))
    hpt = hp.transpose(0, 2, 1)
    cw = conv_w
    cbias = conv_b.reshape(1, -1)
    nxb = SSD_INNER // GROUP_WIDTH
    assert nc % SSD_CHUNKS_PER_STEP == 0
    lb, nb = L * SSD_CHUNKS_PER_STEP, nc // SSD_CHUNKS_PER_STEP
    rowblk = lambda b, gi, c: b * nb + c
    bc0 = SSD_INNER // SSD_STATE
    grid = (g, batch, nb)
    in_specs = [
        pl.BlockSpec((lb, GROUP_WIDTH), lambda gi, b, c: (rowblk(b, gi, c), gi)),
        pl.BlockSpec((lb, GROUP_WIDTH), lambda gi, b, c: (rowblk(b, gi, c), nxb + gi)),
        pl.BlockSpec((lb, SSD_STATE), lambda gi, b, c: (rowblk(b, gi, c), 2 * bc0 + gi)),
        pl.BlockSpec((lb, SSD_STATE), lambda gi, b, c: (rowblk(b, gi, c), 2 * bc0 + g + gi)),
        pl.BlockSpec((1, lb, hpg), lambda gi, b, c: (gi, rowblk(b, gi, c), 0)),
        pl.BlockSpec((1, hpg, lb), lambda gi, b, c: (gi, 0, rowblk(b, gi, c))),
        pl.BlockSpec((SSD_CONV, GROUP_WIDTH), lambda gi, b, c: (0, gi)),
        pl.BlockSpec((SSD_CONV, SSD_STATE), lambda gi, b, c: (0, bc0 + gi)),
        pl.BlockSpec((SSD_CONV, SSD_STATE), lambda gi, b, c: (0, bc0 + g + gi)),
        pl.BlockSpec((1, GROUP_WIDTH), lambda gi, b, c: (0, gi)),
        pl.BlockSpec((1, SSD_STATE), lambda gi, b, c: (0, bc0 + gi)),
        pl.BlockSpec((1, SSD_STATE), lambda gi, b, c: (0, bc0 + g + gi)),
        pl.BlockSpec((1, SUBLANES, hpg), lambda gi, b, c: (gi, 0, 0)),
        pl.BlockSpec((1, hpg, SUBLANES), lambda gi, b, c: (gi, 0, 0)),
        pl.BlockSpec((1, GROUP_WIDTH), lambda gi, b, c: (0, gi)),
    ]
    return pl.pallas_call(
        _ssd_kernel,
        grid=grid,
        in_specs=in_specs,
        out_specs=pl.BlockSpec((lb, GROUP_WIDTH), lambda gi, b, c: (rowblk(b, gi, c), gi)),
        out_shape=jax.ShapeDtypeStruct((n, SSD_INNER), BF16),
        scratch_shapes=[pltpu.VMEM((HEADS_PER_GROUP // 2, LANES, SSD_STATE), F32),
                        pltpu.VMEM((L + SUBLANES, GROUP_WIDTH), F32), pltpu.VMEM((L + SUBLANES, SSD_STATE), F32),
                        pltpu.VMEM((L + SUBLANES, SSD_STATE), F32)],
        compiler_params=_params("parallel", "parallel", "arbitrary"),
        name="ssd_branch",
    )(zx, zx, zx, zx, dt_g, dtt_g, cw, cw, cw, cbias, cbias, cbias, hp, hpt, norm_w.reshape(1, -1))


def _nsa_prep_kernel(q_ref, kc_ref, vc_ref, ks_ref, vs_ref, kw_ref, vw_ref, cos_ref, sin_ref, qw_ref, kw3_ref,
                     qo_ref, kco_ref, vco_ref, kso_ref, vso_ref, kwo_ref, vwo_ref):
    cos, sin = cos_ref[...], sin_ref[...]

    def norm_rope(x, w):
        y = _rms(x, w)
        return y * cos + pltpu.roll(y, HEAD_DIM // 2, 1) * sin

    qw = qw_ref[...]
    tt = q_ref.shape[0]
    for g in range(NSA_KV_HEADS):
        sl = slice(g * HEAD_DIM, (g + 1) * HEAD_DIM)
        for r in range(NSA_REP):
            h = g * NSA_REP + r
            qh = norm_rope(q_ref[:, h * HEAD_DIM:(h + 1) * HEAD_DIM], qw) * Q_PRESCALE
            qo_ref[0, g, r] = qh.astype(qo_ref.dtype)
        kco_ref[0, g] = norm_rope(kc_ref[:, sl], kw3_ref[0:1, :])
        vco_ref[0, g] = vc_ref[:, sl]
        kso_ref[0, g] = norm_rope(ks_ref[:, sl], kw3_ref[1:2, :]).astype(kso_ref.dtype)
        kwo_ref[0, g] = norm_rope(kw_ref[:, sl], kw3_ref[2:3, :]).astype(kwo_ref.dtype)
        vst = vs_ref[:, sl].T.astype(vso_ref.dtype)
        for j in range(tt // SEL_KEY_TILE):
            vso_ref[0, g, j] = vst[:, j * SEL_KEY_TILE:(j + 1) * SEL_KEY_TILE]
        vwt = vw_ref[:, sl].T.astype(vwo_ref.dtype)
        for j in range(tt // WIN_TILE):
            vwo_ref[0, g, j] = vwt[:, j * WIN_TILE:(j + 1) * WIN_TILE]


def nsa_prep(qkv, q_norm_w, k_norm_w, batch, seq, tt=2 * SEL_KEY_TILE):
    half = HEAD_DIM // 2
    inv = ROPE_THETA ** (-jnp.arange(half, dtype=F32) / half)
    ang = jnp.arange(seq).astype(F32)[:, None] * inv[None, :]
    cos = jnp.concatenate([jnp.cos(ang), jnp.cos(ang)], axis=-1)
    sin = jnp.concatenate([-jnp.sin(ang), jnp.sin(ang)], axis=-1)
    nt = seq // tt
    kvb = NSA_WIDTH // NSA_KV_WIDTH
    kv_spec = lambda j: pl.BlockSpec((tt, NSA_KV_WIDTH), lambda b, t: (b * nt + t, kvb + j))
    head_spec = pl.BlockSpec((1, NSA_KV_HEADS, tt, HEAD_DIM), lambda b, t: (b, 0, t, 0))
    head_shape = lambda dt: jax.ShapeDtypeStruct((batch, NSA_KV_HEADS, seq, HEAD_DIM), dt)
    return pl.pallas_call(
        _nsa_prep_kernel,
        grid=(batch, nt),
        in_specs=[pl.BlockSpec((tt, NSA_WIDTH), lambda b, t: (b * nt + t, 0))] + [kv_spec(j) for j in range(6)] + [
            pl.BlockSpec((tt, HEAD_DIM), lambda b, t: (t, 0)), pl.BlockSpec((tt, HEAD_DIM), lambda b, t: (t, 0)),
            pl.BlockSpec((1, HEAD_DIM), lambda b, t: (0, 0)), pl.BlockSpec((3, HEAD_DIM), lambda b, t: (0, 0))],
        out_specs=[pl.BlockSpec((1, NSA_KV_HEADS, NSA_REP, tt, HEAD_DIM), lambda b, t: (b, 0, 0, t, 0)),
                   head_spec, head_spec, head_spec,
                   pl.BlockSpec((1, NSA_KV_HEADS, tt // SEL_KEY_TILE, HEAD_DIM, SEL_KEY_TILE),
                                lambda b, t: (b, 0, t, 0, 0)),
                   head_spec,
                   pl.BlockSpec((1, NSA_KV_HEADS, tt // WIN_TILE, HEAD_DIM, WIN_TILE), lambda b, t: (b, 0, t, 0, 0))],
        out_shape=[jax.ShapeDtypeStruct((batch, NSA_KV_HEADS, NSA_REP, seq, HEAD_DIM), BF16),
                   head_shape(F32), head_shape(F32), head_shape(BF16),
                   jax.ShapeDtypeStruct((batch, NSA_KV_HEADS, seq // SEL_KEY_TILE, HEAD_DIM, SEL_KEY_TILE), BF16),
                   head_shape(BF16),
                   jax.ShapeDtypeStruct((batch, NSA_KV_HEADS, seq // WIN_TILE, HEAD_DIM, WIN_TILE), BF16)],
        compiler_params=_params("parallel", "parallel"),
        name="nsa_prep",
    )(qkv, qkv, qkv, qkv, qkv, qkv, qkv, cos, sin, q_norm_w.reshape(1, -1), k_norm_w)


def _compress_kernel(x_ref, pe_ref, w1_ref, w2_ref, o_ref, *, ncmp, transpose_out):
    nseg = x_ref.shape[2] // CMP_STRIDE
    u = jnp.zeros((nseg, CMP_HID), F32)
    v = jnp.zeros((nseg, CMP_HID), F32)
    for j in range(CMP_STRIDE):
        xj = x_ref[0, 0, pl.ds(j, nseg, stride=CMP_STRIDE), :]
        u = u + _dot((xj + pe_ref[j:j + 1, :]).astype(BF16), w1_ref[j])
        v = v + _dot((xj + pe_ref[CMP_STRIDE + j:CMP_STRIDE + j + 1, :]).astype(BF16), w1_ref[CMP_STRIDE + j])
    hid = _silu(u + pltpu.roll(v, nseg - 1, 0))
    comp = _dot(hid.astype(BF16), w2_ref[...])
    rowi = lax.broadcasted_iota(jnp.int32, comp.shape, 0)
    comp = jnp.where(rowi < ncmp, comp, 0.0)
    o_ref[0, 0] = (comp.T if transpose_out else comp).astype(o_ref.dtype)


def compress(raw, pe, w1, w2, transpose_out):
    b, g, t, hd = raw.shape
    nseg = t // CMP_STRIDE
    full = lambda shape: pl.BlockSpec(shape, lambda i, j: (0,) * len(shape))
    out_dims = (hd, nseg) if transpose_out else (nseg, hd)
    return pl.pallas_call(
        functools.partial(_compress_kernel, ncmp=nseg - 1, transpose_out=transpose_out),
        grid=(b, g),
        in_specs=[pl.BlockSpec((1, 1, t, hd), lambda i, j: (i, j, 0, 0)), full((CMP_BLK, hd)),
                  full((CMP_BLK, hd, CMP_HID)), full((CMP_HID, hd))],
        out_specs=pl.BlockSpec((1, 1) + out_dims, lambda i, j: (i, j, 0, 0)),
        out_shape=jax.ShapeDtypeStruct((b, g) + out_dims, BF16),
        compiler_params=_params("parallel", "parallel"),
        name="nsa_compress",
    )(raw, pe, w1.astype(BF16), w2.astype(BF16))


def _nsa_attn_kernel(q_ref, kc_ref, vct_ref, ks_ref, vst_ref, kw_ref, vwt_ref, glt_ref, ovt_ref, o_ref,
                     selb_ref, acc_ref, ocw_ref, s_ref, p_ref, *, seq, n_sel):
    Q = q_ref.shape[3]
    q0 = pl.program_id(2) * Q
    rows = NSA_REP * Q
    q = q_ref[0, 0].reshape(rows, HEAD_DIM)
    tq = q0 + lax.broadcasted_iota(jnp.int32, (1, Q), 1)
    head = lambda r: slice(r * Q, (r + 1) * Q)

    ncp = kc_ref.shape[2]
    wk = WINDOW + Q
    kt_sz = vst_ref.shape[4]
    start = pl.multiple_of(jnp.clip(q0 - WINDOW, 0, seq - wk), WIN_TILE)
    s_cw = _dot_nt(jnp.concatenate([kc_ref[0, 0], kw_ref[0, 0, pl.ds(start, wk), :], ks_ref[0, 0, 0:kt_sz, :]],
                                   axis=0), q)
    s_c, s_w = s_cw[:ncp], s_cw[ncp:ncp + wk]
    s_ref[0] = s_cw[ncp + wk:]

    cend = lax.broadcasted_iota(jnp.int32, (ncp, 1), 0) * CMP_STRIDE + (CMP_BLK - 1)
    m_c = cend <= tq
    ps = []
    for r in range(NSA_REP):
        s = jnp.where(m_c, s_c[:, head(r)], NEG_INF)
        e = jnp.exp2(s - jnp.max(s, axis=0, keepdims=True))
        p = jnp.where(m_c, e * (1.0 / jnp.sum(e, axis=0, keepdims=True)), 0.0)
        ps.append(p.astype(BF16))
    p_c = jnp.concatenate(ps, axis=1)
    nsb = ovt_ref.shape[0]
    oc_imp = _dot(jnp.concatenate([vct_ref[0, 0], ovt_ref[...]], axis=0), p_c)
    o_c, imp4 = oc_imp[:HEAD_DIM], oc_imp[HEAD_DIM:]

    imp = imp4[:, head(0)]
    for r in range(1, NSA_REP):
        imp = imp + imp4[:, head(r)]
    blk_t = (q0 + lax.broadcasted_iota(jnp.int32, (nsb, Q), 1)) >> SEL_SHIFT
    sb = lax.broadcasted_iota(jnp.int32, (nsb, Q), 0)
    forced = (sb == 0) | (sb == blk_t) | (sb == blk_t - 1)
    imp = jnp.where(sb <= blk_t, jnp.where(forced, FORCE_SCORE, imp), -jnp.inf)
    rank = jnp.zeros((nsb, Q), F32)
    for i in range(nsb):
        ri = imp[i:i + 1, :]
        beats = (ri > imp) | ((ri == imp) & (sb > i))
        rank = rank + jnp.where(beats, 1.0, 0.0)
    selb_ref[...] = jnp.where(rank < n_sel, 0.0, NEG_INF)

    dist = tq - (start + lax.broadcasted_iota(jnp.int32, (wk, 1), 0))
    bias_w = jnp.where((dist >= 0) & (dist < WINDOW), 0.0, NEG_INF)
    ps, ls = [], []
    for r in range(NSA_REP):
        s = s_w[:, head(r)] + bias_w
        p = jnp.exp2(s - jnp.max(s, axis=0, keepdims=True))
        ls.append(jnp.sum(p, axis=0, keepdims=True))
        ps.append(p.astype(BF16))
    t0 = start // WIN_TILE
    vw_t = jnp.concatenate([vwt_ref[0, 0, t0 + j] for j in range(wk // WIN_TILE)], axis=1)
    o_w = _dot(vw_t, jnp.concatenate(ps, axis=1)) * (1.0 / jnp.concatenate(ls, axis=1))

    gate = _sigmoid(glt_ref[0, 0])
    for r in range(NSA_REP):
        ocw_ref[:, head(r)] = (gate[r:r + 1, :] * o_c[:, head(r)]
                               + gate[2 * NSA_REP + r:2 * NSA_REP + r + 1, :] * o_w[:, head(r)])

    blocks_per_tile = kt_sz // SEL_BLK
    n_all = seq // kt_sz
    acc_ref[...] = jnp.zeros_like(acc_ref)
    p_ref[1] = jnp.zeros(p_ref.shape[1:], p_ref.dtype)

    def sel_tile(kt, carry):
        m, l, alpha_prev = carry
        slot = kt & 1
        s_all = s_ref[slot]
        p_prev = p_ref[1 - slot]
        k_next = pl.multiple_of(jnp.minimum(kt + 1, n_all - 1) * kt_sz, kt_sz)
        s_ref[1 - slot] = _dot_nt(ks_ref[0, 0, pl.ds(k_next, kt_sz), :], q)
        pv_prev = _dot(vst_ref[0, 0, jnp.maximum(kt - 1, 0)], p_prev)
        acc_ref[...] = acc_ref[...] * alpha_prev + pv_prev
        k0 = kt * kt_sz
        kpos = k0 + lax.broadcasted_iota(jnp.int32, (kt_sz, 1), 0)
        bias = jnp.concatenate(
            [jnp.broadcast_to(selb_ref[pl.ds(kt * blocks_per_tile + i, 1), :], (SEL_BLK, Q))
             for i in range(blocks_per_tile)], axis=0)
        bias = jnp.where(kpos <= tq, bias, NEG_INF)
        ps, ms, ls, alphas = [], [], [], []
        for r in range(NSA_REP):
            s = s_all[:, head(r)] + bias
            m_old = m[:, head(r)]
            m_new = jnp.maximum(m_old, jnp.max(s, axis=0, keepdims=True))
            p = jnp.exp2(s - m_new)
            alpha = jnp.exp2(m_old - m_new)
            ls.append(alpha * l[:, head(r)] + jnp.sum(p, axis=0, keepdims=True))
            ms.append(m_new)
            alphas.append(alpha)
            ps.append(p.astype(BF16))
        p_ref[slot] = jnp.concatenate(ps, axis=1)
        return jnp.concatenate(ms, axis=1), jnp.concatenate(ls, axis=1), jnp.concatenate(alphas, axis=1)

    n_tiles = (q0 + Q + kt_sz - 1) // kt_sz
    init = (jnp.full((1, rows), NEG_INF, F32), jnp.zeros((1, rows), F32), jnp.ones((1, rows), F32))
    _, l_s, alpha_last = lax.fori_loop(0, n_tiles, sel_tile, init)
    pv_last = _dot(vst_ref[0, 0, n_tiles - 1], p_ref[(n_tiles - 1) & 1])
    o_s = (acc_ref[...] * alpha_last + pv_last) * (1.0 / l_s)

    for r in range(NSA_REP):
        out_t = ocw_ref[:, head(r)] + gate[NSA_REP + r:NSA_REP + r + 1, :] * o_s[:, head(r)]
        o_ref[:, r * HEAD_DIM:(r + 1) * HEAD_DIM] = out_t.T.astype(o_ref.dtype)


def nsa_attention(qh, kcc, vcct, ksn, vst, kwn, vwt, gate_logits_t, batch, seq):
    Q_BLK = NSA_Q_BLK
    assert seq % SEL_KEY_TILE == 0 and seq % Q_BLK == 0 and seq >= WINDOW + Q_BLK
    nq = seq // Q_BLK
    nsb = seq // SEL_BLK
    n_sel = min(N_SEL, nsb)
    ncp = kcc.shape[2]
    ncmp = (seq - CMP_BLK) // CMP_STRIDE + 1
    ci = np.arange(ncp)[None, :]
    sj = np.arange(nsb)[:, None]
    ovt = ((ci * CMP_STRIDE < (sj + 1) * SEL_BLK) & (ci * CMP_STRIDE + CMP_BLK > sj * SEL_BLK) & (ci < ncmp))
    ovt = jnp.asarray(ovt, BF16)
    seq_spec = pl.BlockSpec((1, 1, seq, HEAD_DIM), lambda b, g, i: (b, g, 0, 0))
    cmp_spec = pl.BlockSpec((1, 1, ncp, HEAD_DIM), lambda b, g, i: (b, g, 0, 0))
    tiled = lambda a: pl.BlockSpec((1, 1) + a.shape[2:], lambda b, g, i: (b, g, 0, 0, 0))
    return pl.pallas_call(
        functools.partial(_nsa_attn_kernel, seq=seq, n_sel=n_sel),
        grid=(batch, NSA_KV_HEADS, nq),
        in_specs=[pl.BlockSpec((1, 1, NSA_REP, Q_BLK, HEAD_DIM), lambda b, g, i: (b, g, 0, i, 0)),
                  cmp_spec, pl.BlockSpec((1, 1, HEAD_DIM, ncp), lambda b, g, i: (b, g, 0, 0)),
                  seq_spec, tiled(vst), seq_spec, tiled(vwt),
                  pl.BlockSpec((1, 1, 3 * NSA_REP, Q_BLK), lambda b, g, i: (b, g, 0, i)),
                  pl.BlockSpec((nsb, ncp), lambda b, g, i: (0, 0))],
        out_specs=pl.BlockSpec((Q_BLK, NSA_REP * HEAD_DIM), lambda b, g, i: (b * nq + i, g)),
        out_shape=jax.ShapeDtypeStruct((batch * seq, NSA_WIDTH), BF16),
        scratch_shapes=[pltpu.VMEM((nsb, Q_BLK), F32), pltpu.VMEM((HEAD_DIM, NSA_REP * Q_BLK), F32),
                        pltpu.VMEM((HEAD_DIM, NSA_REP * Q_BLK), F32),
                        pltpu.VMEM((2, SEL_KEY_TILE, NSA_REP * Q_BLK), F32),
                        pltpu.VMEM((2, SEL_KEY_TILE, NSA_REP * Q_BLK), BF16)],
        compiler_params=_params("parallel", "parallel", "arbitrary"),
        name="nsa_attention",
    )(qh, kcc, vcct, ksn, vst, kwn, vwt, gate_logits_t, ovt)


def _merge_kernel(ys_ref, yn_ref, ws_ref, wn_ref, gs_ref, gn_ref, o_ref):
    up_s = _dot(ys_ref[...], ws_ref[...])
    up_n = _dot(yn_ref[...], wn_ref[...])
    o_ref[...] = (_sigmoid(gs_ref[...]) * up_s + _sigmoid(gn_ref[...]) * up_n).astype(o_ref.dtype)


def merge_mixers(y_ssd, y_nsa, w_up_ssd, w_up_nsa, gates, tm=1024, tn=512):
    m = y_ssd.shape[0]
    nb = D_MODEL // tn
    return pl.pallas_call(
        _merge_kernel,
        grid=(m // tm, nb),
        in_specs=[pl.BlockSpec((tm, SSD_INNER), lambda i, j: (i, 0)), pl.BlockSpec((tm, NSA_WIDTH), lambda i, j: (i, 0)),
                  pl.BlockSpec((SSD_INNER, tn), lambda i, j: (0, j)), pl.BlockSpec((NSA_WIDTH, tn), lambda i, j: (0, j)),
                  pl.BlockSpec((tm, tn), lambda i, j: (i, j)), pl.BlockSpec((tm, tn), lambda i, j: (i, nb + j))],
        out_specs=pl.BlockSpec((tm, tn), lambda i, j: (i, j)),
        out_shape=jax.ShapeDtypeStruct((m, D_MODEL), BF16),
        compiler_params=_params("parallel", "arbitrary"),
        name="merge_mixers",
    )(y_ssd, y_nsa, w_up_ssd, w_up_nsa, gates, gates)


def _residual_matmul_kernel(h_ref, x_ref, w_ref, o_ref):
    o_ref[...] = h_ref[...] + _dot(x_ref[...], w_ref[...])


def residual_matmul(h, x, w, tm=1024, tn=1024):
    m, k = x.shape
    n = w.shape[1]
    return pl.pallas_call(
        _residual_matmul_kernel,
        grid=(m // tm, n // tn),
        in_specs=[pl.BlockSpec((tm, tn), lambda i, j: (i, j)), pl.BlockSpec((tm, k), lambda i, j: (i, 0)),
                  pl.BlockSpec((k, tn), lambda i, j: (0, j))],
        out_specs=pl.BlockSpec((tm, tn), lambda i, j: (i, j)),
        out_shape=jax.ShapeDtypeStruct((m, n), F32),
        compiler_params=_params("parallel", "arbitrary"),
        name="residual_matmul",
    )(h, x, w)


def _xattn_router_kernel(h_ref, kv_ref, n2_ref, wq_ref, qn_ref, kn_ref, wo_ref, n3_ref, rw_ref, rb_ref,
                         h2_ref, hf_ref, rt_ref):
    h = h_ref[...]
    qp = _dot(_rms(h, n2_ref[...]).astype(BF16), wq_ref[...])
    kv = kv_ref[...]
    scale = HEAD_DIM ** -0.5
    heads = []
    for hd in range(X_HEADS):
        sl = slice(hd * HEAD_DIM, (hd + 1) * HEAD_DIM)
        qh = _rms(qp[:, sl], qn_ref[...]).astype(BF16)
        kh = _rms(kv[:, sl], kn_ref[...]).astype(BF16)
        vh = kv[:, X_WIDTH + hd * HEAD_DIM:X_WIDTH + (hd + 1) * HEAD_DIM].astype(BF16)
        s = _dot_nt(qh, kh) * scale
        e = jnp.exp(s - jnp.max(s, axis=-1, keepdims=True))
        p = e / jnp.sum(e, axis=-1, keepdims=True)
        heads.append(_dot(p.astype(BF16), vh))
    o = jnp.concatenate(heads, axis=1).astype(BF16)
    h2 = h + _dot(o, wo_ref[...])
    h2_ref[...] = h2
    hf = _rms(h2, n3_ref[...]).astype(BF16)
    hf_ref[...] = hf
    lg = _dot(hf, rw_ref[...]) + rb_ref[...]

    lane = lax.broadcasted_iota(jnp.int32, lg.shape, 1)
    rmax = lambda v: jnp.max(v, axis=-1, keepdims=True)
    rsum = lambda v: jnp.sum(v, axis=-1, keepdims=True)
    first_lane = lambda hit: jnp.min(jnp.where(hit, lane, LANES), axis=-1, keepdims=True)
    is_g = lane < N_GROUPS
    eg = jnp.where(is_g, jnp.exp(lg - rmax(jnp.where(is_g, lg, -jnp.inf))), 0.0)
    pg = eg / rsum(eg)
    pg_top = rmax(pg)
    grp = first_lane(is_g & (pg == pg_top))
    in_grp = (lane >= N_GROUPS) & (((lane - N_GROUPS) >> EPG_SHIFT) == grp)
    ee = jnp.where(in_grp, jnp.exp(lg - rmax(jnp.where(in_grp, lg, -jnp.inf))), 0.0)
    pe = jnp.where(in_grp, ee / rsum(ee), -1.0)
    p1 = rmax(pe)
    i1 = first_lane(pe == p1)
    pe2 = jnp.where(lane == i1, -1.0, pe)
    p2 = rmax(pe2)
    i2 = first_lane(pe2 == p2)
    psum = p1 + p2
    route = jnp.where(lane == 0, (i1 - N_GROUPS).astype(F32),
                      jnp.where(lane == 1, (i2 - N_GROUPS).astype(F32),
                                jnp.where(lane == 2, pg_top * p1 / psum,
                                          jnp.where(lane == 3, pg_top * p2 / psum, 0.0))))
    rt_ref[...] = route


def xattn_router(h1, kv, norm2_w, wq, q_norm_w, k_norm_w, wo, norm3_w, router_w, router_b, batch, seq, tm=512):
    n = batch * seq
    tm = min(tm, seq)
    per_seq = seq // tm
    mlen = kv.shape[0] // batch
    full = lambda shape: pl.BlockSpec(shape, lambda i: (0,) * len(shape))
    return pl.pallas_call(
        _xattn_router_kernel,
        grid=(n // tm,),
        in_specs=[pl.BlockSpec((tm, D_MODEL), lambda i: (i, 0)),
                  pl.BlockSpec((mlen, 2 * X_WIDTH), lambda i: (i // per_seq, 0)),
                  full((1, D_MODEL)), full((D_MODEL, X_WIDTH)), full((1, HEAD_DIM)), full((1, HEAD_DIM)),
                  full((X_WIDTH, D_MODEL)), full((1, D_MODEL)), full((D_MODEL, LANES)), full((1, LANES))],
        out_specs=[pl.BlockSpec((tm, D_MODEL), lambda i: (i, 0)), pl.BlockSpec((tm, D_MODEL), lambda i: (i, 0)),
                   pl.BlockSpec((tm, LANES), lambda i: (i, 0))],
        out_shape=[jax.ShapeDtypeStruct((n, D_MODEL), F32), jax.ShapeDtypeStruct((n, D_MODEL), BF16),
                   jax.ShapeDtypeStruct((n, LANES), F32)],
        compiler_params=_params("parallel"),
        name="xattn_router",
    )(h1, kv, norm2_w.reshape(1, -1), wq, q_norm_w.reshape(1, -1), k_norm_w.reshape(1, -1), wo,
      norm3_w.reshape(1, -1), router_w, router_b)


def _moe_rank_kernel(e_ref, rank_ref, cnt_ref, carry_ref):
    @pl.when(pl.program_id(0) == 0)
    def _():
        carry_ref[...] = jnp.zeros_like(carry_ref)

    t = e_ref.shape[0]
    hit = e_ref[...] == lax.broadcasted_iota(jnp.int32, (t, LANES), 1)
    onehot = jnp.where(hit, 1.0, 0.0).astype(BF16)
    earlier = lax.broadcasted_iota(jnp.int32, (t, t), 0) > lax.broadcasted_iota(jnp.int32, (t, t), 1)
    before = _dot(jnp.where(earlier, 1.0, 0.0).astype(BF16), onehot) + carry_ref[...]
    rank_ref[...] = jnp.sum(jnp.where(hit, before, 0.0), axis=-1, keepdims=True).astype(jnp.int32)
    carry_ref[...] += jnp.sum(jnp.where(hit, 1.0, 0.0), axis=0, keepdims=True)
    cnt_ref[...] = carry_ref[...]


def moe_rank(eid, t=1024):
    a = eid.shape[0]
    return pl.pallas_call(
        _moe_rank_kernel,
        grid=(a // t,),
        in_specs=[pl.BlockSpec((t, 1), lambda i: (i, 0))],
        out_specs=[pl.BlockSpec((t, 1), lambda i: (i, 0)), pl.BlockSpec((1, LANES), lambda i: (0, 0))],
        out_shape=[jax.ShapeDtypeStruct((a, 1), jnp.int32), jax.ShapeDtypeStruct((1, LANES), F32)],
        scratch_shapes=[pltpu.VMEM((1, LANES), F32)],
        compiler_params=_params("arbitrary"),
        name="moe_rank",
    )(eid)


def _moe_ffn_kernel(item_e_ref, item_row_ref, item_nsub_ref, item_nout_ref, x_hbm, wg_ref, wu_ref, wd_ref, y_hbm,
                    x_vmem, y_vmem, sem_x, sem_y, *, hid):
    i, c = pl.program_id(0), pl.program_id(1)
    last_c = pl.num_programs(1) - 1
    th = wg_ref.shape[2]
    nsub, nout = item_nsub_ref[i], item_nout_ref[i]
    row0 = pl.multiple_of(item_row_ref[i], MOE_SUB)
    sub = lambda s: pl.ds(s * MOE_SUB, MOE_SUB)

    def x_copy(item, s):
        row = pl.multiple_of(item_row_ref[item], MOE_SUB)
        return pltpu.make_async_copy(x_hbm.at[pl.ds(row + s * MOE_SUB, MOE_SUB)], x_vmem.at[item & 1, sub(s)],
                                     sem_x.at[item & 1, s])

    def y_copy(row, s):
        return pltpu.make_async_copy(y_vmem.at[sub(s)], y_hbm.at[pl.ds(row + s * MOE_SUB, MOE_SUB)], sem_y.at[s])

    def for_subs(count, fn):
        for s in range(MOE_ITEM_SUBS):
            pl.when(s < count)(functools.partial(fn, s))

    @pl.when(c == 0)
    def _():
        @pl.when(i > 0)
        def _():
            prev_row = pl.multiple_of(item_row_ref[i - 1], MOE_SUB)
            for_subs(item_nout_ref[i - 1], lambda s: y_copy(prev_row, s).wait())

        @pl.when(i == 0)
        def _():
            for_subs(nsub, lambda s: x_copy(i, s).start())

        for_subs(nsub, lambda s: x_copy(i, s).wait())

        @pl.when(i + 1 < pl.num_programs(0))
        def _():
            nxt = jnp.minimum(i + 1, pl.num_programs(0) - 1)
            for_subs(item_nsub_ref[nxt], lambda s: x_copy(nxt, s).start())

        @pl.when(nsub == 0)
        def _():
            y_vmem[...] = jnp.zeros_like(y_vmem)
            for_subs(nout, lambda s: y_copy(row0, s).start())

    def ffn(count):
        rows = pl.ds(0, count * MOE_SUB)
        col_ok = c * th + lax.broadcasted_iota(jnp.int32, (1, th), 1) < hid
        row_ok = c * th + lax.broadcasted_iota(jnp.int32, (th, 1), 0) < hid
        wgu = jnp.concatenate([wg_ref[0], wu_ref[0]], axis=1).astype(BF16)
        wd = jnp.where(row_ok, wd_ref[0], 0.0).astype(BF16)
        gu = _dot(x_vmem[i & 1, rows, :], wgu)
        act = jnp.where(col_ok, _silu(gu[:, :th]) * gu[:, th:], 0.0)
        part = _dot(act.astype(BF16), wd)

        @pl.when(c == 0)
        def _():
            y_vmem[rows, :] = part

        @pl.when(c > 0)
        def _():
            y_vmem[rows, :] += part

    for count in range(1, MOE_ITEM_SUBS + 1):
        pl.when(nsub == count)(functools.partial(ffn, count))

    @pl.when((nsub > 0) & (c == last_c))
    def _():
        for_subs(nsub, lambda s: y_copy(row0, s).start())

    @pl.when((i == pl.num_programs(0) - 1) & (c == last_c))
    def _():
        for_subs(nout, lambda s: y_copy(row0, s).wait())


def moe_ffn(x_buf, item_e, item_row, item_nsub, item_nout, w_gate, w_up, w_down):
    rows, d = x_buf.shape
    hid = w_gate.shape[2]
    th = MOE_HID_TILE
    nc = pl.cdiv(hid, th)
    tile = lambda i, c, ns: jnp.where(ns[i] > 0, c, nc - 1)
    grid_spec = pltpu.PrefetchScalarGridSpec(
        num_scalar_prefetch=4,
        grid=(item_e.shape[0], nc),
        in_specs=[pl.BlockSpec(memory_space=pl.ANY),
                  pl.BlockSpec((1, d, th), lambda i, c, ie, ir, ns, no: (ie[i], 0, tile(i, c, ns))),
                  pl.BlockSpec((1, d, th), lambda i, c, ie, ir, ns, no: (ie[i], 0, tile(i, c, ns))),
                  pl.BlockSpec((1, th, d), lambda i, c, ie, ir, ns, no: (ie[i], tile(i, c, ns), 0))],
        out_specs=pl.BlockSpec(memory_space=pl.ANY),
        scratch_shapes=[pltpu.VMEM((2, MOE_ITEM_SUBS * MOE_SUB, d), BF16),
                        pltpu.VMEM((MOE_ITEM_SUBS * MOE_SUB, d), F32),
                        pltpu.SemaphoreType.DMA((2, MOE_ITEM_SUBS)), pltpu.SemaphoreType.DMA((MOE_ITEM_SUBS,))],
    )
    return pl.pallas_call(
        functools.partial(_moe_ffn_kernel, hid=hid),
        grid_spec=grid_spec,
        out_shape=jax.ShapeDtypeStruct((rows, d), F32),
        compiler_params=_params("arbitrary", "arbitrary"),
        name="moe_ffn",
    )(item_e, item_row, item_nsub, item_nout, x_buf, w_gate, w_up, w_down)


def hier_moe(h2, hf, route, w_gate, w_up, w_down):
    n, d = h2.shape
    n_exp = w_gate.shape[0]
    i32 = jnp.int32
    eid = route[:, 0:TOP_K].astype(i32).reshape(-1)
    wts = route[:, TOP_K:2 * TOP_K]
    n_assign = n * TOP_K
    rank, counts = moe_rank(eid.reshape(n_assign, 1))
    counts = counts[0, :n_exp].astype(i32)
    subs_e = (counts + MOE_SUB - 1) // MOE_SUB
    sub_end = jnp.cumsum(subs_e)
    row_start = (sub_end - subs_e) * MOE_SUB
    is_e = eid[:, None] == jnp.arange(n_exp, dtype=i32)[None, :]
    dest = rank[:, 0] + jnp.sum(jnp.where(is_e, row_start[None, :], 0), axis=1)
    n_subs = (n_assign + n_exp * (MOE_SUB - 1) + MOE_SUB - 1) // MOE_SUB
    rows = n_subs * MOE_SUB
    tok = jnp.repeat(jnp.arange(n, dtype=i32), TOP_K)
    tok_buf = (jnp.arange(rows, dtype=i32) % n).at[dest].set(tok)
    per = MOE_ITEM_SUBS
    n_items = (n_exp * (per - 1) + n_subs + per - 1) // per
    items_e = (subs_e + per - 1) // per
    item_end = jnp.cumsum(items_e)
    n_used = item_end[-1]
    idx = jnp.arange(n_items, dtype=i32)
    used = idx < n_used
    e_of = jnp.minimum(jnp.searchsorted(item_end, idx, side='right'), n_exp - 1).astype(i32)
    j = idx - (item_end - items_e)[e_of]
    item_nsub = jnp.where(used, jnp.clip(subs_e[e_of] - per * j, 0, per), 0)
    fill_first = jnp.minimum(sub_end[-1] + per * (idx - n_used), n_subs)
    item_fill = jnp.where(used, 0, jnp.minimum(n_subs - fill_first, per))
    item_row = jnp.where(used, row_start[e_of] + j * per * MOE_SUB, jnp.minimum(fill_first, n_subs - 1) * MOE_SUB)
    item_e = jnp.where(used, e_of, e_of[jnp.maximum(n_used - 1, 0)])
    y_buf = moe_ffn(hf[tok_buf], item_e.astype(i32), item_row.astype(i32), item_nsub.astype(i32),
                    (item_nsub + item_fill).astype(i32), w_gate, w_up, w_down)
    slot = dest.reshape(n, TOP_K)
    return h2 + wts[:, 0:1] * y_buf[slot[:, 0]] + wts[:, 1:2] * y_buf[slot[:, 1]]


COL_DT = ZX_COLS
COL_QKV = COL_DT + SSD_HEADS
COL_NSA_GATE = COL_QKV + QKV_COLS
COL_MERGE_GATE = COL_NSA_GATE + 3 * NSA_HEADS


def _small_w_in(w_in):
    pad = jnp.zeros(w_in.shape[:2] + (SMALL_COLS - SSD_HEADS - 3 * NSA_HEADS,), w_in.dtype)
    return jnp.concatenate([w_in[:, :, COL_DT:COL_QKV], w_in[:, :, COL_NSA_GATE:COL_MERGE_GATE], pad], axis=2)


def _layer(x, mem, norm1_w, w_in, ssd_conv_w, ssd_conv_b, ssd_dt_bias, ssd_a_log, ssd_d, ssd_norm_w,
           nsa_q_norm_w, nsa_k_norm_w, cmp_pe_k, cmp_w1_k, cmp_w2_k, cmp_pe_v, cmp_w1_v, cmp_w2_v,
           w_up_ssd, w_up_nsa, w_out, norm2_w, mem_norm_w, xq_w, xkv_w, x_q_norm_w, x_k_norm_w, xo_w,
           norm3_w, router_g_w, router_g_b, router_e_w, router_e_b, moe_w_gate, moe_w_up, moe_w_down):
    batch, seq, d = x.shape
    n = batch * seq
    xf = x.reshape(n, d)
    hn = rmsnorm_rows(xf, norm1_w)
    w_cols = lambda c0, nc: w_in[0, :, c0:c0 + nc].astype(BF16)
    zx = matmul(hn, w_cols(0, ZX_COLS), tm=IN_PROJ_TM, tn=IN_PROJ_TN)
    qkv = matmul(hn, w_cols(COL_QKV, QKV_COLS), tm=IN_PROJ_TM, tn=IN_PROJ_TN)
    mgates = matmul(hn, w_cols(COL_MERGE_GATE, GATE_COLS), tm=IN_PROJ_TM, tn=IN_PROJ_TN)
    small = matmul(hn, _small_w_in(w_in)[0].astype(BF16), tm=IN_PROJ_TM)
    y_ssd = ssd_branch(zx, small, ssd_conv_w, ssd_conv_b, ssd_dt_bias, ssd_a_log, ssd_d, ssd_norm_w, batch, seq)
    qh, kcn, vcr, ksn, vsn, kwn, vwn = nsa_prep(qkv, nsa_q_norm_w, nsa_k_norm_w, batch, seq)
    kcc = compress(kcn, cmp_pe_k, cmp_w1_k, cmp_w2_k, transpose_out=False)
    vcct = compress(vcr, cmp_pe_v, cmp_w1_v, cmp_w2_v, transpose_out=True)
    gl = small[:, SSD_HEADS:SSD_HEADS + 3 * NSA_HEADS].reshape(batch, seq, 3, NSA_KV_HEADS, NSA_REP)
    gl = gl.transpose(0, 3, 2, 4, 1).reshape(batch, NSA_KV_HEADS, 3 * NSA_REP, seq)
    y_nsa = nsa_attention(qh, kcc, vcct, ksn, vsn, kwn, vwn, gl, batch, seq)
    merged = merge_mixers(y_ssd, y_nsa, w_up_ssd.astype(BF16), w_up_nsa.astype(BF16), mgates)
    h1 = residual_matmul(xf, merged, w_out.astype(BF16))
    mlen = mem.shape[1]
    mn = rmsnorm_rows(mem.reshape(batch * mlen, d), mem_norm_w, tm=min(512, batch * mlen))
    kv = matmul(mn, xkv_w.astype(BF16))
    n_exp = router_e_w.shape[1]
    rpad = LANES - N_GROUPS - n_exp
    router_w = jnp.concatenate([router_g_w, router_e_w, jnp.zeros((d, rpad), F32)], axis=1).astype(BF16)
    router_b = jnp.concatenate([router_g_b, router_e_b, jnp.zeros((rpad,), F32)]).reshape(1, LANES)
    h2, hf, route = xattn_router(h1, kv, norm2_w, xq_w.astype(BF16), x_q_norm_w, x_k_norm_w, xo_w.astype(BF16),
                                 norm3_w, router_w, router_b, batch, seq)
    out = hier_moe(h2, hf, route, moe_w_gate, moe_w_up, moe_w_down)
    return out.reshape(batch, seq, d)


def kernel(x, mem, norm1_w, w_in, ssd_conv_w, ssd_conv_b, ssd_dt_bias, ssd_a_log, ssd_d, ssd_norm_w, nsa_q_norm_w, nsa_k_norm_w, cmp_pe_k, cmp_w1_k, cmp_w2_k, cmp_pe_v, cmp_w1_v, cmp_w2_v, w_up_ssd, w_up_nsa, w_out, norm2_w, mem_norm_w, xq_w, xkv_w, x_q_norm_w, x_k_norm_w, xo_w, norm3_w, router_g_w, router_g_b, router_e_w, router_e_b, moe_w_gate, moe_w_up, moe_w_down):
    h = x
    for l in range(norm1_w.shape[0]):
        h = _layer(h, mem, norm1_w[l], w_in[l:l + 1], ssd_conv_w[l], ssd_conv_b[l], ssd_dt_bias[l], ssd_a_log[l], ssd_d[l],
                   ssd_norm_w[l], nsa_q_norm_w[l], nsa_k_norm_w[l], cmp_pe_k[l], cmp_w1_k[l], cmp_w2_k[l], cmp_pe_v[l],
                   cmp_w1_v[l], cmp_w2_v[l], w_up_ssd[l], w_up_nsa[l], w_out[l], norm2_w[l], mem_norm_w[l], xq_w[l],
                   xkv_w[l], x_q_norm_w[l], x_k_norm_w[l], xo_w[l], norm3_w[l], router_g_w[l], router_g_b[l],
                   router_e_w[l], router_e_b[l], moe_w_gate[l], moe_w_up[l], moe_w_down[l])
    return h.astype(x.dtype)
```

```python
import functools

import numpy as np
import jax
import jax.numpy as jnp
from jax import lax
from jax.experimental import pallas as pl
from jax.experimental.pallas import tpu as pltpu

F32 = jnp.float32
BF16 = jnp.bfloat16

D_MODEL = 2048
SSD_INNER = 4096
SSD_HEAD_DIM = 64
SSD_HEADS = 64
SSD_GROUPS = 8
SSD_STATE = 128
SSD_CONV = 4
SSD_CHUNK = 256
SSD_CHUNKS_PER_STEP = 4
HEADS_PER_GROUP = SSD_HEADS // SSD_GROUPS
GROUP_WIDTH = SSD_INNER // SSD_GROUPS
HEAD_DIM = 128
NSA_HEADS = 16
NSA_KV_HEADS = 4
NSA_REP = NSA_HEADS // NSA_KV_HEADS
NSA_WIDTH = NSA_HEADS * HEAD_DIM
NSA_KV_WIDTH = NSA_KV_HEADS * HEAD_DIM
CMP_BLK = 32
CMP_STRIDE = 16
CMP_HID = 256
SEL_BLK = 64
SEL_SHIFT = 6
N_SEL = 16
WINDOW = 512
NSA_Q_BLK = 256
WIN_TILE = 128
SEL_KEY_TILE = 256
FORCE_SCORE = 1.0e4
X_HEADS = 4
X_WIDTH = X_HEADS * HEAD_DIM
N_GROUPS = 8
EXPERTS_PER_GROUP = 8
EPG_SHIFT = 3
N_EXPERTS = 64
TOP_K = 2
EXPERT_HIDDEN = 1408
MOE_SUB = 128
MOE_ITEM_SUBS = 8
MOE_HID_TILE = 256
ROPE_THETA = 10000.0
EPS = 1e-6
NEG_INF = -1e30
LOG2E = float(np.log2(np.e))
Q_PRESCALE = HEAD_DIM ** -0.5 * LOG2E
SUBLANES = 8
LANES = 128
VMEM_LIMIT = 56 * 1024 * 1024

ZX_COLS = SSD_INNER + SSD_INNER + 2 * SSD_GROUPS * SSD_STATE
QKV_COLS = NSA_WIDTH + 6 * NSA_KV_WIDTH
GATE_COLS = 2 * D_MODEL
SMALL_COLS = LANES
IN_PROJ_TM, IN_PROJ_TN = 2048, 1024


def _params(*sem):
    return pltpu.CompilerParams(dimension_semantics=sem, vmem_limit_bytes=VMEM_LIMIT)


def _rms(x, w):
    ms = jnp.mean(x * x, axis=-1, keepdims=True)
    return x * lax.rsqrt(ms + EPS) * w


def _sigmoid(x):
    return 0.5 + 0.5 * jnp.tanh(0.5 * x)


def _silu(x):
    h = 0.5 * x
    return h + h * jnp.tanh(h)


def _softplus(x):
    return jnp.maximum(x, 0.0) + jnp.log1p(jnp.exp(-jnp.abs(x)))


def _dot(a, b):
    return jnp.dot(a, b, preferred_element_type=F32)


def _dot_nt(a, b):
    return lax.dot_general(a, b, (((1,), (1,)), ((), ())), preferred_element_type=F32)


def _rmsnorm_kernel(x_ref, w_ref, o_ref):
    o_ref[...] = _rms(x_ref[...], w_ref[...]).astype(o_ref.dtype)


def rmsnorm_rows(x, w, tm=512):
    m, d = x.shape
    return pl.pallas_call(
        _rmsnorm_kernel,
        grid=(m // tm,),
        in_specs=[pl.BlockSpec((tm, d), lambda i: (i, 0)), pl.BlockSpec((1, d), lambda i: (0, 0))],
        out_specs=pl.BlockSpec((tm, d), lambda i: (i, 0)),
        out_shape=jax.ShapeDtypeStruct((m, d), BF16),
        compiler_params=_params("parallel"),
        name="rmsnorm_rows",
    )(x, w.reshape(1, d))


def _matmul_kernel(x_ref, w_ref, o_ref):
    o_ref[...] = _dot(x_ref[...], w_ref[...]).astype(o_ref.dtype)


def matmul(x, w, out_dtype=F32, tm=512, tn=512):
    m, k = x.shape
    n = w.shape[1]
    tm, tn = min(tm, m), min(tn, n)
    return pl.pallas_call(
        _matmul_kernel,
        grid=(m // tm, n // tn),
        in_specs=[pl.BlockSpec((tm, k), lambda i, j: (i, 0)), pl.BlockSpec((k, tn), lambda i, j: (0, j))],
        out_specs=pl.BlockSpec((tm, tn), lambda i, j: (i, j)),
        out_shape=jax.ShapeDtypeStruct((m, n), out_dtype),
        compiler_params=_params("parallel", "arbitrary"),
        name="matmul",
    )(x, w)


def _split3(a):
    h = a.astype(BF16)
    r = a - h.astype(F32)
    m = r.astype(BF16)
    return h, m, (r - m.astype(F32)).astype(BF16)


def _ssd_kernel(z_ref, x_ref, b_ref, c_ref, dt_ref, dtt_ref, *rest):
    @pl.when(pl.program_id(2) == 0)
    def _():
        for ref in rest[-4:]:
            ref[...] = jnp.zeros_like(ref)

    L = x_ref.shape[0] // SSD_CHUNKS_PER_STEP
    for cc in range(SSD_CHUNKS_PER_STEP):
        rows = pl.ds(cc * L, L)
        _ssd_chunk(z_ref.at[rows], x_ref.at[rows], b_ref.at[rows], c_ref.at[rows], dt_ref.at[:, rows],
                   dtt_ref.at[:, :, rows], *rest[:9], rest[9].at[rows], *rest[10:])


def _ssd_chunk(z_ref, x_ref, b_ref, c_ref, dt_ref, dtt_ref, cwx_ref, cwb_ref, cwc_ref, cbx_ref, cbb_ref, cbc_ref,
               hp_ref, hpt_ref, nw_ref, y_ref, state_ref, ext_x_ref, ext_b_ref, ext_c_ref):
    L = x_ref.shape[0]
    H = L // 2

    def conv_silu(u_ref, ext_ref, w, b):
        ext_ref[0:SUBLANES, :] = ext_ref[L:L + SUBLANES, :]
        u = u_ref[...]
        ext_ref[SUBLANES:SUBLANES + L, :] = u
        acc = b + w[SSD_CONV - 1:SSD_CONV, :] * u
        for k in range(SSD_CONV - 1):
            acc = acc + w[k:k + 1, :] * ext_ref[pl.ds(SUBLANES - (SSD_CONV - 1) + k, L), :]
        return _silu(acc)

    xs = conv_silu(x_ref, ext_x_ref, cwx_ref[...], cbx_ref[...])
    bm = conv_silu(b_ref, ext_b_ref, cwb_ref[...], cbb_ref[...])
    cm = conv_silu(c_ref, ext_c_ref, cwc_ref[...], cbc_ref[...])
    bm16, cm16 = bm.astype(BF16), cm.astype(BF16)

    hp = hp_ref[0]
    hpt = hpt_ref[0]
    dt = _softplus(dt_ref[0] + hp[0:1, :])
    dtt = _softplus(dtt_ref[0] + hpt[:, 0:1])
    da = dt * (-jnp.exp(hp[1:2, :]))
    dat = dtt * (-jnp.exp(hpt[:, 1:2]))
    row = lax.broadcasted_iota(jnp.int32, (L, L), 0)
    col = lax.broadcasted_iota(jnp.int32, (L, L), 1)
    tril = jnp.where(row >= col, 1.0, 0.0).astype(BF16)
    triu = jnp.where(row <= col, 1.0, 0.0).astype(BF16)
    acum = sum(_dot(tril, part) for part in _split3(da)) * LOG2E
    acumt = sum(_dot(part, triu) for part in _split3(dat)) * LOG2E
    dtet = jnp.exp2(acumt[:, L - 1:L] - acumt)
    eact = jnp.exp2(acumt)
    cdec = jnp.exp2(acum[L - 1:L, :])

    cbt = _dot_nt(bm16, cm16)
    cbt00, cbt01, cbt11 = cbt[:H, :H], cbt[:H, H:], cbt[H:, H:]
    tri = lax.broadcasted_iota(jnp.int32, (H, H), 0) <= lax.broadcasted_iota(jnp.int32, (H, H), 1)
    xst = xs.T
    d_skip = hp[2:3, :]
    P = SSD_HEAD_DIM
    outs = []
    for p in range(HEADS_PER_GROUP // 2):
        hprev = state_ref[p]
        yoff = _dot_nt(hprev.astype(BF16), cm16)
        new_state = []
        for hh in range(2):
            h = 2 * p + hh
            rs = slice(hh * P, (hh + 1) * P)
            xt = xst[h * P:(h + 1) * P, :]
            xdt = xt * dtt[h:h + 1, :]
            xdt16 = xdt.astype(BF16)
            a_col, a_row = acum[:, h:h + 1], acumt[h:h + 1, :]
            d00 = jnp.exp2(jnp.where(tri, a_row[:, :H] - a_col[:H], -jnp.inf))
            d01 = jnp.exp2(a_row[:, H:] - a_col[:H])
            d11 = jnp.exp2(jnp.where(tri, a_row[:, H:] - a_col[H:], -jnp.inf))
            y_l = _dot(xdt16[:, :H], (cbt00 * d00).astype(BF16))
            y_r = _dot(xdt16[:, :H], (cbt01 * d01).astype(BF16)) + _dot(xdt16[:, H:], (cbt11 * d11).astype(BF16))
            y = jnp.concatenate([y_l, y_r], axis=1)
            outs.append(y + yoff[rs] * eact[h:h + 1, :] + d_skip[:, h:h + 1] * xt)
            new_state.append(hprev[rs] * cdec[:, h:h + 1] + _dot((xdt * dtet[h:h + 1, :]).astype(BF16), bm16))
        state_ref[p] = jnp.concatenate(new_state, axis=0)
    y = jnp.concatenate(outs, axis=0).T * _silu(z_ref[...])
    y_ref[...] = _rms(y, nw_ref[...]).astype(y_ref.dtype)


def ssd_branch(zx, small, conv_w, conv_b, dt_bias, a_log, d_skip, norm_w, batch, seq):
    n = batch * seq
    L = min(SSD_CHUNK, seq)
    nc = seq // L
    g, hpg = SSD_GROUPS, HEADS_PER_GROUP
    dt_raw = small[:, :SSD_HEADS].reshape(n, g, hpg)
    dt_g = dt_raw.transpose(1, 0, 2)
    dtt_g = dt_raw.transpose(1, 2, 0)
    hp = jnp.stack([dt_bias, a_log, d_skip], axis=0).reshape(3, g, hpg).transpose(1, 0, 2)
    hp = jnp.pad(hp, ((0, 0), (0, SUBLANES - 3), (0, 0)))
    hpt = hp.transpose(0, 2, 1)
    cw = conv_w
    cbias = conv_b.reshape(1, -1)
    nxb = SSD_INNER // GROUP_WIDTH
    assert nc % SSD_CHUNKS_PER_STEP == 0
    lb, nb = L * SSD_CHUNKS_PER_STEP, nc // SSD_CHUNKS_PER_STEP
    rowblk = lambda b, gi, c: b * nb + c
    bc0 = SSD_INNER // SSD_STATE
    grid = (g, batch, nb)
    in_specs = [
        pl.BlockSpec((lb, GROUP_WIDTH), lambda gi, b, c: (rowblk(b, gi, c), gi)),
        pl.BlockSpec((lb, GROUP_WIDTH), lambda gi, b, c: (rowblk(b, gi, c), nxb + gi)),
        pl.BlockSpec((lb, SSD_STATE), lambda gi, b, c: (rowblk(b, gi, c), 2 * bc0 + gi)),
        pl.BlockSpec((lb, SSD_STATE), lambda gi, b, c: (rowblk(b, gi, c), 2 * bc0 + g + gi)),
        pl.BlockSpec((1, lb, hpg), lambda gi, b, c: (gi, rowblk(b, gi, c), 0)),
        pl.BlockSpec((1, hpg, lb), lambda gi, b, c: (gi, 0, rowblk(b, gi, c))),
        pl.BlockSpec((SSD_CONV, GROUP_WIDTH), lambda gi, b, c: (0, gi)),
        pl.BlockSpec((SSD_CONV, SSD_STATE), lambda gi, b, c: (0, bc0 + gi)),
        pl.BlockSpec((SSD_CONV, SSD_STATE), lambda gi, b, c: (0, bc0 + g + gi)),
        pl.BlockSpec((1, GROUP_WIDTH), lambda gi, b, c: (0, gi)),
        pl.BlockSpec((1, SSD_STATE), lambda gi, b, c: (0, bc0 + gi)),
        pl.BlockSpec((1, SSD_STATE), lambda gi, b, c: (0, bc0 + g + gi)),
        pl.BlockSpec((1, SUBLANES, hpg), lambda gi, b, c: (gi, 0, 0)),
        pl.BlockSpec((1, hpg, SUBLANES), lambda gi, b, c: (gi, 0, 0)),
        pl.BlockSpec((1, GROUP_WIDTH), lambda gi, b, c: (0, gi)),
    ]
    return pl.pallas_call(
        _ssd_kernel,
        grid=grid,
        in_specs=in_specs,
        out_specs=pl.BlockSpec((lb, GROUP_WIDTH), lambda gi, b, c: (rowblk(b, gi, c), gi)),
        out_shape=jax.ShapeDtypeStruct((n, SSD_INNER), BF16),
        scratch_shapes=[pltpu.VMEM((HEADS_PER_GROUP // 2, LANES, SSD_STATE), F32),
                        pltpu.VMEM((L + SUBLANES, GROUP_WIDTH), F32), pltpu.VMEM((L + SUBLANES, SSD_STATE), F32),
                        pltpu.VMEM((L + SUBLANES, SSD_STATE), F32)],
        compiler_params=_params("parallel", "parallel", "arbitrary"),
        name="ssd_branch",
    )(zx, zx, zx, zx, dt_g, dtt_g, cw, cw, cw, cbias, cbias, cbias, hp, hpt, norm_w.reshape(1, -1))


def _nsa_prep_kernel(q_ref, kc_ref, vc_ref, ks_ref, vs_ref, kw_ref, vw_ref, cos_ref, sin_ref, qw_ref, kw3_ref,
                     qo_ref, kco_ref, vco_ref, kso_ref, vso_ref, kwo_ref, vwo_ref):
    cos, sin = cos_ref[...], sin_ref[...]

    def norm_rope(x, w):
        y = _rms(x, w)
        return y * cos + pltpu.roll(y, HEAD_DIM // 2, 1) * sin

    qw = qw_ref[...]
    tt = q_ref.shape[0]
    for g in range(NSA_KV_HEADS):
        sl = slice(g * HEAD_DIM, (g + 1) * HEAD_DIM)
        for r in range(NSA_REP):
            h = g * NSA_REP + r
            qh = norm_rope(q_ref[:, h * HEAD_DIM:(h + 1) * HEAD_DIM], qw) * Q_PRESCALE
            qo_ref[0, g, r] = qh.astype(qo_ref.dtype)
        kco_ref[0, g] = norm_rope(kc_ref[:, sl], kw3_ref[0:1, :])
        vco_ref[0, g] = vc_ref[:, sl]
        kso_ref[0, g] = norm_rope(ks_ref[:, sl], kw3_ref[1:2, :]).astype(kso_ref.dtype)
        kwo_ref[0, g] = norm_rope(kw_ref[:, sl], kw3_ref[2:3, :]).astype(kwo_ref.dtype)
        vst = vs_ref[:, sl].T.astype(vso_ref.dtype)
        for j in range(tt // SEL_KEY_TILE):
            vso_ref[0, g, j] = vst[:, j * SEL_KEY_TILE:(j + 1) * SEL_KEY_TILE]
        vwt = vw_ref[:, sl].T.astype(vwo_ref.dtype)
        for j in range(tt // WIN_TILE):
            vwo_ref[0, g, j] = vwt[:, j * WIN_TILE:(j + 1) * WIN_TILE]


def nsa_prep(qkv, q_norm_w, k_norm_w, batch, seq, tt=2 * SEL_KEY_TILE):
    half = HEAD_DIM // 2
    inv = ROPE_THETA ** (-jnp.arange(half, dtype=F32) / half)
    ang = jnp.arange(seq).astype(F32)[:, None] * inv[None, :]
    cos = jnp.concatenate([jnp.cos(ang), jnp.cos(ang)], axis=-1)
    sin = jnp.concatenate([-jnp.sin(ang), jnp.sin(ang)], axis=-1)
    nt = seq // tt
    kvb = NSA_WIDTH // NSA_KV_WIDTH
    kv_spec = lambda j: pl.BlockSpec((tt, NSA_KV_WIDTH), lambda b, t: (b * nt + t, kvb + j))
    head_spec = pl.BlockSpec((1, NSA_KV_HEADS, tt, HEAD_DIM), lambda b, t: (b, 0, t, 0))
    head_shape = lambda dt: jax.ShapeDtypeStruct((batch, NSA_KV_HEADS, seq, HEAD_DIM), dt)
    return pl.pallas_call(
        _nsa_prep_kernel,
        grid=(batch, nt),
        in_specs=[pl.BlockSpec((tt, NSA_WIDTH), lambda b, t: (b * nt + t, 0))] + [kv_spec(j) for j in range(6)] + [
            pl.BlockSpec((tt, HEAD_DIM), lambda b, t: (t, 0)), pl.BlockSpec((tt, HEAD_DIM), lambda b, t: (t, 0)),
            pl.BlockSpec((1, HEAD_DIM), lambda b, t: (0, 0)), pl.BlockSpec((3, HEAD_DIM), lambda b, t: (0, 0))],
        out_specs=[pl.BlockSpec((1, NSA_KV_HEADS, NSA_REP, tt, HEAD_DIM), lambda b, t: (b, 0, 0, t, 0)),
                   head_spec, head_spec, head_spec,
                   pl.BlockSpec((1, NSA_KV_HEADS, tt // SEL_KEY_TILE, HEAD_DIM, SEL_KEY_TILE),
                                lambda b, t: (b, 0, t, 0, 0)),
                   head_spec,
                   pl.BlockSpec((1, NSA_KV_HEADS, tt // WIN_TILE, HEAD_DIM, WIN_TILE), lambda b, t: (b, 0, t, 0, 0))],
        out_shape=[jax.ShapeDtypeStruct((batch, NSA_KV_HEADS, NSA_REP, seq, HEAD_DIM), BF16),
                   head_shape(F32), head_shape(F32), head_shape(BF16),
                   jax.ShapeDtypeStruct((batch, NSA_KV_HEADS, seq // SEL_KEY_TILE, HEAD_DIM, SEL_KEY_TILE), BF16),
                   head_shape(BF16),
                   jax.ShapeDtypeStruct((batch, NSA_KV_HEADS, seq // WIN_TILE, HEAD_DIM, WIN_TILE), BF16)],
        compiler_params=_params("parallel", "parallel"),
        name="nsa_prep",
    )(qkv, qkv, qkv, qkv, qkv, qkv, qkv, cos, sin, q_norm_w.reshape(1, -1), k_norm_w)


def _compress_kernel(x_ref, pe_ref, w1_ref, w2_ref, o_ref, *, ncmp, transpose_out):
    nseg = x_ref.shape[2] // CMP_STRIDE
    u = jnp.zeros((nseg, CMP_HID), F32)
    v = jnp.zeros((nseg, CMP_HID), F32)
    for j in range(CMP_STRIDE):
        xj = x_ref[0, 0, pl.ds(j, nseg, stride=CMP_STRIDE), :]
        u = u + _dot((xj + pe_ref[j:j + 1, :]).astype(BF16), w1_ref[j])
        v = v + _dot((xj + pe_ref[CMP_STRIDE + j:CMP_STRIDE + j + 1, :]).astype(BF16), w1_ref[CMP_STRIDE + j])
    hid = _silu(u + pltpu.roll(v, nseg - 1, 0))
    comp = _dot(hid.astype(BF16), w2_ref[...])
    rowi = lax.broadcasted_iota(jnp.int32, comp.shape, 0)
    comp = jnp.where(rowi < ncmp, comp, 0.0)
    o_ref[0, 0] = (comp.T if transpose_out else comp).astype(o_ref.dtype)


def compress(raw, pe, w1, w2, transpose_out):
    b, g, t, hd = raw.shape
    nseg = t // CMP_STRIDE
    full = lambda shape: pl.BlockSpec(shape, lambda i, j: (0,) * len(shape))
    out_dims = (hd, nseg) if transpose_out else (nseg, hd)
    return pl.pallas_call(
        functools.partial(_compress_kernel, ncmp=nseg - 1, transpose_out=transpose_out),
        grid=(b, g),
        in_specs=[pl.BlockSpec((1, 1, t, hd), lambda i, j: (i, j, 0, 0)), full((CMP_BLK, hd)),
                  full((CMP_BLK, hd, CMP_HID)), full((CMP_HID, hd))],
        out_specs=pl.BlockSpec((1, 1) + out_dims, lambda i, j: (i, j, 0, 0)),
        out_shape=jax.ShapeDtypeStruct((b, g) + out_dims, BF16),
        compiler_params=_params("parallel", "parallel"),
        name="nsa_compress",
    )(raw, pe, w1.astype(BF16), w2.astype(BF16))


def _nsa_attn_kernel(q_ref, kc_ref, vct_ref, ks_ref, vst_ref, kw_ref, vwt_ref, glt_ref, ovt_ref, o_ref,
                     selb_ref, acc_ref, ocw_ref, s_ref, p_ref, *, seq, n_sel):
    Q = q_ref.shape[3]
    q0 = pl.program_id(2) * Q
    rows = NSA_REP * Q
    q = q_ref[0, 0].reshape(rows, HEAD_DIM)
    tq = q0 + lax.broadcasted_iota(jnp.int32, (1, Q), 1)
    head = lambda r: slice(r * Q, (r + 1) * Q)

    ncp = kc_ref.shape[2]
    wk = WINDOW + Q
    kt_sz = vst_ref.shape[4]
    start = pl.multiple_of(jnp.clip(q0 - WINDOW, 0, seq - wk), WIN_TILE)
    s_cw = _dot_nt(jnp.concatenate([kc_ref[0, 0], kw_ref[0, 0, pl.ds(start, wk), :], ks_ref[0, 0, 0:kt_sz, :]],
                                   axis=0), q)
    s_c, s_w = s_cw[:ncp], s_cw[ncp:ncp + wk]
    s_ref[0] = s_cw[ncp + wk:]

    cend = lax.broadcasted_iota(jnp.int32, (ncp, 1), 0) * CMP_STRIDE + (CMP_BLK - 1)
    m_c = cend <= tq
    ps = []
    for r in range(NSA_REP):
        s = jnp.where(m_c, s_c[:, head(r)], NEG_INF)
        e = jnp.exp2(s - jnp.max(s, axis=0, keepdims=True))
        p = jnp.where(m_c, e * (1.0 / jnp.sum(e, axis=0, keepdims=True)), 0.0)
        ps.append(p.astype(BF16))
    p_c = jnp.concatenate(ps, axis=1)
    nsb = ovt_ref.shape[0]
    oc_imp = _dot(jnp.concatenate([vct_ref[0, 0], ovt_ref[...]], axis=0), p_c)
    o_c, imp4 = oc_imp[:HEAD_DIM], oc_imp[HEAD_DIM:]

    imp = imp4[:, head(0)]
    for r in range(1, NSA_REP):
        imp = imp + imp4[:, head(r)]
    blk_t = (q0 + lax.broadcasted_iota(jnp.int32, (nsb, Q), 1)) >> SEL_SHIFT
    sb = lax.broadcasted_iota(jnp.int32, (nsb, Q), 0)
    forced = (sb == 0) | (sb == blk_t) | (sb == blk_t - 1)
    imp = jnp.where(sb <= blk_t, jnp.where(forced, FORCE_SCORE, imp), -jnp.inf)
    rank = jnp.zeros((nsb, Q), F32)
    for i in range(nsb):
        ri = imp[i:i + 1, :]
        beats = (ri > imp) | ((ri == imp) & (sb > i))
        rank = rank + jnp.where(beats, 1.0, 0.0)
    selb_ref[...] = jnp.where(rank < n_sel, 0.0, NEG_INF)

    dist = tq - (start + lax.broadcasted_iota(jnp.int32, (wk, 1), 0))
    bias_w = jnp.where((dist >= 0) & (dist < WINDOW), 0.0, NEG_INF)
    ps, ls = [], []
    for r in range(NSA_REP):
        s = s_w[:, head(r)] + bias_w
        p = jnp.exp2(s - jnp.max(s, axis=0, keepdims=True))
        ls.append(jnp.sum(p, axis=0, keepdims=True))
        ps.append(p.astype(BF16))
    t0 = start // WIN_TILE
    vw_t = jnp.concatenate([vwt_ref[0, 0, t0 + j] for j in range(wk // WIN_TILE)], axis=1)
    o_w = _dot(vw_t, jnp.concatenate(ps, axis=1)) * (1.0 / jnp.concatenate(ls, axis=1))

    gate = _sigmoid(glt_ref[0, 0])
    for r in range(NSA_REP):
        ocw_ref[:, head(r)] = (gate[r:r + 1, :] * o_c[:, head(r)]
                               + gate[2 * NSA_REP + r:2 * NSA_REP + r + 1, :] * o_w[:, head(r)])

    blocks_per_tile = kt_sz // SEL_BLK
    n_all = seq // kt_sz
    acc_ref[...] = jnp.zeros_like(acc_ref)
    p_ref[1] = jnp.zeros(p_ref.shape[1:], p_ref.dtype)

    def sel_tile(kt, carry):
        m, l, alpha_prev = carry
        slot = kt & 1
        s_all = s_ref[slot]
        p_prev = p_ref[1 - slot]
        k_next = pl.multiple_of(jnp.minimum(kt + 1, n_all - 1) * kt_sz, kt_sz)
        s_ref[1 - slot] = _dot_nt(ks_ref[0, 0, pl.ds(k_next, kt_sz), :], q)
        pv_prev = _dot(vst_ref[0, 0, jnp.maximum(kt - 1, 0)], p_prev)
        acc_ref[...] = acc_ref[...] * alpha_prev + pv_prev
        k0 = kt * kt_sz
        kpos = k0 + lax.broadcasted_iota(jnp.int32, (kt_sz, 1), 0)
        bias = jnp.concatenate(
            [jnp.broadcast_to(selb_ref[pl.ds(kt * blocks_per_tile + i, 1), :], (SEL_BLK, Q))
             for i in range(blocks_per_tile)], axis=0)
        bias = jnp.where(kpos <= tq, bias, NEG_INF)
        ps, ms, ls, alphas = [], [], [], []
        for r in range(NSA_REP):
            s = s_all[:, head(r)] + bias
            m_old = m[:, head(r)]
            m_new = jnp.maximum(m_old, jnp.max(s, axis=0, keepdims=True))
            p = jnp.exp2(s - m_new)
            alpha = jnp.exp2(m_old - m_new)
            ls.append(alpha * l[:, head(r)] + jnp.sum(p, axis=0, keepdims=True))
            ms.append(m_new)
            alphas.append(alpha)
            ps.append(p.astype(BF16))
        p_ref[slot] = jnp.concatenate(ps, axis=1)
        return jnp.concatenate(ms, axis=1), jnp.concatenate(ls, axis=1), jnp.concatenate(alphas, axis=1)

    n_tiles = (q0 + Q + kt_sz - 1) // kt_sz
    init = (jnp.full((1, rows), NEG_INF, F32), jnp.zeros((1, rows), F32), jnp.ones((1, rows), F32))
    _, l_s, alpha_last = lax.fori_loop(0, n_tiles, sel_tile, init)
    pv_last = _dot(vst_ref[0, 0, n_tiles - 1], p_ref[(n_tiles - 1) & 1])
    o_s = (acc_ref[...] * alpha_last + pv_last) * (1.0 / l_s)

    for r in range(NSA_REP):
        out_t = ocw_ref[:, head(r)] + gate[NSA_REP + r:NSA_REP + r + 1, :] * o_s[:, head(r)]
        o_ref[:, r * HEAD_DIM:(r + 1) * HEAD_DIM] = out_t.T.astype(o_ref.dtype)


def nsa_attention(qh, kcc, vcct, ksn, vst, kwn, vwt, gate_logits_t, batch, seq):
    Q_BLK = NSA_Q_BLK
    assert seq % SEL_KEY_TILE == 0 and seq % Q_BLK == 0 and seq >= WINDOW + Q_BLK
    nq = seq // Q_BLK
    nsb = seq // SEL_BLK
    n_sel = min(N_SEL, nsb)
    ncp = kcc.shape[2]
    ncmp = (seq - CMP_BLK) // CMP_STRIDE + 1
    ci = np.arange(ncp)[None, :]
    sj = np.arange(nsb)[:, None]
    ovt = ((ci * CMP_STRIDE < (sj + 1) * SEL_BLK) & (ci * CMP_STRIDE + CMP_BLK > sj * SEL_BLK) & (ci < ncmp))
    ovt = jnp.asarray(ovt, BF16)
    seq_spec = pl.BlockSpec((1, 1, seq, HEAD_DIM), lambda b, g, i: (b, g, 0, 0))
    cmp_spec = pl.BlockSpec((1, 1, ncp, HEAD_DIM), lambda b, g, i: (b, g, 0, 0))
    tiled = lambda a: pl.BlockSpec((1, 1) + a.shape[2:], lambda b, g, i: (b, g, 0, 0, 0))
    return pl.pallas_call(
        functools.partial(_nsa_attn_kernel, seq=seq, n_sel=n_sel),
        grid=(batch, NSA_KV_HEADS, nq),
        in_specs=[pl.BlockSpec((1, 1, NSA_REP, Q_BLK, HEAD_DIM), lambda b, g, i: (b, g, 0, i, 0)),
                  cmp_spec, pl.BlockSpec((1, 1, HEAD_DIM, ncp), lambda b, g, i: (b, g, 0, 0)),
                  seq_spec, tiled(vst), seq_spec, tiled(vwt),
                  pl.BlockSpec((1, 1, 3 * NSA_REP, Q_BLK), lambda b, g, i: (b, g, 0, i)),
                  pl.BlockSpec((nsb, ncp), lambda b, g, i: (0, 0))],
        out_specs=pl.BlockSpec((Q_BLK, NSA_REP * HEAD_DIM), lambda b, g, i: (b * nq + i, g)),
        out_shape=jax.ShapeDtypeStruct((batch * seq, NSA_WIDTH), BF16),
        scratch_shapes=[pltpu.VMEM((nsb, Q_BLK), F32), pltpu.VMEM((HEAD_DIM, NSA_REP * Q_BLK), F32),
                        pltpu.VMEM((HEAD_DIM, NSA_REP * Q_BLK), F32),
                        pltpu.VMEM((2, SEL_KEY_TILE, NSA_REP * Q_BLK), F32),
                        pltpu.VMEM((2, SEL_KEY_TILE, NSA_REP * Q_BLK), BF16)],
        compiler_params=_params("parallel", "parallel", "arbitrary"),
        name="nsa_attention",
    )(qh, kcc, vcct, ksn, vst, kwn, vwt, gate_logits_t, ovt)


def _merge_kernel(ys_ref, yn_ref, ws_ref, wn_ref, gs_ref, gn_ref, o_ref):
    up_s = _dot(ys_ref[...], ws_ref[...])
    up_n = _dot(yn_ref[...], wn_ref[...])
    o_ref[...] = (_sigmoid(gs_ref[...]) * up_s + _sigmoid(gn_ref[...]) * up_n).astype(o_ref.dtype)


def merge_mixers(y_ssd, y_nsa, w_up_ssd, w_up_nsa, gates, tm=1024, tn=512):
    m = y_ssd.shape[0]
    nb = D_MODEL // tn
    return pl.pallas_call(
        _merge_kernel,
        grid=(m // tm, nb),
        in_specs=[pl.BlockSpec((tm, SSD_INNER), lambda i, j: (i, 0)), pl.BlockSpec((tm, NSA_WIDTH), lambda i, j: (i, 0)),
                  pl.BlockSpec((SSD_INNER, tn), lambda i, j: (0, j)), pl.BlockSpec((NSA_WIDTH, tn), lambda i, j: (0, j)),
                  pl.BlockSpec((tm, tn), lambda i, j: (i, j)), pl.BlockSpec((tm, tn), lambda i, j: (i, nb + j))],
        out_specs=pl.BlockSpec((tm, tn), lambda i, j: (i, j)),
        out_shape=jax.ShapeDtypeStruct((m, D_MODEL), BF16),
        compiler_params=_params("parallel", "arbitrary"),
        name="merge_mixers",
    )(y_ssd, y_nsa, w_up_ssd, w_up_nsa, gates, gates)


def _residual_matmul_kernel(h_ref, x_ref, w_ref, o_ref):
    o_ref[...] = h_ref[...] + _dot(x_ref[...], w_ref[...])


def residual_matmul(h, x, w, tm=1024, tn=1024):
    m, k = x.shape
    n = w.shape[1]
    return pl.pallas_call(
        _residual_matmul_kernel,
        grid=(m // tm, n // tn),
        in_specs=[pl.BlockSpec((tm, tn), lambda i, j: (i, j)), pl.BlockSpec((tm, k), lambda i, j: (i, 0)),
                  pl.BlockSpec((k, tn), lambda i, j: (0, j))],
        out_specs=pl.BlockSpec((tm, tn), lambda i, j: (i, j)),
        out_shape=jax.ShapeDtypeStruct((m, n), F32),
        compiler_params=_params("parallel", "arbitrary"),
        name="residual_matmul",
    )(h, x, w)


def _xattn_router_kernel(h_ref, kv_ref, n2_ref, wq_ref, qn_ref, kn_ref, wo_ref, n3_ref, rw_ref, rb_ref,
                         h2_ref, hf_ref, rt_ref):
    h = h_ref[...]
    qp = _dot(_rms(h, n2_ref[...]).astype(BF16), wq_ref[...])
    kv = kv_ref[...]
    scale = HEAD_DIM ** -0.5
    heads = []
    for hd in range(X_HEADS):
        sl = slice(hd * HEAD_DIM, (hd + 1) * HEAD_DIM)
        qh = _rms(qp[:, sl], qn_ref[...]).astype(BF16)
        kh = _rms(kv[:, sl], kn_ref[...]).astype(BF16)
        vh = kv[:, X_WIDTH + hd * HEAD_DIM:X_WIDTH + (hd + 1) * HEAD_DIM].astype(BF16)
        s = _dot_nt(qh, kh) * scale
        e = jnp.exp(s - jnp.max(s, axis=-1, keepdims=True))
        p = e / jnp.sum(e, axis=-1, keepdims=True)
        heads.append(_dot(p.astype(BF16), vh))
    o = jnp.concatenate(heads, axis=1).astype(BF16)
    h2 = h + _dot(o, wo_ref[...])
    h2_ref[...] = h2
    hf = _rms(h2, n3_ref[...]).astype(BF16)
    hf_ref[...] = hf
    lg = _dot(hf, rw_ref[...]) + rb_ref[...]

    lane = lax.broadcasted_iota(jnp.int32, lg.shape, 1)
    rmax = lambda v: jnp.max(v, axis=-1, keepdims=True)
    rsum = lambda v: jnp.sum(v, axis=-1, keepdims=True)
    first_lane = lambda hit: jnp.min(jnp.where(hit, lane, LANES), axis=-1, keepdims=True)
    is_g = lane < N_GROUPS
    eg = jnp.where(is_g, jnp.exp(lg - rmax(jnp.where(is_g, lg, -jnp.inf))), 0.0)
    pg = eg / rsum(eg)
    pg_top = rmax(pg)
    grp = first_lane(is_g & (pg == pg_top))
    in_grp = (lane >= N_GROUPS) & (((lane - N_GROUPS) >> EPG_SHIFT) == grp)
    ee = jnp.where(in_grp, jnp.exp(lg - rmax(jnp.where(in_grp, lg, -jnp.inf))), 0.0)
    pe = jnp.where(in_grp, ee / rsum(ee), -1.0)
    p1 = rmax(pe)
    i1 = first_lane(pe == p1)
    pe2 = jnp.where(lane == i1, -1.0, pe)
    p2 = rmax(pe2)
    i2 = first_lane(pe2 == p2)
    psum = p1 + p2
    route = jnp.where(lane == 0, (i1 - N_GROUPS).astype(F32),
                      jnp.where(lane == 1, (i2 - N_GROUPS).astype(F32),
                                jnp.where(lane == 2, pg_top * p1 / psum,
                                          jnp.where(lane == 3, pg_top * p2 / psum, 0.0))))
    rt_ref[...] = route


def xattn_router(h1, kv, norm2_w, wq, q_norm_w, k_norm_w, wo, norm3_w, router_w, router_b, batch, seq, tm=512):
    n = batch * seq
    tm = min(tm, seq)
    per_seq = seq // tm
    mlen = kv.shape[0] // batch
    full = lambda shape: pl.BlockSpec(shape, lambda i: (0,) * len(shape))
    return pl.pallas_call(
        _xattn_router_kernel,
        grid=(n // tm,),
        in_specs=[pl.BlockSpec((tm, D_MODEL), lambda i: (i, 0)),
                  pl.BlockSpec((mlen, 2 * X_WIDTH), lambda i: (i // per_seq, 0)),
                  full((1, D_MODEL)), full((D_MODEL, X_WIDTH)), full((1, HEAD_DIM)), full((1, HEAD_DIM)),
                  full((X_WIDTH, D_MODEL)), full((1, D_MODEL)), full((D_MODEL, LANES)), full((1, LANES))],
        out_specs=[pl.BlockSpec((tm, D_MODEL), lambda i: (i, 0)), pl.BlockSpec((tm, D_MODEL), lambda i: (i, 0)),
                   pl.BlockSpec((tm, LANES), lambda i: (i, 0))],
        out_shape=[jax.ShapeDtypeStruct((n, D_MODEL), F32), jax.ShapeDtypeStruct((n, D_MODEL), BF16),
                   jax.ShapeDtypeStruct((n, LANES), F32)],
        compiler_params=_params("parallel"),
        name="xattn_router",
    )(h1, kv, norm2_w.reshape(1, -1), wq, q_norm_w.reshape(1, -1), k_norm_w.reshape(1, -1), wo,
      norm3_w.reshape(1, -1), router_w, router_b)


def _moe_rank_kernel(e_ref, rank_ref, cnt_ref, carry_ref):
    @pl.when(pl.program_id(0) == 0)
    def _():
        carry_ref[...] = jnp.zeros_like(carry_ref)

    t = e_ref.shape[0]
    hit = e_ref[...] == lax.broadcasted_iota(jnp.int32, (t, LANES), 1)
    onehot = jnp.where(hit, 1.0, 0.0).astype(BF16)
    earlier = lax.broadcasted_iota(jnp.int32, (t, t), 0) > lax.broadcasted_iota(jnp.int32, (t, t), 1)
    before = _dot(jnp.where(earlier, 1.0, 0.0).astype(BF16), onehot) + carry_ref[...]
    rank_ref[...] = jnp.sum(jnp.where(hit, before, 0.0), axis=-1, keepdims=True).astype(jnp.int32)
    carry_ref[...] += jnp.sum(jnp.where(hit, 1.0, 0.0), axis=0, keepdims=True)
    cnt_ref[...] = carry_ref[...]


def moe_rank(eid, t=1024):
    a = eid.shape[0]
    return pl.pallas_call(
        _moe_rank_kernel,
        grid=(a // t,),
        in_specs=[pl.BlockSpec((t, 1), lambda i: (i, 0))],
        out_specs=[pl.BlockSpec((t, 1), lambda i: (i, 0)), pl.BlockSpec((1, LANES), lambda i: (0, 0))],
        out_shape=[jax.ShapeDtypeStruct((a, 1), jnp.int32), jax.ShapeDtypeStruct((1, LANES), F32)],
        scratch_shapes=[pltpu.VMEM((1, LANES), F32)],
        compiler_params=_params("arbitrary"),
        name="moe_rank",
    )(eid)


def _moe_ffn_kernel(item_e_ref, item_row_ref, item_nsub_ref, item_nout_ref, x_hbm, wg_ref, wu_ref, wd_ref, y_hbm,
                    x_vmem, y_vmem, sem_x, sem_y, *, hid):
    i, c = pl.program_id(0), pl.program_id(1)
    last_c = pl.num_programs(1) - 1
    th = wg_ref.shape[2]
    nsub, nout = item_nsub_ref[i], item_nout_ref[i]
    row0 = pl.multiple_of(item_row_ref[i], MOE_SUB)
    sub = lambda s: pl.ds(s * MOE_SUB, MOE_SUB)

    def x_copy(item, s):
        row = pl.multiple_of(item_row_ref[item], MOE_SUB)
        return pltpu.make_async_copy(x_hbm.at[pl.ds(row + s * MOE_SUB, MOE_SUB)], x_vmem.at[item & 1, sub(s)],
                                     sem_x.at[item & 1, s])

    def y_copy(row, s):
        return pltpu.make_async_copy(y_vmem.at[sub(s)], y_hbm.at[pl.ds(row + s * MOE_SUB, MOE_SUB)], sem_y.at[s])

    def for_subs(count, fn):
        for s in range(MOE_ITEM_SUBS):
            pl.when(s < count)(functools.partial(fn, s))

    @pl.when(c == 0)
    def _():
        @pl.when(i > 0)
        def _():
            prev_row = pl.multiple_of(item_row_ref[i - 1], MOE_SUB)
            for_subs(item_nout_ref[i - 1], lambda s: y_copy(prev_row, s).wait())

        @pl.when(i == 0)
        def _():
            for_subs(nsub, lambda s: x_copy(i, s).start())

        for_subs(nsub, lambda s: x_copy(i, s).wait())

        @pl.when(i + 1 < pl.num_programs(0))
        def _():
            nxt = jnp.minimum(i + 1, pl.num_programs(0) - 1)
            for_subs(item_nsub_ref[nxt], lambda s: x_copy(nxt, s).start())

        @pl.when(nsub == 0)
        def _():
            y_vmem[...] = jnp.zeros_like(y_vmem)
            for_subs(nout, lambda s: y_copy(row0, s).start())

    def ffn(count):
        rows = pl.ds(0, count * MOE_SUB)
        col_ok = c * th + lax.broadcasted_iota(jnp.int32, (1, th), 1) < hid
        row_ok = c * th + lax.broadcasted_iota(jnp.int32, (th, 1), 0) < hid
        wgu = jnp.concatenate([wg_ref[0], wu_ref[0]], axis=1).astype(BF16)
        wd = jnp.where(row_ok, wd_ref[0], 0.0).astype(BF16)
        gu = _dot(x_vmem[i & 1, rows, :], wgu)
        act = jnp.where(col_ok, _silu(gu[:, :th]) * gu[:, th:], 0.0)
        part = _dot(act.astype(BF16), wd)

        @pl.when(c == 0)
        def _():
            y_vmem[rows, :] = part

        @pl.when(c > 0)
        def _():
            y_vmem[rows, :] += part

    for count in range(1, MOE_ITEM_SUBS + 1):
        pl.when(nsub == count)(functools.partial(ffn, count))

    @pl.when((nsub > 0) & (c == last_c))
    def _():
        for_subs(nsub, lambda s: y_copy(row0, s).start())

    @pl.when((i == pl.num_programs(0) - 1) & (c == last_c))
    def _():
        for_subs(nout, lambda s: y_copy(row0, s).wait())


def moe_ffn(x_buf, item_e, item_row, item_nsub, item_nout, w_gate, w_up, w_down):
    rows, d = x_buf.shape
    hid = w_gate.shape[2]
    th = MOE_HID_TILE
    nc = pl.cdiv(hid, th)
    tile = lambda i, c, ns: jnp.where(ns[i] > 0, c, nc - 1)
    grid_spec = pltpu.PrefetchScalarGridSpec(
        num_scalar_prefetch=4,
        grid=(item_e.shape[0], nc),
        in_specs=[pl.BlockSpec(memory_space=pl.ANY),
                  pl.BlockSpec((1, d, th), lambda i, c, ie, ir, ns, no: (ie[i], 0, tile(i, c, ns))),
                  pl.BlockSpec((1, d, th), lambda i, c, ie, ir, ns, no: (ie[i], 0, tile(i, c, ns))),
                  pl.BlockSpec((1, th, d), lambda i, c, ie, ir, ns, no: (ie[i], tile(i, c, ns), 0))],
        out_specs=pl.BlockSpec(memory_space=pl.ANY),
        scratch_shapes=[pltpu.VMEM((2, MOE_ITEM_SUBS * MOE_SUB, d), BF16),
                        pltpu.VMEM((MOE_ITEM_SUBS * MOE_SUB, d), F32),
                        pltpu.SemaphoreType.DMA((2, MOE_ITEM_SUBS)), pltpu.SemaphoreType.DMA((MOE_ITEM_SUBS,))],
    )
    return pl.pallas_call(
        functools.partial(_moe_ffn_kernel, hid=hid),
        grid_spec=grid_spec,
        out_shape=jax.ShapeDtypeStruct((rows, d), F32),
        compiler_params=_params("arbitrary", "arbitrary"),
        name="moe_ffn",
    )(item_e, item_row, item_nsub, item_nout, x_buf, w_gate, w_up, w_down)


def hier_moe(h2, hf, route, w_gate, w_up, w_down):
    n, d = h2.shape
    n_exp = w_gate.shape[0]
    i32 = jnp.int32
    eid = route[:, 0:TOP_K].astype(i32).reshape(-1)
    wts = route[:, TOP_K:2 * TOP_K]
    n_assign = n * TOP_K
    rank, counts = moe_rank(eid.reshape(n_assign, 1))
    counts = counts[0, :n_exp].astype(i32)
    subs_e = (counts + MOE_SUB - 1) // MOE_SUB
    sub_end = jnp.cumsum(subs_e)
    row_start = (sub_end - subs_e) * MOE_SUB
    is_e = eid[:, None] == jnp.arange(n_exp, dtype=i32)[None, :]
    dest = rank[:, 0] + jnp.sum(jnp.where(is_e, row_start[None, :], 0), axis=1)
    n_subs = (n_assign + n_exp * (MOE_SUB - 1) + MOE_SUB - 1) // MOE_SUB
    rows = n_subs * MOE_SUB
    tok = jnp.repeat(jnp.arange(n, dtype=i32), TOP_K)
    tok_buf = (jnp.arange(rows, dtype=i32) % n).at[dest].set(tok)
    per = MOE_ITEM_SUBS
    n_items = (n_exp * (per - 1) + n_subs + per - 1) // per
    items_e = (subs_e + per - 1) // per
    item_end = jnp.cumsum(items_e)
    n_used = item_end[-1]
    idx = jnp.arange(n_items, dtype=i32)
    used = idx < n_used
    e_of = jnp.minimum(jnp.searchsorted(item_end, idx, side='right'), n_exp - 1).astype(i32)
    j = idx - (item_end - items_e)[e_of]
    item_nsub = jnp.where(used, jnp.clip(subs_e[e_of] - per * j, 0, per), 0)
    fill_first = jnp.minimum(sub_end[-1] + per * (idx - n_used), n_subs)
    item_fill = jnp.where(used, 0, jnp.minimum(n_subs - fill_first, per))
    item_row = jnp.where(used, row_start[e_of] + j * per * MOE_SUB, jnp.minimum(fill_first, n_subs - 1) * MOE_SUB)
    item_e = jnp.where(used, e_of, e_of[jnp.maximum(n_used - 1, 0)])
    y_buf = moe_ffn(hf[tok_buf], item_e.astype(i32), item_row.astype(i32), item_nsub.astype(i32),
                    (item_nsub + item_fill).astype(i32), w_gate, w_up, w_down)
    slot = dest.reshape(n, TOP_K)
    return h2 + wts[:, 0:1] * y_buf[slot[:, 0]] + wts[:, 1:2] * y_buf[slot[:, 1]]


COL_DT = ZX_COLS
COL_QKV = COL_DT + SSD_HEADS
COL_NSA_GATE = COL_QKV + QKV_COLS
COL_MERGE_GATE = COL_NSA_GATE + 3 * NSA_HEADS


def _small_w_in(w_in):
    pad = jnp.zeros(w_in.shape[:2] + (SMALL_COLS - SSD_HEADS - 3 * NSA_HEADS,), w_in.dtype)
    return jnp.concatenate([w_in[:, :, COL_DT:COL_QKV], w_in[:, :, COL_NSA_GATE:COL_MERGE_GATE], pad], axis=2)


def _layer(x, mem, norm1_w, w_in, ssd_conv_w, ssd_conv_b, ssd_dt_bias, ssd_a_log, ssd_d, ssd_norm_w,
           nsa_q_norm_w, nsa_k_norm_w, cmp_pe_k, cmp_w1_k, cmp_w2_k, cmp_pe_v, cmp_w1_v, cmp_w2_v,
           w_up_ssd, w_up_nsa, w_out, norm2_w, mem_norm_w, xq_w, xkv_w, x_q_norm_w, x_k_norm_w, xo_w,
           norm3_w, router_g_w, router_g_b, router_e_w, router_e_b, moe_w_gate, moe_w_up, moe_w_down):
    batch, seq, d = x.shape
    n = batch * seq
    xf = x.reshape(n, d)
    hn = rmsnorm_rows(xf, norm1_w)
    w_cols = lambda c0, nc: w_in[0, :, c0:c0 + nc].astype(BF16)
    zx = matmul(hn, w_cols(0, ZX_COLS), tm=IN_PROJ_TM, tn=IN_PROJ_TN)
    qkv = matmul(hn, w_cols(COL_QKV, QKV_COLS), tm=IN_PROJ_TM, tn=IN_PROJ_TN)
    mgates = matmul(hn, w_cols(COL_MERGE_GATE, GATE_COLS), tm=IN_PROJ_TM, tn=IN_PROJ_TN)
    small = matmul(hn, _small_w_in(w_in)[0].astype(BF16), tm=IN_PROJ_TM)
    y_ssd = ssd_branch(zx, small, ssd_conv_w, ssd_conv_b, ssd_dt_bias, ssd_a_log, ssd_d, ssd_norm_w, batch, seq)
    qh, kcn, vcr, ksn, vsn, kwn, vwn = nsa_prep(qkv, nsa_q_norm_w, nsa_k_norm_w, batch, seq)
    kcc = compress(kcn, cmp_pe_k, cmp_w1_k, cmp_w2_k, transpose_out=False)
    vcct = compress(vcr, cmp_pe_v, cmp_w1_v, cmp_w2_v, transpose_out=True)
    gl = small[:, SSD_HEADS:SSD_HEADS + 3 * NSA_HEADS].reshape(batch, seq, 3, NSA_KV_HEADS, NSA_REP)
    gl = gl.transpose(0, 3, 2, 4, 1).reshape(batch, NSA_KV_HEADS, 3 * NSA_REP, seq)
    y_nsa = nsa_attention(qh, kcc, vcct, ksn, vsn, kwn, vwn, gl, batch, seq)
    merged = merge_mixers(y_ssd, y_nsa, w_up_ssd.astype(BF16), w_up_nsa.astype(BF16), mgates)
    h1 = residual_matmul(xf, merged, w_out.astype(BF16))
    mlen = mem.shape[1]
    mn = rmsnorm_rows(mem.reshape(batch * mlen, d), mem_norm_w, tm=min(512, batch * mlen))
    kv = matmul(mn, xkv_w.astype(BF16))
    n_exp = router_e_w.shape[1]
    rpad = LANES - N_GROUPS - n_exp
    router_w = jnp.concatenate([router_g_w, router_e_w, jnp.zeros((d, rpad), F32)], axis=1).astype(BF16)
    router_b = jnp.concatenate([router_g_b, router_e_b, jnp.zeros((rpad,), F32)]).reshape(1, LANES)
    h2, hf, route = xattn_router(h1, kv, norm2_w, xq_w.astype(BF16), x_q_norm_w, x_k_norm_w, xo_w.astype(BF16),
                                 norm3_w, router_w, router_b, batch, seq)
    out = hier_moe(h2, hf, route, moe_w_gate, moe_w_up, moe_w_down)
    return out.reshape(batch, seq, d)


def kernel(x, mem, norm1_w, w_in, ssd_conv_w, ssd_conv_b, ssd_dt_bias, ssd_a_log, ssd_d, ssd_norm_w, nsa_q_norm_w, nsa_k_norm_w, cmp_pe_k, cmp_w1_k, cmp_w2_k, cmp_pe_v, cmp_w1_v, cmp_w2_v, w_up_ssd, w_up_nsa, w_out, norm2_w, mem_norm_w, xq_w, xkv_w, x_q_norm_w, x_k_norm_w, xo_w, norm3_w, router_g_w, router_g_b, router_e_w, router_e_b, moe_w_gate, moe_w_up, moe_w_down):
    h = x
    for l in range(norm1_w.shape[0]):
        h = _layer(h, mem, norm1_w[l], w_in[l:l + 1], ssd_conv_w[l], ssd_conv_b[l], ssd_dt_bias[l], ssd_a_log[l], ssd_d[l],
                   ssd_norm_w[l], nsa_q_norm_w[l], nsa_k_norm_w[l], cmp_pe_k[l], cmp_w1_k[l], cmp_w2_k[l], cmp_pe_v[l],
                   cmp_w1_v[l], cmp_w2_v[l], w_up_ssd[l], w_up_nsa[l], w_out[l], norm2_w[l], mem_norm_w[l], xq_w[l],
                   xkv_w[l], x_q_norm_w[l], x_k_norm_w[l], xo_w[l], norm3_w[l], router_g_w[l], router_g_b[l],
                   router_e_w[l], router_e_b[l], moe_w_gate[l], moe_w_up[l], moe_w_down[l])
    return h.astype(x.dtype)
```

```python
import functools

import numpy as np
import jax
import jax.numpy as jnp
from jax import lax
from jax.experimental import pallas as pl
from jax.experimental.pallas import tpu as pltpu

F32 = jnp.float32
BF16 = jnp.bfloat16

D_MODEL = 2048
SSD_INNER = 4096
SSD_HEAD_DIM = 64
SSD_HEADS = 64
SSD_GROUPS = 8
SSD_STATE = 128
SSD_CONV = 4
SSD_CHUNK = 256
SSD_CHUNKS_PER_STEP = 4
HEADS_PER_GROUP = SSD_HEADS // SSD_GROUPS
GROUP_WIDTH = SSD_INNER // SSD_GROUPS
HEAD_DIM = 128
NSA_HEADS = 16
NSA_KV_HEADS = 4
NSA_REP = NSA_HEADS // NSA_KV_HEADS
NSA_WIDTH = NSA_HEADS * HEAD_DIM
NSA_KV_WIDTH = NSA_KV_HEADS * HEAD_DIM
CMP_BLK = 32
CMP_STRIDE = 16
CMP_HID = 256
SEL_BLK = 64
SEL_SHIFT = 6
N_SEL = 16
WINDOW = 512
NSA_Q_BLK = 256
WIN_TILE = 128
SEL_KEY_TILE = 256
FORCE_SCORE = 1.0e4
X_HEADS = 4
X_WIDTH = X_HEADS * HEAD_DIM
N_GROUPS = 8
EXPERTS_PER_GROUP = 8
EPG_SHIFT = 3
N_EXPERTS = 64
TOP_K = 2
EXPERT_HIDDEN = 1408
MOE_SUB = 128
MOE_ITEM_SUBS = 8
MOE_HID_TILE = 256
ROPE_THETA = 10000.0
EPS = 1e-6
NEG_INF = -1e30
LOG2E = float(np.log2(np.e))
Q_PRESCALE = HEAD_DIM ** -0.5 * LOG2E
SUBLANES = 8
LANES = 128
VMEM_LIMIT = 56 * 1024 * 1024

ZX_COLS = SSD_INNER + SSD_INNER + 2 * SSD_GROUPS * SSD_STATE
QKV_COLS = NSA_WIDTH + 6 * NSA_KV_WIDTH
GATE_COLS = 2 * D_MODEL
SMALL_COLS = LANES
IN_PROJ_TM, IN_PROJ_TN = 2048, 1024


def _params(*sem):
    return pltpu.CompilerParams(dimension_semantics=sem, vmem_limit_bytes=VMEM_LIMIT)


def _rms(x, w):
    ms = jnp.mean(x * x, axis=-1, keepdims=True)
    return x * lax.rsqrt(ms + EPS) * w


def _sigmoid(x):
    return 0.5 + 0.5 * jnp.tanh(0.5 * x)


def _silu(x):
    h = 0.5 * x
    return h + h * jnp.tanh(h)


def _softplus(x):
    return jnp.maximum(x, 0.0) + jnp.log1p(jnp.exp(-jnp.abs(x)))


def _dot(a, b):
    return jnp.dot(a, b, preferred_element_type=F32)


def _dot_nt(a, b):
    return lax.dot_general(a, b, (((1,), (1,)), ((), ())), preferred_element_type=F32)


def _rmsnorm_kernel(x_ref, w_ref, o_ref):
    o_ref[...] = _rms(x_ref[...], w_ref[...]).astype(o_ref.dtype)


def rmsnorm_rows(x, w, tm=512):
    m, d = x.shape
    return pl.pallas_call(
        _rmsnorm_kernel,
        grid=(m // tm,),
        in_specs=[pl.BlockSpec((tm, d), lambda i: (i, 0)), pl.BlockSpec((1, d), lambda i: (0, 0))],
        out_specs=pl.BlockSpec((tm, d), lambda i: (i, 0)),
        out_shape=jax.ShapeDtypeStruct((m, d), BF16),
        compiler_params=_params("parallel"),
        name="rmsnorm_rows",
    )(x, w.reshape(1, d))


def _matmul_kernel(x_ref, w_ref, o_ref):
    o_ref[...] = _dot(x_ref[...], w_ref[...]).astype(o_ref.dtype)


def matmul(x, w, out_dtype=F32, tm=512, tn=512):
    m, k = x.shape
    n = w.shape[1]
    tm, tn = min(tm, m), min(tn, n)
    return pl.pallas_call(
        _matmul_kernel,
        grid=(m // tm, n // tn),
        in_specs=[pl.BlockSpec((tm, k), lambda i, j: (i, 0)), pl.BlockSpec((k, tn), lambda i, j: (0, j))],
        out_specs=pl.BlockSpec((tm, tn), lambda i, j: (i, j)),
        out_shape=jax.ShapeDtypeStruct((m, n), out_dtype),
        compiler_params=_params("parallel", "arbitrary"),
        name="matmul",
    )(x, w)


def _split3(a):
    h = a.astype(BF16)
    r = a - h.astype(F32)
    m = r.astype(BF16)
    return h, m, (r - m.astype(F32)).astype(BF16)


def _ssd_kernel(z_ref, x_ref, b_ref, c_ref, dt_ref, dtt_ref, *rest):
    @pl.when(pl.program_id(2) == 0)
    def _():
        for ref in rest[-4:]:
            ref[...] = jnp.zeros_like(ref)

    L = x_ref.shape[0] // SSD_CHUNKS_PER_STEP
    for cc in range(SSD_CHUNKS_PER_STEP):
        rows = pl.ds(cc * L, L)
        _ssd_chunk(z_ref.at[rows], x_ref.at[rows], b_ref.at[rows], c_ref.at[rows], dt_ref.at[:, rows],
                   dtt_ref.at[:, :, rows], *rest[:9], rest[9].at[rows], *rest[10:])


def _ssd_chunk(z_ref, x_ref, b_ref, c_ref, dt_ref, dtt_ref, cwx_ref, cwb_ref, cwc_ref, cbx_ref, cbb_ref, cbc_ref,
               hp_ref, hpt_ref, nw_ref, y_ref, state_ref, ext_x_ref, ext_b_ref, ext_c_ref):
    L = x_ref.shape[0]
    H = L // 2

    def conv_silu(u_ref, ext_ref, w, b):
        ext_ref[0:SUBLANES, :] = ext_ref[L:L + SUBLANES, :]
        u = u_ref[...]
        ext_ref[SUBLANES:SUBLANES + L, :] = u
        acc = b + w[SSD_CONV - 1:SSD_CONV, :] * u
        for k in range(SSD_CONV - 1):
            acc = acc + w[k:k + 1, :] * ext_ref[pl.ds(SUBLANES - (SSD_CONV - 1) + k, L), :]
        return _silu(acc)

    xs = conv_silu(x_ref, ext_x_ref, cwx_ref[...], cbx_ref[...])
    bm = conv_silu(b_ref, ext_b_ref, cwb_ref[...], cbb_ref[...])
    cm = conv_silu(c_ref, ext_c_ref, cwc_ref[...], cbc_ref[...])
    bm16, cm16 = bm.astype(BF16), cm.astype(BF16)

    hp = hp_ref[0]
    hpt = hpt_ref[0]
    dt = _softplus(dt_ref[0] + hp[0:1, :])
    dtt = _softplus(dtt_ref[0] + hpt[:, 0:1])
    da = dt * (-jnp.exp(hp[1:2, :]))
    dat = dtt * (-jnp.exp(hpt[:, 1:2]))
    row = lax.broadcasted_iota(jnp.int32, (L, L), 0)
    col = lax.broadcasted_iota(jnp.int32, (L, L), 1)
    tril = jnp.where(row >= col, 1.0, 0.0).astype(BF16)
    triu = jnp.where(row <= col, 1.0, 0.0).astype(BF16)
    acum = sum(_dot(tril, part) for part in _split3(da)) * LOG2E
    acumt = sum(_dot(part, triu) for part in _split3(dat)) * LOG2E
    dtet = jnp.exp2(acumt[:, L - 1:L] - acumt)
    eact = jnp.exp2(acumt)
    cdec = jnp.exp2(acum[L - 1:L, :])

    cbt = _dot_nt(bm16, cm16)
    cbt00, cbt01, cbt11 = cbt[:H, :H], cbt[:H, H:], cbt[H:, H:]
    tri = lax.broadcasted_iota(jnp.int32, (H, H), 0) <= lax.broadcasted_iota(jnp.int32, (H, H), 1)
    xst = xs.T
    d_skip = hp[2:3, :]
    P = SSD_HEAD_DIM
    outs = []
    for p in range(HEADS_PER_GROUP // 2):
        hprev = state_ref[p]
        yoff = _dot_nt(hprev.astype(BF16), cm16)
        new_state = []
        for hh in range(2):
            h = 2 * p + hh
            rs = slice(hh * P, (hh + 1) * P)
            xt = xst[h * P:(h + 1) * P, :]
            xdt = xt * dtt[h:h + 1, :]
            xdt16 = xdt.astype(BF16)
            a_col, a_row = acum[:, h:h + 1], acumt[h:h + 1, :]
            d00 = jnp.exp2(jnp.where(tri, a_row[:, :H] - a_col[:H], -jnp.inf))
            d01 = jnp.exp2(a_row[:, H:] - a_col[:H])
            d11 = jnp.exp2(jnp.where(tri, a_row[:, H:] - a_col[H:], -jnp.inf))
            y_l = _dot(xdt16[:, :H], (cbt00 * d00).astype(BF16))
            y_r = _dot(xdt16[:, :H], (cbt01 * d01).astype(BF16)) + _dot(xdt16[:, H:], (cbt11 * d11).astype(BF16))
            y = jnp.concatenate([y_l, y_r], axis=1)
            outs.append(y + yoff[rs] * eact[h:h + 1, :] + d_skip[:, h:h + 1] * xt)
            new_state.append(hprev[rs] * cdec[:, h:h + 1] + _dot((xdt * dtet[h:h + 1, :]).astype(BF16), bm16))
        state_ref[p] = jnp.concatenate(new_state, axis=0)
    y = jnp.concatenate(outs, axis=0).T * _silu(z_ref[...])
    y_ref[...] = _rms(y, nw_ref[...]).astype(y_ref.dtype)


def ssd_branch(zx, small, conv_w, conv_b, dt_bias, a_log, d_skip, norm_w, batch, seq):
    n = batch * seq
    L = min(SSD_CHUNK, seq)
    nc = seq // L
    g, hpg = SSD_GROUPS, HEADS_PER_GROUP
    dt_raw = small[:, :SSD_HEADS].reshape(n, g, hpg)
    dt_g = dt_raw.transpose(1, 0, 2)
    dtt_g = dt_raw.transpose(1, 2, 0)
    hp = jnp.stack([dt_bias, a_log, d_skip], axis=0).reshape(3, g, hpg).transpose(1, 0, 2)
    hp = jnp.pad(hp, ((0, 0), (0, SUBLANES - 3), (0, 0)))
    hpt = hp.transpose(0, 2, 1)
    cw = conv_w
    cbias = conv_b.reshape(1, -1)
    nxb = SSD_INNER // GROUP_WIDTH
    assert nc % SSD_CHUNKS_PER_STEP == 0
    lb, nb = L * SSD_CHUNKS_PER_STEP, nc // SSD_CHUNKS_PER_STEP
    rowblk = lambda b, gi, c: b * nb + c
    bc0 = SSD_INNER // SSD_STATE
    grid = (g, batch, nb)
    in_specs = [
        pl.BlockSpec((lb, GROUP_WIDTH), lambda gi, b, c: (rowblk(b, gi, c), gi)),
        pl.BlockSpec((lb, GROUP_WIDTH), lambda gi, b, c: (rowblk(b, gi, c), nxb + gi)),
        pl.BlockSpec((lb, SSD_STATE), lambda gi, b, c: (rowblk(b, gi, c), 2 * bc0 + gi)),
        pl.BlockSpec((lb, SSD_STATE), lambda gi, b, c: (rowblk(b, gi, c), 2 * bc0 + g + gi)),
        pl.BlockSpec((1, lb, hpg), lambda gi, b, c: (gi, rowblk(b, gi, c), 0)),
        pl.BlockSpec((1, hpg, lb), lambda gi, b, c: (gi, 0, rowblk(b, gi, c))),
        pl.BlockSpec((SSD_CONV, GROUP_WIDTH), lambda gi, b, c: (0, gi)),
        pl.BlockSpec((SSD_CONV, SSD_STATE), lambda gi, b, c: (0, bc0 + gi)),
        pl.BlockSpec((SSD_CONV, SSD_STATE), lambda gi, b, c: (0, bc0 + g + gi)),
        pl.BlockSpec((1, GROUP_WIDTH), lambda gi, b, c: (0, gi)),
        pl.BlockSpec((1, SSD_STATE), lambda gi, b, c: (0, bc0 + gi)),
        pl.BlockSpec((1, SSD_STATE), lambda gi, b, c: (0, bc0 + g + gi)),
        pl.BlockSpec((1, SUBLANES, hpg), lambda gi, b, c: (gi, 0, 0)),
        pl.BlockSpec((1, hpg, SUBLANES), lambda gi, b, c: (gi, 0, 0)),
        pl.BlockSpec((1, GROUP_WIDTH), lambda gi, b, c: (0, gi)),
    ]
    return pl.pallas_call(
        _ssd_kernel,
        grid=grid,
        in_specs=in_specs,
        out_specs=pl.BlockSpec((lb, GROUP_WIDTH), lambda gi, b, c: (rowblk(b, gi, c), gi)),
        out_shape=jax.ShapeDtypeStruct((n, SSD_INNER), BF16),
        scratch_shapes=[pltpu.VMEM((HEADS_PER_GROUP // 2, LANES, SSD_STATE), F32),
                        pltpu.VMEM((L + SUBLANES, GROUP_WIDTH), F32), pltpu.VMEM((L + SUBLANES, SSD_STATE), F32),
                        pltpu.VMEM((L + SUBLANES, SSD_STATE), F32)],
        compiler_params=_params("parallel", "parallel", "arbitrary"),
        name="ssd_branch",
    )(zx, zx, zx, zx, dt_g, dtt_g, cw, cw, cw, cbias, cbias, cbias, hp, hpt, norm_w.reshape(1, -1))


def _nsa_prep_kernel(q_ref, kc_ref, vc_ref, ks_ref, vs_ref, kw_ref, vw_ref, cos_ref, sin_ref, qw_ref, kw3_ref,
                     qo_ref, kco_ref, vco_ref, kso_ref, vso_ref, kwo_ref, vwo_ref):
    cos, sin = cos_ref[...], sin_ref[...]

    def norm_rope(x, w):
        y = _rms(x, w)
        return y * cos + pltpu.roll(y, HEAD_DIM // 2, 1) * sin

    qw = qw_ref[...]
    tt = q_ref.shape[0]
    for g in range(NSA_KV_HEADS):
        sl = slice(g * HEAD_DIM, (g + 1) * HEAD_DIM)
        for r in range(NSA_REP):
            h = g * NSA_REP + r
            qh = norm_rope(q_ref[:, h * HEAD_DIM:(h + 1) * HEAD_DIM], qw) * Q_PRESCALE
            qo_ref[0, g, r] = qh.astype(qo_ref.dtype)
        kco_ref[0, g] = norm_rope(kc_ref[:, sl], kw3_ref[0:1, :])
        vco_ref[0, g] = vc_ref[:, sl]
        kso_ref[0, g] = norm_rope(ks_ref[:, sl], kw3_ref[1:2, :]).astype(kso_ref.dtype)
        kwo_ref[0, g] = norm_rope(kw_ref[:, sl], kw3_ref[2:3, :]).astype(kwo_ref.dtype)
        vst = vs_ref[:, sl].T.astype(vso_ref.dtype)
        for j in range(tt // SEL_KEY_TILE):
            vso_ref[0, g, j] = vst[:, j * SEL_KEY_TILE:(j + 1) * SEL_KEY_TILE]
        vwt = vw_ref[:, sl].T.astype(vwo_ref.dtype)
        for j in range(tt // WIN_TILE):
            vwo_ref[0, g, j] = vwt[:, j * WIN_TILE:(j + 1) * WIN_TILE]


def nsa_prep(qkv, q_norm_w, k_norm_w, batch, seq, tt=2 * SEL_KEY_TILE):
    half = HEAD_DIM // 2
    inv = ROPE_THETA ** (-jnp.arange(half, dtype=F32) / half)
    ang = jnp.arange(seq).astype(F32)[:, None] * inv[None, :]
    cos = jnp.concatenate([jnp.cos(ang), jnp.cos(ang)], axis=-1)
    sin = jnp.concatenate([-jnp.sin(ang), jnp.sin(ang)], axis=-1)
    nt = seq // tt
    kvb = NSA_WIDTH // NSA_KV_WIDTH
    kv_spec = lambda j: pl.BlockSpec((tt, NSA_KV_WIDTH), lambda b, t: (b * nt + t, kvb + j))
    head_spec = pl.BlockSpec((1, NSA_KV_HEADS, tt, HEAD_DIM), lambda b, t: (b, 0, t, 0))
    head_shape = lambda dt: jax.ShapeDtypeStruct((batch, NSA_KV_HEADS, seq, HEAD_DIM), dt)
    return pl.pallas_call(
        _nsa_prep_kernel,
        grid=(batch, nt),
        in_specs=[pl.BlockSpec((tt, NSA_WIDTH), lambda b, t: (b * nt + t, 0))] + [kv_spec(j) for j in range(6)] + [
            pl.BlockSpec((tt, HEAD_DIM), lambda b, t: (t, 0)), pl.BlockSpec((tt, HEAD_DIM), lambda b, t: (t, 0)),
            pl.BlockSpec((1, HEAD_DIM), lambda b, t: (0, 0)), pl.BlockSpec((3, HEAD_DIM), lambda b, t: (0, 0))],
        out_specs=[pl.BlockSpec((1, NSA_KV_HEADS, NSA_REP, tt, HEAD_DIM), lambda b, t: (b, 0, 0, t, 0)),
                   head_spec, head_spec, head_spec,
                   pl.BlockSpec((1, NSA_KV_HEADS, tt // SEL_KEY_TILE, HEAD_DIM, SEL_KEY_TILE),
                                lambda b, t: (b, 0, t, 0, 0)),
                   head_spec,
                   pl.BlockSpec((1, NSA_KV_HEADS, tt // WIN_TILE, HEAD_DIM, WIN_TILE), lambda b, t: (b, 0, t, 0, 0))],
        out_shape=[jax.ShapeDtypeStruct((batch, NSA_KV_HEADS, NSA_REP, seq, HEAD_DIM), BF16),
                   head_shape(F32), head_shape(F32), head_shape(BF16),
                   jax.ShapeDtypeStruct((batch, NSA_KV_HEADS, seq // SEL_KEY_TILE, HEAD_DIM, SEL_KEY_TILE), BF16),
                   head_shape(BF16),
                   jax.ShapeDtypeStruct((batch, NSA_KV_HEADS, seq // WIN_TILE, HEAD_DIM, WIN_TILE), BF16)],
        compiler_params=_params("parallel", "parallel"),
        name="nsa_prep",
    )(qkv, qkv, qkv, qkv, qkv, qkv, qkv, cos, sin, q_norm_w.reshape(1, -1), k_norm_w)


def _compress_kernel(x_ref, pe_ref, w1_ref, w2_ref, o_ref, *, ncmp, transpose_out):
    nseg = x_ref.shape[2] // CMP_STRIDE
    u = jnp.zeros((nseg, CMP_HID), F32)
    v = jnp.zeros((nseg, CMP_HID), F32)
    for j in range(CMP_STRIDE):
        xj = x_ref[0, 0, pl.ds(j, nseg, stride=CMP_STRIDE), :]
        u = u + _dot((xj + pe_ref[j:j + 1, :]).astype(BF16), w1_ref[j])
        v = v + _dot((xj + pe_ref[CMP_STRIDE + j:CMP_STRIDE + j + 1, :]).astype(BF16), w1_ref[CMP_STRIDE + j])
    hid = _silu(u + pltpu.roll(v, nseg - 1, 0))
    comp = _dot(hid.astype(BF16), w2_ref[...])
    rowi = lax.broadcasted_iota(jnp.int32, comp.shape, 0)
    comp = jnp.where(rowi < ncmp, comp, 0.0)
    o_ref[0, 0] = (comp.T if transpose_out else comp).astype(o_ref.dtype)


def compress(raw, pe, w1, w2, transpose_out):
    b, g, t, hd = raw.shape
    nseg = t // CMP_STRIDE
    full = lambda shape: pl.BlockSpec(shape, lambda i, j: (0,) * len(shape))
    out_dims = (hd, nseg) if transpose_out else (nseg, hd)
    return pl.pallas_call(
        functools.partial(_compress_kernel, ncmp=nseg - 1, transpose_out=transpose_out),
        grid=(b, g),
        in_specs=[pl.BlockSpec((1, 1, t, hd), lambda i, j: (i, j, 0, 0)), full((CMP_BLK, hd)),
                  full((CMP_BLK, hd, CMP_HID)), full((CMP_HID, hd))],
        out_specs=pl.BlockSpec((1, 1) + out_dims, lambda i, j: (i, j, 0, 0)),
        out_shape=jax.ShapeDtypeStruct((b, g) + out_dims, BF16),
        compiler_params=_params("parallel", "parallel"),
        name="nsa_compress",
    )(raw, pe, w1.astype(BF16), w2.astype(BF16))


def _nsa_attn_kernel(q_ref, kc_ref, vct_ref, ks_ref, vst_ref, kw_ref, vwt_ref, glt_ref, ovt_ref, o_ref,
                     selb_ref, acc_ref, ocw_ref, s_ref, p_ref, *, seq, n_sel):
    Q = q_ref.shape[3]
    q0 = pl.program_id(2) * Q
    rows = NSA_REP * Q
    q = q_ref[0, 0].reshape(rows, HEAD_DIM)
    tq = q0 + lax.broadcasted_iota(jnp.int32, (1, Q), 1)
    head = lambda r: slice(r * Q, (r + 1) * Q)

    ncp = kc_ref.shape[2]
    wk = WINDOW + Q
    kt_sz = vst_ref.shape[4]
    start = pl.multiple_of(jnp.clip(q0 - WINDOW, 0, seq - wk), WIN_TILE)
    s_cw = _dot_nt(jnp.concatenate([kc_ref[0, 0], kw_ref[0, 0, pl.ds(start, wk), :], ks_ref[0, 0, 0:kt_sz, :]],
                                   axis=0), q)
    s_c, s_w = s_cw[:ncp], s_cw[ncp:ncp + wk]
    s_ref[0] = s_cw[ncp + wk:]

    cend = lax.broadcasted_iota(jnp.int32, (ncp, 1), 0) * CMP_STRIDE + (CMP_BLK - 1)
    m_c = cend <= tq
    ps = []
    for r in range(NSA_REP):
        s = jnp.where(m_c, s_c[:, head(r)], NEG_INF)
        e = jnp.exp2(s - jnp.max(s, axis=0, keepdims=True))
        p = jnp.where(m_c, e * (1.0 / jnp.sum(e, axis=0, keepdims=True)), 0.0)
        ps.append(p.astype(BF16))
    p_c = jnp.concatenate(ps, axis=1)
    nsb = ovt_ref.shape[0]
    oc_imp = _dot(jnp.concatenate([vct_ref[0, 0], ovt_ref[...]], axis=0), p_c)
    o_c, imp4 = oc_imp[:HEAD_DIM], oc_imp[HEAD_DIM:]

    imp = imp4[:, head(0)]
    for r in range(1, NSA_REP):
        imp = imp + imp4[:, head(r)]
    blk_t = (q0 + lax.broadcasted_iota(jnp.int32, (nsb, Q), 1)) >> SEL_SHIFT
    sb = lax.broadcasted_iota(jnp.int32, (nsb, Q), 0)
    forced = (sb == 0) | (sb == blk_t) | (sb == blk_t - 1)
    imp = jnp.where(sb <= blk_t, jnp.where(forced, FORCE_SCORE, imp), -jnp.inf)
    rank = jnp.zeros((nsb, Q), F32)
    for i in range(nsb):
        ri = imp[i:i + 1, :]
        beats = (ri > imp) | ((ri == imp) & (sb > i))
        rank = rank + jnp.where(beats, 1.0, 0.0)
    selb_ref[...] = jnp.where(rank < n_sel, 0.0, NEG_INF)

    dist = tq - (start + lax.broadcasted_iota(jnp.int32, (wk, 1), 0))
    bias_w = jnp.where((dist >= 0) & (dist < WINDOW), 0.0, NEG_INF)
    ps, ls = [], []
    for r in range(NSA_REP):
        s = s_w[:, head(r)] + bias_w
        p = jnp.exp2(s - jnp.max(s, axis=0, keepdims=True))
        ls.append(jnp.sum(p, axis=0, keepdims=True))
        ps.append(p.astype(BF16))
    t0 = start // WIN_TILE
    vw_t = jnp.concatenate([vwt_ref[0, 0, t0 + j] for j in range(wk // WIN_TILE)], axis=1)
    o_w = _dot(vw_t, jnp.concatenate(ps, axis=1)) * (1.0 / jnp.concatenate(ls, axis=1))

    gate = _sigmoid(glt_ref[0, 0])
    for r in range(NSA_REP):
        ocw_ref[:, head(r)] = (gate[r:r + 1, :] * o_c[:, head(r)]
                               + gate[2 * NSA_REP + r:2 * NSA_REP + r + 1, :] * o_w[:, head(r)])

    blocks_per_tile = kt_sz // SEL_BLK
    n_all = seq // kt_sz
    acc_ref[...] = jnp.zeros_like(acc_ref)
    p_ref[1] = jnp.zeros(p_ref.shape[1:], p_ref.dtype)

    def sel_tile(kt, carry):
        m, l, alpha_prev = carry
        slot = kt & 1
        s_all = s_ref[slot]
        p_prev = p_ref[1 - slot]
        k_next = pl.multiple_of(jnp.minimum(kt + 1, n_all - 1) * kt_sz, kt_sz)
        s_ref[1 - slot] = _dot_nt(ks_ref[0, 0, pl.ds(k_next, kt_sz), :], q)
        pv_prev = _dot(vst_ref[0, 0, jnp.maximum(kt - 1, 0)], p_prev)
        acc_ref[...] = acc_ref[...] * alpha_prev + pv_prev
        k0 = kt * kt_sz
        kpos = k0 + lax.broadcasted_iota(jnp.int32, (kt_sz, 1), 0)
        bias = jnp.concatenate(
            [jnp.broadcast_to(selb_ref[pl.ds(kt * blocks_per_tile + i, 1), :], (SEL_BLK, Q))
             for i in range(blocks_per_tile)], axis=0)
        bias = jnp.where(kpos <= tq, bias, NEG_INF)
        ps, ms, ls, alphas = [], [], [], []
        for r in range(NSA_REP):
            s = s_all[:, head(r)] + bias
            m_old = m[:, head(r)]
            m_new = jnp.maximum(m_old, jnp.max(s, axis=0, keepdims=True))
            p = jnp.exp2(s - m_new)
            alpha = jnp.exp2(m_old - m_new)
            ls.append(alpha * l[:, head(r)] + jnp.sum(p, axis=0, keepdims=True))
            ms.append(m_new)
            alphas.append(alpha)
            ps.append(p.astype(BF16))
        p_ref[slot] = jnp.concatenate(ps, axis=1)
        return jnp.concatenate(ms, axis=1), jnp.concatenate(ls, axis=1), jnp.concatenate(alphas, axis=1)

    n_tiles = (q0 + Q + kt_sz - 1) // kt_sz
    init = (jnp.full((1, rows), NEG_INF, F32), jnp.zeros((1, rows), F32), jnp.ones((1, rows), F32))
    _, l_s, alpha_last = lax.fori_loop(0, n_tiles, sel_tile, init)
    pv_last = _dot(vst_ref[0, 0, n_tiles - 1], p_ref[(n_tiles - 1) & 1])
    o_s = (acc_ref[...] * alpha_last + pv_last) * (1.0 / l_s)

    for r in range(NSA_REP):
        out_t = ocw_ref[:, head(r)] + gate[NSA_REP + r:NSA_REP + r + 1, :] * o_s[:, head(r)]
        o_ref[:, r * HEAD_DIM:(r + 1) * HEAD_DIM] = out_t.T.astype(o_ref.dtype)


def nsa_attention(qh, kcc, vcct, ksn, vst, kwn, vwt, gate_logits_t, batch, seq):
    Q_BLK = NSA_Q_BLK
    assert seq % SEL_KEY_TILE == 0 and seq % Q_BLK == 0 and seq >= WINDOW + Q_BLK
    nq = seq // Q_BLK
    nsb = seq // SEL_BLK
    n_sel = min(N_SEL, nsb)
    ncp = kcc.shape[2]
    ncmp = (seq - CMP_BLK) // CMP_STRIDE + 1
    ci = np.arange(ncp)[None, :]
    sj = np.arange(nsb)[:, None]
    ovt = ((ci * CMP_STRIDE < (sj + 1) * SEL_BLK) & (ci * CMP_STRIDE + CMP_BLK > sj * SEL_BLK) & (ci < ncmp))
    ovt = jnp.asarray(ovt, BF16)
    seq_spec = pl.BlockSpec((1, 1, seq, HEAD_DIM), lambda b, g, i: (b, g, 0, 0))
    cmp_spec = pl.BlockSpec((1, 1, ncp, HEAD_DIM), lambda b, g, i: (b, g, 0, 0))
    tiled = lambda a: pl.BlockSpec((1, 1) + a.shape[2:], lambda b, g, i: (b, g, 0, 0, 0))
    return pl.pallas_call(
        functools.partial(_nsa_attn_kernel, seq=seq, n_sel=n_sel),
        grid=(batch, NSA_KV_HEADS, nq),
        in_specs=[pl.BlockSpec((1, 1, NSA_REP, Q_BLK, HEAD_DIM), lambda b, g, i: (b, g, 0, i, 0)),
                  cmp_spec, pl.BlockSpec((1, 1, HEAD_DIM, ncp), lambda b, g, i: (b, g, 0, 0)),
                  seq_spec, tiled(vst), seq_spec, tiled(vwt),
                  pl.BlockSpec((1, 1, 3 * NSA_REP, Q_BLK), lambda b, g, i: (b, g, 0, i)),
                  pl.BlockSpec((nsb, ncp), lambda b, g, i: (0, 0))],
        out_specs=pl.BlockSpec((Q_BLK, NSA_REP * HEAD_DIM), lambda b, g, i: (b * nq + i, g)),
        out_shape=jax.ShapeDtypeStruct((batch * seq, NSA_WIDTH), BF16),
        scratch_shapes=[pltpu.VMEM((nsb, Q_BLK), F32), pltpu.VMEM((HEAD_DIM, NSA_REP * Q_BLK), F32),
                        pltpu.VMEM((HEAD_DIM, NSA_REP * Q_BLK), F32),
                        pltpu.VMEM((2, SEL_KEY_TILE, NSA_REP * Q_BLK), F32),
                        pltpu.VMEM((2, SEL_KEY_TILE, NSA_REP * Q_BLK), BF16)],
        compiler_params=_params("parallel", "parallel", "arbitrary"),
        name="nsa_attention",
    )(qh, kcc, vcct, ksn, vst, kwn, vwt, gate_logits_t, ovt)


def _merge_kernel(ys_ref, yn_ref, ws_ref, wn_ref, gs_ref, gn_ref, o_ref):
    up_s = _dot(ys_ref[...], ws_ref[...])
    up_n = _dot(yn_ref[...], wn_ref[...])
    o_ref[...] = (_sigmoid(gs_ref[...]) * up_s + _sigmoid(gn_ref[...]) * up_n).astype(o_ref.dtype)


def merge_mixers(y_ssd, y_nsa, w_up_ssd, w_up_nsa, gates, tm=1024, tn=512):
    m = y_ssd.shape[0]
    nb = D_MODEL // tn
    return pl.pallas_call(
        _merge_kernel,
        grid=(m // tm, nb),
        in_specs=[pl.BlockSpec((tm, SSD_INNER), lambda i, j: (i, 0)), pl.BlockSpec((tm, NSA_WIDTH), lambda i, j: (i, 0)),
                  pl.BlockSpec((SSD_INNER, tn), lambda i, j: (0, j)), pl.BlockSpec((NSA_WIDTH, tn), lambda i, j: (0, j)),
                  pl.BlockSpec((tm, tn), lambda i, j: (i, j)), pl.BlockSpec((tm, tn), lambda i, j: (i, nb + j))],
        out_specs=pl.BlockSpec((tm, tn), lambda i, j: (i, j)),
        out_shape=jax.ShapeDtypeStruct((m, D_MODEL), BF16),
        compiler_params=_params("parallel", "arbitrary"),
        name="merge_mixers",
    )(y_ssd, y_nsa, w_up_ssd, w_up_nsa, gates, gates)


def _residual_matmul_kernel(h_ref, x_ref, w_ref, o_ref):
    o_ref[...] = h_ref[...] + _dot(x_ref[...], w_ref[...])


def residual_matmul(h, x, w, tm=1024, tn=1024):
    m, k = x.shape
    n = w.shape[1]
    return pl.pallas_call(
        _residual_matmul_kernel,
        grid=(m // tm, n // tn),
        in_specs=[pl.BlockSpec((tm, tn), lambda i, j: (i, j)), pl.BlockSpec((tm, k), lambda i, j: (i, 0)),
                  pl.BlockSpec((k, tn), lambda i, j: (0, j))],
        out_specs=pl.BlockSpec((tm, tn), lambda i, j: (i, j)),
        out_shape=jax.ShapeDtypeStruct((m, n), F32),
        compiler_params=_params("parallel", "arbitrary"),
        name="residual_matmul",
    )(h, x, w)


def _xattn_router_kernel(h_ref, kv_ref, n2_ref, wq_ref, qn_ref, kn_ref, wo_ref, n3_ref, rw_ref, rb_ref,
                         h2_ref, hf_ref, rt_ref):
    h = h_ref[...]
    qp = _dot(_rms(h, n2_ref[...]).astype(BF16), wq_ref[...])
    kv = kv_ref[...]
    scale = HEAD_DIM ** -0.5
    heads = []
    for hd in range(X_HEADS):
        sl = slice(hd * HEAD_DIM, (hd + 1) * HEAD_DIM)
        qh = _rms(qp[:, sl], qn_ref[...]).astype(BF16)
        kh = _rms(kv[:, sl], kn_ref[...]).astype(BF16)
        vh = kv[:, X_WIDTH + hd * HEAD_DIM:X_WIDTH + (hd + 1) * HEAD_DIM].astype(BF16)
        s = _dot_nt(qh, kh) * scale
        e = jnp.exp(s - jnp.max(s, axis=-1, keepdims=True))
        p = e / jnp.sum(e, axis=-1, keepdims=True)
        heads.append(_dot(p.astype(BF16), vh))
    o = jnp.concatenate(heads, axis=1).astype(BF16)
    h2 = h + _dot(o, wo_ref[...])
    h2_ref[...] = h2
    hf = _rms(h2, n3_ref[...]).astype(BF16)
    hf_ref[...] = hf
    lg = _dot(hf, rw_ref[...]) + rb_ref[...]

    lane = lax.broadcasted_iota(jnp.int32, lg.shape, 1)
    rmax = lambda v: jnp.max(v, axis=-1, keepdims=True)
    rsum = lambda v: jnp.sum(v, axis=-1, keepdims=True)
    first_lane = lambda hit: jnp.min(jnp.where(hit, lane, LANES), axis=-1, keepdims=True)
    is_g = lane < N_GROUPS
    eg = jnp.where(is_g, jnp.exp(lg - rmax(jnp.where(is_g, lg, -jnp.inf))), 0.0)
    pg = eg / rsum(eg)
    pg_top = rmax(pg)
    grp = first_lane(is_g & (pg == pg_top))
    in_grp = (lane >= N_GROUPS) & (((lane - N_GROUPS) >> EPG_SHIFT) == grp)
    ee = jnp.where(in_grp, jnp.exp(lg - rmax(jnp.where(in_grp, lg, -jnp.inf))), 0.0)
    pe = jnp.where(in_grp, ee / rsum(ee), -1.0)
    p1 = rmax(pe)
    i1 = first_lane(pe == p1)
    pe2 = jnp.where(lane == i1, -1.0, pe)
    p2 = rmax(pe2)
    i2 = first_lane(pe2 == p2)
    psum = p1 + p2
    route = jnp.where(lane == 0, (i1 - N_GROUPS).astype(F32),
                      jnp.where(lane == 1, (i2 - N_GROUPS).astype(F32),
                                jnp.where(lane == 2, pg_top * p1 / psum,
                                          jnp.where(lane == 3, pg_top * p2 / psum, 0.0))))
    rt_ref[...] = route


def xattn_router(h1, kv, norm2_w, wq, q_norm_w, k_norm_w, wo, norm3_w, router_w, router_b, batch, seq, tm=512):
    n = batch * seq
    tm = min(tm, seq)
    per_seq = seq // tm
    mlen = kv.shape[0] // batch
    full = lambda shape: pl.BlockSpec(shape, lambda i: (0,) * len(shape))
    return pl.pallas_call(
        _xattn_router_kernel,
        grid=(n // tm,),
        in_specs=[pl.BlockSpec((tm, D_MODEL), lambda i: (i, 0)),
                  pl.BlockSpec((mlen, 2 * X_WIDTH), lambda i: (i // per_seq, 0)),
                  full((1, D_MODEL)), full((D_MODEL, X_WIDTH)), full((1, HEAD_DIM)), full((1, HEAD_DIM)),
                  full((X_WIDTH, D_MODEL)), full((1, D_MODEL)), full((D_MODEL, LANES)), full((1, LANES))],
        out_specs=[pl.BlockSpec((tm, D_MODEL), lambda i: (i, 0)), pl.BlockSpec((tm, D_MODEL), lambda i: (i, 0)),
                   pl.BlockSpec((tm, LANES), lambda i: (i, 0))],
        out_shape=[jax.ShapeDtypeStruct((n, D_MODEL), F32), jax.ShapeDtypeStruct((n, D_MODEL), BF16),
                   jax.ShapeDtypeStruct((n, LANES), F32)],
        compiler_params=_params("parallel"),
        name="xattn_router",
    )(h1, kv, norm2_w.reshape(1, -1), wq, q_norm_w.reshape(1, -1), k_norm_w.reshape(1, -1), wo,
      norm3_w.reshape(1, -1), router_w, router_b)


def _moe_rank_kernel(e_ref, rank_ref, cnt_ref, carry_ref):
    @pl.when(pl.program_id(0) == 0)
    def _():
        carry_ref[...] = jnp.zeros_like(carry_ref)

    t = e_ref.shape[0]
    hit = e_ref[...] == lax.broadcasted_iota(jnp.int32, (t, LANES), 1)
    onehot = jnp.where(hit, 1.0, 0.0).astype(BF16)
    earlier = lax.broadcasted_iota(jnp.int32, (t, t), 0) > lax.broadcasted_iota(jnp.int32, (t, t), 1)
    before = _dot(jnp.where(earlier, 1.0, 0.0).astype(BF16), onehot) + carry_ref[...]
    rank_ref[...] = jnp.sum(jnp.where(hit, before, 0.0), axis=-1, keepdims=True).astype(jnp.int32)
    carry_ref[...] += jnp.sum(jnp.where(hit, 1.0, 0.0), axis=0, keepdims=True)
    cnt_ref[...] = carry_ref[...]


def moe_rank(eid, t=1024):
    a = eid.shape[0]
    return pl.pallas_call(
        _moe_rank_kernel,
        grid=(a // t,),
        in_specs=[pl.BlockSpec((t, 1), lambda i: (i, 0))],
        out_specs=[pl.BlockSpec((t, 1), lambda i: (i, 0)), pl.BlockSpec((1, LANES), lambda i: (0, 0))],
        out_shape=[jax.ShapeDtypeStruct((a, 1), jnp.int32), jax.ShapeDtypeStruct((1, LANES), F32)],
        scratch_shapes=[pltpu.VMEM((1, LANES), F32)],
        compiler_params=_params("arbitrary"),
        name="moe_rank",
    )(eid)


def _moe_ffn_kernel(item_e_ref, item_row_ref, item_nsub_ref, item_nout_ref, x_hbm, wg_ref, wu_ref, wd_ref, y_hbm,
                    x_vmem, y_vmem, sem_x, sem_y, *, hid):
    i, c = pl.program_id(0), pl.program_id(1)
    last_c = pl.num_programs(1) - 1
    th = wg_ref.shape[2]
    nsub, nout = item_nsub_ref[i], item_nout_ref[i]
    row0 = pl.multiple_of(item_row_ref[i], MOE_SUB)
    sub = lambda s: pl.ds(s * MOE_SUB, MOE_SUB)

    def x_copy(item, s):
        row = pl.multiple_of(item_row_ref[item], MOE_SUB)
        return pltpu.make_async_copy(x_hbm.at[pl.ds(row + s * MOE_SUB, MOE_SUB)], x_vmem.at[item & 1, sub(s)],
                                     sem_x.at[item & 1, s])

    def y_copy(row, s):
        return pltpu.make_async_copy(y_vmem.at[sub(s)], y_hbm.at[pl.ds(row + s * MOE_SUB, MOE_SUB)], sem_y.at[s])

    def for_subs(count, fn):
        for s in range(MOE_ITEM_SUBS):
            pl.when(s < count)(functools.partial(fn, s))

    @pl.when(c == 0)
    def _():
        @pl.when(i > 0)
        def _():
            prev_row = pl.multiple_of(item_row_ref[i - 1], MOE_SUB)
            for_subs(item_nout_ref[i - 1], lambda s: y_copy(prev_row, s).wait())

        @pl.when(i == 0)
        def _():
            for_subs(nsub, lambda s: x_copy(i, s).start())

        for_subs(nsub, lambda s: x_copy(i, s).wait())

        @pl.when(i + 1 < pl.num_programs(0))
        def _():
            nxt = jnp.minimum(i + 1, pl.num_programs(0) - 1)
            for_subs(item_nsub_ref[nxt], lambda s: x_copy(nxt, s).start())

        @pl.when(nsub == 0)
        def _():
            y_vmem[...] = jnp.zeros_like(y_vmem)
            for_subs(nout, lambda s: y_copy(row0, s).start())

    def ffn(count):
        rows = pl.ds(0, count * MOE_SUB)
        col_ok = c * th + lax.broadcasted_iota(jnp.int32, (1, th), 1) < hid
        row_ok = c * th + lax.broadcasted_iota(jnp.int32, (th, 1), 0) < hid
        wgu = jnp.concatenate([wg_ref[0], wu_ref[0]], axis=1).astype(BF16)
        wd = jnp.where(row_ok, wd_ref[0], 0.0).astype(BF16)
        gu = _dot(x_vmem[i & 1, rows, :], wgu)
        act = jnp.where(col_ok, _silu(gu[:, :th]) * gu[:, th:], 0.0)
        part = _dot(act.astype(BF16), wd)

        @pl.when(c == 0)
        def _():
            y_vmem[rows, :] = part

        @pl.when(c > 0)
        def _():
            y_vmem[rows, :] += part

    for count in range(1, MOE_ITEM_SUBS + 1):
        pl.when(nsub == count)(functools.partial(ffn, count))

    @pl.when((nsub > 0) & (c == last_c))
    def _():
        for_subs(nsub, lambda s: y_copy(row0, s).start())

    @pl.when((i == pl.num_programs(0) - 1) & (c == last_c))
    def _():
        for_subs(nout, lambda s: y_copy(row0, s).wait())


def moe_ffn(x_buf, item_e, item_row, item_nsub, item_nout, n_active, w_gate, w_up, w_down):
    rows, d = x_buf.shape
    hid = w_gate.shape[2]
    th = MOE_HID_TILE
    nc = pl.cdiv(hid, th)
    tile = lambda i, c, ns: jnp.where(ns[i] > 0, c, nc - 1)
    grid_spec = pltpu.PrefetchScalarGridSpec(
        num_scalar_prefetch=4,
        grid=(n_active, nc),
        in_specs=[pl.BlockSpec(memory_space=pl.ANY),
                  pl.BlockSpec((1, d, th), lambda i, c, ie, ir, ns, no: (ie[i], 0, tile(i, c, ns))),
                  pl.BlockSpec((1, d, th), lambda i, c, ie, ir, ns, no: (ie[i], 0, tile(i, c, ns))),
                  pl.BlockSpec((1, th, d), lambda i, c, ie, ir, ns, no: (ie[i], tile(i, c, ns), 0))],
        out_specs=pl.BlockSpec(memory_space=pl.ANY),
        scratch_shapes=[pltpu.VMEM((2, MOE_ITEM_SUBS * MOE_SUB, d), BF16),
                        pltpu.VMEM((MOE_ITEM_SUBS * MOE_SUB, d), F32),
                        pltpu.SemaphoreType.DMA((2, MOE_ITEM_SUBS)), pltpu.SemaphoreType.DMA((MOE_ITEM_SUBS,))],
    )
    return pl.pallas_call(
        functools.partial(_moe_ffn_kernel, hid=hid),
        grid_spec=grid_spec,
        out_shape=jax.ShapeDtypeStruct((rows, d), F32),
        compiler_params=_params("arbitrary", "arbitrary"),
        name="moe_ffn",
    )(item_e, item_row, item_nsub, item_nout, x_buf, w_gate, w_up, w_down)


def hier_moe(h2, hf, route, w_gate, w_up, w_down):
    n, d = h2.shape
    n_exp = w_gate.shape[0]
    i32 = jnp.int32
    eid = route[:, 0:TOP_K].astype(i32).reshape(-1)
    wts = route[:, TOP_K:2 * TOP_K]
    n_assign = n * TOP_K
    rank, counts = moe_rank(eid.reshape(n_assign, 1))
    counts = counts[0, :n_exp].astype(i32)
    subs_e = (counts + MOE_SUB - 1) // MOE_SUB
    sub_end = jnp.cumsum(subs_e)
    row_start = (sub_end - subs_e) * MOE_SUB
    is_e = eid[:, None] == jnp.arange(n_exp, dtype=i32)[None, :]
    dest = rank[:, 0] + jnp.sum(jnp.where(is_e, row_start[None, :], 0), axis=1)
    n_subs = (n_assign + n_exp * (MOE_SUB - 1) + MOE_SUB - 1) // MOE_SUB
    rows = n_subs * MOE_SUB
    tok = jnp.repeat(jnp.arange(n, dtype=i32), TOP_K)
    tok_buf = (jnp.arange(rows, dtype=i32) % n).at[dest].set(tok)
    per = MOE_ITEM_SUBS
    n_items = (n_exp * (per - 1) + n_subs + per - 1) // per
    items_e = (subs_e + per - 1) // per
    item_end = jnp.cumsum(items_e)
    n_used = item_end[-1]
    idx = jnp.arange(n_items, dtype=i32)
    used = idx < n_used
    e_of = jnp.minimum(jnp.searchsorted(item_end, idx, side='right'), n_exp - 1).astype(i32)
    j = idx - (item_end - items_e)[e_of]
    item_nsub = jnp.where(used, jnp.clip(subs_e[e_of] - per * j, 0, per), 0)
    fill_first = jnp.minimum(sub_end[-1] + per * (idx - n_used), n_subs)
    item_fill = jnp.where(used, 0, jnp.minimum(n_subs - fill_first, per))
    item_row = jnp.where(used, row_start[e_of] + j * per * MOE_SUB, jnp.minimum(fill_first, n_subs - 1) * MOE_SUB)
    item_e = jnp.where(used, e_of, e_of[jnp.maximum(n_used - 1, 0)])
    n_active = n_used + (n_subs - sub_end[-1] + per - 1) // per
    y_buf = moe_ffn(hf[tok_buf], item_e.astype(i32), item_row.astype(i32), item_nsub.astype(i32),
                    (item_nsub + item_fill).astype(i32), n_active.astype(i32), w_gate, w_up, w_down)
    slot = dest.reshape(n, TOP_K)
    return h2 + wts[:, 0:1] * y_buf[slot[:, 0]] + wts[:, 1:2] * y_buf[slot[:, 1]]


COL_DT = ZX_COLS
COL_QKV = COL_DT + SSD_HEADS
COL_NSA_GATE = COL_QKV + QKV_COLS
COL_MERGE_GATE = COL_NSA_GATE + 3 * NSA_HEADS


def _small_w_in(w_in):
    pad = jnp.zeros(w_in.shape[:2] + (SMALL_COLS - SSD_HEADS - 3 * NSA_HEADS,), w_in.dtype)
    return jnp.concatenate([w_in[:, :, COL_DT:COL_QKV], w_in[:, :, COL_NSA_GATE:COL_MERGE_GATE], pad], axis=2)


def _layer(x, mem, norm1_w, w_in, ssd_conv_w, ssd_conv_b, ssd_dt_bias, ssd_a_log, ssd_d, ssd_norm_w,
           nsa_q_norm_w, nsa_k_norm_w, cmp_pe_k, cmp_w1_k, cmp_w2_k, cmp_pe_v, cmp_w1_v, cmp_w2_v,
           w_up_ssd, w_up_nsa, w_out, norm2_w, mem_norm_w, xq_w, xkv_w, x_q_norm_w, x_k_norm_w, xo_w,
           norm3_w, router_g_w, router_g_b, router_e_w, router_e_b, moe_w_gate, moe_w_up, moe_w_down):
    batch, seq, d = x.shape
    n = batch * seq
    xf = x.reshape(n, d)
    hn = rmsnorm_rows(xf, norm1_w)
    w_cols = lambda c0, nc: w_in[0, :, c0:c0 + nc].astype(BF16)
    zx = matmul(hn, w_cols(0, ZX_COLS), tm=IN_PROJ_TM, tn=IN_PROJ_TN)
    qkv = matmul(hn, w_cols(COL_QKV, QKV_COLS), tm=IN_PROJ_TM, tn=IN_PROJ_TN)
    mgates = matmul(hn, w_cols(COL_MERGE_GATE, GATE_COLS), tm=IN_PROJ_TM, tn=IN_PROJ_TN)
    small = matmul(hn, _small_w_in(w_in)[0].astype(BF16), tm=IN_PROJ_TM)
    y_ssd = ssd_branch(zx, small, ssd_conv_w, ssd_conv_b, ssd_dt_bias, ssd_a_log, ssd_d, ssd_norm_w, batch, seq)
    qh, kcn, vcr, ksn, vsn, kwn, vwn = nsa_prep(qkv, nsa_q_norm_w, nsa_k_norm_w, batch, seq)
    kcc = compress(kcn, cmp_pe_k, cmp_w1_k, cmp_w2_k, transpose_out=False)
    vcct = compress(vcr, cmp_pe_v, cmp_w1_v, cmp_w2_v, transpose_out=True)
    gl = small[:, SSD_HEADS:SSD_HEADS + 3 * NSA_HEADS].reshape(batch, seq, 3, NSA_KV_HEADS, NSA_REP)
    gl = gl.transpose(0, 3, 2, 4, 1).reshape(batch, NSA_KV_HEADS, 3 * NSA_REP, seq)
    y_nsa = nsa_attention(qh, kcc, vcct, ksn, vsn, kwn, vwn, gl, batch, seq)
    merged = merge_mixers(y_ssd, y_nsa, w_up_ssd.astype(BF16), w_up_nsa.astype(BF16), mgates)
    h1 = residual_matmul(xf, merged, w_out.astype(BF16))
    mlen = mem.shape[1]
    mn = rmsnorm_rows(mem.reshape(batch * mlen, d), mem_norm_w, tm=min(512, batch * mlen))
    kv = matmul(mn, xkv_w.astype(BF16))
    n_exp = router_e_w.shape[1]
    rpad = LANES - N_GROUPS - n_exp
    router_w = jnp.concatenate([router_g_w, router_e_w, jnp.zeros((d, rpad), F32)], axis=1).astype(BF16)
    router_b = jnp.concatenate([router_g_b, router_e_b, jnp.zeros((rpad,), F32)]).reshape(1, LANES)
    h2, hf, route = xattn_router(h1, kv, norm2_w, xq_w.astype(BF16), x_q_norm_w, x_k_norm_w, xo_w.astype(BF16),
                                 norm3_w, router_w, router_b, batch, seq)
    out = hier_moe(h2, hf, route, moe_w_gate, moe_w_up, moe_w_down)
    return out.reshape(batch, seq, d)


def kernel(x, mem, norm1_w, w_in, ssd_conv_w, ssd_conv_b, ssd_dt_bias, ssd_a_log, ssd_d, ssd_norm_w, nsa_q_norm_w, nsa_k_norm_w, cmp_pe_k, cmp_w1_k, cmp_w2_k, cmp_pe_v, cmp_w1_v, cmp_w2_v, w_up_ssd, w_up_nsa, w_out, norm2_w, mem_norm_w, xq_w, xkv_w, x_q_norm_w, x_k_norm_w, xo_w, norm3_w, router_g_w, router_g_b, router_e_w, router_e_b, moe_w_gate, moe_w_up, moe_w_down):
    h = x
    for l in range(norm1_w.shape[0]):
        h = _layer(h, mem, norm1_w[l], w_in[l:l + 1], ssd_conv_w[l], ssd_conv_b[l], ssd_dt_bias[l], ssd_a_log[l], ssd_d[l],
                   ssd_norm_w[l], nsa_q_norm_w[l], nsa_k_norm_w[l], cmp_pe_k[l], cmp_w1_k[l], cmp_w2_k[l], cmp_pe_v[l],
                   cmp_w1_v[l], cmp_w2_v[l], w_up_ssd[l], w_up_nsa[l], w_out[l], norm2_w[l], mem_norm_w[l], xq_w[l],
                   xkv_w[l], x_q_norm_w[l], x_k_norm_w[l], xo_w[l], norm3_w[l], router_g_w[l], router_g_b[l],
                   router_e_w[l], router_e_b[l], moe_w_gate[l], moe_w_up[l], moe_w_down[l])
    return h.astype(x.dtype)
```

```python
import functools

import numpy as np
import jax
import jax.numpy as jnp
from jax import lax
from jax.experimental import pallas as pl
from jax.experimental.pallas import tpu as pltpu

F32 = jnp.float32
BF16 = jnp.bfloat16

D_MODEL = 2048
SSD_INNER = 4096
SSD_HEAD_DIM = 64
SSD_HEADS = 64
SSD_GROUPS = 8
SSD_STATE = 128
SSD_CONV = 4
SSD_CHUNK = 256
SSD_CHUNKS_PER_STEP = 4
HEADS_PER_GROUP = SSD_HEADS // SSD_GROUPS
GROUP_WIDTH = SSD_INNER // SSD_GROUPS
HEAD_DIM = 128
NSA_HEADS = 16
NSA_KV_HEADS = 4
NSA_REP = NSA_HEADS // NSA_KV_HEADS
NSA_WIDTH = NSA_HEADS * HEAD_DIM
NSA_KV_WIDTH = NSA_KV_HEADS * HEAD_DIM
CMP_BLK = 32
CMP_STRIDE = 16
CMP_HID = 256
SEL_BLK = 64
SEL_SHIFT = 6
N_SEL = 16
WINDOW = 512
NSA_Q_BLK = 256
WIN_TILE = 128
SEL_KEY_TILE = 256
FORCE_SCORE = 1.0e4
X_HEADS = 4
X_WIDTH = X_HEADS * HEAD_DIM
N_GROUPS = 8
EXPERTS_PER_GROUP = 8
EPG_SHIFT = 3
N_EXPERTS = 64
TOP_K = 2
EXPERT_HIDDEN = 1408
MOE_SUB = 128
MOE_ITEM_SUBS = 8
MOE_HID_TILE = 256
ROPE_THETA = 10000.0
EPS = 1e-6
NEG_INF = -1e30
LOG2E = float(np.log2(np.e))
Q_PRESCALE = HEAD_DIM ** -0.5 * LOG2E
SUBLANES = 8
LANES = 128
VMEM_LIMIT = 56 * 1024 * 1024

ZX_COLS = SSD_INNER + SSD_INNER + 2 * SSD_GROUPS * SSD_STATE
QKV_COLS = NSA_WIDTH + 6 * NSA_KV_WIDTH
GATE_COLS = 2 * D_MODEL
SMALL_COLS = LANES
IN_PROJ_TM, IN_PROJ_TN = 2048, 1024


def _params(*sem):
    return pltpu.CompilerParams(dimension_semantics=sem, vmem_limit_bytes=VMEM_LIMIT)


def _rms(x, w):
    ms = jnp.mean(x * x, axis=-1, keepdims=True)
    return x * lax.rsqrt(ms + EPS) * w


def _sigmoid(x):
    return 0.5 + 0.5 * jnp.tanh(0.5 * x)


def _silu(x):
    h = 0.5 * x
    return h + h * jnp.tanh(h)


def _softplus(x):
    return jnp.maximum(x, 0.0) + jnp.log1p(jnp.exp(-jnp.abs(x)))


def _dot(a, b):
    return jnp.dot(a, b, preferred_element_type=F32)


def _dot_nt(a, b):
    return lax.dot_general(a, b, (((1,), (1,)), ((), ())), preferred_element_type=F32)


def _rmsnorm_kernel(x_ref, w_ref, o_ref):
    o_ref[...] = _rms(x_ref[...], w_ref[...]).astype(o_ref.dtype)


def rmsnorm_rows(x, w, tm=512):
    m, d = x.shape
    return pl.pallas_call(
        _rmsnorm_kernel,
        grid=(m // tm,),
        in_specs=[pl.BlockSpec((tm, d), lambda i: (i, 0)), pl.BlockSpec((1, d), lambda i: (0, 0))],
        out_specs=pl.BlockSpec((tm, d), lambda i: (i, 0)),
        out_shape=jax.ShapeDtypeStruct((m, d), BF16),
        compiler_params=_params("parallel"),
        name="rmsnorm_rows",
    )(x, w.reshape(1, d))


def _matmul_kernel(x_ref, w_ref, o_ref):
    o_ref[...] = _dot(x_ref[...], w_ref[...]).astype(o_ref.dtype)


def matmul(x, w, out_dtype=F32, tm=512, tn=512):
    m, k = x.shape
    n = w.shape[1]
    tm, tn = min(tm, m), min(tn, n)
    return pl.pallas_call(
        _matmul_kernel,
        grid=(m // tm, n // tn),
        in_specs=[pl.BlockSpec((tm, k), lambda i, j: (i, 0)), pl.BlockSpec((k, tn), lambda i, j: (0, j))],
        out_specs=pl.BlockSpec((tm, tn), lambda i, j: (i, j)),
        out_shape=jax.ShapeDtypeStruct((m, n), out_dtype),
        compiler_params=_params("parallel", "arbitrary"),
        name="matmul",
    )(x, w)


def _split3(a):
    h = a.astype(BF16)
    r = a - h.astype(F32)
    m = r.astype(BF16)
    return h, m, (r - m.astype(F32)).astype(BF16)


def _ssd_kernel(z_ref, x_ref, b_ref, c_ref, dt_ref, dtt_ref, *rest):
    @pl.when(pl.program_id(2) == 0)
    def _():
        for ref in rest[-4:]:
            ref[...] = jnp.zeros_like(ref)

    L = x_ref.shape[0] // SSD_CHUNKS_PER_STEP
    for cc in range(SSD_CHUNKS_PER_STEP):
        rows = pl.ds(cc * L, L)
        _ssd_chunk(z_ref.at[rows], x_ref.at[rows], b_ref.at[rows], c_ref.at[rows], dt_ref.at[:, rows],
                   dtt_ref.at[:, :, rows], *rest[:9], rest[9].at[rows], *rest[10:])


def _ssd_chunk(z_ref, x_ref, b_ref, c_ref, dt_ref, dtt_ref, cwx_ref, cwb_ref, cwc_ref, cbx_ref, cbb_ref, cbc_ref,
               hp_ref, hpt_ref, nw_ref, y_ref, state_ref, ext_x_ref, ext_b_ref, ext_c_ref):
    L = x_ref.shape[0]
    H = L // 2

    def conv_silu(u_ref, ext_ref, w, b):
        ext_ref[0:SUBLANES, :] = ext_ref[L:L + SUBLANES, :]
        u = u_ref[...]
        ext_ref[SUBLANES:SUBLANES + L, :] = u
        acc = b + w[SSD_CONV - 1:SSD_CONV, :] * u
        for k in range(SSD_CONV - 1):
            acc = acc + w[k:k + 1, :] * ext_ref[pl.ds(SUBLANES - (SSD_CONV - 1) + k, L), :]
        return _silu(acc)

    xs = conv_silu(x_ref, ext_x_ref, cwx_ref[...], cbx_ref[...])
    bm = conv_silu(b_ref, ext_b_ref, cwb_ref[...], cbb_ref[...])
    cm = conv_silu(c_ref, ext_c_ref, cwc_ref[...], cbc_ref[...])
    bm16, cm16 = bm.astype(BF16), cm.astype(BF16)

    hp = hp_ref[0]
    hpt = hpt_ref[0]
    dt = _softplus(dt_ref[0] + hp[0:1, :])
    dtt = _softplus(dtt_ref[0] + hpt[:, 0:1])
    da = dt * (-jnp.exp(hp[1:2, :]))
    dat = dtt * (-jnp.exp(hpt[:, 1:2]))
    row = lax.broadcasted_iota(jnp.int32, (L, L), 0)
    col = lax.broadcasted_iota(jnp.int32, (L, L), 1)
    tril = jnp.where(row >= col, 1.0, 0.0).astype(BF16)
    triu = jnp.where(row <= col, 1.0, 0.0).astype(BF16)
    acum = sum(_dot(tril, part) for part in _split3(da)) * LOG2E
    acumt = sum(_dot(part, triu) for part in _split3(dat)) * LOG2E
    dtet = jnp.exp2(acumt[:, L - 1:L] - acumt)
    eact = jnp.exp2(acumt)
    cdec = jnp.exp2(acum[L - 1:L, :])

    cbt = _dot_nt(bm16, cm16)
    cbt00, cbt01, cbt11 = cbt[:H, :H], cbt[:H, H:], cbt[H:, H:]
    tri = lax.broadcasted_iota(jnp.int32, (H, H), 0) <= lax.broadcasted_iota(jnp.int32, (H, H), 1)
    xst = xs.T
    d_skip = hp[2:3, :]
    P = SSD_HEAD_DIM
    outs = []
    for p in range(HEADS_PER_GROUP // 2):
        hprev = state_ref[p]
        yoff = _dot_nt(hprev.astype(BF16), cm16)
        new_state = []
        for hh in range(2):
            h = 2 * p + hh
            rs = slice(hh * P, (hh + 1) * P)
            xt = xst[h * P:(h + 1) * P, :]
            xdt = xt * dtt[h:h + 1, :]
            xdt16 = xdt.astype(BF16)
            a_col, a_row = acum[:, h:h + 1], acumt[h:h + 1, :]
            d00 = jnp.exp2(jnp.where(tri, a_row[:, :H] - a_col[:H], -jnp.inf))
            d01 = jnp.exp2(a_row[:, H:] - a_col[:H])
            d11 = jnp.exp2(jnp.where(tri, a_row[:, H:] - a_col[H:], -jnp.inf))
            y_l = _dot(xdt16[:, :H], (cbt00 * d00).astype(BF16))
            y_r = _dot(xdt16[:, :H], (cbt01 * d01).astype(BF16)) + _dot(xdt16[:, H:], (cbt11 * d11).astype(BF16))
            y = jnp.concatenate([y_l, y_r], axis=1)
            outs.append(y + yoff[rs] * eact[h:h + 1, :] + d_skip[:, h:h + 1] * xt)
            new_state.append(hprev[rs] * cdec[:, h:h + 1] + _dot((xdt * dtet[h:h + 1, :]).astype(BF16), bm16))
        state_ref[p] = jnp.concatenate(new_state, axis=0)
    y = jnp.concatenate(outs, axis=0).T * _silu(z_ref[...])
    y_ref[...] = _rms(y, nw_ref[...]).astype(y_ref.dtype)


def ssd_branch(zx, small, conv_w, conv_b, dt_bias, a_log, d_skip, norm_w, batch, seq):
    n = batch * seq
    L = min(SSD_CHUNK, seq)
    nc = seq // L
    g, hpg = SSD_GROUPS, HEADS_PER_GROUP
    dt_raw = small[:, :SSD_HEADS].reshape(n, g, hpg)
    dt_g = dt_raw.transpose(1, 0, 2)
    dtt_g = dt_raw.transpose(1, 2, 0)
    hp = jnp.stack([dt_bias, a_log, d_skip], axis=0).reshape(3, g, hpg).transpose(1, 0, 2)
    hp = jnp.pad(hp, ((0, 0), (0, SUBLANES - 3), (0, 0)))
    hpt = hp.transpose(0, 2, 1)
    cw = conv_w
    cbias = conv_b.reshape(1, -1)
    nxb = SSD_INNER // GROUP_WIDTH
    assert nc % SSD_CHUNKS_PER_STEP == 0
    lb, nb = L * SSD_CHUNKS_PER_STEP, nc // SSD_CHUNKS_PER_STEP
    rowblk = lambda b, gi, c: b * nb + c
    bc0 = SSD_INNER // SSD_STATE
    grid = (g, batch, nb)
    in_specs = [
        pl.BlockSpec((lb, GROUP_WIDTH), lambda gi, b, c: (rowblk(b, gi, c), gi)),
        pl.BlockSpec((lb, GROUP_WIDTH), lambda gi, b, c: (rowblk(b, gi, c), nxb + gi)),
        pl.BlockSpec((lb, SSD_STATE), lambda gi, b, c: (rowblk(b, gi, c), 2 * bc0 + gi)),
        pl.BlockSpec((lb, SSD_STATE), lambda gi, b, c: (rowblk(b, gi, c), 2 * bc0 + g + gi)),
        pl.BlockSpec((1, lb, hpg), lambda gi, b, c: (gi, rowblk(b, gi, c), 0)),
        pl.BlockSpec((1, hpg, lb), lambda gi, b, c: (gi, 0, rowblk(b, gi, c))),
        pl.BlockSpec((SSD_CONV, GROUP_WIDTH), lambda gi, b, c: (0, gi)),
        pl.BlockSpec((SSD_CONV, SSD_STATE), lambda gi, b, c: (0, bc0 + gi)),
        pl.BlockSpec((SSD_CONV, SSD_STATE), lambda gi, b, c: (0, bc0 + g + gi)),
        pl.BlockSpec((1, GROUP_WIDTH), lambda gi, b, c: (0, gi)),
        pl.BlockSpec((1, SSD_STATE), lambda gi, b, c: (0, bc0 + gi)),
        pl.BlockSpec((1, SSD_STATE), lambda gi, b, c: (0, bc0 + g + gi)),
        pl.BlockSpec((1, SUBLANES, hpg), lambda gi, b, c: (gi, 0, 0)),
        pl.BlockSpec((1, hpg, SUBLANES), lambda gi, b, c: (gi, 0, 0)),
        pl.BlockSpec((1, GROUP_WIDTH), lambda gi, b, c: (0, gi)),
    ]
    return pl.pallas_call(
        _ssd_kernel,
        grid=grid,
        in_specs=in_specs,
        out_specs=pl.BlockSpec((lb, GROUP_WIDTH), lambda gi, b, c: (rowblk(b, gi, c), gi)),
        out_shape=jax.ShapeDtypeStruct((n, SSD_INNER), BF16),
        scratch_shapes=[pltpu.VMEM((HEADS_PER_GROUP // 2, LANES, SSD_STATE), F32),
                        pltpu.VMEM((L + SUBLANES, GROUP_WIDTH), F32), pltpu.VMEM((L + SUBLANES, SSD_STATE), F32),
                        pltpu.VMEM((L + SUBLANES, SSD_STATE), F32)],
        compiler_params=_params("parallel", "parallel", "arbitrary"),
        name="ssd_branch",
    )(zx, zx, zx, zx, dt_g, dtt_g, cw, cw, cw, cbias, cbias, cbias, hp, hpt, norm_w.reshape(1, -1))


def _nsa_prep_kernel(q_ref, kc_ref, vc_ref, ks_ref, vs_ref, kw_ref, vw_ref, cos_ref, sin_ref, qw_ref, kw3_ref,
                     qo_ref, kco_ref, vco_ref, kso_ref, vso_ref, kwo_ref, vwo_ref):
    cos, sin = cos_ref[...], sin_ref[...]

    def norm_rope(x, w):
        y = _rms(x, w)
        return y * cos + pltpu.roll(y, HEAD_DIM // 2, 1) * sin

    qw = qw_ref[...]
    tt = q_ref.shape[0]
    for g in range(NSA_KV_HEADS):
        sl = slice(g * HEAD_DIM, (g + 1) * HEAD_DIM)
        for r in range(NSA_REP):
            h = g * NSA_REP + r
            qh = norm_rope(q_ref[:, h * HEAD_DIM:(h + 1) * HEAD_DIM], qw) * Q_PRESCALE
            qo_ref[0, g, r] = qh.astype(qo_ref.dtype)
        kco_ref[0, g] = norm_rope(kc_ref[:, sl], kw3_ref[0:1, :])
        vco_ref[0, g] = vc_ref[:, sl]
        kso_ref[0, g] = norm_rope(ks_ref[:, sl], kw3_ref[1:2, :]).astype(kso_ref.dtype)
        kwo_ref[0, g] = norm_rope(kw_ref[:, sl], kw3_ref[2:3, :]).astype(kwo_ref.dtype)
        vst = vs_ref[:, sl].T.astype(vso_ref.dtype)
        for j in range(tt // SEL_KEY_TILE):
            vso_ref[0, g, j] = vst[:, j * SEL_KEY_TILE:(j + 1) * SEL_KEY_TILE]
        vwt = vw_ref[:, sl].T.astype(vwo_ref.dtype)
        for j in range(tt // WIN_TILE):
            vwo_ref[0, g, j] = vwt[:, j * WIN_TILE:(j + 1) * WIN_TILE]


def nsa_prep(qkv, q_norm_w, k_norm_w, batch, seq, tt=2 * SEL_KEY_TILE):
    half = HEAD_DIM // 2
    inv = ROPE_THETA ** (-jnp.arange(half, dtype=F32) / half)
    ang = jnp.arange(seq).astype(F32)[:, None] * inv[None, :]
    cos = jnp.concatenate([jnp.cos(ang), jnp.cos(ang)], axis=-1)
    sin = jnp.concatenate([-jnp.sin(ang), jnp.sin(ang)], axis=-1)
    nt = seq // tt
    kvb = NSA_WIDTH // NSA_KV_WIDTH
    kv_spec = lambda j: pl.BlockSpec((tt, NSA_KV_WIDTH), lambda b, t: (b * nt + t, kvb + j))
    head_spec = pl.BlockSpec((1, NSA_KV_HEADS, tt, HEAD_DIM), lambda b, t: (b, 0, t, 0))
    head_shape = lambda dt: jax.ShapeDtypeStruct((batch, NSA_KV_HEADS, seq, HEAD_DIM), dt)
    return pl.pallas_call(
        _nsa_prep_kernel,
        grid=(batch, nt),
        in_specs=[pl.BlockSpec((tt, NSA_WIDTH), lambda b, t: (b * nt + t, 0))] + [kv_spec(j) for j in range(6)] + [
            pl.BlockSpec((tt, HEAD_DIM), lambda b, t: (t, 0)), pl.BlockSpec((tt, HEAD_DIM), lambda b, t: (t, 0)),
            pl.BlockSpec((1, HEAD_DIM), lambda b, t: (0, 0)), pl.BlockSpec((3, HEAD_DIM), lambda b, t: (0, 0))],
        out_specs=[pl.BlockSpec((1, NSA_KV_HEADS, NSA_REP, tt, HEAD_DIM), lambda b, t: (b, 0, 0, t, 0)),
                   head_spec, head_spec, head_spec,
                   pl.BlockSpec((1, NSA_KV_HEADS, tt // SEL_KEY_TILE, HEAD_DIM, SEL_KEY_TILE),
                                lambda b, t: (b, 0, t, 0, 0)),
                   head_spec,
                   pl.BlockSpec((1, NSA_KV_HEADS, tt // WIN_TILE, HEAD_DIM, WIN_TILE), lambda b, t: (b, 0, t, 0, 0))],
        out_shape=[jax.ShapeDtypeStruct((batch, NSA_KV_HEADS, NSA_REP, seq, HEAD_DIM), BF16),
                   head_shape(F32), head_shape(F32), head_shape(BF16),
                   jax.ShapeDtypeStruct((batch, NSA_KV_HEADS, seq // SEL_KEY_TILE, HEAD_DIM, SEL_KEY_TILE), BF16),
                   head_shape(BF16),
                   jax.ShapeDtypeStruct((batch, NSA_KV_HEADS, seq // WIN_TILE, HEAD_DIM, WIN_TILE), BF16)],
        compiler_params=_params("parallel", "parallel"),
        name="nsa_prep",
    )(qkv, qkv, qkv, qkv, qkv, qkv, qkv, cos, sin, q_norm_w.reshape(1, -1), k_norm_w)


def _compress_kernel(x_ref, pe_ref, w1_ref, w2_ref, o_ref, *, ncmp, transpose_out):
    nseg = x_ref.shape[2] // CMP_STRIDE
    u = jnp.zeros((nseg, CMP_HID), F32)
    v = jnp.zeros((nseg, CMP_HID), F32)
    for j in range(CMP_STRIDE):
        xj = x_ref[0, 0, pl.ds(j, nseg, stride=CMP_STRIDE), :]
        u = u + _dot((xj + pe_ref[j:j + 1, :]).astype(BF16), w1_ref[j])
        v = v + _dot((xj + pe_ref[CMP_STRIDE + j:CMP_STRIDE + j + 1, :]).astype(BF16), w1_ref[CMP_STRIDE + j])
    hid = _silu(u + pltpu.roll(v, nseg - 1, 0))
    comp = _dot(hid.astype(BF16), w2_ref[...])
    rowi = lax.broadcasted_iota(jnp.int32, comp.shape, 0)
    comp = jnp.where(rowi < ncmp, comp, 0.0)
    o_ref[0, 0] = (comp.T if transpose_out else comp).astype(o_ref.dtype)


def compress(raw, pe, w1, w2, transpose_out):
    b, g, t, hd = raw.shape
    nseg = t // CMP_STRIDE
    full = lambda shape: pl.BlockSpec(shape, lambda i, j: (0,) * len(shape))
    out_dims = (hd, nseg) if transpose_out else (nseg, hd)
    return pl.pallas_call(
        functools.partial(_compress_kernel, ncmp=nseg - 1, transpose_out=transpose_out),
        grid=(b, g),
        in_specs=[pl.BlockSpec((1, 1, t, hd), lambda i, j: (i, j, 0, 0)), full((CMP_BLK, hd)),
                  full((CMP_BLK, hd, CMP_HID)), full((CMP_HID, hd))],
        out_specs=pl.BlockSpec((1, 1) + out_dims, lambda i, j: (i, j, 0, 0)),
        out_shape=jax.ShapeDtypeStruct((b, g) + out_dims, BF16),
        compiler_params=_params("parallel", "parallel"),
        name="nsa_compress",
    )(raw, pe, w1.astype(BF16), w2.astype(BF16))


def _nsa_attn_kernel(q_ref, kc_ref, vct_ref, ks_ref, vst_ref, kw_ref, vwt_ref, glt_ref, ovt_ref, o_ref,
                     selb_ref, acc_ref, ocw_ref, s_ref, p_ref, *, seq, n_sel):
    Q = q_ref.shape[3]
    q0 = pl.program_id(2) * Q
    rows = NSA_REP * Q
    q = q_ref[0, 0].reshape(rows, HEAD_DIM)
    tq = q0 + lax.broadcasted_iota(jnp.int32, (1, Q), 1)
    head = lambda r: slice(r * Q, (r + 1) * Q)

    ncp = kc_ref.shape[2]
    wk = WINDOW + Q
    kt_sz = vst_ref.shape[4]
    start = pl.multiple_of(jnp.clip(q0 - WINDOW, 0, seq - wk), WIN_TILE)
    s_cw = _dot_nt(jnp.concatenate([kc_ref[0, 0], kw_ref[0, 0, pl.ds(start, wk), :], ks_ref[0, 0, 0:kt_sz, :]],
                                   axis=0), q)
    s_c, s_w = s_cw[:ncp], s_cw[ncp:ncp + wk]
    s_ref[0] = s_cw[ncp + wk:]

    cend = lax.broadcasted_iota(jnp.int32, (ncp, 1), 0) * CMP_STRIDE + (CMP_BLK - 1)
    m_c = cend <= tq
    ps = []
    for r in range(NSA_REP):
        s = jnp.where(m_c, s_c[:, head(r)], NEG_INF)
        e = jnp.exp2(s - jnp.max(s, axis=0, keepdims=True))
        p = jnp.where(m_c, e * (1.0 / jnp.sum(e, axis=0, keepdims=True)), 0.0)
        ps.append(p.astype(BF16))
    p_c = jnp.concatenate(ps, axis=1)
    nsb = ovt_ref.shape[0]
    oc_imp = _dot(jnp.concatenate([vct_ref[0, 0], ovt_ref[...]], axis=0), p_c)
    o_c, imp4 = oc_imp[:HEAD_DIM], oc_imp[HEAD_DIM:]

    imp = imp4[:, head(0)]
    for r in range(1, NSA_REP):
        imp = imp + imp4[:, head(r)]
    blk_t = (q0 + lax.broadcasted_iota(jnp.int32, (nsb, Q), 1)) >> SEL_SHIFT
    sb = lax.broadcasted_iota(jnp.int32, (nsb, Q), 0)
    forced = (sb == 0) | (sb == blk_t) | (sb == blk_t - 1)
    imp = jnp.where(sb <= blk_t, jnp.where(forced, FORCE_SCORE, imp), -jnp.inf)
    rank = jnp.zeros((nsb, Q), F32)
    for i in range(nsb):
        ri = imp[i:i + 1, :]
        beats = (ri > imp) | ((ri == imp) & (sb > i))
        rank = rank + jnp.where(beats, 1.0, 0.0)
    selb_ref[...] = jnp.where(rank < n_sel, 0.0, NEG_INF)

    dist = tq - (start + lax.broadcasted_iota(jnp.int32, (wk, 1), 0))
    bias_w = jnp.where((dist >= 0) & (dist < WINDOW), 0.0, NEG_INF)
    ps, ls = [], []
    for r in range(NSA_REP):
        s = s_w[:, head(r)] + bias_w
        p = jnp.exp2(s - jnp.max(s, axis=0, keepdims=True))
        ls.append(jnp.sum(p, axis=0, keepdims=True))
        ps.append(p.astype(BF16))
    t0 = start // WIN_TILE
    vw_t = jnp.concatenate([vwt_ref[0, 0, t0 + j] for j in range(wk // WIN_TILE)], axis=1)
    o_w = _dot(vw_t, jnp.concatenate(ps, axis=1)) * (1.0 / jnp.concatenate(ls, axis=1))

    gate = _sigmoid(glt_ref[0, 0])
    for r in range(NSA_REP):
        ocw_ref[:, head(r)] = (gate[r:r + 1, :] * o_c[:, head(r)]
                               + gate[2 * NSA_REP + r:2 * NSA_REP + r + 1, :] * o_w[:, head(r)])

    blocks_per_tile = kt_sz // SEL_BLK
    n_all = seq // kt_sz
    acc_ref[...] = jnp.zeros_like(acc_ref)
    p_ref[1] = jnp.zeros(p_ref.shape[1:], p_ref.dtype)

    def sel_tile(kt, carry):
        m, l, alpha_prev = carry
        slot = kt & 1
        s_all = s_ref[slot]
        p_prev = p_ref[1 - slot]
        k_next = pl.multiple_of(jnp.minimum(kt + 1, n_all - 1) * kt_sz, kt_sz)
        s_ref[1 - slot] = _dot_nt(ks_ref[0, 0, pl.ds(k_next, kt_sz), :], q)
        pv_prev = _dot(vst_ref[0, 0, jnp.maximum(kt - 1, 0)], p_prev)
        acc_ref[...] = acc_ref[...] * alpha_prev + pv_prev
        k0 = kt * kt_sz
        kpos = k0 + lax.broadcasted_iota(jnp.int32, (kt_sz, 1), 0)
        bias = jnp.concatenate(
            [jnp.broadcast_to(selb_ref[pl.ds(kt * blocks_per_tile + i, 1), :], (SEL_BLK, Q))
             for i in range(blocks_per_tile)], axis=0)
        bias = jnp.where(kpos <= tq, bias, NEG_INF)
        ps, ms, ls, alphas = [], [], [], []
        for r in range(NSA_REP):
            s = s_all[:, head(r)] + bias
            m_old = m[:, head(r)]
            m_new = jnp.maximum(m_old, jnp.max(s, axis=0, keepdims=True))
            p = jnp.exp2(s - m_new)
            alpha = jnp.exp2(m_old - m_new)
            ls.append(alpha * l[:, head(r)] + jnp.sum(p, axis=0, keepdims=True))
            ms.append(m_new)
            alphas.append(alpha)
            ps.append(p.astype(BF16))
        p_ref[slot] = jnp.concatenate(ps, axis=1)
        return jnp.concatenate(ms, axis=1), jnp.concatenate(ls, axis=1), jnp.concatenate(alphas, axis=1)

    n_tiles = (q0 + Q + kt_sz - 1) // kt_sz
    init = (jnp.full((1, rows), NEG_INF, F32), jnp.zeros((1, rows), F32), jnp.ones((1, rows), F32))
    _, l_s, alpha_last = lax.fori_loop(0, n_tiles, sel_tile, init)
    pv_last = _dot(vst_ref[0, 0, n_tiles - 1], p_ref[(n_tiles - 1) & 1])
    o_s = (acc_ref[...] * alpha_last + pv_last) * (1.0 / l_s)

    for r in range(NSA_REP):
        out_t = ocw_ref[:, head(r)] + gate[NSA_REP + r:NSA_REP + r + 1, :] * o_s[:, head(r)]
        o_ref[:, r * HEAD_DIM:(r + 1) * HEAD_DIM] = out_t.T.astype(o_ref.dtype)


def nsa_attention(qh, kcc, vcct, ksn, vst, kwn, vwt, gate_logits_t, batch, seq):
    Q_BLK = NSA_Q_BLK
    assert seq % SEL_KEY_TILE == 0 and seq % Q_BLK == 0 and seq >= WINDOW + Q_BLK
    nq = seq // Q_BLK
    nsb = seq // SEL_BLK
    n_sel = min(N_SEL, nsb)
    ncp = kcc.shape[2]
    ncmp = (seq - CMP_BLK) // CMP_STRIDE + 1
    ci = np.arange(ncp)[None, :]
    sj = np.arange(nsb)[:, None]
    ovt = ((ci * CMP_STRIDE < (sj + 1) * SEL_BLK) & (ci * CMP_STRIDE + CMP_BLK > sj * SEL_BLK) & (ci < ncmp))
    ovt = jnp.asarray(ovt, BF16)
    seq_spec = pl.BlockSpec((1, 1, seq, HEAD_DIM), lambda b, g, i: (b, g, 0, 0))
    cmp_spec = pl.BlockSpec((1, 1, ncp, HEAD_DIM), lambda b, g, i: (b, g, 0, 0))
    tiled = lambda a: pl.BlockSpec((1, 1) + a.shape[2:], lambda b, g, i: (b, g, 0, 0, 0))
    return pl.pallas_call(
        functools.partial(_nsa_attn_kernel, seq=seq, n_sel=n_sel),
        grid=(batch, NSA_KV_HEADS, nq),
        in_specs=[pl.BlockSpec((1, 1, NSA_REP, Q_BLK, HEAD_DIM), lambda b, g, i: (b, g, 0, i, 0)),
                  cmp_spec, pl.BlockSpec((1, 1, HEAD_DIM, ncp), lambda b, g, i: (b, g, 0, 0)),
                  seq_spec, tiled(vst), seq_spec, tiled(vwt),
                  pl.BlockSpec((1, 1, 3 * NSA_REP, Q_BLK), lambda b, g, i: (b, g, 0, i)),
                  pl.BlockSpec((nsb, ncp), lambda b, g, i: (0, 0))],
        out_specs=pl.BlockSpec((Q_BLK, NSA_REP * HEAD_DIM), lambda b, g, i: (b * nq + i, g)),
        out_shape=jax.ShapeDtypeStruct((batch * seq, NSA_WIDTH), BF16),
        scratch_shapes=[pltpu.VMEM((nsb, Q_BLK), F32), pltpu.VMEM((HEAD_DIM, NSA_REP * Q_BLK), F32),
                        pltpu.VMEM((HEAD_DIM, NSA_REP * Q_BLK), F32),
                        pltpu.VMEM((2, SEL_KEY_TILE, NSA_REP * Q_BLK), F32),
                        pltpu.VMEM((2, SEL_KEY_TILE, NSA_REP * Q_BLK), BF16)],
        compiler_params=_params("parallel", "parallel", "arbitrary"),
        name="nsa_attention",
    )(qh, kcc, vcct, ksn, vst, kwn, vwt, gate_logits_t, ovt)


def _merge_kernel(ys_ref, yn_ref, ws_ref, wn_ref, gs_ref, gn_ref, o_ref):
    up_s = _dot(ys_ref[...], ws_ref[...])
    up_n = _dot(yn_ref[...], wn_ref[...])
    o_ref[...] = (_sigmoid(gs_ref[...]) * up_s + _sigmoid(gn_ref[...]) * up_n).astype(o_ref.dtype)


def merge_mixers(y_ssd, y_nsa, w_up_ssd, w_up_nsa, gates, tm=1024, tn=512):
    m = y_ssd.shape[0]
    nb = D_MODEL // tn
    return pl.pallas_call(
        _merge_kernel,
        grid=(m // tm, nb),
        in_specs=[pl.BlockSpec((tm, SSD_INNER), lambda i, j: (i, 0)), pl.BlockSpec((tm, NSA_WIDTH), lambda i, j: (i, 0)),
                  pl.BlockSpec((SSD_INNER, tn), lambda i, j: (0, j)), pl.BlockSpec((NSA_WIDTH, tn), lambda i, j: (0, j)),
                  pl.BlockSpec((tm, tn), lambda i, j: (i, j)), pl.BlockSpec((tm, tn), lambda i, j: (i, nb + j))],
        out_specs=pl.BlockSpec((tm, tn), lambda i, j: (i, j)),
        out_shape=jax.ShapeDtypeStruct((m, D_MODEL), BF16),
        compiler_params=_params("parallel", "arbitrary"),
        name="merge_mixers",
    )(y_ssd, y_nsa, w_up_ssd, w_up_nsa, gates, gates)


def _residual_matmul_kernel(h_ref, x_ref, w_ref, o_ref):
    o_ref[...] = h_ref[...] + _dot(x_ref[...], w_ref[...])


def residual_matmul(h, x, w, tm=1024, tn=1024):
    m, k = x.shape
    n = w.shape[1]
    return pl.pallas_call(
        _residual_matmul_kernel,
        grid=(m // tm, n // tn),
        in_specs=[pl.BlockSpec((tm, tn), lambda i, j: (i, j)), pl.BlockSpec((tm, k), lambda i, j: (i, 0)),
                  pl.BlockSpec((k, tn), lambda i, j: (0, j))],
        out_specs=pl.BlockSpec((tm, tn), lambda i, j: (i, j)),
        out_shape=jax.ShapeDtypeStruct((m, n), F32),
        compiler_params=_params("parallel", "arbitrary"),
        name="residual_matmul",
    )(h, x, w)


def _xattn_router_kernel(h_ref, kv_ref, n2_ref, wq_ref, qn_ref, kn_ref, wo_ref, n3_ref, rw_ref, rb_ref,
                         h2_ref, hf_ref, rt_ref):
    h = h_ref[...]
    qp = _dot(_rms(h, n2_ref[...]).astype(BF16), wq_ref[...])
    kv = kv_ref[...]
    scale = HEAD_DIM ** -0.5
    heads = []
    for hd in range(X_HEADS):
        sl = slice(hd * HEAD_DIM, (hd + 1) * HEAD_DIM)
        qh = _rms(qp[:, sl], qn_ref[...]).astype(BF16)
        kh = _rms(kv[:, sl], kn_ref[...]).astype(BF16)
        vh = kv[:, X_WIDTH + hd * HEAD_DIM:X_WIDTH + (hd + 1) * HEAD_DIM].astype(BF16)
        s = _dot_nt(qh, kh) * scale
        e = jnp.exp(s - jnp.max(s, axis=-1, keepdims=True))
        p = e / jnp.sum(e, axis=-1, keepdims=True)
        heads.append(_dot(p.astype(BF16), vh))
    o = jnp.concatenate(heads, axis=1).astype(BF16)
    h2 = h + _dot(o, wo_ref[...])
    h2_ref[...] = h2
    hf = _rms(h2, n3_ref[...]).astype(BF16)
    hf_ref[...] = hf
    lg = _dot(hf, rw_ref[...]) + rb_ref[...]

    lane = lax.broadcasted_iota(jnp.int32, lg.shape, 1)
    rmax = lambda v: jnp.max(v, axis=-1, keepdims=True)
    rsum = lambda v: jnp.sum(v, axis=-1, keepdims=True)
    first_lane = lambda hit: jnp.min(jnp.where(hit, lane, LANES), axis=-1, keepdims=True)
    is_g = lane < N_GROUPS
    eg = jnp.where(is_g, jnp.exp(lg - rmax(jnp.where(is_g, lg, -jnp.inf))), 0.0)
    pg = eg / rsum(eg)
    pg_top = rmax(pg)
    grp = first_lane(is_g & (pg == pg_top))
    in_grp = (lane >= N_GROUPS) & (((lane - N_GROUPS) >> EPG_SHIFT) == grp)
    ee = jnp.where(in_grp, jnp.exp(lg - rmax(jnp.where(in_grp, lg, -jnp.inf))), 0.0)
    pe = jnp.where(in_grp, ee / rsum(ee), -1.0)
    p1 = rmax(pe)
    i1 = first_lane(pe == p1)
    pe2 = jnp.where(lane == i1, -1.0, pe)
    p2 = rmax(pe2)
    i2 = first_lane(pe2 == p2)
    psum = p1 + p2
    route = jnp.where(lane == 0, (i1 - N_GROUPS).astype(F32),
                      jnp.where(lane == 1, (i2 - N_GROUPS).astype(F32),
                                jnp.where(lane == 2, pg_top * p1 / psum,
                                          jnp.where(lane == 3, pg_top * p2 / psum, 0.0))))
    rt_ref[...] = route


def xattn_router(h1, kv, norm2_w, wq, q_norm_w, k_norm_w, wo, norm3_w, router_w, router_b, batch, seq, tm=512):
    n = batch * seq
    tm = min(tm, seq)
    per_seq = seq // tm
    mlen = kv.shape[0] // batch
    full = lambda shape: pl.BlockSpec(shape, lambda i: (0,) * len(shape))
    return pl.pallas_call(
        _xattn_router_kernel,
        grid=(n // tm,),
        in_specs=[pl.BlockSpec((tm, D_MODEL), lambda i: (i, 0)),
                  pl.BlockSpec((mlen, 2 * X_WIDTH), lambda i: (i // per_seq, 0)),
                  full((1, D_MODEL)), full((D_MODEL, X_WIDTH)), full((1, HEAD_DIM)), full((1, HEAD_DIM)),
                  full((X_WIDTH, D_MODEL)), full((1, D_MODEL)), full((D_MODEL, LANES)), full((1, LANES))],
        out_specs=[pl.BlockSpec((tm, D_MODEL), lambda i: (i, 0)), pl.BlockSpec((tm, D_MODEL), lambda i: (i, 0)),
                   pl.BlockSpec((tm, LANES), lambda i: (i, 0))],
        out_shape=[jax.ShapeDtypeStruct((n, D_MODEL), F32), jax.ShapeDtypeStruct((n, D_MODEL), BF16),
                   jax.ShapeDtypeStruct((n, LANES), F32)],
        compiler_params=_params("parallel"),
        name="xattn_router",
    )(h1, kv, norm2_w.reshape(1, -1), wq, q_norm_w.reshape(1, -1), k_norm_w.reshape(1, -1), wo,
      norm3_w.reshape(1, -1), router_w, router_b)


def _moe_rank_kernel(e_ref, rank_ref, cnt_ref, carry_ref):
    @pl.when(pl.program_id(0) == 0)
    def _():
        carry_ref[...] = jnp.zeros_like(carry_ref)

    t = e_ref.shape[0]
    hit = e_ref[...] == lax.broadcasted_iota(jnp.int32, (t, LANES), 1)
    onehot = jnp.where(hit, 1.0, 0.0).astype(BF16)
    earlier = lax.broadcasted_iota(jnp.int32, (t, t), 0) > lax.broadcasted_iota(jnp.int32, (t, t), 1)
    before = _dot(jnp.where(earlier, 1.0, 0.0).astype(BF16), onehot) + carry_ref[...]
    rank_ref[...] = jnp.sum(jnp.where(hit, before, 0.0), axis=-1, keepdims=True).astype(jnp.int32)
    carry_ref[...] += jnp.sum(jnp.where(hit, 1.0, 0.0), axis=0, keepdims=True)
    cnt_ref[...] = carry_ref[...]


def moe_rank(eid, t=1024):
    a = eid.shape[0]
    return pl.pallas_call(
        _moe_rank_kernel,
        grid=(a // t,),
        in_specs=[pl.BlockSpec((t, 1), lambda i: (i, 0))],
        out_specs=[pl.BlockSpec((t, 1), lambda i: (i, 0)), pl.BlockSpec((1, LANES), lambda i: (0, 0))],
        out_shape=[jax.ShapeDtypeStruct((a, 1), jnp.int32), jax.ShapeDtypeStruct((1, LANES), F32)],
        scratch_shapes=[pltpu.VMEM((1, LANES), F32)],
        compiler_params=_params("arbitrary"),
        name="moe_rank",
    )(eid)


def _moe_ffn_kernel(item_e_ref, item_row_ref, item_nsub_ref, item_nout_ref, x_hbm, wg_ref, wu_ref, wd_ref, y_hbm,
                    x_vmem, y_vmem, sem_x, sem_y, *, hid):
    i, c = pl.program_id(0), pl.program_id(1)
    last_c = pl.num_programs(1) - 1
    th = wg_ref.shape[2]
    nsub, nout = item_nsub_ref[i], item_nout_ref[i]
    row0 = pl.multiple_of(item_row_ref[i], MOE_SUB)
    sub = lambda s: pl.ds(s * MOE_SUB, MOE_SUB)

    def x_copy(item, s):
        row = pl.multiple_of(item_row_ref[item], MOE_SUB)
        return pltpu.make_async_copy(x_hbm.at[pl.ds(row + s * MOE_SUB, MOE_SUB)], x_vmem.at[item & 1, sub(s)],
                                     sem_x.at[item & 1, s])

    def y_copy(row, s):
        return pltpu.make_async_copy(y_vmem.at[sub(s)], y_hbm.at[pl.ds(row + s * MOE_SUB, MOE_SUB)], sem_y.at[s])

    def for_subs(count, fn):
        for s in range(MOE_ITEM_SUBS):
            pl.when(s < count)(functools.partial(fn, s))

    @pl.when(c == 0)
    def _():
        @pl.when(i > 0)
        def _():
            prev_row = pl.multiple_of(item_row_ref[i - 1], MOE_SUB)
            for_subs(item_nout_ref[i - 1], lambda s: y_copy(prev_row, s).wait())

        @pl.when(i == 0)
        def _():
            for_subs(nsub, lambda s: x_copy(i, s).start())

        for_subs(nsub, lambda s: x_copy(i, s).wait())

        @pl.when(i + 1 < pl.num_programs(0))
        def _():
            nxt = jnp.minimum(i + 1, pl.num_programs(0) - 1)
            for_subs(item_nsub_ref[nxt], lambda s: x_copy(nxt, s).start())

        @pl.when(nsub == 0)
        def _():
            y_vmem[...] = jnp.zeros_like(y_vmem)
            for_subs(nout, lambda s: y_copy(row0, s).start())

    def ffn(count):
        rows = pl.ds(0, count * MOE_SUB)
        col_ok = c * th + lax.broadcasted_iota(jnp.int32, (1, th), 1) < hid
        row_ok = c * th + lax.broadcasted_iota(jnp.int32, (th, 1), 0) < hid
        wgu = jnp.concatenate([wg_ref[0], wu_ref[0]], axis=1).astype(BF16)
        wd = jnp.where(row_ok, wd_ref[0], 0.0).astype(BF16)
        gu = _dot(x_vmem[i & 1, rows, :], wgu)
        act = jnp.where(col_ok, _silu(gu[:, :th]) * gu[:, th:], 0.0)
        part = _dot(act.astype(BF16), wd)

        @pl.when(c == 0)
        def _():
            y_vmem[rows, :] = part

        @pl.when(c > 0)
        def _():
            y_vmem[rows, :] += part

    for count in range(1, MOE_ITEM_SUBS + 1):
        pl.when(nsub == count)(functools.partial(ffn, count))

    @pl.when((nsub > 0) & (c == last_c))
    def _():
        for_subs(nsub, lambda s: y_copy(row0, s).start())

    @pl.when((i == pl.num_programs(0) - 1) & (c == last_c))
    def _():
        for_subs(nout, lambda s: y_copy(row0, s).wait())


def moe_ffn(x_buf, item_e, item_row, item_nsub, item_nout, n_active, w_gate, w_up, w_down):
    rows, d = x_buf.shape
    hid = w_gate.shape[2]
    th = MOE_HID_TILE
    nc = pl.cdiv(hid, th)
    tile = lambda i, c, ns: jnp.where(ns[i] > 0, c, nc - 1)
    grid_spec = pltpu.PrefetchScalarGridSpec(
        num_scalar_prefetch=4,
        grid=(n_active, nc),
        in_specs=[pl.BlockSpec(memory_space=pl.ANY),
                  pl.BlockSpec((1, d, th), lambda i, c, ie, ir, ns, no: (ie[i], 0, tile(i, c, ns))),
                  pl.BlockSpec((1, d, th), lambda i, c, ie, ir, ns, no: (ie[i], 0, tile(i, c, ns))),
                  pl.BlockSpec((1, th, d), lambda i, c, ie, ir, ns, no: (ie[i], tile(i, c, ns), 0))],
        out_specs=pl.BlockSpec(memory_space=pl.ANY),
        scratch_shapes=[pltpu.VMEM((2, MOE_ITEM_SUBS * MOE_SUB, d), BF16),
                        pltpu.VMEM((MOE_ITEM_SUBS * MOE_SUB, d), F32),
                        pltpu.SemaphoreType.DMA((2, MOE_ITEM_SUBS)), pltpu.SemaphoreType.DMA((MOE_ITEM_SUBS,))],
    )
    return pl.pallas_call(
        functools.partial(_moe_ffn_kernel, hid=hid),
        grid_spec=grid_spec,
        out_shape=jax.ShapeDtypeStruct((rows, d), F32),
        compiler_params=_params("arbitrary", "arbitrary"),
        name="moe_ffn",
    )(item_e, item_row, item_nsub, item_nout, x_buf, w_gate, w_up, w_down)


def hier_moe(h2, hf, route, w_gate, w_up, w_down):
    n, d = h2.shape
    n_exp = w_gate.shape[0]
    i32 = jnp.int32
    eid = route[:, 0:TOP_K].astype(i32).reshape(-1)
    wts = route[:, TOP_K:2 * TOP_K]
    n_assign = n * TOP_K
    rank, counts = moe_rank(eid.reshape(n_assign, 1))
    counts = counts[0, :n_exp].astype(i32)
    subs_e = (counts + MOE_SUB - 1) // MOE_SUB
    sub_end = jnp.cumsum(subs_e)
    row_start = (sub_end - subs_e) * MOE_SUB
    is_e = eid[:, None] == jnp.arange(n_exp, dtype=i32)[None, :]
    dest = rank[:, 0] + jnp.sum(jnp.where(is_e, row_start[None, :], 0), axis=1)
    n_subs = (n_assign + n_exp * (MOE_SUB - 1) + MOE_SUB - 1) // MOE_SUB
    rows = n_subs * MOE_SUB
    tok = jnp.repeat(jnp.arange(n, dtype=i32), TOP_K)
    tok_buf = (jnp.arange(rows, dtype=i32) % n).at[dest].set(tok, unique_indices=True)
    per = MOE_ITEM_SUBS
    n_items = (n_exp * (per - 1) + n_subs + per - 1) // per
    items_e = (subs_e + per - 1) // per
    item_end = jnp.cumsum(items_e)
    n_used = item_end[-1]
    idx = jnp.arange(n_items, dtype=i32)
    used = idx < n_used
    e_of = jnp.minimum(jnp.searchsorted(item_end, idx, side='right'), n_exp - 1).astype(i32)
    j = idx - (item_end - items_e)[e_of]
    item_nsub = jnp.where(used, jnp.clip(subs_e[e_of] - per * j, 0, per), 0)
    fill_first = jnp.minimum(sub_end[-1] + per * (idx - n_used), n_subs)
    item_fill = jnp.where(used, 0, jnp.minimum(n_subs - fill_first, per))
    item_row = jnp.where(used, row_start[e_of] + j * per * MOE_SUB, jnp.minimum(fill_first, n_subs - 1) * MOE_SUB)
    item_e = jnp.where(used, e_of, e_of[jnp.maximum(n_used - 1, 0)])
    n_active = n_used + (n_subs - sub_end[-1] + per - 1) // per
    y_buf = moe_ffn(hf[tok_buf], item_e.astype(i32), item_row.astype(i32), item_nsub.astype(i32),
                    (item_nsub + item_fill).astype(i32), n_active.astype(i32), w_gate, w_up, w_down)
    slot = dest.reshape(n, TOP_K)
    y0 = y_buf.at[slot[:, 0]].get(unique_indices=True)
    y1 = y_buf.at[slot[:, 1]].get(unique_indices=True)
    return h2 + wts[:, 0:1] * y0 + wts[:, 1:2] * y1


COL_DT = ZX_COLS
COL_QKV = COL_DT + SSD_HEADS
COL_NSA_GATE = COL_QKV + QKV_COLS
COL_MERGE_GATE = COL_NSA_GATE + 3 * NSA_HEADS


def _small_w_in(w_in):
    pad = jnp.zeros(w_in.shape[:2] + (SMALL_COLS - SSD_HEADS - 3 * NSA_HEADS,), w_in.dtype)
    return jnp.concatenate([w_in[:, :, COL_DT:COL_QKV], w_in[:, :, COL_NSA_GATE:COL_MERGE_GATE], pad], axis=2)


def _layer(x, mem, norm1_w, w_in, ssd_conv_w, ssd_conv_b, ssd_dt_bias, ssd_a_log, ssd_d, ssd_norm_w,
           nsa_q_norm_w, nsa_k_norm_w, cmp_pe_k, cmp_w1_k, cmp_w2_k, cmp_pe_v, cmp_w1_v, cmp_w2_v,
           w_up_ssd, w_up_nsa, w_out, norm2_w, mem_norm_w, xq_w, xkv_w, x_q_norm_w, x_k_norm_w, xo_w,
           norm3_w, router_g_w, router_g_b, router_e_w, router_e_b, moe_w_gate, moe_w_up, moe_w_down):
    batch, seq, d = x.shape
    n = batch * seq
    xf = x.reshape(n, d)
    hn = rmsnorm_rows(xf, norm1_w)
    w_cols = lambda c0, nc: w_in[0, :, c0:c0 + nc].astype(BF16)
    zx = matmul(hn, w_cols(0, ZX_COLS), tm=IN_PROJ_TM, tn=IN_PROJ_TN)
    qkv = matmul(hn, w_cols(COL_QKV, QKV_COLS), tm=IN_PROJ_TM, tn=IN_PROJ_TN)
    mgates = matmul(hn, w_cols(COL_MERGE_GATE, GATE_COLS), tm=IN_PROJ_TM, tn=IN_PROJ_TN)
    small = matmul(hn, _small_w_in(w_in)[0].astype(BF16), tm=IN_PROJ_TM)
    y_ssd = ssd_branch(zx, small, ssd_conv_w, ssd_conv_b, ssd_dt_bias, ssd_a_log, ssd_d, ssd_norm_w, batch, seq)
    qh, kcn, vcr, ksn, vsn, kwn, vwn = nsa_prep(qkv, nsa_q_norm_w, nsa_k_norm_w, batch, seq)
    kcc = compress(kcn, cmp_pe_k, cmp_w1_k, cmp_w2_k, transpose_out=False)
    vcct = compress(vcr, cmp_pe_v, cmp_w1_v, cmp_w2_v, transpose_out=True)
    gl = small[:, SSD_HEADS:SSD_HEADS + 3 * NSA_HEADS].reshape(batch, seq, 3, NSA_KV_HEADS, NSA_REP)
    gl = gl.transpose(0, 3, 2, 4, 1).reshape(batch, NSA_KV_HEADS, 3 * NSA_REP, seq)
    y_nsa = nsa_attention(qh, kcc, vcct, ksn, vsn, kwn, vwn, gl, batch, seq)
    merged = merge_mixers(y_ssd, y_nsa, w_up_ssd.astype(BF16), w_up_nsa.astype(BF16), mgates)
    h1 = residual_matmul(xf, merged, w_out.astype(BF16))
    mlen = mem.shape[1]
    mn = rmsnorm_rows(mem.reshape(batch * mlen, d), mem_norm_w, tm=min(512, batch * mlen))
    kv = matmul(mn, xkv_w.astype(BF16))
    n_exp = router_e_w.shape[1]
    rpad = LANES - N_GROUPS - n_exp
    router_w = jnp.concatenate([router_g_w, router_e_w, jnp.zeros((d, rpad), F32)], axis=1).astype(BF16)
    router_b = jnp.concatenate([router_g_b, router_e_b, jnp.zeros((rpad,), F32)]).reshape(1, LANES)
    h2, hf, route = xattn_router(h1, kv, norm2_w, xq_w.astype(BF16), x_q_norm_w, x_k_norm_w, xo_w.astype(BF16),
                                 norm3_w, router_w, router_b, batch, seq)
    out = hier_moe(h2, hf, route, moe_w_gate, moe_w_up, moe_w_down)
    return out.reshape(batch, seq, d)


def kernel(x, mem, norm1_w, w_in, ssd_conv_w, ssd_conv_b, ssd_dt_bias, ssd_a_log, ssd_d, ssd_norm_w, nsa_q_norm_w, nsa_k_norm_w, cmp_pe_k, cmp_w1_k, cmp_w2_k, cmp_pe_v, cmp_w1_v, cmp_w2_v, w_up_ssd, w_up_nsa, w_out, norm2_w, mem_norm_w, xq_w, xkv_w, x_q_norm_w, x_k_norm_w, xo_w, norm3_w, router_g_w, router_g_b, router_e_w, router_e_b, moe_w_gate, moe_w_up, moe_w_down):
    h = x
    for l in range(norm1_w.shape[0]):
        h = _layer(h, mem, norm1_w[l], w_in[l:l + 1], ssd_conv_w[l], ssd_conv_b[l], ssd_dt_bias[l], ssd_a_log[l], ssd_d[l],
                   ssd_norm_w[l], nsa_q_norm_w[l], nsa_k_norm_w[l], cmp_pe_k[l], cmp_w1_k[l], cmp_w2_k[l], cmp_pe_v[l],
                   cmp_w1_v[l], cmp_w2_v[l], w_up_ssd[l], w_up_nsa[l], w_out[l], norm2_w[l], mem_norm_w[l], xq_w[l],
                   xkv_w[l], x_q_norm_w[l], x_k_norm_w[l], xo_w[l], norm3_w[l], router_g_w[l], router_g_b[l],
                   router_e_w[l], router_e_b[l], moe_w_gate[l], moe_w_up[l], moe_w_down[l])
    return h.astype(x.dtype)
```

```python
import functools

import numpy as np
import jax
import jax.numpy as jnp
from jax import lax
from jax.experimental import pallas as pl
from jax.experimental.pallas import tpu as pltpu

F32 = jnp.float32
BF16 = jnp.bfloat16

D_MODEL = 2048
SSD_INNER = 4096
SSD_HEAD_DIM = 64
SSD_HEADS = 64
SSD_GROUPS = 8
SSD_STATE = 128
SSD_CONV = 4
SSD_CHUNK = 256
SSD_CHUNKS_PER_STEP = 4
HEADS_PER_GROUP = SSD_HEADS // SSD_GROUPS
GROUP_WIDTH = SSD_INNER // SSD_GROUPS
HEAD_DIM = 128
NSA_HEADS = 16
NSA_KV_HEADS = 4
NSA_REP = NSA_HEADS // NSA_KV_HEADS
NSA_WIDTH = NSA_HEADS * HEAD_DIM
NSA_KV_WIDTH = NSA_KV_HEADS * HEAD_DIM
CMP_BLK = 32
CMP_STRIDE = 16
CMP_HID = 256
SEL_BLK = 64
SEL_SHIFT = 6
N_SEL = 16
WINDOW = 512
NSA_Q_BLK = 256
WIN_TILE = 128
SEL_KEY_TILE = 256
FORCE_SCORE = 1.0e4
X_HEADS = 4
X_WIDTH = X_HEADS * HEAD_DIM
N_GROUPS = 8
EXPERTS_PER_GROUP = 8
EPG_SHIFT = 3
N_EXPERTS = 64
TOP_K = 2
EXPERT_HIDDEN = 1408
MOE_SUB = 128
MOE_ITEM_SUBS = 8
MOE_HID_TILE = 256
ROPE_THETA = 10000.0
EPS = 1e-6
NEG_INF = -1e30
LOG2E = float(np.log2(np.e))
Q_PRESCALE = HEAD_DIM ** -0.5 * LOG2E
SUBLANES = 8
LANES = 128
VMEM_LIMIT = 56 * 1024 * 1024

ZX_COLS = SSD_INNER + SSD_INNER + 2 * SSD_GROUPS * SSD_STATE
QKV_COLS = NSA_WIDTH + 6 * NSA_KV_WIDTH
GATE_COLS = 2 * D_MODEL
SMALL_COLS = LANES
IN_PROJ_TM, IN_PROJ_TN = 2048, 1024


def _params(*sem):
    return pltpu.CompilerParams(dimension_semantics=sem, vmem_limit_bytes=VMEM_LIMIT)


def _rms(x, w):
    ms = jnp.mean(x * x, axis=-1, keepdims=True)
    return x * lax.rsqrt(ms + EPS) * w


def _sigmoid(x):
    return 0.5 + 0.5 * jnp.tanh(0.5 * x)


def _silu(x):
    h = 0.5 * x
    return h + h * jnp.tanh(h)


def _softplus(x):
    return jnp.maximum(x, 0.0) + jnp.log1p(jnp.exp(-jnp.abs(x)))


def _dot(a, b):
    return jnp.dot(a, b, preferred_element_type=F32)


def _dot_nt(a, b):
    return lax.dot_general(a, b, (((1,), (1,)), ((), ())), preferred_element_type=F32)


def _rmsnorm_kernel(x_ref, w_ref, o_ref):
    o_ref[...] = _rms(x_ref[...], w_ref[...]).astype(o_ref.dtype)


def rmsnorm_rows(x, w, tm=512):
    m, d = x.shape
    return pl.pallas_call(
        _rmsnorm_kernel,
        grid=(m // tm,),
        in_specs=[pl.BlockSpec((tm, d), lambda i: (i, 0)), pl.BlockSpec((1, d), lambda i: (0, 0))],
        out_specs=pl.BlockSpec((tm, d), lambda i: (i, 0)),
        out_shape=jax.ShapeDtypeStruct((m, d), BF16),
        compiler_params=_params("parallel"),
        name="rmsnorm_rows",
    )(x, w.reshape(1, d))


def _matmul_kernel(x_ref, w_ref, o_ref):
    o_ref[...] = _dot(x_ref[...], w_ref[...]).astype(o_ref.dtype)


def matmul(x, w, out_dtype=F32, tm=512, tn=512, fuse_w_producer=False):
    m, k = x.shape
    n = w.shape[1]
    tm, tn = min(tm, m), min(tn, n)
    params = pltpu.CompilerParams(dimension_semantics=("parallel", "arbitrary"), vmem_limit_bytes=VMEM_LIMIT,
                                  allow_input_fusion=[False, True] if fuse_w_producer else None)
    return pl.pallas_call(
        _matmul_kernel,
        grid=(m // tm, n // tn),
        in_specs=[pl.BlockSpec((tm, k), lambda i, j: (i, 0)), pl.BlockSpec((k, tn), lambda i, j: (0, j))],
        out_specs=pl.BlockSpec((tm, tn), lambda i, j: (i, j)),
        out_shape=jax.ShapeDtypeStruct((m, n), out_dtype),
        compiler_params=params,
        name="matmul",
    )(x, w)


def _split3(a):
    h = a.astype(BF16)
    r = a - h.astype(F32)
    m = r.astype(BF16)
    return h, m, (r - m.astype(F32)).astype(BF16)


def _ssd_kernel(z_ref, x_ref, b_ref, c_ref, dt_ref, dtt_ref, *rest):
    @pl.when(pl.program_id(2) == 0)
    def _():
        for ref in rest[-4:]:
            ref[...] = jnp.zeros_like(ref)

    L = x_ref.shape[0] // SSD_CHUNKS_PER_STEP
    for cc in range(SSD_CHUNKS_PER_STEP):
        rows = pl.ds(cc * L, L)
        _ssd_chunk(z_ref.at[rows], x_ref.at[rows], b_ref.at[rows], c_ref.at[rows], dt_ref.at[:, rows],
                   dtt_ref.at[:, :, rows], *rest[:9], rest[9].at[rows], *rest[10:])


def _ssd_chunk(z_ref, x_ref, b_ref, c_ref, dt_ref, dtt_ref, cwx_ref, cwb_ref, cwc_ref, cbx_ref, cbb_ref, cbc_ref,
               hp_ref, hpt_ref, nw_ref, y_ref, state_ref, ext_x_ref, ext_b_ref, ext_c_ref):
    L = x_ref.shape[0]
    H = L // 2

    def conv_silu(u_ref, ext_ref, w, b):
        ext_ref[0:SUBLANES, :] = ext_ref[L:L + SUBLANES, :]
        u = u_ref[...]
        ext_ref[SUBLANES:SUBLANES + L, :] = u
        acc = b + w[SSD_CONV - 1:SSD_CONV, :] * u
        for k in range(SSD_CONV - 1):
            acc = acc + w[k:k + 1, :] * ext_ref[pl.ds(SUBLANES - (SSD_CONV - 1) + k, L), :]
        return _silu(acc)

    xs = conv_silu(x_ref, ext_x_ref, cwx_ref[...], cbx_ref[...])
    bm = conv_silu(b_ref, ext_b_ref, cwb_ref[...], cbb_ref[...])
    cm = conv_silu(c_ref, ext_c_ref, cwc_ref[...], cbc_ref[...])
    bm16, cm16 = bm.astype(BF16), cm.astype(BF16)

    hp = hp_ref[0]
    hpt = hpt_ref[0]
    dt = _softplus(dt_ref[0] + hp[0:1, :])
    dtt = _softplus(dtt_ref[0] + hpt[:, 0:1])
    da = dt * (-jnp.exp(hp[1:2, :]))
    dat = dtt * (-jnp.exp(hpt[:, 1:2]))
    row = lax.broadcasted_iota(jnp.int32, (L, L), 0)
    col = lax.broadcasted_iota(jnp.int32, (L, L), 1)
    tril = jnp.where(row >= col, 1.0, 0.0).astype(BF16)
    triu = jnp.where(row <= col, 1.0, 0.0).astype(BF16)
    acum = sum(_dot(tril, part) for part in _split3(da)) * LOG2E
    acumt = sum(_dot(part, triu) for part in _split3(dat)) * LOG2E
    dtet = jnp.exp2(acumt[:, L - 1:L] - acumt)
    eact = jnp.exp2(acumt)
    cdec = jnp.exp2(acum[L - 1:L, :])

    cbt = _dot_nt(bm16, cm16)
    cbt00, cbt01, cbt11 = cbt[:H, :H], cbt[:H, H:], cbt[H:, H:]
    tri = lax.broadcasted_iota(jnp.int32, (H, H), 0) <= lax.broadcasted_iota(jnp.int32, (H, H), 1)
    xst = xs.T
    d_skip = hp[2:3, :]
    P = SSD_HEAD_DIM
    outs = []
    for p in range(HEADS_PER_GROUP // 2):
        hprev = state_ref[p]
        yoff = _dot_nt(hprev.astype(BF16), cm16)
        new_state = []
        for hh in range(2):
            h = 2 * p + hh
            rs = slice(hh * P, (hh + 1) * P)
            xt = xst[h * P:(h + 1) * P, :]
            xdt = xt * dtt[h:h + 1, :]
            xdt16 = xdt.astype(BF16)
            a_col, a_row = acum[:, h:h + 1], acumt[h:h + 1, :]
            d00 = jnp.exp2(jnp.where(tri, a_row[:, :H] - a_col[:H], -jnp.inf))
            d01 = jnp.exp2(a_row[:, H:] - a_col[:H])
            d11 = jnp.exp2(jnp.where(tri, a_row[:, H:] - a_col[H:], -jnp.inf))
            y_l = _dot(xdt16[:, :H], (cbt00 * d00).astype(BF16))
            y_r = _dot(xdt16[:, :H], (cbt01 * d01).astype(BF16)) + _dot(xdt16[:, H:], (cbt11 * d11).astype(BF16))
            y = jnp.concatenate([y_l, y_r], axis=1)
            outs.append(y + yoff[rs] * eact[h:h + 1, :] + d_skip[:, h:h + 1] * xt)
            new_state.append(hprev[rs] * cdec[:, h:h + 1] + _dot((xdt * dtet[h:h + 1, :]).astype(BF16), bm16))
        state_ref[p] = jnp.concatenate(new_state, axis=0)
    y = jnp.concatenate(outs, axis=0).T * _silu(z_ref[...])
    y_ref[...] = _rms(y, nw_ref[...]).astype(y_ref.dtype)


def ssd_branch(zx, small, conv_w, conv_b, dt_bias, a_log, d_skip, norm_w, batch, seq):
    n = batch * seq
    L = min(SSD_CHUNK, seq)
    nc = seq // L
    g, hpg = SSD_GROUPS, HEADS_PER_GROUP
    dt_raw = small[:, :SSD_HEADS].reshape(n, g, hpg)
    dt_g = dt_raw.transpose(1, 0, 2)
    dtt_g = dt_raw.transpose(1, 2, 0)
    hp = jnp.stack([dt_bias, a_log, d_skip], axis=0).reshape(3, g, hpg).transpose(1, 0, 2)
    hp = jnp.pad(hp, ((0, 0), (0, SUBLANES - 3), (0, 0)))
    hpt = hp.transpose(0, 2, 1)
    cw = conv_w
    cbias = conv_b.reshape(1, -1)
    nxb = SSD_INNER // GROUP_WIDTH
    assert nc % SSD_CHUNKS_PER_STEP == 0
    lb, nb = L * SSD_CHUNKS_PER_STEP, nc // SSD_CHUNKS_PER_STEP
    rowblk = lambda b, gi, c: b * nb + c
    bc0 = SSD_INNER // SSD_STATE
    grid = (g, batch, nb)
    in_specs = [
        pl.BlockSpec((lb, GROUP_WIDTH), lambda gi, b, c: (rowblk(b, gi, c), gi)),
        pl.BlockSpec((lb, GROUP_WIDTH), lambda gi, b, c: (rowblk(b, gi, c), nxb + gi)),
        pl.BlockSpec((lb, SSD_STATE), lambda gi, b, c: (rowblk(b, gi, c), 2 * bc0 + gi)),
        pl.BlockSpec((lb, SSD_STATE), lambda gi, b, c: (rowblk(b, gi, c), 2 * bc0 + g + gi)),
        pl.BlockSpec((1, lb, hpg), lambda gi, b, c: (gi, rowblk(b, gi, c), 0)),
        pl.BlockSpec((1, hpg, lb), lambda gi, b, c: (gi, 0, rowblk(b, gi, c))),
        pl.BlockSpec((SSD_CONV, GROUP_WIDTH), lambda gi, b, c: (0, gi)),
        pl.BlockSpec((SSD_CONV, SSD_STATE), lambda gi, b, c: (0, bc0 + gi)),
        pl.BlockSpec((SSD_CONV, SSD_STATE), lambda gi, b, c: (0, bc0 + g + gi)),
        pl.BlockSpec((1, GROUP_WIDTH), lambda gi, b, c: (0, gi)),
        pl.BlockSpec((1, SSD_STATE), lambda gi, b, c: (0, bc0 + gi)),
        pl.BlockSpec((1, SSD_STATE), lambda gi, b, c: (0, bc0 + g + gi)),
        pl.BlockSpec((1, SUBLANES, hpg), lambda gi, b, c: (gi, 0, 0)),
        pl.BlockSpec((1, hpg, SUBLANES), lambda gi, b, c: (gi, 0, 0)),
        pl.BlockSpec((1, GROUP_WIDTH), lambda gi, b, c: (0, gi)),
    ]
    return pl.pallas_call(
        _ssd_kernel,
        grid=grid,
        in_specs=in_specs,
        out_specs=pl.BlockSpec((lb, GROUP_WIDTH), lambda gi, b, c: (rowblk(b, gi, c), gi)),
        out_shape=jax.ShapeDtypeStruct((n, SSD_INNER), BF16),
        scratch_shapes=[pltpu.VMEM((HEADS_PER_GROUP // 2, LANES, SSD_STATE), F32),
                        pltpu.VMEM((L + SUBLANES, GROUP_WIDTH), F32), pltpu.VMEM((L + SUBLANES, SSD_STATE), F32),
                        pltpu.VMEM((L + SUBLANES, SSD_STATE), F32)],
        compiler_params=_params("parallel", "parallel", "arbitrary"),
        name="ssd_branch",
    )(zx, zx, zx, zx, dt_g, dtt_g, cw, cw, cw, cbias, cbias, cbias, hp, hpt, norm_w.reshape(1, -1))


def _nsa_prep_kernel(q_ref, kc_ref, vc_ref, ks_ref, vs_ref, kw_ref, vw_ref, cos_ref, sin_ref, qw_ref, kw3_ref,
                     qo_ref, kco_ref, vco_ref, kso_ref, vso_ref, kwo_ref, vwo_ref):
    cos, sin = cos_ref[...], sin_ref[...]

    def norm_rope(x, w):
        y = _rms(x, w)
        return y * cos + pltpu.roll(y, HEAD_DIM // 2, 1) * sin

    qw = qw_ref[...]
    tt = q_ref.shape[0]
    for g in range(NSA_KV_HEADS):
        sl = slice(g * HEAD_DIM, (g + 1) * HEAD_DIM)
        for r in range(NSA_REP):
            h = g * NSA_REP + r
            qh = norm_rope(q_ref[:, h * HEAD_DIM:(h + 1) * HEAD_DIM], qw) * Q_PRESCALE
            qo_ref[0, g, r] = qh.astype(qo_ref.dtype)
        kco_ref[0, g] = norm_rope(kc_ref[:, sl], kw3_ref[0:1, :])
        vco_ref[0, g] = vc_ref[:, sl]
        kso_ref[0, g] = norm_rope(ks_ref[:, sl], kw3_ref[1:2, :]).astype(kso_ref.dtype)
        kwo_ref[0, g] = norm_rope(kw_ref[:, sl], kw3_ref[2:3, :]).astype(kwo_ref.dtype)
        vst = vs_ref[:, sl].T.astype(vso_ref.dtype)
        for j in range(tt // SEL_KEY_TILE):
            vso_ref[0, g, j] = vst[:, j * SEL_KEY_TILE:(j + 1) * SEL_KEY_TILE]
        vwt = vw_ref[:, sl].T.astype(vwo_ref.dtype)
        for j in range(tt // WIN_TILE):
            vwo_ref[0, g, j] = vwt[:, j * WIN_TILE:(j + 1) * WIN_TILE]


def nsa_prep(qkv, q_norm_w, k_norm_w, batch, seq, tt=2 * SEL_KEY_TILE):
    half = HEAD_DIM // 2
    inv = ROPE_THETA ** (-jnp.arange(half, dtype=F32) / half)
    ang = jnp.arange(seq).astype(F32)[:, None] * inv[None, :]
    cos = jnp.concatenate([jnp.cos(ang), jnp.cos(ang)], axis=-1)
    sin = jnp.concatenate([-jnp.sin(ang), jnp.sin(ang)], axis=-1)
    nt = seq // tt
    kvb = NSA_WIDTH // NSA_KV_WIDTH
    kv_spec = lambda j: pl.BlockSpec((tt, NSA_KV_WIDTH), lambda b, t: (b * nt + t, kvb + j))
    head_spec = pl.BlockSpec((1, NSA_KV_HEADS, tt, HEAD_DIM), lambda b, t: (b, 0, t, 0))
    head_shape = lambda dt: jax.ShapeDtypeStruct((batch, NSA_KV_HEADS, seq, HEAD_DIM), dt)
    return pl.pallas_call(
        _nsa_prep_kernel,
        grid=(batch, nt),
        in_specs=[pl.BlockSpec((tt, NSA_WIDTH), lambda b, t: (b * nt + t, 0))] + [kv_spec(j) for j in range(6)] + [
            pl.BlockSpec((tt, HEAD_DIM), lambda b, t: (t, 0)), pl.BlockSpec((tt, HEAD_DIM), lambda b, t: (t, 0)),
            pl.BlockSpec((1, HEAD_DIM), lambda b, t: (0, 0)), pl.BlockSpec((3, HEAD_DIM), lambda b, t: (0, 0))],
        out_specs=[pl.BlockSpec((1, NSA_KV_HEADS, NSA_REP, tt, HEAD_DIM), lambda b, t: (b, 0, 0, t, 0)),
                   head_spec, head_spec, head_spec,
                   pl.BlockSpec((1, NSA_KV_HEADS, tt // SEL_KEY_TILE, HEAD_DIM, SEL_KEY_TILE),
                                lambda b, t: (b, 0, t, 0, 0)),
                   head_spec,
                   pl.BlockSpec((1, NSA_KV_HEADS, tt // WIN_TILE, HEAD_DIM, WIN_TILE), lambda b, t: (b, 0, t, 0, 0))],
        out_shape=[jax.ShapeDtypeStruct((batch, NSA_KV_HEADS, NSA_REP, seq, HEAD_DIM), BF16),
                   head_shape(F32), head_shape(F32), head_shape(BF16),
                   jax.ShapeDtypeStruct((batch, NSA_KV_HEADS, seq // SEL_KEY_TILE, HEAD_DIM, SEL_KEY_TILE), BF16),
                   head_shape(BF16),
                   jax.ShapeDtypeStruct((batch, NSA_KV_HEADS, seq // WIN_TILE, HEAD_DIM, WIN_TILE), BF16)],
        compiler_params=_params("parallel", "parallel"),
        name="nsa_prep",
    )(qkv, qkv, qkv, qkv, qkv, qkv, qkv, cos, sin, q_norm_w.reshape(1, -1), k_norm_w)


def _compress_kernel(x_ref, pe_ref, w1_ref, w2_ref, o_ref, *, ncmp, transpose_out):
    nseg = x_ref.shape[2] // CMP_STRIDE
    u = jnp.zeros((nseg, CMP_HID), F32)
    v = jnp.zeros((nseg, CMP_HID), F32)
    for j in range(CMP_STRIDE):
        xj = x_ref[0, 0, pl.ds(j, nseg, stride=CMP_STRIDE), :]
        u = u + _dot((xj + pe_ref[j:j + 1, :]).astype(BF16), w1_ref[j])
        v = v + _dot((xj + pe_ref[CMP_STRIDE + j:CMP_STRIDE + j + 1, :]).astype(BF16), w1_ref[CMP_STRIDE + j])
    hid = _silu(u + pltpu.roll(v, nseg - 1, 0))
    comp = _dot(hid.astype(BF16), w2_ref[...])
    rowi = lax.broadcasted_iota(jnp.int32, comp.shape, 0)
    comp = jnp.where(rowi < ncmp, comp, 0.0)
    o_ref[0, 0] = (comp.T if transpose_out else comp).astype(o_ref.dtype)


def compress(raw, pe, w1, w2, transpose_out):
    b, g, t, hd = raw.shape
    nseg = t // CMP_STRIDE
    full = lambda shape: pl.BlockSpec(shape, lambda i, j: (0,) * len(shape))
    out_dims = (hd, nseg) if transpose_out else (nseg, hd)
    return pl.pallas_call(
        functools.partial(_compress_kernel, ncmp=nseg - 1, transpose_out=transpose_out),
        grid=(b, g),
        in_specs=[pl.BlockSpec((1, 1, t, hd), lambda i, j: (i, j, 0, 0)), full((CMP_BLK, hd)),
                  full((CMP_BLK, hd, CMP_HID)), full((CMP_HID, hd))],
        out_specs=pl.BlockSpec((1, 1) + out_dims, lambda i, j: (i, j, 0, 0)),
        out_shape=jax.ShapeDtypeStruct((b, g) + out_dims, BF16),
        compiler_params=_params("parallel", "parallel"),
        name="nsa_compress",
    )(raw, pe, w1.astype(BF16), w2.astype(BF16))


def _nsa_attn_kernel(q_ref, kc_ref, vct_ref, ks_ref, vst_ref, kw_ref, vwt_ref, glt_ref, ovt_ref, o_ref,
                     selb_ref, acc_ref, ocw_ref, s_ref, p_ref, *, seq, n_sel):
    Q = q_ref.shape[3]
    q0 = pl.program_id(2) * Q
    rows = NSA_REP * Q
    q = q_ref[0, 0].reshape(rows, HEAD_DIM)
    tq = q0 + lax.broadcasted_iota(jnp.int32, (1, Q), 1)
    head = lambda r: slice(r * Q, (r + 1) * Q)

    ncp = kc_ref.shape[2]
    wk = WINDOW + Q
    kt_sz = vst_ref.shape[4]
    start = pl.multiple_of(jnp.clip(q0 - WINDOW, 0, seq - wk), WIN_TILE)
    s_cw = _dot_nt(jnp.concatenate([kc_ref[0, 0], kw_ref[0, 0, pl.ds(start, wk), :], ks_ref[0, 0, 0:kt_sz, :]],
                                   axis=0), q)
    s_c, s_w = s_cw[:ncp], s_cw[ncp:ncp + wk]
    s_ref[0] = s_cw[ncp + wk:]

    cend = lax.broadcasted_iota(jnp.int32, (ncp, 1), 0) * CMP_STRIDE + (CMP_BLK - 1)
    m_c = cend <= tq
    ps = []
    for r in range(NSA_REP):
        s = jnp.where(m_c, s_c[:, head(r)], NEG_INF)
        e = jnp.exp2(s - jnp.max(s, axis=0, keepdims=True))
        p = jnp.where(m_c, e * (1.0 / jnp.sum(e, axis=0, keepdims=True)), 0.0)
        ps.append(p.astype(BF16))
    p_c = jnp.concatenate(ps, axis=1)
    nsb = ovt_ref.shape[0]
    oc_imp = _dot(jnp.concatenate([vct_ref[0, 0], ovt_ref[...]], axis=0), p_c)
    o_c, imp4 = oc_imp[:HEAD_DIM], oc_imp[HEAD_DIM:]

    imp = imp4[:, head(0)]
    for r in range(1, NSA_REP):
        imp = imp + imp4[:, head(r)]
    blk_t = (q0 + lax.broadcasted_iota(jnp.int32, (nsb, Q), 1)) >> SEL_SHIFT
    sb = lax.broadcasted_iota(jnp.int32, (nsb, Q), 0)
    forced = (sb == 0) | (sb == blk_t) | (sb == blk_t - 1)
    imp = jnp.where(sb <= blk_t, jnp.where(forced, FORCE_SCORE, imp), -jnp.inf)
    rank = jnp.zeros((nsb, Q), F32)
    for i in range(nsb):
        ri = imp[i:i + 1, :]
        beats = (ri > imp) | ((ri == imp) & (sb > i))
        rank = rank + jnp.where(beats, 1.0, 0.0)
    selb_ref[...] = jnp.where(rank < n_sel, 0.0, NEG_INF)

    dist = tq - (start + lax.broadcasted_iota(jnp.int32, (wk, 1), 0))
    bias_w = jnp.where((dist >= 0) & (dist < WINDOW), 0.0, NEG_INF)
    ps, ls = [], []
    for r in range(NSA_REP):
        s = s_w[:, head(r)] + bias_w
        p = jnp.exp2(s - jnp.max(s, axis=0, keepdims=True))
        ls.append(jnp.sum(p, axis=0, keepdims=True))
        ps.append(p.astype(BF16))
    t0 = start // WIN_TILE
    vw_t = jnp.concatenate([vwt_ref[0, 0, t0 + j] for j in range(wk // WIN_TILE)], axis=1)
    o_w = _dot(vw_t, jnp.concatenate(ps, axis=1)) * (1.0 / jnp.concatenate(ls, axis=1))

    gate = _sigmoid(glt_ref[0, 0])
    for r in range(NSA_REP):
        ocw_ref[:, head(r)] = (gate[r:r + 1, :] * o_c[:, head(r)]
                               + gate[2 * NSA_REP + r:2 * NSA_REP + r + 1, :] * o_w[:, head(r)])

    blocks_per_tile = kt_sz // SEL_BLK
    n_all = seq // kt_sz
    acc_ref[...] = jnp.zeros_like(acc_ref)
    p_ref[1] = jnp.zeros(p_ref.shape[1:], p_ref.dtype)

    def sel_tile(kt, carry):
        m, l, alpha_prev = carry
        slot = kt & 1
        s_all = s_ref[slot]
        p_prev = p_ref[1 - slot]
        k_next = pl.multiple_of(jnp.minimum(kt + 1, n_all - 1) * kt_sz, kt_sz)
        s_ref[1 - slot] = _dot_nt(ks_ref[0, 0, pl.ds(k_next, kt_sz), :], q)
        pv_prev = _dot(vst_ref[0, 0, jnp.maximum(kt - 1, 0)], p_prev)
        acc_ref[...] = acc_ref[...] * alpha_prev + pv_prev
        k0 = kt * kt_sz
        kpos = k0 + lax.broadcasted_iota(jnp.int32, (kt_sz, 1), 0)
        bias = jnp.concatenate(
            [jnp.broadcast_to(selb_ref[pl.ds(kt * blocks_per_tile + i, 1), :], (SEL_BLK, Q))
             for i in range(blocks_per_tile)], axis=0)
        bias = jnp.where(kpos <= tq, bias, NEG_INF)
        ps, ms, ls, alphas = [], [], [], []
        for r in range(NSA_REP):
            s = s_all[:, head(r)] + bias
            m_old = m[:, head(r)]
            m_new = jnp.maximum(m_old, jnp.max(s, axis=0, keepdims=True))
            p = jnp.exp2(s - m_new)
            alpha = jnp.exp2(m_old - m_new)
            ls.append(alpha * l[:, head(r)] + jnp.sum(p, axis=0, keepdims=True))
            ms.append(m_new)
            alphas.append(alpha)
            ps.append(p.astype(BF16))
        p_ref[slot] = jnp.concatenate(ps, axis=1)
        return jnp.concatenate(ms, axis=1), jnp.concatenate(ls, axis=1), jnp.concatenate(alphas, axis=1)

    n_tiles = (q0 + Q + kt_sz - 1) // kt_sz
    init = (jnp.full((1, rows), NEG_INF, F32), jnp.zeros((1, rows), F32), jnp.ones((1, rows), F32))
    _, l_s, alpha_last = lax.fori_loop(0, n_tiles, sel_tile, init)
    pv_last = _dot(vst_ref[0, 0, n_tiles - 1], p_ref[(n_tiles - 1) & 1])
    o_s = (acc_ref[...] * alpha_last + pv_last) * (1.0 / l_s)

    for r in range(NSA_REP):
        out_t = ocw_ref[:, head(r)] + gate[NSA_REP + r:NSA_REP + r + 1, :] * o_s[:, head(r)]
        o_ref[:, r * HEAD_DIM:(r + 1) * HEAD_DIM] = out_t.T.astype(o_ref.dtype)


def nsa_attention(qh, kcc, vcct, ksn, vst, kwn, vwt, gate_logits_t, batch, seq):
    Q_BLK = NSA_Q_BLK
    assert seq % SEL_KEY_TILE == 0 and seq % Q_BLK == 0 and seq >= WINDOW + Q_BLK
    nq = seq // Q_BLK
    nsb = seq // SEL_BLK
    n_sel = min(N_SEL, nsb)
    ncp = kcc.shape[2]
    ncmp = (seq - CMP_BLK) // CMP_STRIDE + 1
    ci = np.arange(ncp)[None, :]
    sj = np.arange(nsb)[:, None]
    ovt = ((ci * CMP_STRIDE < (sj + 1) * SEL_BLK) & (ci * CMP_STRIDE + CMP_BLK > sj * SEL_BLK) & (ci < ncmp))
    ovt = jnp.asarray(ovt, BF16)
    seq_spec = pl.BlockSpec((1, 1, seq, HEAD_DIM), lambda b, g, i: (b, g, 0, 0))
    cmp_spec = pl.BlockSpec((1, 1, ncp, HEAD_DIM), lambda b, g, i: (b, g, 0, 0))
    tiled = lambda a: pl.BlockSpec((1, 1) + a.shape[2:], lambda b, g, i: (b, g, 0, 0, 0))
    return pl.pallas_call(
        functools.partial(_nsa_attn_kernel, seq=seq, n_sel=n_sel),
        grid=(batch, NSA_KV_HEADS, nq),
        in_specs=[pl.BlockSpec((1, 1, NSA_REP, Q_BLK, HEAD_DIM), lambda b, g, i: (b, g, 0, i, 0)),
                  cmp_spec, pl.BlockSpec((1, 1, HEAD_DIM, ncp), lambda b, g, i: (b, g, 0, 0)),
                  seq_spec, tiled(vst), seq_spec, tiled(vwt),
                  pl.BlockSpec((1, 1, 3 * NSA_REP, Q_BLK), lambda b, g, i: (b, g, 0, i)),
                  pl.BlockSpec((nsb, ncp), lambda b, g, i: (0, 0))],
        out_specs=pl.BlockSpec((Q_BLK, NSA_REP * HEAD_DIM), lambda b, g, i: (b * nq + i, g)),
        out_shape=jax.ShapeDtypeStruct((batch * seq, NSA_WIDTH), BF16),
        scratch_shapes=[pltpu.VMEM((nsb, Q_BLK), F32), pltpu.VMEM((HEAD_DIM, NSA_REP * Q_BLK), F32),
                        pltpu.VMEM((HEAD_DIM, NSA_REP * Q_BLK), F32),
                        pltpu.VMEM((2, SEL_KEY_TILE, NSA_REP * Q_BLK), F32),
                        pltpu.VMEM((2, SEL_KEY_TILE, NSA_REP * Q_BLK), BF16)],
        compiler_params=_params("parallel", "parallel", "arbitrary"),
        name="nsa_attention",
    )(qh, kcc, vcct, ksn, vst, kwn, vwt, gate_logits_t, ovt)


def _merge_kernel(ys_ref, yn_ref, ws_ref, wn_ref, gs_ref, gn_ref, o_ref):
    up_s = _dot(ys_ref[...], ws_ref[...])
    up_n = _dot(yn_ref[...], wn_ref[...])
    o_ref[...] = (_sigmoid(gs_ref[...]) * up_s + _sigmoid(gn_ref[...]) * up_n).astype(o_ref.dtype)


def merge_mixers(y_ssd, y_nsa, w_up_ssd, w_up_nsa, gates, tm=1024, tn=512):
    m = y_ssd.shape[0]
    nb = D_MODEL // tn
    return pl.pallas_call(
        _merge_kernel,
        grid=(m // tm, nb),
        in_specs=[pl.BlockSpec((tm, SSD_INNER), lambda i, j: (i, 0)), pl.BlockSpec((tm, NSA_WIDTH), lambda i, j: (i, 0)),
                  pl.BlockSpec((SSD_INNER, tn), lambda i, j: (0, j)), pl.BlockSpec((NSA_WIDTH, tn), lambda i, j: (0, j)),
                  pl.BlockSpec((tm, tn), lambda i, j: (i, j)), pl.BlockSpec((tm, tn), lambda i, j: (i, nb + j))],
        out_specs=pl.BlockSpec((tm, tn), lambda i, j: (i, j)),
        out_shape=jax.ShapeDtypeStruct((m, D_MODEL), BF16),
        compiler_params=_params("parallel", "arbitrary"),
        name="merge_mixers",
    )(y_ssd, y_nsa, w_up_ssd, w_up_nsa, gates, gates)


def _residual_matmul_kernel(h_ref, x_ref, w_ref, o_ref):
    o_ref[...] = h_ref[...] + _dot(x_ref[...], w_ref[...])


def residual_matmul(h, x, w, tm=1024, tn=1024):
    m, k = x.shape
    n = w.shape[1]
    return pl.pallas_call(
        _residual_matmul_kernel,
        grid=(m // tm, n // tn),
        in_specs=[pl.BlockSpec((tm, tn), lambda i, j: (i, j)), pl.BlockSpec((tm, k), lambda i, j: (i, 0)),
                  pl.BlockSpec((k, tn), lambda i, j: (0, j))],
        out_specs=pl.BlockSpec((tm, tn), lambda i, j: (i, j)),
        out_shape=jax.ShapeDtypeStruct((m, n), F32),
        compiler_params=_params("parallel", "arbitrary"),
        name="residual_matmul",
    )(h, x, w)


def _xattn_router_kernel(h_ref, kv_ref, n2_ref, wq_ref, qn_ref, kn_ref, wo_ref, n3_ref, rw_ref, rb_ref,
                         h2_ref, hf_ref, rt_ref):
    h = h_ref[...]
    qp = _dot(_rms(h, n2_ref[...]).astype(BF16), wq_ref[...])
    kv = kv_ref[...]
    scale = HEAD_DIM ** -0.5
    heads = []
    for hd in range(X_HEADS):
        sl = slice(hd * HEAD_DIM, (hd + 1) * HEAD_DIM)
        qh = _rms(qp[:, sl], qn_ref[...]).astype(BF16)
        kh = _rms(kv[:, sl], kn_ref[...]).astype(BF16)
        vh = kv[:, X_WIDTH + hd * HEAD_DIM:X_WIDTH + (hd + 1) * HEAD_DIM].astype(BF16)
        s = _dot_nt(qh, kh) * scale
        e = jnp.exp(s - jnp.max(s, axis=-1, keepdims=True))
        p = e / jnp.sum(e, axis=-1, keepdims=True)
        heads.append(_dot(p.astype(BF16), vh))
    o = jnp.concatenate(heads, axis=1).astype(BF16)
    h2 = h + _dot(o, wo_ref[...])
    h2_ref[...] = h2
    hf = _rms(h2, n3_ref[...]).astype(BF16)
    hf_ref[...] = hf
    lg = _dot(hf, rw_ref[...]) + rb_ref[...]

    lane = lax.broadcasted_iota(jnp.int32, lg.shape, 1)
    rmax = lambda v: jnp.max(v, axis=-1, keepdims=True)
    rsum = lambda v: jnp.sum(v, axis=-1, keepdims=True)
    first_lane = lambda hit: jnp.min(jnp.where(hit, lane, LANES), axis=-1, keepdims=True)
    is_g = lane < N_GROUPS
    eg = jnp.where(is_g, jnp.exp(lg - rmax(jnp.where(is_g, lg, -jnp.inf))), 0.0)
    pg = eg / rsum(eg)
    pg_top = rmax(pg)
    grp = first_lane(is_g & (pg == pg_top))
    in_grp = (lane >= N_GROUPS) & (((lane - N_GROUPS) >> EPG_SHIFT) == grp)
    ee = jnp.where(in_grp, jnp.exp(lg - rmax(jnp.where(in_grp, lg, -jnp.inf))), 0.0)
    pe = jnp.where(in_grp, ee / rsum(ee), -1.0)
    p1 = rmax(pe)
    i1 = first_lane(pe == p1)
    pe2 = jnp.where(lane == i1, -1.0, pe)
    p2 = rmax(pe2)
    i2 = first_lane(pe2 == p2)
    psum = p1 + p2
    route = jnp.where(lane == 0, (i1 - N_GROUPS).astype(F32),
                      jnp.where(lane == 1, (i2 - N_GROUPS).astype(F32),
                                jnp.where(lane == 2, pg_top * p1 / psum,
                                          jnp.where(lane == 3, pg_top * p2 / psum, 0.0))))
    rt_ref[...] = route


def xattn_router(h1, kv, norm2_w, wq, q_norm_w, k_norm_w, wo, norm3_w, router_w, router_b, batch, seq, tm=512):
    n = batch * seq
    tm = min(tm, seq)
    per_seq = seq // tm
    mlen = kv.shape[0] // batch
    full = lambda shape: pl.BlockSpec(shape, lambda i: (0,) * len(shape))
    return pl.pallas_call(
        _xattn_router_kernel,
        grid=(n // tm,),
        in_specs=[pl.BlockSpec((tm, D_MODEL), lambda i: (i, 0)),
                  pl.BlockSpec((mlen, 2 * X_WIDTH), lambda i: (i // per_seq, 0)),
                  full((1, D_MODEL)), full((D_MODEL, X_WIDTH)), full((1, HEAD_DIM)), full((1, HEAD_DIM)),
                  full((X_WIDTH, D_MODEL)), full((1, D_MODEL)), full((D_MODEL, LANES)), full((1, LANES))],
        out_specs=[pl.BlockSpec((tm, D_MODEL), lambda i: (i, 0)), pl.BlockSpec((tm, D_MODEL), lambda i: (i, 0)),
                   pl.BlockSpec((tm, LANES), lambda i: (i, 0))],
        out_shape=[jax.ShapeDtypeStruct((n, D_MODEL), F32), jax.ShapeDtypeStruct((n, D_MODEL), BF16),
                   jax.ShapeDtypeStruct((n, LANES), F32)],
        compiler_params=_params("parallel"),
        name="xattn_router",
    )(h1, kv, norm2_w.reshape(1, -1), wq, q_norm_w.reshape(1, -1), k_norm_w.reshape(1, -1), wo,
      norm3_w.reshape(1, -1), router_w, router_b)


def _moe_rank_kernel(e_ref, rank_ref, cnt_ref, carry_ref):
    @pl.when(pl.program_id(0) == 0)
    def _():
        carry_ref[...] = jnp.zeros_like(carry_ref)

    t = e_ref.shape[0]
    hit = e_ref[...] == lax.broadcasted_iota(jnp.int32, (t, LANES), 1)
    onehot = jnp.where(hit, 1.0, 0.0).astype(BF16)
    earlier = lax.broadcasted_iota(jnp.int32, (t, t), 0) > lax.broadcasted_iota(jnp.int32, (t, t), 1)
    before = _dot(jnp.where(earlier, 1.0, 0.0).astype(BF16), onehot) + carry_ref[...]
    rank_ref[...] = jnp.sum(jnp.where(hit, before, 0.0), axis=-1, keepdims=True).astype(jnp.int32)
    carry_ref[...] += jnp.sum(jnp.where(hit, 1.0, 0.0), axis=0, keepdims=True)
    cnt_ref[...] = carry_ref[...]


def moe_rank(eid, t=1024):
    a = eid.shape[0]
    return pl.pallas_call(
        _moe_rank_kernel,
        grid=(a // t,),
        in_specs=[pl.BlockSpec((t, 1), lambda i: (i, 0))],
        out_specs=[pl.BlockSpec((t, 1), lambda i: (i, 0)), pl.BlockSpec((1, LANES), lambda i: (0, 0))],
        out_shape=[jax.ShapeDtypeStruct((a, 1), jnp.int32), jax.ShapeDtypeStruct((1, LANES), F32)],
        scratch_shapes=[pltpu.VMEM((1, LANES), F32)],
        compiler_params=_params("arbitrary"),
        name="moe_rank",
    )(eid)


def _moe_ffn_kernel(item_e_ref, item_row_ref, item_nsub_ref, item_nout_ref, x_hbm, wg_ref, wu_ref, wd_ref, y_hbm,
                    x_vmem, y_vmem, sem_x, sem_y, *, hid):
    i, c = pl.program_id(0), pl.program_id(1)
    last_c = pl.num_programs(1) - 1
    th = wg_ref.shape[2]
    nsub, nout = item_nsub_ref[i], item_nout_ref[i]
    row0 = pl.multiple_of(item_row_ref[i], MOE_SUB)
    sub = lambda s: pl.ds(s * MOE_SUB, MOE_SUB)

    def x_copy(item, s):
        row = pl.multiple_of(item_row_ref[item], MOE_SUB)
        return pltpu.make_async_copy(x_hbm.at[pl.ds(row + s * MOE_SUB, MOE_SUB)], x_vmem.at[item & 1, sub(s)],
                                     sem_x.at[item & 1, s])

    def y_copy(row, s):
        return pltpu.make_async_copy(y_vmem.at[sub(s)], y_hbm.at[pl.ds(row + s * MOE_SUB, MOE_SUB)], sem_y.at[s])

    def for_subs(count, fn):
        for s in range(MOE_ITEM_SUBS):
            pl.when(s < count)(functools.partial(fn, s))

    @pl.when(c == 0)
    def _():
        @pl.when(i > 0)
        def _():
            prev_row = pl.multiple_of(item_row_ref[i - 1], MOE_SUB)
            for_subs(item_nout_ref[i - 1], lambda s: y_copy(prev_row, s).wait())

        @pl.when(i == 0)
        def _():
            for_subs(nsub, lambda s: x_copy(i, s).start())

        for_subs(nsub, lambda s: x_copy(i, s).wait())

        @pl.when(i + 1 < pl.num_programs(0))
        def _():
            nxt = jnp.minimum(i + 1, pl.num_programs(0) - 1)
            for_subs(item_nsub_ref[nxt], lambda s: x_copy(nxt, s).start())

        @pl.when(nsub == 0)
        def _():
            y_vmem[...] = jnp.zeros_like(y_vmem)
            for_subs(nout, lambda s: y_copy(row0, s).start())

    def ffn(count):
        rows = pl.ds(0, count * MOE_SUB)
        col_ok = c * th + lax.broadcasted_iota(jnp.int32, (1, th), 1) < hid
        row_ok = c * th + lax.broadcasted_iota(jnp.int32, (th, 1), 0) < hid
        wgu = jnp.concatenate([wg_ref[0], wu_ref[0]], axis=1).astype(BF16)
        wd = jnp.where(row_ok, wd_ref[0], 0.0).astype(BF16)
        gu = _dot(x_vmem[i & 1, rows, :], wgu)
        act = jnp.where(col_ok, _silu(gu[:, :th]) * gu[:, th:], 0.0)
        part = _dot(act.astype(BF16), wd)

        @pl.when(c == 0)
        def _():
            y_vmem[rows, :] = part

        @pl.when(c > 0)
        def _():
            y_vmem[rows, :] += part

    for count in range(1, MOE_ITEM_SUBS + 1):
        pl.when(nsub == count)(functools.partial(ffn, count))

    @pl.when((nsub > 0) & (c == last_c))
    def _():
        for_subs(nsub, lambda s: y_copy(row0, s).start())

    @pl.when((i == pl.num_programs(0) - 1) & (c == last_c))
    def _():
        for_subs(nout, lambda s: y_copy(row0, s).wait())


def moe_ffn(x_buf, item_e, item_row, item_nsub, item_nout, n_active, w_gate, w_up, w_down):
    rows, d = x_buf.shape
    hid = w_gate.shape[2]
    th = MOE_HID_TILE
    nc = pl.cdiv(hid, th)
    tile = lambda i, c, ns: jnp.where(ns[i] > 0, c, nc - 1)
    grid_spec = pltpu.PrefetchScalarGridSpec(
        num_scalar_prefetch=4,
        grid=(n_active, nc),
        in_specs=[pl.BlockSpec(memory_space=pl.ANY),
                  pl.BlockSpec((1, d, th), lambda i, c, ie, ir, ns, no: (ie[i], 0, tile(i, c, ns))),
                  pl.BlockSpec((1, d, th), lambda i, c, ie, ir, ns, no: (ie[i], 0, tile(i, c, ns))),
                  pl.BlockSpec((1, th, d), lambda i, c, ie, ir, ns, no: (ie[i], tile(i, c, ns), 0))],
        out_specs=pl.BlockSpec(memory_space=pl.ANY),
        scratch_shapes=[pltpu.VMEM((2, MOE_ITEM_SUBS * MOE_SUB, d), BF16),
                        pltpu.VMEM((MOE_ITEM_SUBS * MOE_SUB, d), F32),
                        pltpu.SemaphoreType.DMA((2, MOE_ITEM_SUBS)), pltpu.SemaphoreType.DMA((MOE_ITEM_SUBS,))],
    )
    return pl.pallas_call(
        functools.partial(_moe_ffn_kernel, hid=hid),
        grid_spec=grid_spec,
        out_shape=jax.ShapeDtypeStruct((rows, d), F32),
        compiler_params=_params("arbitrary", "arbitrary"),
        name="moe_ffn",
    )(item_e, item_row, item_nsub, item_nout, x_buf, w_gate, w_up, w_down)


def hier_moe(h2, hf, route, w_gate, w_up, w_down):
    n, d = h2.shape
    n_exp = w_gate.shape[0]
    i32 = jnp.int32
    eid = route[:, 0:TOP_K].astype(i32).reshape(-1)
    wts = route[:, TOP_K:2 * TOP_K]
    n_assign = n * TOP_K
    rank, counts = moe_rank(eid.reshape(n_assign, 1))
    counts = counts[0, :n_exp].astype(i32)
    subs_e = (counts + MOE_SUB - 1) // MOE_SUB
    sub_end = jnp.cumsum(subs_e)
    row_start = (sub_end - subs_e) * MOE_SUB
    is_e = eid[:, None] == jnp.arange(n_exp, dtype=i32)[None, :]
    dest = rank[:, 0] + jnp.sum(jnp.where(is_e, row_start[None, :], 0), axis=1)
    n_subs = (n_assign + n_exp * (MOE_SUB - 1) + MOE_SUB - 1) // MOE_SUB
    rows = n_subs * MOE_SUB
    tok = jnp.repeat(jnp.arange(n, dtype=i32), TOP_K)
    tok_buf = (jnp.arange(rows, dtype=i32) % n).at[dest].set(tok)
    per = MOE_ITEM_SUBS
    n_items = (n_exp * (per - 1) + n_subs + per - 1) // per
    items_e = (subs_e + per - 1) // per
    item_end = jnp.cumsum(items_e)
    n_used = item_end[-1]
    idx = jnp.arange(n_items, dtype=i32)
    used = idx < n_used
    e_of = jnp.minimum(jnp.searchsorted(item_end, idx, side='right'), n_exp - 1).astype(i32)
    j = idx - (item_end - items_e)[e_of]
    item_nsub = jnp.where(used, jnp.clip(subs_e[e_of] - per * j, 0, per), 0)
    fill_first = jnp.minimum(sub_end[-1] + per * (idx - n_used), n_subs)
    item_fill = jnp.where(used, 0, jnp.minimum(n_subs - fill_first, per))
    item_row = jnp.where(used, row_start[e_of] + j * per * MOE_SUB, jnp.minimum(fill_first, n_subs - 1) * MOE_SUB)
    item_e = jnp.where(used, e_of, e_of[jnp.maximum(n_used - 1, 0)])
    n_active = n_used + (n_subs - sub_end[-1] + per - 1) // per
    y_buf = moe_ffn(hf[tok_buf], item_e.astype(i32), item_row.astype(i32), item_nsub.astype(i32),
                    (item_nsub + item_fill).astype(i32), n_active.astype(i32), w_gate, w_up, w_down)
    slot = dest.reshape(n, TOP_K)
    return h2 + wts[:, 0:1] * y_buf[slot[:, 0]] + wts[:, 1:2] * y_buf[slot[:, 1]]


COL_DT = ZX_COLS
COL_QKV = COL_DT + SSD_HEADS
COL_NSA_GATE = COL_QKV + QKV_COLS
COL_MERGE_GATE = COL_NSA_GATE + 3 * NSA_HEADS


def _small_w_in(w_in):
    pad = jnp.zeros(w_in.shape[:2] + (SMALL_COLS - SSD_HEADS - 3 * NSA_HEADS,), w_in.dtype)
    return jnp.concatenate([w_in[:, :, COL_DT:COL_QKV], w_in[:, :, COL_NSA_GATE:COL_MERGE_GATE], pad], axis=2)


def _layer(x, mem, norm1_w, w_in, ssd_conv_w, ssd_conv_b, ssd_dt_bias, ssd_a_log, ssd_d, ssd_norm_w,
           nsa_q_norm_w, nsa_k_norm_w, cmp_pe_k, cmp_w1_k, cmp_w2_k, cmp_pe_v, cmp_w1_v, cmp_w2_v,
           w_up_ssd, w_up_nsa, w_out, norm2_w, mem_norm_w, xq_w, xkv_w, x_q_norm_w, x_k_norm_w, xo_w,
           norm3_w, router_g_w, router_g_b, router_e_w, router_e_b, moe_w_gate, moe_w_up, moe_w_down):
    batch, seq, d = x.shape
    n = batch * seq
    xf = x.reshape(n, d)
    hn = rmsnorm_rows(xf, norm1_w)
    w_cols = lambda c0, nc: w_in[0, :, c0:c0 + nc].astype(BF16)
    zx = matmul(hn, w_cols(0, ZX_COLS), tm=IN_PROJ_TM, tn=IN_PROJ_TN, fuse_w_producer=True)
    qkv = matmul(hn, w_cols(COL_QKV, QKV_COLS), tm=IN_PROJ_TM, tn=IN_PROJ_TN, fuse_w_producer=True)
    mgates = matmul(hn, w_cols(COL_MERGE_GATE, GATE_COLS), tm=IN_PROJ_TM, tn=IN_PROJ_TN, fuse_w_producer=True)
    small = matmul(hn, _small_w_in(w_in)[0].astype(BF16), tm=IN_PROJ_TM)
    y_ssd = ssd_branch(zx, small, ssd_conv_w, ssd_conv_b, ssd_dt_bias, ssd_a_log, ssd_d, ssd_norm_w, batch, seq)
    qh, kcn, vcr, ksn, vsn, kwn, vwn = nsa_prep(qkv, nsa_q_norm_w, nsa_k_norm_w, batch, seq)
    kcc = compress(kcn, cmp_pe_k, cmp_w1_k, cmp_w2_k, transpose_out=False)
    vcct = compress(vcr, cmp_pe_v, cmp_w1_v, cmp_w2_v, transpose_out=True)
    gl = small[:, SSD_HEADS:SSD_HEADS + 3 * NSA_HEADS].reshape(batch, seq, 3, NSA_KV_HEADS, NSA_REP)
    gl = gl.transpose(0, 3, 2, 4, 1).reshape(batch, NSA_KV_HEADS, 3 * NSA_REP, seq)
    y_nsa = nsa_attention(qh, kcc, vcct, ksn, vsn, kwn, vwn, gl, batch, seq)
    merged = merge_mixers(y_ssd, y_nsa, w_up_ssd.astype(BF16), w_up_nsa.astype(BF16), mgates)
    h1 = residual_matmul(xf, merged, w_out.astype(BF16))
    mlen = mem.shape[1]
    mn = rmsnorm_rows(mem.reshape(batch * mlen, d), mem_norm_w, tm=min(512, batch * mlen))
    kv = matmul(mn, xkv_w.astype(BF16))
    n_exp = router_e_w.shape[1]
    rpad = LANES - N_GROUPS - n_exp
    router_w = jnp.concatenate([router_g_w, router_e_w, jnp.zeros((d, rpad), F32)], axis=1).astype(BF16)
    router_b = jnp.concatenate([router_g_b, router_e_b, jnp.zeros((rpad,), F32)]).reshape(1, LANES)
    h2, hf, route = xattn_router(h1, kv, norm2_w, xq_w.astype(BF16), x_q_norm_w, x_k_norm_w, xo_w.astype(BF16),
                                 norm3_w, router_w, router_b, batch, seq)
    out = hier_moe(h2, hf, route, moe_w_gate, moe_w_up, moe_w_down)
    return out.reshape(batch, seq, d)


def kernel(x, mem, norm1_w, w_in, ssd_conv_w, ssd_conv_b, ssd_dt_bias, ssd_a_log, ssd_d, ssd_norm_w, nsa_q_norm_w, nsa_k_norm_w, cmp_pe_k, cmp_w1_k, cmp_w2_k, cmp_pe_v, cmp_w1_v, cmp_w2_v, w_up_ssd, w_up_nsa, w_out, norm2_w, mem_norm_w, xq_w, xkv_w, x_q_norm_w, x_k_norm_w, xo_w, norm3_w, router_g_w, router_g_b, router_e_w, router_e_b, moe_w_gate, moe_w_up, moe_w_down):
    h = x
    for l in range(norm1_w.shape[0]):
        h = _layer(h, mem, norm1_w[l], w_in[l:l + 1], ssd_conv_w[l], ssd_conv_b[l], ssd_dt_bias[l], ssd_a_log[l], ssd_d[l],
                   ssd_norm_w[l], nsa_q_norm_w[l], nsa_k_norm_w[l], cmp_pe_k[l], cmp_w1_k[l], cmp_w2_k[l], cmp_pe_v[l],
                   cmp_w1_v[l], cmp_w2_v[l], w_up_ssd[l], w_up_nsa[l], w_out[l], norm2_w[l], mem_norm_w[l], xq_w[l],
                   xkv_w[l], x_q_norm_w[l], x_k_norm_w[l], xo_w[l], norm3_w[l], router_g_w[l], router_g_b[l],
                   router_e_w[l], router_e_b[l], moe_w_gate[l], moe_w_up[l], moe_w_down[l])
    return h.astype(x.dtype)
```
